```python
import jax, jax.numpy as jnp
from jax import lax
import numpy as np

D_MODEL = 1024
BATCH = 16
SEQ = 4096
DEPTH = 1

HEAD_DIM = 64
N_FOX_HEADS = 8
N_DIL_HEADS = 8
FOX_WIDTH = N_FOX_HEADS * HEAD_DIM
DIL_WIDTH = N_DIL_HEADS * HEAD_DIM
IN_COLS = 3 * FOX_WIDTH + N_FOX_HEADS + 3 * DIL_WIDTH
Q_BLOCK = 128
DILATED_BRANCHES = ((128, 1), (512, 4), (2048, 16))
T5_NUM_BUCKETS = 32
T5_MAX_DISTANCE = 2048
N_EXPERTS = 256
TOP_K = 8
N_EXPERT_GROUPS = 8
TOP_K_GROUPS = 4
EXPERT_HIDDEN = 256
SHARED_HIDDEN = 256
ROUTED_SCALE = 2.5
DISPATCH_BLOCK = 128
N_MOD = 6
EPS = 1e-6

kernel_name = "hybrid_fox_dilated_moe_adaln"


def rmsnorm(x):
    xf = x.astype(jnp.float32)
    return (xf * lax.rsqrt(jnp.mean(xf * xf, axis=-1, keepdims=True) + EPS)).astype(x.dtype)


def _t5_bucket(dist):
    max_exact = T5_NUM_BUCKETS // 2
    d = np.maximum(dist, 1).astype(np.float32)
    large = max_exact + (np.log(d / max_exact) / np.log(T5_MAX_DISTANCE / max_exact)
                         * (T5_NUM_BUCKETS - max_exact)).astype(np.int32)
    large = np.minimum(large, T5_NUM_BUCKETS - 1)
    return np.where(dist < max_exact, dist, large).astype(np.int32)


def fox_attention(q, k, v, f_logit, b_forget):
    B, S, H, E = q.shape
    log_f = jax.nn.log_sigmoid(f_logit.astype(jnp.float32) + b_forget.astype(jnp.float32))
    F = jnp.cumsum(log_f, axis=1).transpose(0, 2, 1)
    scale = HEAD_DIM ** -0.5
    outs = []
    for i in range(S // Q_BLOCK):
        q0, q1 = i * Q_BLOCK, (i + 1) * Q_BLOCK
        s = jnp.einsum('bqhe,bkhe->bhqk', q[:, q0:q1], k[:, :q1]).astype(jnp.float32) * scale
        s = s + F[:, :, q0:q1, None] - F[:, :, None, :q1]
        causal = np.arange(q0, q1)[:, None] >= np.arange(q1)[None, :]
        p = jax.nn.softmax(jnp.where(causal, s, -jnp.inf), axis=-1)
        outs.append(jnp.einsum('bhqk,bkhe->bqhe', p.astype(v.dtype), v[:, :q1]))
    return jnp.concatenate(outs, axis=1)


def dilated_branch(q, k, v, rel_bias, window, dilation):
    B, S, H, E = q.shape
    blk = window // dilation
    span = blk * dilation
    L = -(-S // span) * span
    nb = L // span

    def to_blocks(t):
        t = jnp.pad(t, ((0, 0), (0, L - S), (0, 0), (0, 0)))
        t = t.reshape(B, L // dilation, dilation, H, E).transpose(0, 2, 1, 3, 4)
        return t.reshape(B, dilation, nb, blk, H, E)

    def with_prev(t):
        prev = jnp.pad(t[:, :, :-1], ((0, 0), (0, 0), (1, 0), (0, 0), (0, 0), (0, 0)))
        return jnp.concatenate([prev, t], axis=3)

    qb = to_blocks(q)
    kk = with_prev(to_blocks(k))
    vv = with_prev(to_blocks(v))
    rel = np.arange(blk)[:, None] + blk - np.arange(2 * blk)[None, :]
    band = (rel >= 0) & (rel <= blk)
    key_ok = (np.arange(nb)[:, None] * blk + np.arange(2 * blk)[None, :] - blk) >= 0
    mask = (band[None] & key_ok[:, None, :])[None, None, :, None]
    bucket = _t5_bucket(np.clip(rel, 0, blk) * dilation)
    bias = rel_bias[bucket].astype(jnp.float32).transpose(2, 0, 1)
    s = jnp.einsum('brnqhe,brnkhe->brnhqk', qb, kk).astype(jnp.float32) * (HEAD_DIM ** -0.5) + bias
    s = jnp.where(mask, s, -jnp.inf)
    lse = jax.nn.logsumexp(s, axis=-1)
    p = jnp.exp(s - lse[..., None])
    o = jnp.einsum('brnhqk,brnkhe->brnqhe', p.astype(vv.dtype), vv)
    o = o.reshape(B, dilation, L // dilation, H, E).transpose(0, 2, 1, 3, 4).reshape(B, L, H, E)[:, :S]
    lse = lse.transpose(0, 1, 2, 4, 3).reshape(B, dilation, L // dilation, H)
    lse = lse.transpose(0, 2, 1, 3).reshape(B, L, H)[:, :S]
    return o, lse


def dilated_mixture(q, k, v, rel_bias):
    outs, lses = [], []
    for window, dilation in DILATED_BRANCHES:
        o, lse = dilated_branch(q, k, v, rel_bias, window, dilation)
        outs.append(o)
        lses.append(lse)
    w = jax.nn.softmax(jnp.stack(lses, axis=-1), axis=-1)
    o = jnp.stack(outs, axis=-1)
    return jnp.sum(o * w[:, :, :, None, :].astype(o.dtype), axis=-1)


def hybrid_mixer(h, w_in, b_forget, g_fox_out, g_dil_out, w_out, rel_bias):
    B, S, _ = h.shape
    proj = h @ w_in
    o1, o2, o3, o4 = FOX_WIDTH, 2 * FOX_WIDTH, 3 * FOX_WIDTH, 3 * FOX_WIDTH + N_FOX_HEADS
    heads = lambda t, n: t.reshape(B, S, n, HEAD_DIM)
    q_f = heads(proj[..., :o1], N_FOX_HEADS)
    k_f = heads(proj[..., o1:o2], N_FOX_HEADS)
    v_f = heads(proj[..., o2:o3], N_FOX_HEADS)
    f_logit = proj[..., o3:o4]
    q_d = heads(proj[..., o4:o4 + DIL_WIDTH], N_DIL_HEADS)
    k_d = heads(proj[..., o4 + DIL_WIDTH:o4 + 2 * DIL_WIDTH], N_DIL_HEADS)
    v_d = heads(proj[..., o4 + 2 * DIL_WIDTH:], N_DIL_HEADS)
    y_fox = fox_attention(q_f, k_f, v_f, f_logit, b_forget).reshape(B, S, FOX_WIDTH)
    y_dil = dilated_mixture(q_d, k_d, v_d, rel_bias).reshape(B, S, DIL_WIDTH)
    merged = jnp.concatenate([rmsnorm(y_fox) * g_fox_out, rmsnorm(y_dil) * g_dil_out], axis=-1)
    return merged @ w_out


def swiglu(x, w_gate, w_up, w_down):
    return (jax.nn.silu(x @ w_gate) * (x @ w_up)) @ w_down


def moe(h, w_router, router_bias, w_exp_gate, w_exp_up, w_exp_down, w_sh_gate, w_sh_up, w_sh_down):
    B, S, D = h.shape
    N = B * S
    hf = h.reshape(N, D)
    scores = jax.nn.sigmoid((hf @ w_router).astype(jnp.float32))
    sel = scores + router_bias.astype(jnp.float32)
    grp = lax.top_k(sel.reshape(N, N_EXPERT_GROUPS, N_EXPERTS // N_EXPERT_GROUPS), 2)[0].sum(-1)
    _, gidx = lax.top_k(grp, TOP_K_GROUPS)
    gmask = jax.nn.one_hot(gidx, N_EXPERT_GROUPS, dtype=jnp.float32).sum(1) > 0
    emask = jnp.repeat(gmask, N_EXPERTS // N_EXPERT_GROUPS, axis=1)
    _, eidx = lax.top_k(jnp.where(emask, sel, -jnp.inf), TOP_K)
    wts = jnp.take_along_axis(scores, eidx, axis=1)
    wts = wts / jnp.sum(wts, axis=-1, keepdims=True) * ROUTED_SCALE

    NK = N * TOP_K
    flat_e = eidx.reshape(NK).astype(jnp.int32)
    flat_tok = (jnp.arange(NK, dtype=jnp.int32) // TOP_K)
    flat_w = wts.reshape(NK)
    order = jnp.argsort(flat_e)
    se, stok, sw = flat_e[order], flat_tok[order], flat_w[order]
    counts = jnp.bincount(flat_e, length=N_EXPERTS).astype(jnp.int32)
    padded = (counts + DISPATCH_BLOCK - 1) // DISPATCH_BLOCK * DISPATCH_BLOCK
    start = jnp.cumsum(counts) - counts
    pend = jnp.cumsum(padded)
    pstart = pend - padded
    dest = pstart[se] + (jnp.arange(NK, dtype=jnp.int32) - start[se])
    R = NK + N_EXPERTS * DISPATCH_BLOCK
    nblk = R // DISPATCH_BLOCK
    row_tok = jnp.full((R,), N, dtype=jnp.int32).at[dest].set(stok)
    row_w = jnp.zeros((R,), jnp.float32).at[dest].set(sw)
    block_e = jnp.minimum(jnp.searchsorted(pend, jnp.arange(nblk, dtype=jnp.int32) * DISPATCH_BLOCK,
                                           side='right'), N_EXPERTS - 1).astype(jnp.int32)
    h_pad = jnp.concatenate([hf, jnp.zeros((1, D), hf.dtype)], axis=0)

    def step(acc, blk):
        tok, w, e = blk
        xb = h_pad[tok]
        y = swiglu(xb, w_exp_gate[e], w_exp_up[e], w_exp_down[e])
        return acc.at[tok].add((y * w[:, None]).astype(acc.dtype)), None

    acc, _ = lax.scan(step, jnp.zeros((N + 1, D), hf.dtype),
                      (row_tok.reshape(nblk, DISPATCH_BLOCK), row_w.reshape(nblk, DISPATCH_BLOCK), block_e))
    routed = acc[:N]
    shared = swiglu(hf, w_sh_gate, w_sh_up, w_sh_down)
    return (routed + shared).reshape(B, S, D)


def setup_inputs(seed: int = 0) -> dict:
    key = jax.random.key(seed)
    ks = jax.random.split(key, 20)
    D, E, Hd = D_MODEL, N_EXPERTS, EXPERT_HIDDEN
    nrm = lambda k, shape, fan: jax.random.normal(k, shape, jnp.float32) * (fan ** -0.5)
    b_forget = (jnp.linspace(1.0, 6.0, N_FOX_HEADS, dtype=jnp.float32)[None, :]
                + 0.1 * jax.random.normal(ks[3], (DEPTH, N_FOX_HEADS), jnp.float32))
    return {
        "x": jax.random.normal(ks[0], (BATCH, SEQ, D), jnp.float32),
        "c": jax.random.normal(ks[1], (BATCH, D), jnp.float32),
        "w_in": nrm(ks[2], (DEPTH, D, IN_COLS), D),
        "b_forget": b_forget,
        "g_fox_out": 1.0 + 0.02 * jax.random.normal(ks[4], (DEPTH, FOX_WIDTH), jnp.float32),
        "g_dil_out": 1.0 + 0.02 * jax.random.normal(ks[5], (DEPTH, DIL_WIDTH), jnp.float32),
        "w_out": nrm(ks[6], (DEPTH, FOX_WIDTH + DIL_WIDTH, D), FOX_WIDTH + DIL_WIDTH),
        "w_ada": 0.5 * nrm(ks[7], (DEPTH, D, N_MOD * D), D),
        "b_ada": 0.02 * jax.random.normal(ks[8], (DEPTH, N_MOD * D), jnp.float32),
        "w_router": nrm(ks[9], (DEPTH, D, E), D),
        "router_bias": 0.01 * jax.random.normal(ks[10], (DEPTH, E), jnp.float32),
        "w_exp_gate": nrm(ks[11], (DEPTH, E, D, Hd), D),
        "w_exp_up": nrm(ks[12], (DEPTH, E, D, Hd), D),
        "w_exp_down": nrm(ks[13], (DEPTH, E, Hd, D), Hd),
        "w_sh_gate": nrm(ks[14], (DEPTH, D, SHARED_HIDDEN), D),
        "w_sh_up": nrm(ks[15], (DEPTH, D, SHARED_HIDDEN), D),
        "w_sh_down": nrm(ks[16], (DEPTH, SHARED_HIDDEN, D), SHARED_HIDDEN),
        "rel_bias": 0.5 * jax.random.normal(ks[17], (T5_NUM_BUCKETS, N_DIL_HEADS), jnp.float32),
        "g_final": 1.0 + 0.02 * jax.random.normal(ks[18], (D,), jnp.float32),
    }


def reference(x, c, w_in, b_forget, g_fox_out, g_dil_out, w_out, w_ada, b_ada, w_router, router_bias,
              w_exp_gate, w_exp_up, w_exp_down, w_sh_gate, w_sh_up, w_sh_down, rel_bias, g_final):
    c_act = jax.nn.silu(c)
    for l in range(DEPTH):
        mod = c_act @ w_ada[l] + b_ada[l]
        shift1, scale1, gate1, shift2, scale2, gate2 = [m[:, None, :] for m in jnp.split(mod, N_MOD, axis=-1)]
        h = rmsnorm(x) * (1.0 + scale1) + shift1
        x = x + gate1 * hybrid_mixer(h, w_in[l], b_forget[l], g_fox_out[l], g_dil_out[l], w_out[l], rel_bias)
        h = rmsnorm(x) * (1.0 + scale2) + shift2
        x = x + gate2 * moe(h, w_router[l], router_bias[l], w_exp_gate[l], w_exp_up[l], w_exp_down[l],
                            w_sh_gate[l], w_sh_up[l], w_sh_down[l])
    return rmsnorm(x) * g_final
```

```python
import functools

import numpy as np
import jax
import jax.numpy as jnp
from jax import lax
from jax.experimental import pallas as pl
from jax.experimental.pallas import tpu as pltpu

F32 = jnp.float32
BF16 = jnp.bfloat16

HEAD_DIM = 64
N_HEADS = 8
WIDTH = N_HEADS * HEAD_DIM
LANES = 128
SUBLANES = 8
HEAD_PAIRS = WIDTH // LANES
DIL_BLOCK = 128
DILATIONS = (1, 4, 16)
T5_NUM_BUCKETS = 32
T5_MAX_DISTANCE = 2048
N_EXPERTS = 256
TOP_K = 8
N_GROUPS = 8
TOP_K_GROUPS = 4
ROUTED_SCALE = 2.5
EPS = 1e-6
NEG = -1e30
VMEM_LIMIT = 56 * 1024 * 1024

ROW_TILE = 512
FOX_TILE = 256
FFN_BLOCK = 256
COMBINE_TILE = 128


def _params(semantics):
    return pltpu.CompilerParams(dimension_semantics=semantics, vmem_limit_bytes=VMEM_LIMIT)


def _rms(x):
    return x * lax.rsqrt(jnp.mean(x * x, axis=-1, keepdims=True) + EPS)


def _silu(x):
    return x * jax.nn.sigmoid(x)


def _mod_kernel(c_ref, w_ref, b_ref, o_ref):
    o_ref[...] = jnp.dot(_silu(c_ref[...]), w_ref[...], precision=lax.Precision.HIGHEST,
                         preferred_element_type=F32) + b_ref[...]


def _modulation(c, w_ada, b_ada):
    bsz, d = c.shape
    n = w_ada.shape[1]
    tn = 1536
    return pl.pallas_call(
        _mod_kernel,
        grid=(n // tn,),
        in_specs=[pl.BlockSpec((bsz, d), lambda j: (0, 0)),
                  pl.BlockSpec((d, tn), lambda j: (0, j)),
                  pl.BlockSpec((1, tn), lambda j: (0, j))],
        out_specs=pl.BlockSpec((bsz, tn), lambda j: (0, j)),
        out_shape=jax.ShapeDtypeStruct((bsz, n), F32),
        compiler_params=_params(("arbitrary",)),
        name="adaln_mod",
    )(c, w_ada, b_ada.reshape(1, n))


def _inproj_kernel(x_ref, sc_ref, sh_ref, wf_ref, wd_ref, wl_ref, of_ref, od_ref, ol_ref):
    h = _rms(x_ref[0]) * (1.0 + sc_ref[0]) + sh_ref[0]
    hb = h.astype(BF16)
    of_ref[0] = jnp.dot(hb, wf_ref[...], preferred_element_type=F32).astype(BF16)
    od_ref[0] = jnp.dot(hb, wd_ref[...], preferred_element_type=F32).astype(BF16)
    ol_ref[0] = jnp.dot(hb, wl_ref[...], preferred_element_type=F32)


def _in_projection(x, scale1, shift1, w_fox, w_dil, w_flog):
    bsz, s, d = x.shape
    tm = ROW_TILE
    vec = pl.BlockSpec((1, 1, d), lambda b, i: (b, 0, 0))
    full = lambda w: pl.BlockSpec(w.shape, lambda b, i: (0, 0))
    row = lambda n: pl.BlockSpec((1, tm, n), lambda b, i: (b, i, 0))
    return pl.pallas_call(
        _inproj_kernel,
        grid=(bsz, s // tm),
        in_specs=[row(d), vec, vec, full(w_fox), full(w_dil), full(w_flog)],
        out_specs=[row(3 * WIDTH), row(3 * WIDTH), row(LANES)],
        out_shape=[jax.ShapeDtypeStruct((bsz, s, 3 * WIDTH), BF16),
                   jax.ShapeDtypeStruct((bsz, s, 3 * WIDTH), BF16),
                   jax.ShapeDtypeStruct((bsz, s, LANES), F32)],
        compiler_params=_params(("parallel", "arbitrary")),
        name="norm1_inproj",
    )(x, scale1, shift1, w_fox, w_dil, w_flog)


def _forget_kernel(f_ref, b_ref, o_ref):
    z = f_ref[0] + b_ref[...]
    x = jnp.minimum(z, 0.0) - jnp.log1p(jnp.exp(-jnp.abs(z)))
    lane = lax.broadcasted_iota(jnp.int32, x.shape, 1)
    shift = 1
    while shift < x.shape[1]:
        x = x + jnp.where(lane >= shift, pltpu.roll(x, shift, axis=1), 0.0)
        shift *= 2
    o_ref[0] = x


def _forget_cumsum(f_t, b_forget):
    bsz, h, s = f_t.shape
    return pl.pallas_call(
        _forget_kernel,
        grid=(bsz,),
        in_specs=[pl.BlockSpec((1, h, s), lambda b: (b, 0, 0)),
                  pl.BlockSpec((h, 1), lambda b: (0, 0))],
        out_specs=pl.BlockSpec((1, h, s), lambda b: (b, 0, 0)),
        out_shape=jax.ShapeDtypeStruct((bsz, h, s), F32),
        compiler_params=_params(("parallel",)),
        name="forget_cumsum",
    )(f_t, b_forget.reshape(h, 1))


def _stack_heads(q):
    lane = lax.broadcasted_iota(jnp.int32, q.shape, 1)
    zero = jnp.zeros_like(q)
    return jnp.concatenate([jnp.where(lane < HEAD_DIM, q, zero), jnp.where(lane >= HEAD_DIM, q, zero)], axis=0)


def _unstack_heads(o2):
    rows = o2.shape[0] // 2
    lane = lax.broadcasted_iota(jnp.int32, (rows, LANES), 1)
    return jnp.where(lane < HEAD_DIM, o2[:rows], o2[rows:])


def _fox_kernel(q_ref, k_ref, v_ref, f_ref, o_ref, m_s, l_s, acc_s, *, t):
    i = pl.program_id(2)
    q2 = _stack_heads(q_ref[0])
    m_s[...] = jnp.full(m_s.shape, NEG, F32)
    l_s[...] = jnp.zeros(l_s.shape, F32)
    acc_s[...] = jnp.zeros(acc_s.shape, F32)

    def step(j, masked):
        ks = pl.multiple_of(j * t, t)
        k = k_ref[0, pl.ds(ks, t), :]
        v = v_ref[0, pl.ds(ks, t), :]
        s = lax.dot_general(q2, k, (((1,), (1,)), ((), ())), preferred_element_type=F32)
        f = f_ref[0, 0, :, pl.ds(ks, t)]
        s = s - jnp.concatenate([jnp.broadcast_to(f[0:1], (t, t)), jnp.broadcast_to(f[1:2], (t, t))], axis=0)
        if masked:
            row = lax.broadcasted_iota(jnp.int32, (t, t), 0)
            col = lax.broadcasted_iota(jnp.int32, (t, t), 1)
            keep = jnp.concatenate([col <= row, col <= row], axis=0)
            s = jnp.where(keep, s, NEG)
        m_prev = m_s[...]
        m_new = jnp.maximum(m_prev, jnp.max(s, axis=1, keepdims=True))
        alpha = jnp.exp(m_prev - m_new)
        p = jnp.exp(s - m_new)
        l_s[...] = alpha * l_s[...] + jnp.sum(p, axis=1, keepdims=True)
        acc_s[...] = alpha * acc_s[...] + jnp.dot(p.astype(BF16), v, preferred_element_type=F32)
        m_s[...] = m_new

    def body(j, carry):
        step(j, False)
        return carry

    lax.fori_loop(0, i, body, 0)
    step(i, True)
    o_ref[0] = _unstack_heads(acc_s[...] / l_s[...])


def _fox_attention(qkv, f_cum):
    bsz, s, _ = qkv.shape
    t = FOX_TILE
    f4 = f_cum.reshape(bsz, HEAD_PAIRS, 2, s)
    return pl.pallas_call(
        functools.partial(_fox_kernel, t=t),
        grid=(bsz, HEAD_PAIRS, s // t),
        in_specs=[pl.BlockSpec((1, t, LANES), lambda b, h, i: (b, i, h)),
                  pl.BlockSpec((1, s, LANES), lambda b, h, i: (b, 0, HEAD_PAIRS + h)),
                  pl.BlockSpec((1, s, LANES), lambda b, h, i: (b, 0, 2 * HEAD_PAIRS + h)),
                  pl.BlockSpec((1, 1, 2, s), lambda b, h, i: (b, h, 0, 0))],
        out_specs=pl.BlockSpec((1, t, LANES), lambda b, h, i: (b, i, h)),
        out_shape=jax.ShapeDtypeStruct((bsz, s, WIDTH), F32),
        scratch_shapes=[pltpu.VMEM((2 * t, 1), F32), pltpu.VMEM((2 * t, 1), F32),
                        pltpu.VMEM((2 * t, LANES), F32)],
        compiler_params=_params(("parallel", "parallel", "arbitrary")),
        name="fox_attention",
    )(qkv, qkv, qkv, f4)


def _t5_bucket(dist):
    max_exact = T5_NUM_BUCKETS // 2
    d = np.maximum(dist, 1).astype(np.float32)
    large = max_exact + (np.log(d / max_exact) / np.log(T5_MAX_DISTANCE / max_exact)
                         * (T5_NUM_BUCKETS - max_exact)).astype(np.int32)
    large = np.minimum(large, T5_NUM_BUCKETS - 1)
    return np.where(dist < max_exact, dist, large).astype(np.int32)


def _dilated_bias(rel_bias):
    blk = DIL_BLOCK
    rel = np.arange(blk)[:, None] + blk - np.arange(2 * blk)[None, :]
    band = (rel >= 0) & (rel <= blk)
    tables = []
    for dil in DILATIONS:
        bucket = _t5_bucket(np.clip(rel, 0, blk) * dil)
        bias = rel_bias[bucket].astype(F32).transpose(2, 0, 1)
        tables.append(jnp.where(band[None], bias, NEG))
    return jnp.stack(tables).reshape(len(DILATIONS), HEAD_PAIRS, 2 * blk, 2 * blk)


def _dil_kernel(q_ref, k_ref, v_ref, bias_ref, o_ref, qf, kf, vf, ob0, ob1, ob2, ls0, ls1, ls2, *, s_len):
    blk = DIL_BLOCK
    qf[...] = q_ref[0].astype(F32)
    kf[...] = k_ref[0].astype(F32)
    vf[...] = v_ref[0].astype(F32)
    col = lax.broadcasted_iota(jnp.int32, (2 * blk, 2 * blk), 1)

    for bi, (dil, ob, ls) in enumerate(zip(DILATIONS, (ob0, ob1, ob2), (ls0, ls1, ls2))):
        span = blk * dil
        nb = s_len // span

        def rows(start, dil=dil):
            return pl.ds(start, blk) if dil == 1 else pl.ds(start, blk, stride=dil)

        def block(tix, carry, bi=bi, span=span, nb=nb, ob=ob, ls=ls, rows=rows):
            n = tix % nb
            start = n * span + tix // nb
            prev = jnp.maximum(start - span, 0)
            q2 = _stack_heads(qf[rows(start), :].astype(BF16))
            kk = jnp.concatenate([kf[rows(prev), :], kf[rows(start), :]], axis=0).astype(BF16)
            vv = jnp.concatenate([vf[rows(prev), :], vf[rows(start), :]], axis=0).astype(BF16)
            s = lax.dot_general(q2, kk, (((1,), (1,)), ((), ())), preferred_element_type=F32)
            s = s + bias_ref[bi, 0]
            s = jnp.where((col < blk) & (n == 0), NEG, s)
            m = jnp.max(s, axis=1, keepdims=True)
            p = jnp.exp(s - m)
            l = jnp.sum(p, axis=1, keepdims=True)
            o2 = jnp.dot(p.astype(BF16), vv, preferred_element_type=F32) / l
            lse = jnp.broadcast_to(m + jnp.log(l), (2 * blk, LANES))
            ob[rows(start), :] = _unstack_heads(o2)
            ls[rows(start), :] = _unstack_heads(lse)
            return carry

        lax.fori_loop(0, dil * nb, block, 0)

    chunk = 512
    for c in range(s_len // chunk):
        r = pl.ds(c * chunk, chunk)
        l0, l1, l2 = ls0[r, :], ls1[r, :], ls2[r, :]
        mx = jnp.maximum(jnp.maximum(l0, l1), l2)
        e0, e1, e2 = jnp.exp(l0 - mx), jnp.exp(l1 - mx), jnp.exp(l2 - mx)
        o_ref[0, r, :] = (e0 * ob0[r, :] + e1 * ob1[r, :] + e2 * ob2[r, :]) / (e0 + e1 + e2)


def _dilated_attention(qkv, bias):
    bsz, s, _ = qkv.shape
    col = lambda off: pl.BlockSpec((1, s, LANES), lambda b, h: (b, 0, off + h))
    buf = pltpu.VMEM((s, LANES), F32)
    return pl.pallas_call(
        functools.partial(_dil_kernel, s_len=s),
        grid=(bsz, HEAD_PAIRS),
        in_specs=[col(0), col(HEAD_PAIRS), col(2 * HEAD_PAIRS),
                  pl.BlockSpec((len(DILATIONS), 1, 2 * DIL_BLOCK, 2 * DIL_BLOCK), lambda b, h: (0, h, 0, 0))],
        out_specs=pl.BlockSpec((1, s, LANES), lambda b, h: (b, 0, h)),
        out_shape=jax.ShapeDtypeStruct((bsz, s, WIDTH), F32),
        scratch_shapes=[buf] * 9,
        compiler_params=_params(("parallel", "arbitrary")),
        name="dilated_attention",
    )(qkv, qkv, qkv, bias)


def _to_row_tiles(dst_ref, x):
    rows = x.shape[0]
    for c in range(x.shape[1] // LANES):
        dst_ref[pl.ds(c, rows, stride=SUBLANES), :] = x[:, c * LANES:(c + 1) * LANES]


def _from_row_tiles(src_ref, rows, base=0, stride=SUBLANES):
    return [src_ref[pl.ds(base + c, rows, stride=stride), :] for c in range(SUBLANES)]


def _outproj_kernel(yf_ref, yd_ref, x_ref, g1_ref, sc_ref, sh_ref, g2_ref, gf_ref, gd_ref, wo1_ref, wo2_ref,
                    wrh_ref, wrl_ref, wsg_ref, wsu_ref, wsd_ref, xp_ref, h3_ref, lg_ref):
    nf = (_rms(yf_ref[0]) * gf_ref[...]).astype(BF16)
    nd = (_rms(yd_ref[0]) * gd_ref[...]).astype(BF16)
    mix = (jnp.dot(nf, wo1_ref[...], preferred_element_type=F32)
           + jnp.dot(nd, wo2_ref[...], preferred_element_type=F32))
    x1 = x_ref[0] + g1_ref[0] * mix
    h2 = _rms(x1) * (1.0 + sc_ref[0]) + sh_ref[0]
    hb = h2.astype(BF16)
    hl = (h2 - hb.astype(F32)).astype(BF16)
    lg_ref[...] = (jnp.dot(hb, wrh_ref[...], preferred_element_type=F32)
                   + jnp.dot(hl, wrh_ref[...], preferred_element_type=F32)
                   + jnp.dot(hb, wrl_ref[...], preferred_element_type=F32))
    act = _silu(jnp.dot(hb, wsg_ref[...], preferred_element_type=F32)) * jnp.dot(
        hb, wsu_ref[...], preferred_element_type=F32)
    shared = jnp.dot(act.astype(BF16), wsd_ref[...], preferred_element_type=F32)
    xp_ref[0] = x1 + g2_ref[0] * shared
    _to_row_tiles(h3_ref, h2)


def _out_projection(y_fox, y_dil, x, gate1, scale2, shift2, gate2, g_fox, g_dil, wo1, wo2, wr_hi, wr_lo,
                    wsg, wsu, wsd):
    bsz, s, d = x.shape
    tm = ROW_TILE
    nt = s // tm
    vec = pl.BlockSpec((1, 1, d), lambda b, i: (b, 0, 0))
    full = lambda w: pl.BlockSpec(w.shape, lambda b, i: (0,) * w.ndim)
    row = lambda n: pl.BlockSpec((1, tm, n), lambda b, i: (b, i, 0))
    return pl.pallas_call(
        _outproj_kernel,
        grid=(bsz, nt),
        in_specs=[row(WIDTH), row(WIDTH), row(d), vec, vec, vec, vec, full(g_fox), full(g_dil), full(wo1),
                  full(wo2), full(wr_hi), full(wr_lo), full(wsg), full(wsu), full(wsd)],
        out_specs=[row(d),
                   pl.BlockSpec((tm * SUBLANES, LANES), lambda b, i: (b * nt + i, 0)),
                   pl.BlockSpec((tm, N_EXPERTS), lambda b, i: (b * nt + i, 0))],
        out_shape=[jax.ShapeDtypeStruct((bsz, s, d), F32),
                   jax.ShapeDtypeStruct((bsz * s * SUBLANES, LANES), F32),
                   jax.ShapeDtypeStruct((bsz * s, N_EXPERTS), F32)],
        compiler_params=_params(("parallel", "arbitrary")),
        name="outproj_norm2_router_shared",
    )(y_fox, y_dil, x, gate1, scale2, shift2, gate2, g_fox, g_dil, wo1, wo2, wr_hi, wr_lo, wsg, wsu, wsd)


def _route(logits, router_bias):
    n = logits.shape[0]
    scores = jax.nn.sigmoid(logits)
    sel = scores + router_bias.astype(F32)
    grp = lax.top_k(sel.reshape(n, N_GROUPS, N_EXPERTS // N_GROUPS), 2)[0].sum(-1)
    _, gidx = lax.top_k(grp, TOP_K_GROUPS)
    gmask = jax.nn.one_hot(gidx, N_GROUPS, dtype=F32).sum(1) > 0
    emask = jnp.repeat(gmask, N_EXPERTS // N_GROUPS, axis=1)
    _, eidx = lax.top_k(jnp.where(emask, sel, -jnp.inf), TOP_K)
    wts = jnp.take_along_axis(scores, eidx, axis=1)
    wts = wts / jnp.sum(wts, axis=-1, keepdims=True) * ROUTED_SCALE
    return eidx.astype(jnp.int32), wts


def _dispatch(eidx, wts):
    n = eidx.shape[0]
    nk = n * TOP_K
    flat_e = eidx.reshape(nk)
    order = jnp.argsort(flat_e)
    se = flat_e[order]
    counts = jnp.bincount(flat_e, length=N_EXPERTS).astype(jnp.int32)
    padded = (counts + FFN_BLOCK - 1) // FFN_BLOCK * FFN_BLOCK
    start = jnp.cumsum(counts) - counts
    pend = jnp.cumsum(padded)
    pstart = pend - padded
    dest_sorted = (pstart[se] + (jnp.arange(nk, dtype=jnp.int32) - start[se])).astype(jnp.int32)
    rows = nk + N_EXPERTS * FFN_BLOCK
    nblk = rows // FFN_BLOCK
    row_tok = jnp.zeros((rows,), jnp.int32).at[dest_sorted].set((order // TOP_K).astype(jnp.int32))
    row_w = jnp.zeros((rows,), F32).at[dest_sorted].set(wts.reshape(nk)[order])
    dest = jnp.zeros((nk,), jnp.int32).at[order].set(dest_sorted)
    block_e = jnp.minimum(jnp.searchsorted(pend, jnp.arange(nblk, dtype=jnp.int32) * FFN_BLOCK, side='right'),
                          N_EXPERTS - 1).astype(jnp.int32)
    nused = (pend[-1] // FFN_BLOCK).astype(jnp.int32).reshape(1)
    return row_tok, row_w, dest, block_e, nused


def _tile_copy(src_hbm, src_row, dst_vmem, dst_row, sem):
    return pltpu.make_async_copy(src_hbm.at[pl.ds(pl.multiple_of(src_row * SUBLANES, SUBLANES), SUBLANES), :],
                                 dst_vmem.at[pl.ds(pl.multiple_of(dst_row * SUBLANES, SUBLANES), SUBLANES), :],
                                 sem)


def _gather_start(idx_ref, n, src_hbm, dst_vmem, sem):
    def body(r, carry):
        _tile_copy(src_hbm, idx_ref[0, 0, r], dst_vmem, r, sem).start()
        return carry
    lax.fori_loop(0, n, body, 0)


def _gather_wait(src_hbm, dst_vmem, sem):
    pltpu.make_async_copy(src_hbm.at[pl.ds(0, dst_vmem.shape[0]), :], dst_vmem, sem).wait()


def _ffn_kernel(be_ref, nu_ref, tok_ref, tokn_ref, w_ref, h3_hbm, wg_ref, wu_ref, wd_ref, y_ref,
                xb0, xb1, sem, wgb, wub, wdb):
    i = pl.program_id(0)
    nused = nu_ref[0]
    bufs = (xb0, xb1)

    @pl.when(i == 0)
    def _():
        _gather_start(tok_ref, FFN_BLOCK, h3_hbm, xb0, sem.at[0])

    for slot in range(2):
        @pl.when((i + 1 < nused) & ((i + 1) % 2 == slot))
        def _(slot=slot):
            _gather_start(tokn_ref, FFN_BLOCK, h3_hbm, bufs[slot], sem.at[slot])

    @pl.when((i == 0) | (be_ref[i] != be_ref[jnp.maximum(i - 1, 0)]))
    def _():
        wgb[...] = wg_ref[0].astype(BF16)
        wub[...] = wu_ref[0].astype(BF16)
        wdb[...] = wd_ref[0].astype(BF16)

    for slot in range(2):
        @pl.when((i < nused) & (i % 2 == slot))
        def _(slot=slot):
            _gather_wait(h3_hbm, bufs[slot], sem.at[slot])
            x = jnp.concatenate([c.astype(BF16) for c in _from_row_tiles(bufs[slot], FFN_BLOCK)], axis=1)
            act = _silu(jnp.dot(x, wgb[...], preferred_element_type=F32)) * jnp.dot(
                x, wub[...], preferred_element_type=F32)
            y = jnp.dot(act.astype(BF16), wdb[...], preferred_element_type=F32) * w_ref[...]
            _to_row_tiles(y_ref, y)

    @pl.when(i >= nused)
    def _():
        y_ref[...] = jnp.zeros(y_ref.shape, F32)


def _routed_experts(h3, row_tok, row_w, block_e, nused, w_gate, w_up, w_down):
    rows = row_tok.shape[0]
    nblk = rows // FFN_BLOCK
    _, d, hid = w_gate.shape
    tok3 = row_tok.reshape(nblk, 1, FFN_BLOCK)
    last = lambda i, be, nu: jnp.minimum(i, nu[0] - 1)
    grid_spec = pltpu.PrefetchScalarGridSpec(
        num_scalar_prefetch=2,
        grid=(nblk,),
        in_specs=[pl.BlockSpec((1, 1, FFN_BLOCK), lambda i, be, nu: (last(i, be, nu), 0, 0),
                               memory_space=pltpu.SMEM),
                  pl.BlockSpec((1, 1, FFN_BLOCK), lambda i, be, nu: (last(i + 1, be, nu), 0, 0),
                               memory_space=pltpu.SMEM),
                  pl.BlockSpec((FFN_BLOCK, 1), lambda i, be, nu: (last(i, be, nu), 0)),
                  pl.BlockSpec(memory_space=pl.ANY),
                  pl.BlockSpec((1, d, hid), lambda i, be, nu: (be[i], 0, 0)),
                  pl.BlockSpec((1, d, hid), lambda i, be, nu: (be[i], 0, 0)),
                  pl.BlockSpec((1, hid, d), lambda i, be, nu: (be[i], 0, 0))],
        out_specs=pl.BlockSpec((FFN_BLOCK * SUBLANES, LANES), lambda i, be, nu: (i, 0)),
        scratch_shapes=[pltpu.VMEM((FFN_BLOCK * SUBLANES, LANES), F32),
                        pltpu.VMEM((FFN_BLOCK * SUBLANES, LANES), F32),
                        pltpu.SemaphoreType.DMA((2,)),
                        pltpu.VMEM((d, hid), BF16), pltpu.VMEM((d, hid), BF16), pltpu.VMEM((hid, d), BF16)],
    )
    return pl.pallas_call(
        _ffn_kernel,
        grid_spec=grid_spec,
        out_shape=jax.ShapeDtypeStruct((rows * SUBLANES, LANES), F32),
        compiler_params=_params(("arbitrary",)),
        name="routed_experts",
    )(block_e, nused, tok3, tok3, row_w.reshape(rows, 1), h3, w_gate, w_up, w_down)


def _combine_kernel(dst_ref, dstn_ref, xp_ref, g2_ref, gfin_ref, ys_hbm, o_ref, yb0, yb1, sem, *, nsteps):
    i = pl.program_id(0)
    tm = COMBINE_TILE
    bufs = (yb0, yb1)

    @pl.when(i == 0)
    def _():
        _gather_start(dst_ref, tm * TOP_K, ys_hbm, yb0, sem.at[0])

    for slot in range(2):
        @pl.when((i + 1 < nsteps) & ((i + 1) % 2 == slot))
        def _(slot=slot):
            _gather_start(dstn_ref, tm * TOP_K, ys_hbm, bufs[slot], sem.at[slot])

    for slot in range(2):
        @pl.when(i % 2 == slot)
        def _(slot=slot):
            _gather_wait(ys_hbm, bufs[slot], sem.at[slot])
            cols = None
            for k in range(TOP_K):
                part = _from_row_tiles(bufs[slot], tm, base=k * SUBLANES, stride=TOP_K * SUBLANES)
                cols = part if cols is None else [a + b for a, b in zip(cols, part)]
            routed = jnp.concatenate(cols, axis=1)
            o_ref[0] = _rms(xp_ref[0] + g2_ref[0] * routed) * gfin_ref[...]


def _combine(ys, dest, xp, gate2, g_final):
    bsz, s, d = xp.shape
    tm = COMBINE_TILE
    nt = s // tm
    nsteps = bsz * nt
    dst3 = dest.reshape(nsteps, 1, tm * TOP_K)
    vec = pl.BlockSpec((1, 1, d), lambda i: (i // nt, 0, 0))
    return pl.pallas_call(
        functools.partial(_combine_kernel, nsteps=nsteps),
        grid=(nsteps,),
        in_specs=[pl.BlockSpec((1, 1, tm * TOP_K), lambda i: (i, 0, 0), memory_space=pltpu.SMEM),
                  pl.BlockSpec((1, 1, tm * TOP_K), lambda i: (jnp.minimum(i + 1, nsteps - 1), 0, 0),
                               memory_space=pltpu.SMEM),
                  pl.BlockSpec((1, tm, d), lambda i: (i // nt, i % nt, 0)),
                  vec,
                  pl.BlockSpec((1, d), lambda i: (0, 0)),
                  pl.BlockSpec(memory_space=pl.ANY)],
        out_specs=pl.BlockSpec((1, tm, d), lambda i: (i // nt, i % nt, 0)),
        out_shape=jax.ShapeDtypeStruct((bsz, s, d), F32),
        scratch_shapes=[pltpu.VMEM((tm * TOP_K * SUBLANES, LANES), F32),
                        pltpu.VMEM((tm * TOP_K * SUBLANES, LANES), F32),
                        pltpu.SemaphoreType.DMA((2,))],
        compiler_params=_params(("arbitrary",)),
        name="combine_final_norm",
    )(dst3, dst3, xp, gate2, g_final.reshape(1, d), ys)


def kernel(x, c, w_in, b_forget, g_fox_out, g_dil_out, w_out, w_ada, b_ada, w_router, router_bias,
           w_exp_gate, w_exp_up, w_exp_down, w_sh_gate, w_sh_up, w_sh_down, rel_bias, g_final):
    bsz, s, d = x.shape
    depth = w_in.shape[0]
    assert depth == 1 and d == SUBLANES * LANES and s % (DIL_BLOCK * DILATIONS[-1]) == 0
    l = 0
    mod = _modulation(c, w_ada[l], b_ada[l])
    shift1, scale1, gate1, shift2, scale2, gate2 = [m[:, None, :] for m in jnp.split(mod, 6, axis=-1)]

    qscale = HEAD_DIM ** -0.5
    o3 = 3 * WIDTH
    w = w_in[l]
    w_fox = jnp.concatenate([w[:, :WIDTH] * qscale, w[:, WIDTH:o3]], axis=1).astype(BF16)
    w_flog = jnp.pad(w[:, o3:o3 + N_HEADS], ((0, 0), (0, LANES - N_HEADS))).astype(BF16)
    wd0 = o3 + N_HEADS
    w_dil = jnp.concatenate([w[:, wd0:wd0 + WIDTH] * qscale, w[:, wd0 + WIDTH:]], axis=1).astype(BF16)

    qkv_f, qkv_d, flog = _in_projection(x, scale1, shift1, w_fox, w_dil, w_flog)
    f_cum = _forget_cumsum(flog[:, :, :N_HEADS].transpose(0, 2, 1), b_forget[l])
    y_fox = _fox_attention(qkv_f, f_cum)
    y_dil = _dilated_attention(qkv_d, _dilated_bias(rel_bias))

    wr = w_router[l]
    wr_hi = wr.astype(BF16)
    wr_lo = (wr - wr_hi.astype(F32)).astype(BF16)
    wo = w_out[l].astype(BF16)
    xp, h3, logits = _out_projection(
        y_fox, y_dil, x, gate1, scale2, shift2, gate2, g_fox_out[l].reshape(1, WIDTH),
        g_dil_out[l].reshape(1, WIDTH), wo[:WIDTH], wo[WIDTH:], wr_hi, wr_lo,
        w_sh_gate[l].astype(BF16), w_sh_up[l].astype(BF16), w_sh_down[l].astype(BF16))

    eidx, wts = _route(logits, router_bias[l])
    row_tok, row_w, dest, block_e, nused = _dispatch(eidx, wts)
    ys = _routed_experts(h3, row_tok, row_w, block_e, nused, w_exp_gate[l], w_exp_up[l], w_exp_down[l])
    return _combine(ys, dest, xp, gate2, g_final)
```

```python
import functools

import numpy as np
import jax
import jax.numpy as jnp
from jax import lax
from jax.experimental import pallas as pl
from jax.experimental.pallas import tpu as pltpu

F32 = jnp.float32
BF16 = jnp.bfloat16

HEAD_DIM = 64
N_HEADS = 8
WIDTH = N_HEADS * HEAD_DIM
LANES = 128
SUBLANES = 8
HEAD_PAIRS = WIDTH // LANES
DIL_BLOCK = 128
DILATIONS = (1, 4, 16)
T5_NUM_BUCKETS = 32
T5_MAX_DISTANCE = 2048
N_EXPERTS = 256
TOP_K = 8
N_GROUPS = 8
TOP_K_GROUPS = 4
ROUTED_SCALE = 2.5
EPS = 1e-6
NEG = -1e30
VMEM_LIMIT = 56 * 1024 * 1024

ROW_TILE = 512
FOX_TILE = 256
FFN_BLOCK = 256
COMBINE_TILE = 128
SCATTER_TILE = 128
ROUTE_TILE = 512


def _params(semantics):
    return pltpu.CompilerParams(dimension_semantics=semantics, vmem_limit_bytes=VMEM_LIMIT)


def _rms(x):
    return x * lax.rsqrt(jnp.mean(x * x, axis=-1, keepdims=True) + EPS)


def _silu(x):
    return x * jax.nn.sigmoid(x)


def _dot_nt(a, b):
    return lax.dot_general(a, b, (((1,), (1,)), ((), ())), preferred_element_type=F32)


def _mod_kernel(c_ref, w_ref, b_ref, o_ref):
    o_ref[...] = jnp.dot(_silu(c_ref[...]), w_ref[...], precision=lax.Precision.HIGHEST,
                         preferred_element_type=F32) + b_ref[...]


def _modulation(c, w_ada, b_ada):
    bsz, d = c.shape
    n = w_ada.shape[1]
    tn = 1536
    return pl.pallas_call(
        _mod_kernel,
        grid=(n // tn,),
        in_specs=[pl.BlockSpec((bsz, d), lambda j: (0, 0)),
                  pl.BlockSpec((d, tn), lambda j: (0, j)),
                  pl.BlockSpec((1, tn), lambda j: (0, j))],
        out_specs=pl.BlockSpec((bsz, tn), lambda j: (0, j)),
        out_shape=jax.ShapeDtypeStruct((bsz, n), F32),
        compiler_params=_params(("arbitrary",)),
        name="adaln_mod",
    )(c, w_ada, b_ada.reshape(1, n))


def _inproj_kernel(x_ref, sc_ref, sh_ref, wf_ref, wd_ref, wl_ref, of_ref, od_ref, ol_ref):
    h = _rms(x_ref[0]) * (1.0 + sc_ref[0]) + sh_ref[0]
    hb = h.astype(BF16)
    of_ref[0] = jnp.dot(hb, wf_ref[...], preferred_element_type=F32).astype(BF16)
    od_ref[0] = jnp.dot(hb, wd_ref[...], preferred_element_type=F32).astype(BF16)
    ol_ref[0] = jnp.dot(hb, wl_ref[...], preferred_element_type=F32)


def _in_projection(x, scale1, shift1, w_fox, w_dil, w_flog):
    bsz, s, d = x.shape
    tm = ROW_TILE
    vec = pl.BlockSpec((1, 1, d), lambda b, i: (b, 0, 0))
    full = lambda w: pl.BlockSpec(w.shape, lambda b, i: (0, 0))
    row = lambda n: pl.BlockSpec((1, tm, n), lambda b, i: (b, i, 0))
    return pl.pallas_call(
        _inproj_kernel,
        grid=(bsz, s // tm),
        in_specs=[row(d), vec, vec, full(w_fox), full(w_dil), full(w_flog)],
        out_specs=[row(3 * WIDTH), row(3 * WIDTH), row(LANES)],
        out_shape=[jax.ShapeDtypeStruct((bsz, s, 3 * WIDTH), BF16),
                   jax.ShapeDtypeStruct((bsz, s, 3 * WIDTH), BF16),
                   jax.ShapeDtypeStruct((bsz, s, LANES), F32)],
        compiler_params=_params(("parallel", "arbitrary")),
        name="norm1_inproj",
    )(x, scale1, shift1, w_fox, w_dil, w_flog)


def _forget_kernel(f_ref, b_ref, o_ref):
    z = f_ref[0] + b_ref[...]
    x = jnp.minimum(z, 0.0) - jnp.log1p(jnp.exp(-jnp.abs(z)))
    lane = lax.broadcasted_iota(jnp.int32, x.shape, 1)
    shift = 1
    while shift < x.shape[1]:
        x = x + jnp.where(lane >= shift, pltpu.roll(x, shift, axis=1), 0.0)
        shift *= 2
    o_ref[0] = x


def _forget_cumsum(f_t, b_forget):
    bsz, h, s = f_t.shape
    return pl.pallas_call(
        _forget_kernel,
        grid=(bsz,),
        in_specs=[pl.BlockSpec((1, h, s), lambda b: (b, 0, 0)),
                  pl.BlockSpec((h, 1), lambda b: (0, 0))],
        out_specs=pl.BlockSpec((1, h, s), lambda b: (b, 0, 0)),
        out_shape=jax.ShapeDtypeStruct((bsz, h, s), F32),
        compiler_params=_params(("parallel",)),
        name="forget_cumsum",
    )(f_t, b_forget.reshape(h, 1))


def _stack_heads(q):
    lane = lax.broadcasted_iota(jnp.int32, q.shape, 1)
    zero = jnp.zeros_like(q)
    return jnp.concatenate([jnp.where(lane < HEAD_DIM, q, zero), jnp.where(lane >= HEAD_DIM, q, zero)], axis=0)


def _unstack_heads(o2):
    rows = o2.shape[0] // 2
    lane = lax.broadcasted_iota(jnp.int32, (rows, LANES), 1)
    return jnp.where(lane < HEAD_DIM, o2[:rows], o2[rows:])


def _fox_kernel(q_ref, k_ref, v_ref, f_ref, o_ref, m_s, l_s, acc_s, *, t):
    i = pl.program_id(2)
    q2 = _stack_heads(q_ref[0])
    m_s[...] = jnp.full(m_s.shape, NEG, F32)
    l_s[...] = jnp.zeros(l_s.shape, F32)
    acc_s[...] = jnp.zeros(acc_s.shape, F32)

    def step(j, masked):
        ks = pl.multiple_of(j * t, t)
        k = k_ref[0, pl.ds(ks, t), :]
        v = v_ref[0, pl.ds(ks, t), :]
        s = lax.dot_general(q2, k, (((1,), (1,)), ((), ())), preferred_element_type=F32)
        f = f_ref[0, 0, :, pl.ds(ks, t)]
        s = s - jnp.concatenate([jnp.broadcast_to(f[0:1], (t, t)), jnp.broadcast_to(f[1:2], (t, t))], axis=0)
        if masked:
            row = lax.broadcasted_iota(jnp.int32, (t, t), 0)
            col = lax.broadcasted_iota(jnp.int32, (t, t), 1)
            keep = jnp.concatenate([col <= row, col <= row], axis=0)
            s = jnp.where(keep, s, NEG)
        m_prev = m_s[...]
        m_new = jnp.maximum(m_prev, jnp.max(s, axis=1, keepdims=True))
        alpha = jnp.exp(m_prev - m_new)
        p = jnp.exp(s - m_new)
        l_s[...] = alpha * l_s[...] + jnp.sum(p, axis=1, keepdims=True)
        acc_s[...] = alpha * acc_s[...] + jnp.dot(p.astype(BF16), v, preferred_element_type=F32)
        m_s[...] = m_new

    def body(j, carry):
        step(j, False)
        return carry

    lax.fori_loop(0, i, body, 0)
    step(i, True)
    o_ref[0] = _unstack_heads(acc_s[...] / l_s[...])


def _fox_attention(qkv, f_cum):
    bsz, s, _ = qkv.shape
    t = FOX_TILE
    f4 = f_cum.reshape(bsz, HEAD_PAIRS, 2, s)
    return pl.pallas_call(
        functools.partial(_fox_kernel, t=t),
        grid=(bsz, HEAD_PAIRS, s // t),
        in_specs=[pl.BlockSpec((1, t, LANES), lambda b, h, i: (b, i, h)),
                  pl.BlockSpec((1, s, LANES), lambda b, h, i: (b, 0, HEAD_PAIRS + h)),
                  pl.BlockSpec((1, s, LANES), lambda b, h, i: (b, 0, 2 * HEAD_PAIRS + h)),
                  pl.BlockSpec((1, 1, 2, s), lambda b, h, i: (b, h, 0, 0))],
        out_specs=pl.BlockSpec((1, t, LANES), lambda b, h, i: (b, i, h)),
        out_shape=jax.ShapeDtypeStruct((bsz, s, WIDTH), F32),
        scratch_shapes=[pltpu.VMEM((2 * t, 1), F32), pltpu.VMEM((2 * t, 1), F32),
                        pltpu.VMEM((2 * t, LANES), F32)],
        compiler_params=_params(("parallel", "parallel", "arbitrary")),
        name="fox_attention",
    )(qkv, qkv, qkv, f4)


def _t5_bucket(dist):
    max_exact = T5_NUM_BUCKETS // 2
    d = np.maximum(dist, 1).astype(np.float32)
    large = max_exact + (np.log(d / max_exact) / np.log(T5_MAX_DISTANCE / max_exact)
                         * (T5_NUM_BUCKETS - max_exact)).astype(np.int32)
    large = np.minimum(large, T5_NUM_BUCKETS - 1)
    return np.where(dist < max_exact, dist, large).astype(np.int32)


def _dilated_bias(rel_bias):
    blk = DIL_BLOCK
    rel = np.arange(blk)[:, None] + blk - np.arange(2 * blk)[None, :]
    band = (rel >= 0) & (rel <= blk)
    tables = []
    for dil in DILATIONS:
        bucket = _t5_bucket(np.clip(rel, 0, blk) * dil)
        bias = rel_bias[bucket].astype(F32).transpose(2, 0, 1)
        tables.append(jnp.where(band[None], bias, NEG))
    return jnp.stack(tables).reshape(len(DILATIONS), HEAD_PAIRS, 2 * blk, 2 * blk)


def _dil_kernel(q_ref, k_ref, v_ref, bias_ref, o_ref, qf, kf, vf, ob0, ob1, ob2, ls0, ls1, ls2, *, s_len):
    blk = DIL_BLOCK
    qf[...] = q_ref[0].astype(F32)
    kf[...] = k_ref[0].astype(F32)
    vf[...] = v_ref[0].astype(F32)
    col = lax.broadcasted_iota(jnp.int32, (2 * blk, 2 * blk), 1)

    for bi, (dil, ob, ls) in enumerate(zip(DILATIONS, (ob0, ob1, ob2), (ls0, ls1, ls2))):
        span = blk * dil
        nb = s_len // span

        def rows(start, dil=dil):
            return pl.ds(start, blk) if dil == 1 else pl.ds(start, blk, stride=dil)

        def block(tix, carry, bi=bi, span=span, nb=nb, ob=ob, ls=ls, rows=rows):
            n = tix % nb
            start = n * span + tix // nb
            prev = jnp.maximum(start - span, 0)
            q2 = _stack_heads(qf[rows(start), :].astype(BF16))
            kk = jnp.concatenate([kf[rows(prev), :], kf[rows(start), :]], axis=0).astype(BF16)
            vv = jnp.concatenate([vf[rows(prev), :], vf[rows(start), :]], axis=0).astype(BF16)
            s = lax.dot_general(q2, kk, (((1,), (1,)), ((), ())), preferred_element_type=F32)
            s = s + bias_ref[bi, 0]
            s = jnp.where((col < blk) & (n == 0), NEG, s)
            m = jnp.max(s, axis=1, keepdims=True)
            p = jnp.exp(s - m)
            l = jnp.sum(p, axis=1, keepdims=True)
            o2 = jnp.dot(p.astype(BF16), vv, preferred_element_type=F32) / l
            lse = jnp.broadcast_to(m + jnp.log(l), (2 * blk, LANES))
            ob[rows(start), :] = _unstack_heads(o2)
            ls[rows(start), :] = _unstack_heads(lse)
            return carry

        lax.fori_loop(0, dil * nb, block, 0)

    chunk = 512
    for c in range(s_len // chunk):
        r = pl.ds(c * chunk, chunk)
        l0, l1, l2 = ls0[r, :], ls1[r, :], ls2[r, :]
        mx = jnp.maximum(jnp.maximum(l0, l1), l2)
        e0, e1, e2 = jnp.exp(l0 - mx), jnp.exp(l1 - mx), jnp.exp(l2 - mx)
        o_ref[0, r, :] = (e0 * ob0[r, :] + e1 * ob1[r, :] + e2 * ob2[r, :]) / (e0 + e1 + e2)


def _dilated_attention(qkv, bias):
    bsz, s, _ = qkv.shape
    col = lambda off: pl.BlockSpec((1, s, LANES), lambda b, h: (b, 0, off + h))
    buf = pltpu.VMEM((s, LANES), F32)
    return pl.pallas_call(
        functools.partial(_dil_kernel, s_len=s),
        grid=(bsz, HEAD_PAIRS),
        in_specs=[col(0), col(HEAD_PAIRS), col(2 * HEAD_PAIRS),
                  pl.BlockSpec((len(DILATIONS), 1, 2 * DIL_BLOCK, 2 * DIL_BLOCK), lambda b, h: (0, h, 0, 0))],
        out_specs=pl.BlockSpec((1, s, LANES), lambda b, h: (b, 0, h)),
        out_shape=jax.ShapeDtypeStruct((bsz, s, WIDTH), F32),
        scratch_shapes=[buf] * 9,
        compiler_params=_params(("parallel", "arbitrary")),
        name="dilated_attention",
    )(qkv, qkv, qkv, bias)


def _to_row_tiles(dst_ref, x):
    rows = x.shape[0]
    for c in range(x.shape[1] // LANES):
        dst_ref[pl.ds(c, rows, stride=SUBLANES), :] = x[:, c * LANES:(c + 1) * LANES]


def _from_row_tiles(src_ref, rows, base=0, stride=SUBLANES):
    return [src_ref[pl.ds(base + c, rows, stride=stride), :] for c in range(SUBLANES)]


def _outproj_kernel(yf_ref, yd_ref, x_ref, g1_ref, sc_ref, sh_ref, g2_ref, gf_ref, gd_ref, wo1_ref, wo2_ref,
                    wrh_ref, wrl_ref, wsg_ref, wsu_ref, wsd_ref, xp_ref, h3_ref, lg_ref):
    nf = (_rms(yf_ref[0]) * gf_ref[...]).astype(BF16)
    nd = (_rms(yd_ref[0]) * gd_ref[...]).astype(BF16)
    mix = (jnp.dot(nf, wo1_ref[...], preferred_element_type=F32)
           + jnp.dot(nd, wo2_ref[...], preferred_element_type=F32))
    x1 = x_ref[0] + g1_ref[0] * mix
    h2 = _rms(x1) * (1.0 + sc_ref[0]) + sh_ref[0]
    hb = h2.astype(BF16)
    hl = (h2 - hb.astype(F32)).astype(BF16)
    lg_ref[...] = _dot_nt(wrh_ref[...], hb) + _dot_nt(wrh_ref[...], hl) + _dot_nt(wrl_ref[...], hb)
    act = _silu(jnp.dot(hb, wsg_ref[...], preferred_element_type=F32)) * jnp.dot(
        hb, wsu_ref[...], preferred_element_type=F32)
    shared = jnp.dot(act.astype(BF16), wsd_ref[...], preferred_element_type=F32)
    xp_ref[0] = x1 + g2_ref[0] * shared
    _to_row_tiles(h3_ref, h2)


def _out_projection(y_fox, y_dil, x, gate1, scale2, shift2, gate2, g_fox, g_dil, wo1, wo2, wr_hi, wr_lo,
                    wsg, wsu, wsd):
    bsz, s, d = x.shape
    tm = ROW_TILE
    nt = s // tm
    vec = pl.BlockSpec((1, 1, d), lambda b, i: (b, 0, 0))
    full = lambda w: pl.BlockSpec(w.shape, lambda b, i: (0,) * w.ndim)
    row = lambda n: pl.BlockSpec((1, tm, n), lambda b, i: (b, i, 0))
    return pl.pallas_call(
        _outproj_kernel,
        grid=(bsz, nt),
        in_specs=[row(WIDTH), row(WIDTH), row(d), vec, vec, vec, vec, full(g_fox), full(g_dil), full(wo1),
                  full(wo2), full(wr_hi), full(wr_lo), full(wsg), full(wsu), full(wsd)],
        out_specs=[row(d),
                   pl.BlockSpec((tm * SUBLANES, LANES), lambda b, i: (b * nt + i, 0)),
                   pl.BlockSpec((N_EXPERTS, tm), lambda b, i: (0, b * nt + i))],
        out_shape=[jax.ShapeDtypeStruct((bsz, s, d), F32),
                   jax.ShapeDtypeStruct((bsz * s * SUBLANES, LANES), F32),
                   jax.ShapeDtypeStruct((N_EXPERTS, bsz * s), F32)],
        compiler_params=_params(("parallel", "arbitrary")),
        name="outproj_norm2_router_shared",
    )(y_fox, y_dil, x, gate1, scale2, shift2, gate2, g_fox, g_dil, wo1, wo2, wr_hi, wr_lo, wsg, wsu, wsd)


def _first_argmax(v, row, size):
    m = jnp.max(v, axis=0, keepdims=True)
    return m, jnp.min(jnp.where(v == m, row, size), axis=0, keepdims=True)


def _route_kernel(lg_ref, rb_ref, tri_ref, e_ref, w_ref, rk_ref, cnt_ref, cnt_s):
    @pl.when(pl.program_id(0) == 0)
    def _():
        cnt_s[...] = jnp.zeros(cnt_s.shape, F32)

    t = lg_ref.shape[1]
    gsz = N_EXPERTS // N_GROUPS
    scores = jax.nn.sigmoid(lg_ref[...])
    sel = scores + rb_ref[...]
    row_g = lax.broadcasted_iota(jnp.int32, (gsz, t), 0)
    grp = []
    for g in range(N_GROUPS):
        v = sel[g * gsz:(g + 1) * gsz]
        m1, i1 = _first_argmax(v, row_g, gsz)
        m2 = jnp.max(jnp.where(row_g == i1, -jnp.inf, v), axis=0, keepdims=True)
        grp.append(m1 + m2)
    gv = jnp.concatenate(grp, axis=0)
    row8 = lax.broadcasted_iota(jnp.int32, (N_GROUPS, t), 0)
    pen = jnp.full((N_GROUPS, t), -jnp.inf, F32)
    for _ in range(TOP_K_GROUPS):
        _, ix = _first_argmax(gv, row8, N_GROUPS)
        pen = jnp.where(row8 == ix, 0.0, pen)
        gv = jnp.where(row8 == ix, -jnp.inf, gv)
    selm = jnp.concatenate([sel[g * gsz:(g + 1) * gsz] + pen[g:g + 1] for g in range(N_GROUPS)], axis=0)

    row = lax.broadcasted_iota(jnp.int32, (N_EXPERTS, t), 0)
    v = selm
    idxs, scs = [], []
    for _ in range(TOP_K):
        _, ix = _first_argmax(v, row, N_EXPERTS)
        hit = row == ix
        idxs.append(ix)
        scs.append(jnp.sum(jnp.where(hit, scores, 0.0), axis=0, keepdims=True))
        v = jnp.where(hit, -jnp.inf, v)
    chosen = jnp.where(v != selm, 1.0, 0.0)
    before = jnp.dot(chosen.astype(BF16), tri_ref[...], preferred_element_type=F32) + cnt_s[...]
    rks = [jnp.sum(jnp.where(row == ix, before, 0.0), axis=0, keepdims=True) for ix in idxs]
    sc = jnp.concatenate(scs, axis=0)
    e_ref[...] = jnp.concatenate(idxs, axis=0)
    w_ref[...] = sc / jnp.sum(sc, axis=0, keepdims=True) * ROUTED_SCALE
    rk_ref[...] = jnp.concatenate(rks, axis=0).astype(jnp.int32)
    cnt_s[...] = cnt_s[...] + jnp.sum(chosen, axis=1, keepdims=True)
    cnt_ref[...] = cnt_s[...]


def _route(logits_t, router_bias):
    e, n = logits_t.shape
    t = ROUTE_TILE
    tri = jnp.triu(jnp.ones((t, t), BF16), k=1)
    tile = pl.BlockSpec((TOP_K, t), lambda i: (0, i))
    return pl.pallas_call(
        _route_kernel,
        grid=(n // t,),
        in_specs=[pl.BlockSpec((e, t), lambda i: (0, i)),
                  pl.BlockSpec((e, 1), lambda i: (0, 0)),
                  pl.BlockSpec((t, t), lambda i: (0, 0))],
        out_specs=[tile, tile, tile, pl.BlockSpec((e, 1), lambda i: (0, 0))],
        out_shape=[jax.ShapeDtypeStruct((TOP_K, n), jnp.int32), jax.ShapeDtypeStruct((TOP_K, n), F32),
                   jax.ShapeDtypeStruct((TOP_K, n), jnp.int32), jax.ShapeDtypeStruct((e, 1), F32)],
        scratch_shapes=[pltpu.VMEM((e, 1), F32)],
        compiler_params=_params(("arbitrary",)),
        name="route_topk_rank",
    )(logits_t, router_bias.reshape(e, 1).astype(F32), tri)


def _dest_kernel(e_ref, rk_ref, ps_ref, d_ref):
    t = e_ref.shape[1]
    row = lax.broadcasted_iota(jnp.int32, (N_EXPERTS, t), 0)
    ps = ps_ref[...]
    base = [jnp.sum(jnp.where(row == e_ref[k:k + 1, :], ps, 0.0), axis=0, keepdims=True) for k in range(TOP_K)]
    d_ref[...] = jnp.concatenate(base, axis=0).astype(jnp.int32) + rk_ref[...]


def _dest_rows(eidx, rank, pstart):
    _, n = eidx.shape
    t = ROUTE_TILE
    tile = pl.BlockSpec((TOP_K, t), lambda i: (0, i))
    return pl.pallas_call(
        _dest_kernel,
        grid=(n // t,),
        in_specs=[tile, tile, pl.BlockSpec((N_EXPERTS, 1), lambda i: (0, 0))],
        out_specs=tile,
        out_shape=jax.ShapeDtypeStruct((TOP_K, n), jnp.int32),
        compiler_params=_params(("parallel",)),
        name="dest_rows",
    )(eidx, rank, pstart.astype(F32).reshape(N_EXPERTS, 1))


def _block_plan(counts, nblk):
    counts = counts.reshape(N_EXPERTS).astype(jnp.int32)
    padded = (counts + FFN_BLOCK - 1) // FFN_BLOCK * FFN_BLOCK
    pend = jnp.cumsum(padded).astype(jnp.int32)
    pstart = pend - padded
    block_e = jnp.minimum(jnp.searchsorted(pend, jnp.arange(nblk, dtype=jnp.int32) * FFN_BLOCK, side='right'),
                          N_EXPERTS - 1).astype(jnp.int32)
    nused = (pend[-1:] // FFN_BLOCK).astype(jnp.int32)
    return pstart, pstart + counts, pend, block_e, nused


def _tile_copy(src_hbm, src_row, dst_ref, dst_row, sem):
    return pltpu.make_async_copy(src_hbm.at[pl.ds(pl.multiple_of(src_row * SUBLANES, SUBLANES), SUBLANES), :],
                                 dst_ref.at[pl.ds(pl.multiple_of(dst_row * SUBLANES, SUBLANES), SUBLANES), :],
                                 sem)


def _scatter_kernel(cend_ref, pend_ref, dst_ref, h3_hbm, z_hbm, xs_hbm, sem, zsem, *, n_assign):
    i = pl.program_id(0)
    tm = dst_ref.shape[1]

    def zero_copy(r):
        return pltpu.make_async_copy(z_hbm, xs_hbm.at[pl.ds(pl.multiple_of(r * SUBLANES, SUBLANES), SUBLANES), :],
                                     zsem)

    def wait_tile():
        span = pl.ds(0, tm * TOP_K * SUBLANES)
        pltpu.make_async_copy(h3_hbm.at[span, :], xs_hbm.at[span, :], sem).wait()

    @pl.when(i == 0)
    def _():
        def per_expert(e, carry):
            def per_row(r, c):
                zero_copy(r).start()
                return c
            return lax.fori_loop(cend_ref[e], pend_ref[e], per_row, carry)
        lax.fori_loop(0, N_EXPERTS, per_expert, 0)

    @pl.when(i > 0)
    def _():
        wait_tile()

    for k in range(TOP_K):
        def body(j, carry, k=k):
            _tile_copy(h3_hbm, i * tm + j, xs_hbm, dst_ref[k, j], sem).start()
            return carry
        lax.fori_loop(0, tm, body, 0)

    @pl.when(i == pl.num_programs(0) - 1)
    def _():
        wait_tile()

        def wait_zero(r, carry):
            zero_copy(r).wait()
            return carry
        lax.fori_loop(0, pend_ref[N_EXPERTS - 1] - n_assign, wait_zero, 0)


def _dispatch_rows(h3, dest, cend, pend, rows):
    _, n = dest.shape
    tm = SCATTER_TILE
    grid_spec = pltpu.PrefetchScalarGridSpec(
        num_scalar_prefetch=2,
        grid=(n // tm,),
        in_specs=[pl.BlockSpec((TOP_K, tm), lambda i, ce, pe: (0, i), memory_space=pltpu.SMEM),
                  pl.BlockSpec(memory_space=pl.ANY),
                  pl.BlockSpec(memory_space=pl.ANY)],
        out_specs=pl.BlockSpec(memory_space=pl.ANY),
        scratch_shapes=[pltpu.SemaphoreType.DMA, pltpu.SemaphoreType.DMA],
    )
    return pl.pallas_call(
        functools.partial(_scatter_kernel, n_assign=n * TOP_K),
        grid_spec=grid_spec,
        out_shape=jax.ShapeDtypeStruct((rows * SUBLANES, LANES), F32),
        compiler_params=_params(("arbitrary",)),
        name="dispatch_scatter",
    )(cend, pend, dest, h3, jnp.zeros((SUBLANES, LANES), F32))


def _ffn_kernel(be_ref, nu_ref, x_ref, wg_ref, wu_ref, wd_ref, y_ref, wgb, wub, wdb):
    i = pl.program_id(0)
    nused = nu_ref[0]

    @pl.when((i == 0) | (be_ref[i] != be_ref[jnp.maximum(i - 1, 0)]))
    def _():
        wgb[...] = wg_ref[0].astype(BF16)
        wub[...] = wu_ref[0].astype(BF16)
        wdb[...] = wd_ref[0].astype(BF16)

    @pl.when(i < nused)
    def _():
        x = jnp.concatenate([c.astype(BF16) for c in _from_row_tiles(x_ref, FFN_BLOCK)], axis=1)
        act = _silu(jnp.dot(x, wgb[...], preferred_element_type=F32)) * jnp.dot(
            x, wub[...], preferred_element_type=F32)
        _to_row_tiles(y_ref, jnp.dot(act.astype(BF16), wdb[...], preferred_element_type=F32))

    @pl.when(i >= nused)
    def _():
        y_ref[...] = jnp.zeros(y_ref.shape, F32)


def _routed_experts(xs, block_e, nused, w_gate, w_up, w_down):
    rows = xs.shape[0] // SUBLANES
    nblk = rows // FFN_BLOCK
    _, d, hid = w_gate.shape
    grid_spec = pltpu.PrefetchScalarGridSpec(
        num_scalar_prefetch=2,
        grid=(nblk,),
        in_specs=[pl.BlockSpec((FFN_BLOCK * SUBLANES, LANES), lambda i, be, nu: (jnp.minimum(i, nu[0] - 1), 0)),
                  pl.BlockSpec((1, d, hid), lambda i, be, nu: (be[i], 0, 0)),
                  pl.BlockSpec((1, d, hid), lambda i, be, nu: (be[i], 0, 0)),
                  pl.BlockSpec((1, hid, d), lambda i, be, nu: (be[i], 0, 0))],
        out_specs=pl.BlockSpec((FFN_BLOCK * SUBLANES, LANES), lambda i, be, nu: (i, 0)),
        scratch_shapes=[pltpu.VMEM((d, hid), BF16), pltpu.VMEM((d, hid), BF16), pltpu.VMEM((hid, d), BF16)],
    )
    return pl.pallas_call(
        _ffn_kernel,
        grid_spec=grid_spec,
        out_shape=jax.ShapeDtypeStruct((rows * SUBLANES, LANES), F32),
        compiler_params=_params(("arbitrary",)),
        name="routed_experts",
    )(block_e, nused, xs, w_gate, w_up, w_down)


def _gather_start(idx_ref, src_hbm, dst_vmem, sem):
    nk, tm = idx_ref.shape
    for k in range(nk):
        def body(j, carry, k=k):
            _tile_copy(src_hbm, idx_ref[k, j], dst_vmem, k * tm + j, sem).start()
            return carry
        lax.fori_loop(0, tm, body, 0)


def _gather_wait(src_hbm, dst_vmem, sem):
    pltpu.make_async_copy(src_hbm.at[pl.ds(0, dst_vmem.shape[0]), :], dst_vmem, sem).wait()


def _combine_kernel(dst_ref, dstn_ref, wt_ref, xp_ref, g2_ref, gfin_ref, ys_hbm, o_ref, yb0, yb1, sem, *, nsteps):
    i = pl.program_id(0)
    tm = COMBINE_TILE
    bufs = (yb0, yb1)

    @pl.when(i == 0)
    def _():
        _gather_start(dst_ref, ys_hbm, yb0, sem.at[0])

    for slot in range(2):
        @pl.when((i + 1 < nsteps) & ((i + 1) % 2 == slot))
        def _(slot=slot):
            _gather_start(dstn_ref, ys_hbm, bufs[slot], sem.at[slot])

    for slot in range(2):
        @pl.when(i % 2 == slot)
        def _(slot=slot):
            _gather_wait(ys_hbm, bufs[slot], sem.at[slot])
            wt = wt_ref[...]
            cols = None
            for k in range(TOP_K):
                part = [c * wt[:, k:k + 1] for c in _from_row_tiles(bufs[slot], tm, base=k * tm * SUBLANES)]
                cols = part if cols is None else [a + b for a, b in zip(cols, part)]
            routed = jnp.concatenate(cols, axis=1)
            o_ref[0] = _rms(xp_ref[0] + g2_ref[0] * routed) * gfin_ref[...]


def _combine(ys, dest, wts_t, xp, gate2, g_final):
    bsz, s, d = xp.shape
    tm = COMBINE_TILE
    nt = s // tm
    nsteps = bsz * nt
    vec = pl.BlockSpec((1, 1, d), lambda i: (i // nt, 0, 0))
    return pl.pallas_call(
        functools.partial(_combine_kernel, nsteps=nsteps),
        grid=(nsteps,),
        in_specs=[pl.BlockSpec((TOP_K, tm), lambda i: (0, i), memory_space=pltpu.SMEM),
                  pl.BlockSpec((TOP_K, tm), lambda i: (0, jnp.minimum(i + 1, nsteps - 1)),
                               memory_space=pltpu.SMEM),
                  pl.BlockSpec((tm, TOP_K), lambda i: (i, 0)),
                  pl.BlockSpec((1, tm, d), lambda i: (i // nt, i % nt, 0)),
                  vec,
                  pl.BlockSpec((1, d), lambda i: (0, 0)),
                  pl.BlockSpec(memory_space=pl.ANY)],
        out_specs=pl.BlockSpec((1, tm, d), lambda i: (i // nt, i % nt, 0)),
        out_shape=jax.ShapeDtypeStruct((bsz, s, d), F32),
        scratch_shapes=[pltpu.VMEM((tm * TOP_K * SUBLANES, LANES), F32),
                        pltpu.VMEM((tm * TOP_K * SUBLANES, LANES), F32),
                        pltpu.SemaphoreType.DMA((2,))],
        compiler_params=_params(("arbitrary",)),
        name="combine_final_norm",
    )(dest, dest, wts_t, xp, gate2, g_final.reshape(1, d), ys)


def kernel(x, c, w_in, b_forget, g_fox_out, g_dil_out, w_out, w_ada, b_ada, w_router, router_bias,
           w_exp_gate, w_exp_up, w_exp_down, w_sh_gate, w_sh_up, w_sh_down, rel_bias, g_final):
    bsz, s, d = x.shape
    depth = w_in.shape[0]
    assert depth == 1 and d == SUBLANES * LANES and s % (DIL_BLOCK * DILATIONS[-1]) == 0
    l = 0
    mod = _modulation(c, w_ada[l], b_ada[l])
    shift1, scale1, gate1, shift2, scale2, gate2 = [m[:, None, :] for m in jnp.split(mod, 6, axis=-1)]

    qscale = HEAD_DIM ** -0.5
    o3 = 3 * WIDTH
    w = w_in[l]
    w_fox = jnp.concatenate([w[:, :WIDTH] * qscale, w[:, WIDTH:o3]], axis=1).astype(BF16)
    w_flog = jnp.pad(w[:, o3:o3 + N_HEADS], ((0, 0), (0, LANES - N_HEADS))).astype(BF16)
    wd0 = o3 + N_HEADS
    w_dil = jnp.concatenate([w[:, wd0:wd0 + WIDTH] * qscale, w[:, wd0 + WIDTH:]], axis=1).astype(BF16)

    qkv_f, qkv_d, flog = _in_projection(x, scale1, shift1, w_fox, w_dil, w_flog)
    f_cum = _forget_cumsum(flog[:, :, :N_HEADS].transpose(0, 2, 1), b_forget[l])
    y_fox = _fox_attention(qkv_f, f_cum)
    y_dil = _dilated_attention(qkv_d, _dilated_bias(rel_bias))

    wr = w_router[l].T
    wr_hi = wr.astype(BF16)
    wr_lo = (wr - wr_hi.astype(F32)).astype(BF16)
    wo = w_out[l].astype(BF16)
    xp, h3, logits_t = _out_projection(
        y_fox, y_dil, x, gate1, scale2, shift2, gate2, g_fox_out[l].reshape(1, WIDTH),
        g_dil_out[l].reshape(1, WIDTH), wo[:WIDTH], wo[WIDTH:], wr_hi, wr_lo,
        w_sh_gate[l].astype(BF16), w_sh_up[l].astype(BF16), w_sh_down[l].astype(BF16))

    eidx, wts, rank, counts = _route(logits_t, router_bias[l])
    rows = bsz * s * TOP_K + N_EXPERTS * FFN_BLOCK
    pstart, cend, pend, block_e, nused = _block_plan(counts, rows // FFN_BLOCK)
    dest = _dest_rows(eidx, rank, pstart)
    xs = _dispatch_rows(h3, dest, cend, pend, rows)
    ys = _routed_experts(xs, block_e, nused, w_exp_gate[l], w_exp_up[l], w_exp_down[l])
    return _combine(ys, dest, wts.T, xp, gate2, g_final)
```

```python
import functools

import numpy as np
import jax
import jax.numpy as jnp
from jax import lax
from jax.experimental import pallas as pl
from jax.experimental.pallas import tpu as pltpu

F32 = jnp.float32
BF16 = jnp.bfloat16

HEAD_DIM = 64
N_HEADS = 8
WIDTH = N_HEADS * HEAD_DIM
LANES = 128
SUBLANES = 8
HEAD_PAIRS = WIDTH // LANES
DIL_BLOCK = 128
DILATIONS = (1, 4, 16)
T5_NUM_BUCKETS = 32
T5_MAX_DISTANCE = 2048
N_EXPERTS = 256
TOP_K = 8
N_GROUPS = 8
TOP_K_GROUPS = 4
ROUTED_SCALE = 2.5
EPS = 1e-6
NEG = -1e30
VMEM_LIMIT = 56 * 1024 * 1024

ROW_TILE = 512
FOX_TILE = 256
FFN_BLOCK = 256
COMBINE_TILE = 128
SCATTER_TILE = 128
ROUTE_TILE = 512
ISSUE_UNROLL = 8


def _params(semantics):
    return pltpu.CompilerParams(dimension_semantics=semantics, vmem_limit_bytes=VMEM_LIMIT)


def _rms(x):
    return x * lax.rsqrt(jnp.mean(x * x, axis=-1, keepdims=True) + EPS)


def _silu(x):
    return x * jax.nn.sigmoid(x)


def _dot_nt(a, b):
    return lax.dot_general(a, b, (((1,), (1,)), ((), ())), preferred_element_type=F32)


def _mod_kernel(c_ref, w_ref, b_ref, o_ref):
    o_ref[...] = jnp.dot(_silu(c_ref[...]), w_ref[...], precision=lax.Precision.HIGHEST,
                         preferred_element_type=F32) + b_ref[...]


def _modulation(c, w_ada, b_ada):
    bsz, d = c.shape
    n = w_ada.shape[1]
    tn = 1536
    return pl.pallas_call(
        _mod_kernel,
        grid=(n // tn,),
        in_specs=[pl.BlockSpec((bsz, d), lambda j: (0, 0)),
                  pl.BlockSpec((d, tn), lambda j: (0, j)),
                  pl.BlockSpec((1, tn), lambda j: (0, j))],
        out_specs=pl.BlockSpec((bsz, tn), lambda j: (0, j)),
        out_shape=jax.ShapeDtypeStruct((bsz, n), F32),
        compiler_params=_params(("arbitrary",)),
        name="adaln_mod",
    )(c, w_ada, b_ada.reshape(1, n))


def _inproj_kernel(x_ref, sc_ref, sh_ref, wq_ref, wk_ref, wv_ref, wd_ref, wl_ref,
                   oq_ref, ok_ref, ov_ref, od_ref, ol_ref):
    h = _rms(x_ref[0]) * (1.0 + sc_ref[0]) + sh_ref[0]
    hb = h.astype(BF16)
    oq_ref[0] = _dot_nt(wq_ref[...], hb).astype(BF16)
    ov_ref[0] = _dot_nt(wv_ref[...], hb).astype(BF16)
    ok_ref[0] = jnp.dot(hb, wk_ref[...], preferred_element_type=F32).astype(BF16)
    od_ref[0] = jnp.dot(hb, wd_ref[...], preferred_element_type=F32).astype(BF16)
    ol_ref[0] = jnp.dot(hb, wl_ref[...], preferred_element_type=F32)


def _in_projection(x, scale1, shift1, wq_t, wk, wv_t, w_dil, w_flog):
    bsz, s, d = x.shape
    tm = ROW_TILE
    vec = pl.BlockSpec((1, 1, d), lambda b, i: (b, 0, 0))
    full = lambda w: pl.BlockSpec(w.shape, lambda b, i: (0, 0))
    row = lambda n: pl.BlockSpec((1, tm, n), lambda b, i: (b, i, 0))
    col = pl.BlockSpec((1, WIDTH, tm), lambda b, i: (b, 0, i))
    return pl.pallas_call(
        _inproj_kernel,
        grid=(bsz, s // tm),
        in_specs=[row(d), vec, vec, full(wq_t), full(wk), full(wv_t), full(w_dil), full(w_flog)],
        out_specs=[col, row(WIDTH), col, row(3 * WIDTH), row(LANES)],
        out_shape=[jax.ShapeDtypeStruct((bsz, WIDTH, s), BF16),
                   jax.ShapeDtypeStruct((bsz, s, WIDTH), BF16),
                   jax.ShapeDtypeStruct((bsz, WIDTH, s), BF16),
                   jax.ShapeDtypeStruct((bsz, s, 3 * WIDTH), BF16),
                   jax.ShapeDtypeStruct((bsz, s, LANES), F32)],
        compiler_params=_params(("parallel", "arbitrary")),
        name="norm1_inproj",
    )(x, scale1, shift1, wq_t, wk, wv_t, w_dil, w_flog)


FORGET_PARTS = 3


def _split_bf16(x):
    parts = []
    for _ in range(FORGET_PARTS):
        p = x.astype(BF16)
        parts.append(p)
        x = x - p.astype(F32)
    return parts


def _forget_kernel(f_ref, b_ref, tri_ref, place_ref, o_ref):
    blk = tri_ref.shape[0]
    carry = jnp.zeros((1, LANES), F32)
    for c in range(f_ref.shape[1] // blk):
        rows = pl.ds(c * blk, blk)
        z = f_ref[0, rows, :] + b_ref[...]
        ls = jnp.minimum(z, 0.0) - jnp.log1p(jnp.exp(-jnp.abs(z)))
        cum = carry
        for p in _split_bf16(ls):
            cum = cum + jnp.dot(tri_ref[...], p, preferred_element_type=F32)
        carry = cum[blk - 1:blk, :]
        out = None
        for n, p in enumerate(_split_bf16(cum)):
            d = jnp.dot(p, place_ref[n], preferred_element_type=F32)
            out = d if out is None else out + d
        o_ref[0, rows, :] = out.astype(BF16)


def _forget_cumsum(flog, b_forget):
    bsz, s, _ = flog.shape
    blk = 512
    place = np.zeros((FORGET_PARTS, LANES, WIDTH), np.float32)
    for h in range(N_HEADS):
        for n in range(FORGET_PARTS):
            place[n, h, (h // 2) * LANES + (h % 2) * HEAD_DIM + n] = 1.0
    return pl.pallas_call(
        _forget_kernel,
        grid=(bsz,),
        in_specs=[pl.BlockSpec((1, s, LANES), lambda b: (b, 0, 0)),
                  pl.BlockSpec((1, LANES), lambda b: (0, 0)),
                  pl.BlockSpec((blk, blk), lambda b: (0, 0)),
                  pl.BlockSpec(place.shape, lambda b: (0, 0, 0))],
        out_specs=pl.BlockSpec((1, s, WIDTH), lambda b: (b, 0, 0)),
        out_shape=jax.ShapeDtypeStruct((bsz, s, WIDTH), BF16),
        compiler_params=_params(("parallel",)),
        name="forget_cumsum",
    )(flog, jnp.pad(b_forget, (0, LANES - N_HEADS)).reshape(1, LANES),
      jnp.tril(jnp.ones((blk, blk), BF16)), jnp.asarray(place, BF16))


def _stack_heads(q):
    lane = lax.broadcasted_iota(jnp.int32, q.shape, 1)
    zero = jnp.zeros_like(q)
    return jnp.concatenate([jnp.where(lane < HEAD_DIM, q, zero), jnp.where(lane >= HEAD_DIM, q, zero)], axis=0)


def _unstack_heads(o2):
    rows = o2.shape[0] // 2
    lane = lax.broadcasted_iota(jnp.int32, (rows, LANES), 1)
    return jnp.where(lane < HEAD_DIM, o2[:rows], o2[rows:])


def _fox_kernel(q_ref, k_ref, fa_ref, v_ref, o_ref, m_s, l_s, acc_s, *, t):
    i = pl.program_id(2)
    q_t = q_ref[0]
    row = lax.broadcasted_iota(jnp.int32, (LANES, 2 * t), 0)
    col = lax.broadcasted_iota(jnp.int32, (LANES, 2 * t), 1)
    head_row = jnp.where(col < t, 0, HEAD_DIM)
    q2 = jnp.concatenate([q_t, q_t], axis=1)
    q2 = jnp.where((row >= head_row) & (row < head_row + HEAD_DIM), q2, jnp.zeros_like(q2))
    minus = jnp.where((row >= head_row) & (row < head_row + FORGET_PARTS), -1.0, 0.0).astype(BF16)
    qa = jnp.concatenate([q2, minus], axis=0)
    m_s[...] = jnp.full(m_s.shape, NEG, F32)
    l_s[...] = jnp.zeros(l_s.shape, F32)
    acc_s[...] = jnp.zeros(acc_s.shape, F32)

    def step(ks, tk, masked):
        kk = jnp.concatenate([k_ref[0, pl.ds(ks, tk), :], fa_ref[0, pl.ds(ks, tk), :]], axis=1)
        s = jnp.dot(kk, qa, preferred_element_type=F32)
        if masked:
            r = lax.broadcasted_iota(jnp.int32, (tk, 2 * t), 0)
            c = lax.broadcasted_iota(jnp.int32, (tk, 2 * t), 1)
            s = jnp.where(r <= jnp.where(c >= t, c - t, c), s, NEG)
        m_prev = m_s[...]
        m_new = jnp.maximum(m_prev, jnp.max(s, axis=0, keepdims=True))
        alpha = jnp.exp(m_prev - m_new)
        p = jnp.exp(s - m_new)
        l_s[...] = alpha * l_s[...] + jnp.sum(p, axis=0, keepdims=True)
        acc_s[...] = alpha * acc_s[...] + jnp.dot(v_ref[0, :, pl.ds(ks, tk)], p.astype(BF16),
                                                  preferred_element_type=F32)
        m_s[...] = m_new

    def body(j, carry):
        step(pl.multiple_of(j * 2 * t, 2 * t), 2 * t, False)
        return carry

    npair = i // 2
    lax.fori_loop(0, npair, body, 0)

    @pl.when(i % 2 == 1)
    def _():
        step(pl.multiple_of(npair * 2 * t, 2 * t), t, False)

    step(pl.multiple_of(i * t, t), t, True)
    o2 = acc_s[...] / l_s[...]
    o_t = jnp.where(lax.broadcasted_iota(jnp.int32, (LANES, t), 0) < HEAD_DIM, o2[:, :t], o2[:, t:])
    o_ref[0] = o_t.T


def _fox_attention(q_t, k, f_aug, v_t):
    bsz, s, _ = k.shape
    t = FOX_TILE
    keys = pl.BlockSpec((1, s, LANES), lambda b, h, i: (b, 0, h))
    return pl.pallas_call(
        functools.partial(_fox_kernel, t=t),
        grid=(bsz, HEAD_PAIRS, s // t),
        in_specs=[pl.BlockSpec((1, LANES, t), lambda b, h, i: (b, h, i)),
                  keys, keys,
                  pl.BlockSpec((1, LANES, s), lambda b, h, i: (b, h, 0))],
        out_specs=pl.BlockSpec((1, t, LANES), lambda b, h, i: (b, i, h)),
        out_shape=jax.ShapeDtypeStruct((bsz, s, WIDTH), F32),
        scratch_shapes=[pltpu.VMEM((1, 2 * t), F32), pltpu.VMEM((1, 2 * t), F32),
                        pltpu.VMEM((LANES, 2 * t), F32)],
        compiler_params=_params(("parallel", "parallel", "arbitrary")),
        name="fox_attention",
    )(q_t, k, f_aug, v_t)


def _t5_bucket(dist):
    max_exact = T5_NUM_BUCKETS // 2
    d = np.maximum(dist, 1).astype(np.float32)
    large = max_exact + (np.log(d / max_exact) / np.log(T5_MAX_DISTANCE / max_exact)
                         * (T5_NUM_BUCKETS - max_exact)).astype(np.int32)
    large = np.minimum(large, T5_NUM_BUCKETS - 1)
    return np.where(dist < max_exact, dist, large).astype(np.int32)


def _dilated_bias(rel_bias):
    blk = DIL_BLOCK
    rel = np.arange(blk)[:, None] + blk - np.arange(2 * blk)[None, :]
    band = (rel >= 0) & (rel <= blk)
    tables = []
    for dil in DILATIONS:
        bucket = _t5_bucket(np.clip(rel, 0, blk) * dil)
        bias = rel_bias[bucket].astype(F32).transpose(2, 0, 1)
        tables.append(jnp.where(band[None], bias, NEG))
    return jnp.stack(tables).reshape(len(DILATIONS), HEAD_PAIRS, 2 * blk, 2 * blk)


def _dil_kernel(q_ref, k_ref, v_ref, bias_ref, o_ref, qf, kf, vf, ob0, ob1, ob2, ls0, ls1, ls2, *, s_len):
    blk = DIL_BLOCK
    qf[...] = q_ref[0].astype(F32)
    kf[...] = k_ref[0].astype(F32)
    vf[...] = v_ref[0].astype(F32)
    col = lax.broadcasted_iota(jnp.int32, (2 * blk, 2 * blk), 1)

    for bi, (dil, ob, ls) in enumerate(zip(DILATIONS, (ob0, ob1, ob2), (ls0, ls1, ls2))):
        span = blk * dil
        nb = s_len // span

        def rows(start, dil=dil):
            return pl.ds(start, blk) if dil == 1 else pl.ds(start, blk, stride=dil)

        def block(tix, carry, bi=bi, span=span, nb=nb, ob=ob, ls=ls, rows=rows):
            n = tix % nb
            start = n * span + tix // nb
            prev = jnp.maximum(start - span, 0)
            q2 = _stack_heads(qf[rows(start), :].astype(BF16))
            kk = jnp.concatenate([kf[rows(prev), :], kf[rows(start), :]], axis=0).astype(BF16)
            vv = jnp.concatenate([vf[rows(prev), :], vf[rows(start), :]], axis=0).astype(BF16)
            s = lax.dot_general(q2, kk, (((1,), (1,)), ((), ())), preferred_element_type=F32)
            s = s + bias_ref[bi, 0]
            s = jnp.where((col < blk) & (n == 0), NEG, s)
            m = jnp.max(s, axis=1, keepdims=True)
            p = jnp.exp(s - m)
            l = jnp.sum(p, axis=1, keepdims=True)
            o2 = jnp.dot(p.astype(BF16), vv, preferred_element_type=F32) / l
            lse = jnp.broadcast_to(m + jnp.log(l), (2 * blk, LANES))
            ob[rows(start), :] = _unstack_heads(o2)
            ls[rows(start), :] = _unstack_heads(lse)
            return carry

        lax.fori_loop(0, dil * nb, block, 0)

    chunk = 512
    for c in range(s_len // chunk):
        r = pl.ds(c * chunk, chunk)
        l0, l1, l2 = ls0[r, :], ls1[r, :], ls2[r, :]
        mx = jnp.maximum(jnp.maximum(l0, l1), l2)
        e0, e1, e2 = jnp.exp(l0 - mx), jnp.exp(l1 - mx), jnp.exp(l2 - mx)
        o_ref[0, r, :] = (e0 * ob0[r, :] + e1 * ob1[r, :] + e2 * ob2[r, :]) / (e0 + e1 + e2)


def _dilated_attention(qkv, bias):
    bsz, s, _ = qkv.shape
    col = lambda off: pl.BlockSpec((1, s, LANES), lambda b, h: (b, 0, off + h))
    buf = pltpu.VMEM((s, LANES), F32)
    return pl.pallas_call(
        functools.partial(_dil_kernel, s_len=s),
        grid=(bsz, HEAD_PAIRS),
        in_specs=[col(0), col(HEAD_PAIRS), col(2 * HEAD_PAIRS),
                  pl.BlockSpec((len(DILATIONS), 1, 2 * DIL_BLOCK, 2 * DIL_BLOCK), lambda b, h: (0, h, 0, 0))],
        out_specs=pl.BlockSpec((1, s, LANES), lambda b, h: (b, 0, h)),
        out_shape=jax.ShapeDtypeStruct((bsz, s, WIDTH), F32),
        scratch_shapes=[buf] * 9,
        compiler_params=_params(("parallel", "arbitrary")),
        name="dilated_attention",
    )(qkv, qkv, qkv, bias)


def _to_row_tiles(dst_ref, x):
    rows = x.shape[0]
    for c in range(x.shape[1] // LANES):
        dst_ref[pl.ds(c, rows, stride=SUBLANES), :] = x[:, c * LANES:(c + 1) * LANES]


def _from_row_tiles(src_ref, rows, base=0, stride=SUBLANES):
    return [src_ref[pl.ds(base + c, rows, stride=stride), :] for c in range(SUBLANES)]


def _outproj_kernel(yf_ref, yd_ref, x_ref, g1_ref, sc_ref, sh_ref, g2_ref, gf_ref, gd_ref, wo1_ref, wo2_ref,
                    wrh_ref, wrl_ref, wsg_ref, wsu_ref, wsd_ref, xp_ref, h3_ref, lg_ref):
    nf = (_rms(yf_ref[0]) * gf_ref[...]).astype(BF16)
    nd = (_rms(yd_ref[0]) * gd_ref[...]).astype(BF16)
    mix = (jnp.dot(nf, wo1_ref[...], preferred_element_type=F32)
           + jnp.dot(nd, wo2_ref[...], preferred_element_type=F32))
    x1 = x_ref[0] + g1_ref[0] * mix
    h2 = _rms(x1) * (1.0 + sc_ref[0]) + sh_ref[0]
    hb = h2.astype(BF16)
    hl = (h2 - hb.astype(F32)).astype(BF16)
    lg_ref[...] = _dot_nt(wrh_ref[...], hb) + _dot_nt(wrh_ref[...], hl) + _dot_nt(wrl_ref[...], hb)
    act = _silu(jnp.dot(hb, wsg_ref[...], preferred_element_type=F32)) * jnp.dot(
        hb, wsu_ref[...], preferred_element_type=F32)
    shared = jnp.dot(act.astype(BF16), wsd_ref[...], preferred_element_type=F32)
    xp_ref[0] = x1 + g2_ref[0] * shared
    _to_row_tiles(h3_ref, h2)


def _out_projection(y_fox, y_dil, x, gate1, scale2, shift2, gate2, g_fox, g_dil, wo1, wo2, wr_hi, wr_lo,
                    wsg, wsu, wsd):
    bsz, s, d = x.shape
    tm = ROW_TILE
    nt = s // tm
    vec = pl.BlockSpec((1, 1, d), lambda b, i: (b, 0, 0))
    full = lambda w: pl.BlockSpec(w.shape, lambda b, i: (0,) * w.ndim)
    row = lambda n: pl.BlockSpec((1, tm, n), lambda b, i: (b, i, 0))
    return pl.pallas_call(
        _outproj_kernel,
        grid=(bsz, nt),
        in_specs=[row(WIDTH), row(WIDTH), row(d), vec, vec, vec, vec, full(g_fox), full(g_dil), full(wo1),
                  full(wo2), full(wr_hi), full(wr_lo), full(wsg), full(wsu), full(wsd)],
        out_specs=[row(d),
                   pl.BlockSpec((tm * SUBLANES, LANES), lambda b, i: (b * nt + i, 0)),
                   pl.BlockSpec((N_EXPERTS, tm), lambda b, i: (0, b * nt + i))],
        out_shape=[jax.ShapeDtypeStruct((bsz, s, d), F32),
                   jax.ShapeDtypeStruct((bsz * s * SUBLANES, LANES), F32),
                   jax.ShapeDtypeStruct((N_EXPERTS, bsz * s), F32)],
        compiler_params=_params(("parallel", "arbitrary")),
        name="outproj_norm2_router_shared",
    )(y_fox, y_dil, x, gate1, scale2, shift2, gate2, g_fox, g_dil, wo1, wo2, wr_hi, wr_lo, wsg, wsu, wsd)


def _first_argmax(v, row, size):
    m = jnp.max(v, axis=0, keepdims=True)
    return m, jnp.min(jnp.where(v == m, row, size), axis=0, keepdims=True)


def _route_kernel(lg_ref, rb_ref, tri_ref, e_ref, w_ref, rk_ref, cnt_ref, cnt_s):
    @pl.when(pl.program_id(0) == 0)
    def _():
        cnt_s[...] = jnp.zeros(cnt_s.shape, F32)

    t = lg_ref.shape[1]
    gsz = N_EXPERTS // N_GROUPS
    scores = jax.nn.sigmoid(lg_ref[...])
    sel = scores + rb_ref[...]
    row_g = lax.broadcasted_iota(jnp.int32, (gsz, t), 0)
    grp = []
    for g in range(N_GROUPS):
        v = sel[g * gsz:(g + 1) * gsz]
        m1, i1 = _first_argmax(v, row_g, gsz)
        m2 = jnp.max(jnp.where(row_g == i1, -jnp.inf, v), axis=0, keepdims=True)
        grp.append(m1 + m2)
    gv = jnp.concatenate(grp, axis=0)
    row8 = lax.broadcasted_iota(jnp.int32, (N_GROUPS, t), 0)
    pen = jnp.full((N_GROUPS, t), -jnp.inf, F32)
    for _ in range(TOP_K_GROUPS):
        _, ix = _first_argmax(gv, row8, N_GROUPS)
        pen = jnp.where(row8 == ix, 0.0, pen)
        gv = jnp.where(row8 == ix, -jnp.inf, gv)
    selm = jnp.concatenate([sel[g * gsz:(g + 1) * gsz] + pen[g:g + 1] for g in range(N_GROUPS)], axis=0)

    row = lax.broadcasted_iota(jnp.int32, (N_EXPERTS, t), 0)
    v = selm
    idxs, scs = [], []
    for _ in range(TOP_K):
        _, ix = _first_argmax(v, row, N_EXPERTS)
        hit = row == ix
        idxs.append(ix)
        scs.append(jnp.sum(jnp.where(hit, scores, 0.0), axis=0, keepdims=True))
        v = jnp.where(hit, -jnp.inf, v)
    chosen = jnp.where(v != selm, 1.0, 0.0)
    before = jnp.dot(chosen.astype(BF16), tri_ref[...], preferred_element_type=F32) + cnt_s[...]
    rks = [jnp.sum(jnp.where(row == ix, before, 0.0), axis=0, keepdims=True) for ix in idxs]
    sc = jnp.concatenate(scs, axis=0)
    e_ref[...] = jnp.concatenate(idxs, axis=0)
    w_ref[...] = sc / jnp.sum(sc, axis=0, keepdims=True) * ROUTED_SCALE
    rk_ref[...] = jnp.concatenate(rks, axis=0).astype(jnp.int32)
    cnt_s[...] = cnt_s[...] + jnp.sum(chosen, axis=1, keepdims=True)
    cnt_ref[...] = cnt_s[...]


def _route(logits_t, router_bias):
    e, n = logits_t.shape
    t = ROUTE_TILE
    tri = jnp.triu(jnp.ones((t, t), BF16), k=1)
    tile = pl.BlockSpec((TOP_K, t), lambda i: (0, i))
    return pl.pallas_call(
        _route_kernel,
        grid=(n // t,),
        in_specs=[pl.BlockSpec((e, t), lambda i: (0, i)),
                  pl.BlockSpec((e, 1), lambda i: (0, 0)),
                  pl.BlockSpec((t, t), lambda i: (0, 0))],
        out_specs=[tile, tile, tile, pl.BlockSpec((e, 1), lambda i: (0, 0))],
        out_shape=[jax.ShapeDtypeStruct((TOP_K, n), jnp.int32), jax.ShapeDtypeStruct((TOP_K, n), F32),
                   jax.ShapeDtypeStruct((TOP_K, n), jnp.int32), jax.ShapeDtypeStruct((e, 1), F32)],
        scratch_shapes=[pltpu.VMEM((e, 1), F32)],
        compiler_params=_params(("arbitrary",)),
        name="route_topk_rank",
    )(logits_t, router_bias.reshape(e, 1).astype(F32), tri)


def _dest_kernel(e_ref, rk_ref, ps_ref, d_ref):
    t = e_ref.shape[1]
    row = lax.broadcasted_iota(jnp.int32, (N_EXPERTS, t), 0)
    ps = ps_ref[...]
    base = [jnp.sum(jnp.where(row == e_ref[k:k + 1, :], ps, 0.0), axis=0, keepdims=True) for k in range(TOP_K)]
    d_ref[...] = jnp.concatenate(base, axis=0).astype(jnp.int32) + rk_ref[...]


def _dest_rows(eidx, rank, pstart):
    _, n = eidx.shape
    t = ROUTE_TILE
    tile = pl.BlockSpec((TOP_K, t), lambda i: (0, i))
    return pl.pallas_call(
        _dest_kernel,
        grid=(n // t,),
        in_specs=[tile, tile, pl.BlockSpec((N_EXPERTS, 1), lambda i: (0, 0))],
        out_specs=tile,
        out_shape=jax.ShapeDtypeStruct((TOP_K, n), jnp.int32),
        compiler_params=_params(("parallel",)),
        name="dest_rows",
    )(eidx, rank, pstart.astype(F32).reshape(N_EXPERTS, 1))


def _block_plan(counts, nblk):
    counts = counts.reshape(N_EXPERTS).astype(jnp.int32)
    padded = (counts + FFN_BLOCK - 1) // FFN_BLOCK * FFN_BLOCK
    pend = jnp.cumsum(padded).astype(jnp.int32)
    pstart = pend - padded
    block_e = jnp.minimum(jnp.searchsorted(pend, jnp.arange(nblk, dtype=jnp.int32) * FFN_BLOCK, side='right'),
                          N_EXPERTS - 1).astype(jnp.int32)
    nused = (pend[-1:] // FFN_BLOCK).astype(jnp.int32)
    return pstart, pstart + counts, pend, block_e, nused


def _tile_copy(src_ref, src_row, dst_ref, dst_row, sem):
    return pltpu.make_async_copy(src_ref.at[pl.ds(pl.multiple_of(src_row * SUBLANES, SUBLANES), SUBLANES), :],
                                 dst_ref.at[pl.ds(pl.multiple_of(dst_row * SUBLANES, SUBLANES), SUBLANES), :],
                                 sem)


def _scatter_kernel(cend_ref, pend_ref, dst_ref, h3_ref, z_ref, xs_hbm, st0, st1, sem, zsem, *, n_assign):
    i = pl.program_id(0)
    nsteps = pl.num_programs(0)
    tm = dst_ref.shape[1]
    stage = (st0, st1)

    def zero_copy(r):
        return pltpu.make_async_copy(z_ref, xs_hbm.at[pl.ds(pl.multiple_of(r * SUBLANES, SUBLANES), SUBLANES), :],
                                     zsem)

    def wait_slot(slot):
        for _ in range(TOP_K):
            pltpu.make_async_copy(stage[slot], xs_hbm.at[pl.ds(0, tm * SUBLANES), :], sem.at[slot]).wait()

    @pl.when(i == 0)
    def _():
        def per_expert(e, carry):
            def per_row(r, c):
                zero_copy(r).start()
                return c
            return lax.fori_loop(cend_ref[e], pend_ref[e], per_row, carry)
        lax.fori_loop(0, N_EXPERTS, per_expert, 0)

    for slot in range(2):
        @pl.when(i % 2 == slot)
        def _(slot=slot):
            @pl.when(i >= 2)
            def _():
                wait_slot(slot)
            stage[slot][...] = h3_ref[...]
            for k in range(TOP_K):
                def body(jj, carry, k=k):
                    for u in range(ISSUE_UNROLL):
                        j = jj * ISSUE_UNROLL + u
                        _tile_copy(stage[slot], j, xs_hbm, dst_ref[k, j], sem.at[slot]).start()
                    return carry
                lax.fori_loop(0, tm // ISSUE_UNROLL, body, 0)

    @pl.when(i == nsteps - 1)
    def _():
        wait_slot(0)
        wait_slot(1)

        def wait_zero(r, carry):
            zero_copy(r).wait()
            return carry
        lax.fori_loop(0, pend_ref[N_EXPERTS - 1] - n_assign, wait_zero, 0)


def _dispatch_rows(h3, dest, cend, pend, rows):
    _, n = dest.shape
    tm = SCATTER_TILE
    assert n // tm >= 2
    stage = pltpu.VMEM((tm * SUBLANES, LANES), F32)
    grid_spec = pltpu.PrefetchScalarGridSpec(
        num_scalar_prefetch=2,
        grid=(n // tm,),
        in_specs=[pl.BlockSpec((TOP_K, tm), lambda i, ce, pe: (0, i), memory_space=pltpu.SMEM),
                  pl.BlockSpec((tm * SUBLANES, LANES), lambda i, ce, pe: (i, 0)),
                  pl.BlockSpec((SUBLANES, LANES), lambda i, ce, pe: (0, 0))],
        out_specs=pl.BlockSpec(memory_space=pl.ANY),
        scratch_shapes=[stage, stage, pltpu.SemaphoreType.DMA((2,)), pltpu.SemaphoreType.DMA],
    )
    return pl.pallas_call(
        functools.partial(_scatter_kernel, n_assign=n * TOP_K),
        grid_spec=grid_spec,
        out_shape=jax.ShapeDtypeStruct((rows * SUBLANES, LANES), F32),
        compiler_params=_params(("arbitrary",)),
        name="dispatch_scatter",
    )(cend, pend, dest, h3, jnp.zeros((SUBLANES, LANES), F32))


def _ffn_kernel(be_ref, nu_ref, x_ref, wg_ref, wu_ref, wd_ref, y_ref, wgb, wub, wdb):
    i = pl.program_id(0)
    nused = nu_ref[0]

    @pl.when((i == 0) | (be_ref[i] != be_ref[jnp.maximum(i - 1, 0)]))
    def _():
        wgb[...] = wg_ref[0].astype(BF16)
        wub[...] = wu_ref[0].astype(BF16)
        wdb[...] = wd_ref[0].astype(BF16)

    @pl.when(i < nused)
    def _():
        x = jnp.concatenate([c.astype(BF16) for c in _from_row_tiles(x_ref, FFN_BLOCK)], axis=1)
        act = _silu(jnp.dot(x, wgb[...], preferred_element_type=F32)) * jnp.dot(
            x, wub[...], preferred_element_type=F32)
        _to_row_tiles(y_ref, jnp.dot(act.astype(BF16), wdb[...], preferred_element_type=F32))

    @pl.when(i >= nused)
    def _():
        y_ref[...] = jnp.zeros(y_ref.shape, F32)


def _routed_experts(xs, block_e, nused, w_gate, w_up, w_down):
    rows = xs.shape[0] // SUBLANES
    nblk = rows // FFN_BLOCK
    _, d, hid = w_gate.shape
    grid_spec = pltpu.PrefetchScalarGridSpec(
        num_scalar_prefetch=2,
        grid=(nblk,),
        in_specs=[pl.BlockSpec((FFN_BLOCK * SUBLANES, LANES), lambda i, be, nu: (jnp.minimum(i, nu[0] - 1), 0)),
                  pl.BlockSpec((1, d, hid), lambda i, be, nu: (be[i], 0, 0)),
                  pl.BlockSpec((1, d, hid), lambda i, be, nu: (be[i], 0, 0)),
                  pl.BlockSpec((1, hid, d), lambda i, be, nu: (be[i], 0, 0))],
        out_specs=pl.BlockSpec((FFN_BLOCK * SUBLANES, LANES), lambda i, be, nu: (i, 0)),
        scratch_shapes=[pltpu.VMEM((d, hid), BF16), pltpu.VMEM((d, hid), BF16), pltpu.VMEM((hid, d), BF16)],
    )
    return pl.pallas_call(
        _ffn_kernel,
        grid_spec=grid_spec,
        out_shape=jax.ShapeDtypeStruct((rows * SUBLANES, LANES), F32),
        compiler_params=_params(("arbitrary",)),
        name="routed_experts",
    )(block_e, nused, xs, w_gate, w_up, w_down)


def _gather_start(idx_ref, src_hbm, dst_vmem, sem):
    nk, tm = idx_ref.shape
    for k in range(nk):
        def body(jj, carry, k=k):
            for u in range(ISSUE_UNROLL):
                j = jj * ISSUE_UNROLL + u
                _tile_copy(src_hbm, idx_ref[k, j], dst_vmem, k * tm + j, sem).start()
            return carry
        lax.fori_loop(0, tm // ISSUE_UNROLL, body, 0)


def _gather_wait(src_hbm, dst_vmem, sem):
    pltpu.make_async_copy(src_hbm.at[pl.ds(0, dst_vmem.shape[0]), :], dst_vmem, sem).wait()


def _combine_kernel(dst_ref, dstn_ref, wt_ref, xp_ref, g2_ref, gfin_ref, ys_hbm, o_ref, yb0, yb1, sem, *, nsteps):
    i = pl.program_id(0)
    tm = COMBINE_TILE
    bufs = (yb0, yb1)

    @pl.when(i == 0)
    def _():
        _gather_start(dst_ref, ys_hbm, yb0, sem.at[0])

    for slot in range(2):
        @pl.when((i + 1 < nsteps) & ((i + 1) % 2 == slot))
        def _(slot=slot):
            _gather_start(dstn_ref, ys_hbm, bufs[slot], sem.at[slot])

    for slot in range(2):
        @pl.when(i % 2 == slot)
        def _(slot=slot):
            _gather_wait(ys_hbm, bufs[slot], sem.at[slot])
            wt = wt_ref[...]
            cols = None
            for k in range(TOP_K):
                part = [c * wt[:, k:k + 1] for c in _from_row_tiles(bufs[slot], tm, base=k * tm * SUBLANES)]
                cols = part if cols is None else [a + b for a, b in zip(cols, part)]
            routed = jnp.concatenate(cols, axis=1)
            o_ref[0] = _rms(xp_ref[0] + g2_ref[0] * routed) * gfin_ref[...]


def _combine(ys, dest, wts_t, xp, gate2, g_final):
    bsz, s, d = xp.shape
    tm = COMBINE_TILE
    nt = s // tm
    nsteps = bsz * nt
    vec = pl.BlockSpec((1, 1, d), lambda i: (i // nt, 0, 0))
    return pl.pallas_call(
        functools.partial(_combine_kernel, nsteps=nsteps),
        grid=(nsteps,),
        in_specs=[pl.BlockSpec((TOP_K, tm), lambda i: (0, i), memory_space=pltpu.SMEM),
                  pl.BlockSpec((TOP_K, tm), lambda i: (0, jnp.minimum(i + 1, nsteps - 1)),
                               memory_space=pltpu.SMEM),
                  pl.BlockSpec((tm, TOP_K), lambda i: (i, 0)),
                  pl.BlockSpec((1, tm, d), lambda i: (i // nt, i % nt, 0)),
                  vec,
                  pl.BlockSpec((1, d), lambda i: (0, 0)),
                  pl.BlockSpec(memory_space=pl.ANY)],
        out_specs=pl.BlockSpec((1, tm, d), lambda i: (i // nt, i % nt, 0)),
        out_shape=jax.ShapeDtypeStruct((bsz, s, d), F32),
        scratch_shapes=[pltpu.VMEM((tm * TOP_K * SUBLANES, LANES), F32),
                        pltpu.VMEM((tm * TOP_K * SUBLANES, LANES), F32),
                        pltpu.SemaphoreType.DMA((2,))],
        compiler_params=_params(("arbitrary",)),
        name="combine_final_norm",
    )(dest, dest, wts_t, xp, gate2, g_final.reshape(1, d), ys)


def kernel(x, c, w_in, b_forget, g_fox_out, g_dil_out, w_out, w_ada, b_ada, w_router, router_bias,
           w_exp_gate, w_exp_up, w_exp_down, w_sh_gate, w_sh_up, w_sh_down, rel_bias, g_final):
    bsz, s, d = x.shape
    depth = w_in.shape[0]
    assert depth == 1 and d == SUBLANES * LANES and s % (DIL_BLOCK * DILATIONS[-1]) == 0
    l = 0
    mod = _modulation(c, w_ada[l], b_ada[l])
    shift1, scale1, gate1, shift2, scale2, gate2 = [m[:, None, :] for m in jnp.split(mod, 6, axis=-1)]

    qscale = HEAD_DIM ** -0.5
    o3 = 3 * WIDTH
    w = w_in[l]
    wq_t = (w[:, :WIDTH] * qscale).T.astype(BF16)
    wk = w[:, WIDTH:2 * WIDTH].astype(BF16)
    wv_t = w[:, 2 * WIDTH:o3].T.astype(BF16)
    w_flog = jnp.pad(w[:, o3:o3 + N_HEADS], ((0, 0), (0, LANES - N_HEADS))).astype(BF16)
    wd0 = o3 + N_HEADS
    w_dil = jnp.concatenate([w[:, wd0:wd0 + WIDTH] * qscale, w[:, wd0 + WIDTH:]], axis=1).astype(BF16)

    q_t, k_f, v_t, qkv_d, flog = _in_projection(x, scale1, shift1, wq_t, wk, wv_t, w_dil, w_flog)
    y_fox = _fox_attention(q_t, k_f, _forget_cumsum(flog, b_forget[l]), v_t)
    y_dil = _dilated_attention(qkv_d, _dilated_bias(rel_bias))

    wr = w_router[l].T
    wr_hi = wr.astype(BF16)
    wr_lo = (wr - wr_hi.astype(F32)).astype(BF16)
    wo = w_out[l].astype(BF16)
    xp, h3, logits_t = _out_projection(
        y_fox, y_dil, x, gate1, scale2, shift2, gate2, g_fox_out[l].reshape(1, WIDTH),
        g_dil_out[l].reshape(1, WIDTH), wo[:WIDTH], wo[WIDTH:], wr_hi, wr_lo,
        w_sh_gate[l].astype(BF16), w_sh_up[l].astype(BF16), w_sh_down[l].astype(BF16))

    eidx, wts, rank, counts = _route(logits_t, router_bias[l])
    rows = bsz * s * TOP_K + N_EXPERTS * FFN_BLOCK
    pstart, cend, pend, block_e, nused = _block_plan(counts, rows // FFN_BLOCK)
    dest = _dest_rows(eidx, rank, pstart)
    xs = _dispatch_rows(h3, dest, cend, pend, rows)
    ys = _routed_experts(xs, block_e, nused, w_exp_gate[l], w_exp_up[l], w_exp_down[l])
    return _combine(ys, dest, wts.T, xp, gate2, g_final)
```

```python
import functools

import numpy as np
import jax
import jax.numpy as jnp
from jax import lax
from jax.experimental import pallas as pl
from jax.experimental.pallas import tpu as pltpu

F32 = jnp.float32
BF16 = jnp.bfloat16

HEAD_DIM = 64
N_HEADS = 8
WIDTH = N_HEADS * HEAD_DIM
LANES = 128
TILE_ROWS = 4
U32 = jnp.uint32
DIL_UNROLL = 4
HEAD_PAIRS = WIDTH // LANES
DIL_BLOCK = 128
DILATIONS = (1, 4, 16)
T5_NUM_BUCKETS = 32
T5_MAX_DISTANCE = 2048
N_EXPERTS = 256
TOP_K = 8
N_GROUPS = 8
TOP_K_GROUPS = 4
ROUTED_SCALE = 2.5
EPS = 1e-6
NEG = -1e30
VMEM_LIMIT = 56 * 1024 * 1024

ROW_TILE = 512
FOX_TILE = 256
FFN_BLOCK = 256
COMBINE_TILE = 128
SCATTER_TILE = 128
ROUTE_TILE = 512
ISSUE_UNROLL = 8


def _params(semantics):
    return pltpu.CompilerParams(dimension_semantics=semantics, vmem_limit_bytes=VMEM_LIMIT)


def _rms(x):
    return x * lax.rsqrt(jnp.mean(x * x, axis=-1, keepdims=True) + EPS)


def _silu(x):
    return x * jax.nn.sigmoid(x)


def _dot_nt(a, b):
    return lax.dot_general(a, b, (((1,), (1,)), ((), ())), preferred_element_type=F32)


def _mod_kernel(c_ref, w_ref, b_ref, o_ref):
    o_ref[...] = jnp.dot(_silu(c_ref[...]), w_ref[...], precision=lax.Precision.HIGHEST,
                         preferred_element_type=F32) + b_ref[...]


def _modulation(c, w_ada, b_ada):
    bsz, d = c.shape
    n = w_ada.shape[1]
    tn = 1536
    return pl.pallas_call(
        _mod_kernel,
        grid=(n // tn,),
        in_specs=[pl.BlockSpec((bsz, d), lambda j: (0, 0)),
                  pl.BlockSpec((d, tn), lambda j: (0, j)),
                  pl.BlockSpec((1, tn), lambda j: (0, j))],
        out_specs=pl.BlockSpec((bsz, tn), lambda j: (0, j)),
        out_shape=jax.ShapeDtypeStruct((bsz, n), F32),
        compiler_params=_params(("arbitrary",)),
        name="adaln_mod",
    )(c, w_ada, b_ada.reshape(1, n))


def _inproj_kernel(x_ref, sc_ref, sh_ref, wq_ref, wk_ref, wv_ref, wd_ref, wl_ref,
                   oq_ref, ok_ref, ov_ref, od_ref, ol_ref):
    h = _rms(x_ref[0]) * (1.0 + sc_ref[0]) + sh_ref[0]
    hb = h.astype(BF16)
    oq_ref[0] = _dot_nt(wq_ref[...], hb).astype(BF16)
    ov_ref[0] = _dot_nt(wv_ref[...], hb).astype(BF16)
    ok_ref[0] = jnp.dot(hb, wk_ref[...], preferred_element_type=F32).astype(BF16)
    od_ref[0] = jnp.dot(hb, wd_ref[...], preferred_element_type=F32).astype(BF16)
    ol_ref[0] = jnp.dot(hb, wl_ref[...], preferred_element_type=F32)


def _in_projection(x, scale1, shift1, wq_t, wk, wv_t, w_dil, w_flog):
    bsz, s, d = x.shape
    tm = ROW_TILE
    vec = pl.BlockSpec((1, 1, d), lambda b, i: (b, 0, 0))
    full = lambda w: pl.BlockSpec(w.shape, lambda b, i: (0, 0))
    row = lambda n: pl.BlockSpec((1, tm, n), lambda b, i: (b, i, 0))
    col = pl.BlockSpec((1, WIDTH, tm), lambda b, i: (b, 0, i))
    return pl.pallas_call(
        _inproj_kernel,
        grid=(bsz, s // tm),
        in_specs=[row(d), vec, vec, full(wq_t), full(wk), full(wv_t), full(w_dil), full(w_flog)],
        out_specs=[col, row(WIDTH), col, row(3 * WIDTH), row(LANES)],
        out_shape=[jax.ShapeDtypeStruct((bsz, WIDTH, s), BF16),
                   jax.ShapeDtypeStruct((bsz, s, WIDTH), BF16),
                   jax.ShapeDtypeStruct((bsz, WIDTH, s), BF16),
                   jax.ShapeDtypeStruct((bsz, s, 3 * WIDTH), BF16),
                   jax.ShapeDtypeStruct((bsz, s, LANES), F32)],
        compiler_params=_params(("parallel", "arbitrary")),
        name="norm1_inproj",
    )(x, scale1, shift1, wq_t, wk, wv_t, w_dil, w_flog)


FORGET_PARTS = 3


def _split_bf16(x):
    parts = []
    for _ in range(FORGET_PARTS):
        p = x.astype(BF16)
        parts.append(p)
        x = x - p.astype(F32)
    return parts


def _forget_kernel(f_ref, b_ref, tri_ref, place_ref, o_ref):
    blk = tri_ref.shape[0]
    carry = jnp.zeros((1, LANES), F32)
    for c in range(f_ref.shape[1] // blk):
        rows = pl.ds(c * blk, blk)
        z = f_ref[0, rows, :] + b_ref[...]
        ls = jnp.minimum(z, 0.0) - jnp.log1p(jnp.exp(-jnp.abs(z)))
        cum = carry
        for p in _split_bf16(ls):
            cum = cum + jnp.dot(tri_ref[...], p, preferred_element_type=F32)
        carry = cum[blk - 1:blk, :]
        out = None
        for n, p in enumerate(_split_bf16(cum)):
            d = jnp.dot(p, place_ref[n], preferred_element_type=F32)
            out = d if out is None else out + d
        o_ref[0, rows, :] = out.astype(BF16)


def _forget_cumsum(flog, b_forget):
    bsz, s, _ = flog.shape
    blk = 512
    place = np.zeros((FORGET_PARTS, LANES, WIDTH), np.float32)
    for h in range(N_HEADS):
        for n in range(FORGET_PARTS):
            place[n, h, (h // 2) * LANES + (h % 2) * HEAD_DIM + n] = 1.0
    return pl.pallas_call(
        _forget_kernel,
        grid=(bsz,),
        in_specs=[pl.BlockSpec((1, s, LANES), lambda b: (b, 0, 0)),
                  pl.BlockSpec((1, LANES), lambda b: (0, 0)),
                  pl.BlockSpec((blk, blk), lambda b: (0, 0)),
                  pl.BlockSpec(place.shape, lambda b: (0, 0, 0))],
        out_specs=pl.BlockSpec((1, s, WIDTH), lambda b: (b, 0, 0)),
        out_shape=jax.ShapeDtypeStruct((bsz, s, WIDTH), BF16),
        compiler_params=_params(("parallel",)),
        name="forget_cumsum",
    )(flog, jnp.pad(b_forget, (0, LANES - N_HEADS)).reshape(1, LANES),
      jnp.tril(jnp.ones((blk, blk), BF16)), jnp.asarray(place, BF16))


def _stack_heads(q):
    lane = lax.broadcasted_iota(jnp.int32, q.shape, 1)
    zero = jnp.zeros_like(q)
    return jnp.concatenate([jnp.where(lane < HEAD_DIM, q, zero), jnp.where(lane >= HEAD_DIM, q, zero)], axis=0)


def _unstack_heads(o2):
    rows = o2.shape[0] // 2
    lane = lax.broadcasted_iota(jnp.int32, (rows, LANES), 1)
    return jnp.where(lane < HEAD_DIM, o2[:rows], o2[rows:])


def _fox_kernel(q_ref, k_ref, fa_ref, v_ref, o_ref, m_s, l_s, acc_s, *, t):
    i = pl.program_id(2)
    q_t = q_ref[0]
    row = lax.broadcasted_iota(jnp.int32, (LANES, 2 * t), 0)
    col = lax.broadcasted_iota(jnp.int32, (LANES, 2 * t), 1)
    head_row = jnp.where(col < t, 0, HEAD_DIM)
    q2 = jnp.concatenate([q_t, q_t], axis=1)
    q2 = jnp.where((row >= head_row) & (row < head_row + HEAD_DIM), q2, jnp.zeros_like(q2))
    minus = jnp.where((row >= head_row) & (row < head_row + FORGET_PARTS), -1.0, 0.0).astype(BF16)
    qa = jnp.concatenate([q2, minus], axis=0)
    m_s[...] = jnp.full(m_s.shape, NEG, F32)
    l_s[...] = jnp.zeros(l_s.shape, F32)
    acc_s[...] = jnp.zeros(acc_s.shape, F32)

    def step(ks, tk, masked):
        kk = jnp.concatenate([k_ref[0, pl.ds(ks, tk), :], fa_ref[0, pl.ds(ks, tk), :]], axis=1)
        s = jnp.dot(kk, qa, preferred_element_type=F32)
        if masked:
            r = lax.broadcasted_iota(jnp.int32, (tk, 2 * t), 0)
            c = lax.broadcasted_iota(jnp.int32, (tk, 2 * t), 1)
            s = jnp.where(r <= jnp.where(c >= t, c - t, c), s, NEG)
        m_prev = m_s[...]
        m_new = jnp.maximum(m_prev, jnp.max(s, axis=0, keepdims=True))
        alpha = jnp.exp(m_prev - m_new)
        p = jnp.exp(s - m_new)
        l_s[...] = alpha * l_s[...] + jnp.sum(p, axis=0, keepdims=True)
        acc_s[...] = alpha * acc_s[...] + jnp.dot(v_ref[0, :, pl.ds(ks, tk)], p.astype(BF16),
                                                  preferred_element_type=F32)
        m_s[...] = m_new

    def body(j, carry):
        step(pl.multiple_of(j * 2 * t, 2 * t), 2 * t, False)
        return carry

    npair = i // 2
    lax.fori_loop(0, npair, body, 0)

    @pl.when(i % 2 == 1)
    def _():
        step(pl.multiple_of(npair * 2 * t, 2 * t), t, False)

    step(pl.multiple_of(i * t, t), t, True)
    o2 = acc_s[...] / l_s[...]
    o_t = jnp.where(lax.broadcasted_iota(jnp.int32, (LANES, t), 0) < HEAD_DIM, o2[:, :t], o2[:, t:])
    o_ref[0] = o_t.T


def _fox_attention(q_t, k, f_aug, v_t):
    bsz, s, _ = k.shape
    t = FOX_TILE
    keys = pl.BlockSpec((1, s, LANES), lambda b, h, i: (b, 0, h))
    return pl.pallas_call(
        functools.partial(_fox_kernel, t=t),
        grid=(bsz, HEAD_PAIRS, s // t),
        in_specs=[pl.BlockSpec((1, LANES, t), lambda b, h, i: (b, h, i)),
                  keys, keys,
                  pl.BlockSpec((1, LANES, s), lambda b, h, i: (b, h, 0))],
        out_specs=pl.BlockSpec((1, t, LANES), lambda b, h, i: (b, i, h)),
        out_shape=jax.ShapeDtypeStruct((bsz, s, WIDTH), F32),
        scratch_shapes=[pltpu.VMEM((1, 2 * t), F32), pltpu.VMEM((1, 2 * t), F32),
                        pltpu.VMEM((LANES, 2 * t), F32)],
        compiler_params=_params(("parallel", "parallel", "arbitrary")),
        name="fox_attention",
    )(q_t, k, f_aug, v_t)


def _t5_bucket(dist):
    max_exact = T5_NUM_BUCKETS // 2
    d = np.maximum(dist, 1).astype(np.float32)
    large = max_exact + (np.log(d / max_exact) / np.log(T5_MAX_DISTANCE / max_exact)
                         * (T5_NUM_BUCKETS - max_exact)).astype(np.int32)
    large = np.minimum(large, T5_NUM_BUCKETS - 1)
    return np.where(dist < max_exact, dist, large).astype(np.int32)


def _dilated_bias(rel_bias):
    blk = DIL_BLOCK
    period = 3 * blk
    m = np.arange(period)
    rel = np.where(m < 2 * blk, blk - m, blk - (m - period))
    band = (rel >= 0) & (rel <= blk)
    onehot = np.zeros((len(DILATIONS), period, T5_NUM_BUCKETS), np.float32)
    for bi, dil in enumerate(DILATIONS):
        onehot[bi, m, _t5_bucket(np.clip(rel, 0, blk) * dil)] = 1.0
    w = jnp.einsum('bmk,kh->bhm', jnp.asarray(onehot), rel_bias.astype(F32),
                   precision=lax.Precision.HIGHEST)
    w = jnp.where(jnp.asarray(band), w, NEG)
    flat = jnp.tile(w, (1, 1, blk))[:, :, :blk * (period - 1)]
    table = flat.reshape(len(DILATIONS), N_HEADS, blk, period - 1)[..., :2 * blk]
    return table.reshape(len(DILATIONS), HEAD_PAIRS, 2 * blk, 2 * blk)


def _dil_kernel(q_ref, k_ref, v_ref, bias_ref, o_ref, qf, kf, vf, ob0, ob1, ob2, ls0, ls1, ls2, *, s_len):
    blk = DIL_BLOCK
    qf[...] = q_ref[0].astype(F32)
    kf[...] = k_ref[0].astype(F32)
    vf[...] = v_ref[0].astype(F32)
    col = lax.broadcasted_iota(jnp.int32, (2 * blk, 2 * blk), 1)

    for bi, (dil, ob, ls) in enumerate(zip(DILATIONS, (ob0, ob1, ob2), (ls0, ls1, ls2))):
        span = blk * dil
        nb = s_len // span

        def rows(start, dil=dil):
            return pl.ds(start, blk) if dil == 1 else pl.ds(start, blk, stride=dil)

        def block(tix, carry, bi=bi, span=span, nb=nb, ob=ob, ls=ls, rows=rows):
            n = tix % nb
            start = n * span + tix // nb
            prev = jnp.maximum(start - span, 0)
            q2 = _stack_heads(qf[rows(start), :].astype(BF16))
            kk = jnp.concatenate([kf[rows(prev), :], kf[rows(start), :]], axis=0).astype(BF16)
            vv = jnp.concatenate([vf[rows(prev), :], vf[rows(start), :]], axis=0).astype(BF16)
            s = lax.dot_general(q2, kk, (((1,), (1,)), ((), ())), preferred_element_type=F32)
            s = s + bias_ref[bi, 0]
            s = jnp.where((col < blk) & (n == 0), NEG, s)
            m = jnp.max(s, axis=1, keepdims=True)
            p = jnp.exp(s - m)
            l = jnp.sum(p, axis=1, keepdims=True)
            o2 = jnp.dot(p.astype(BF16), vv, preferred_element_type=F32) / l
            lse = jnp.broadcast_to(m + jnp.log(l), (2 * blk, LANES))
            ob[rows(start), :] = _unstack_heads(o2)
            ls[rows(start), :] = _unstack_heads(lse)
            return carry

        def blocks(g, carry, block=block):
            for u in range(DIL_UNROLL):
                block(g * DIL_UNROLL + u, carry)
            return carry

        lax.fori_loop(0, dil * nb // DIL_UNROLL, blocks, 0)

    chunk = 512
    for c in range(s_len // chunk):
        r = pl.ds(c * chunk, chunk)
        l0, l1, l2 = ls0[r, :], ls1[r, :], ls2[r, :]
        mx = jnp.maximum(jnp.maximum(l0, l1), l2)
        e0, e1, e2 = jnp.exp(l0 - mx), jnp.exp(l1 - mx), jnp.exp(l2 - mx)
        o_ref[0, r, :] = (e0 * ob0[r, :] + e1 * ob1[r, :] + e2 * ob2[r, :]) / (e0 + e1 + e2)


def _dilated_attention(qkv, bias):
    bsz, s, _ = qkv.shape
    col = lambda off: pl.BlockSpec((1, s, LANES), lambda b, h: (b, 0, off + h))
    buf = pltpu.VMEM((s, LANES), F32)
    return pl.pallas_call(
        functools.partial(_dil_kernel, s_len=s),
        grid=(bsz, HEAD_PAIRS),
        in_specs=[col(0), col(HEAD_PAIRS), col(2 * HEAD_PAIRS),
                  pl.BlockSpec((len(DILATIONS), 1, 2 * DIL_BLOCK, 2 * DIL_BLOCK), lambda b, h: (0, h, 0, 0))],
        out_specs=pl.BlockSpec((1, s, LANES), lambda b, h: (b, 0, h)),
        out_shape=jax.ShapeDtypeStruct((bsz, s, WIDTH), F32),
        scratch_shapes=[buf] * 9,
        compiler_params=_params(("parallel", "arbitrary")),
        name="dilated_attention",
    )(qkv, qkv, qkv, bias)


def _bf16_bits(x):
    return pltpu.bitcast(x.astype(BF16).astype(F32), U32)


def _to_row_tiles(dst_ref, x):
    rows, d = x.shape
    for c in range(TILE_ROWS):
        lo = _bf16_bits(x[:, c * LANES:(c + 1) * LANES]) >> 16
        hi = _bf16_bits(x[:, d // 2 + c * LANES:d // 2 + (c + 1) * LANES])
        dst_ref[pl.ds(c, rows, stride=TILE_ROWS), :] = lo | hi


def _from_row_tiles(src_ref, rows, base=0):
    lo, hi = [], []
    for c in range(TILE_ROWS):
        w = src_ref[pl.ds(base + c, rows, stride=TILE_ROWS), :]
        lo.append(pltpu.bitcast(w << 16, F32))
        hi.append(pltpu.bitcast(w & jnp.uint32(0xFFFF0000), F32))
    return lo + hi


def _outproj_kernel(yf_ref, yd_ref, x_ref, g1_ref, sc_ref, sh_ref, g2_ref, gf_ref, gd_ref, wo1_ref, wo2_ref,
                    wrh_ref, wrl_ref, wsg_ref, wsu_ref, wsd_ref, xp_ref, h3_ref, lg_ref):
    nf = (_rms(yf_ref[0]) * gf_ref[...]).astype(BF16)
    nd = (_rms(yd_ref[0]) * gd_ref[...]).astype(BF16)
    mix = (jnp.dot(nf, wo1_ref[...], preferred_element_type=F32)
           + jnp.dot(nd, wo2_ref[...], preferred_element_type=F32))
    x1 = x_ref[0] + g1_ref[0] * mix
    h2 = _rms(x1) * (1.0 + sc_ref[0]) + sh_ref[0]
    hb = h2.astype(BF16)
    hl = (h2 - hb.astype(F32)).astype(BF16)
    lg_ref[...] = _dot_nt(wrh_ref[...], hb) + _dot_nt(wrh_ref[...], hl) + _dot_nt(wrl_ref[...], hb)
    act = _silu(jnp.dot(hb, wsg_ref[...], preferred_element_type=F32)) * jnp.dot(
        hb, wsu_ref[...], preferred_element_type=F32)
    shared = jnp.dot(act.astype(BF16), wsd_ref[...], preferred_element_type=F32)
    xp_ref[0] = x1 + g2_ref[0] * shared
    _to_row_tiles(h3_ref, h2)


def _out_projection(y_fox, y_dil, x, gate1, scale2, shift2, gate2, g_fox, g_dil, wo1, wo2, wr_hi, wr_lo,
                    wsg, wsu, wsd):
    bsz, s, d = x.shape
    tm = ROW_TILE
    nt = s // tm
    vec = pl.BlockSpec((1, 1, d), lambda b, i: (b, 0, 0))
    full = lambda w: pl.BlockSpec(w.shape, lambda b, i: (0,) * w.ndim)
    row = lambda n: pl.BlockSpec((1, tm, n), lambda b, i: (b, i, 0))
    return pl.pallas_call(
        _outproj_kernel,
        grid=(bsz, nt),
        in_specs=[row(WIDTH), row(WIDTH), row(d), vec, vec, vec, vec, full(g_fox), full(g_dil), full(wo1),
                  full(wo2), full(wr_hi), full(wr_lo), full(wsg), full(wsu), full(wsd)],
        out_specs=[row(d),
                   pl.BlockSpec((tm * TILE_ROWS, LANES), lambda b, i: (b * nt + i, 0)),
                   pl.BlockSpec((N_EXPERTS, tm), lambda b, i: (0, b * nt + i))],
        out_shape=[jax.ShapeDtypeStruct((bsz, s, d), F32),
                   jax.ShapeDtypeStruct((bsz * s * TILE_ROWS, LANES), U32),
                   jax.ShapeDtypeStruct((N_EXPERTS, bsz * s), F32)],
        compiler_params=_params(("parallel", "arbitrary")),
        name="outproj_norm2_router_shared",
    )(y_fox, y_dil, x, gate1, scale2, shift2, gate2, g_fox, g_dil, wo1, wo2, wr_hi, wr_lo, wsg, wsu, wsd)


def _first_argmax(v, row, size):
    m = jnp.max(v, axis=0, keepdims=True)
    return m, jnp.min(jnp.where(v == m, row, size), axis=0, keepdims=True)


def _route_kernel(lg_ref, rb_ref, tri_ref, e_ref, w_ref, rk_ref, cnt_ref, cnt_s):
    @pl.when(pl.program_id(0) == 0)
    def _():
        cnt_s[...] = jnp.zeros(cnt_s.shape, F32)

    t = lg_ref.shape[1]
    gsz = N_EXPERTS // N_GROUPS
    scores = jax.nn.sigmoid(lg_ref[...])
    sel = scores + rb_ref[...]
    row_g = lax.broadcasted_iota(jnp.int32, (gsz, t), 0)
    grp = []
    for g in range(N_GROUPS):
        v = sel[g * gsz:(g + 1) * gsz]
        m1, i1 = _first_argmax(v, row_g, gsz)
        m2 = jnp.max(jnp.where(row_g == i1, -jnp.inf, v), axis=0, keepdims=True)
        grp.append(m1 + m2)
    gv = jnp.concatenate(grp, axis=0)
    row8 = lax.broadcasted_iota(jnp.int32, (N_GROUPS, t), 0)
    pen = jnp.full((N_GROUPS, t), -jnp.inf, F32)
    for _ in range(TOP_K_GROUPS):
        _, ix = _first_argmax(gv, row8, N_GROUPS)
        pen = jnp.where(row8 == ix, 0.0, pen)
        gv = jnp.where(row8 == ix, -jnp.inf, gv)
    selm = jnp.concatenate([sel[g * gsz:(g + 1) * gsz] + pen[g:g + 1] for g in range(N_GROUPS)], axis=0)

    row = lax.broadcasted_iota(jnp.int32, (N_EXPERTS, t), 0)
    v = selm
    idxs, scs = [], []
    for _ in range(TOP_K):
        _, ix = _first_argmax(v, row, N_EXPERTS)
        hit = row == ix
        idxs.append(ix)
        scs.append(jnp.sum(jnp.where(hit, scores, 0.0), axis=0, keepdims=True))
        v = jnp.where(hit, -jnp.inf, v)
    chosen = jnp.where(v != selm, 1.0, 0.0)
    before = jnp.dot(chosen.astype(BF16), tri_ref[...], preferred_element_type=F32) + cnt_s[...]
    rks = [jnp.sum(jnp.where(row == ix, before, 0.0), axis=0, keepdims=True) for ix in idxs]
    sc = jnp.concatenate(scs, axis=0)
    e_ref[...] = jnp.concatenate(idxs, axis=0)
    w_ref[...] = sc / jnp.sum(sc, axis=0, keepdims=True) * ROUTED_SCALE
    rk_ref[...] = jnp.concatenate(rks, axis=0).astype(jnp.int32)
    cnt_s[...] = cnt_s[...] + jnp.sum(chosen, axis=1, keepdims=True)
    cnt_ref[...] = cnt_s[...]


def _route(logits_t, router_bias):
    e, n = logits_t.shape
    t = ROUTE_TILE
    tri = jnp.triu(jnp.ones((t, t), BF16), k=1)
    tile = pl.BlockSpec((TOP_K, t), lambda i: (0, i))
    return pl.pallas_call(
        _route_kernel,
        grid=(n // t,),
        in_specs=[pl.BlockSpec((e, t), lambda i: (0, i)),
                  pl.BlockSpec((e, 1), lambda i: (0, 0)),
                  pl.BlockSpec((t, t), lambda i: (0, 0))],
        out_specs=[tile, tile, tile, pl.BlockSpec((e, 1), lambda i: (0, 0))],
        out_shape=[jax.ShapeDtypeStruct((TOP_K, n), jnp.int32), jax.ShapeDtypeStruct((TOP_K, n), F32),
                   jax.ShapeDtypeStruct((TOP_K, n), jnp.int32), jax.ShapeDtypeStruct((e, 1), F32)],
        scratch_shapes=[pltpu.VMEM((e, 1), F32)],
        compiler_params=_params(("arbitrary",)),
        name="route_topk_rank",
    )(logits_t, router_bias.reshape(e, 1).astype(F32), tri)


def _dest_kernel(e_ref, rk_ref, ps_ref, d_ref):
    t = e_ref.shape[1]
    row = lax.broadcasted_iota(jnp.int32, (N_EXPERTS, t), 0)
    ps = ps_ref[...]
    base = [jnp.sum(jnp.where(row == e_ref[k:k + 1, :], ps, 0.0), axis=0, keepdims=True) for k in range(TOP_K)]
    d_ref[...] = jnp.concatenate(base, axis=0).astype(jnp.int32) + rk_ref[...]


def _dest_rows(eidx, rank, pstart):
    _, n = eidx.shape
    t = ROUTE_TILE
    tile = pl.BlockSpec((TOP_K, t), lambda i: (0, i))
    return pl.pallas_call(
        _dest_kernel,
        grid=(n // t,),
        in_specs=[tile, tile, pl.BlockSpec((N_EXPERTS, 1), lambda i: (0, 0))],
        out_specs=tile,
        out_shape=jax.ShapeDtypeStruct((TOP_K, n), jnp.int32),
        compiler_params=_params(("parallel",)),
        name="dest_rows",
    )(eidx, rank, pstart.astype(F32).reshape(N_EXPERTS, 1))


def _block_plan(counts, nblk):
    counts = counts.reshape(N_EXPERTS).astype(jnp.int32)
    padded = (counts + FFN_BLOCK - 1) // FFN_BLOCK * FFN_BLOCK
    pend = jnp.cumsum(padded).astype(jnp.int32)
    pstart = pend - padded
    first_row = jnp.arange(nblk, dtype=jnp.int32) * FFN_BLOCK
    block_e = jnp.minimum(jnp.sum(pend[None, :] <= first_row[:, None], axis=1), N_EXPERTS - 1).astype(jnp.int32)
    nused = (pend[-1:] // FFN_BLOCK).astype(jnp.int32)
    return pstart, pstart + counts, pend, block_e, nused


def _tile_copy(src_ref, src_row, dst_ref, dst_row, sem):
    return pltpu.make_async_copy(src_ref.at[pl.ds(pl.multiple_of(src_row * TILE_ROWS, TILE_ROWS), TILE_ROWS), :],
                                 dst_ref.at[pl.ds(pl.multiple_of(dst_row * TILE_ROWS, TILE_ROWS), TILE_ROWS), :],
                                 sem)


def _scatter_kernel(cend_ref, pend_ref, dst_ref, h3_ref, z_ref, xs_hbm, st0, st1, sem, zsem, *, n_assign):
    i = pl.program_id(0)
    nsteps = pl.num_programs(0)
    tm = dst_ref.shape[1]
    stage = (st0, st1)

    def zero_copy(r):
        return pltpu.make_async_copy(z_ref, xs_hbm.at[pl.ds(pl.multiple_of(r * TILE_ROWS, TILE_ROWS), TILE_ROWS), :],
                                     zsem)

    def wait_slot(slot):
        for _ in range(TOP_K):
            pltpu.make_async_copy(stage[slot], xs_hbm.at[pl.ds(0, tm * TILE_ROWS), :], sem.at[slot]).wait()

    @pl.when(i == 0)
    def _():
        def per_expert(e, carry):
            def per_row(r, c):
                zero_copy(r).start()
                return c
            return lax.fori_loop(cend_ref[e], pend_ref[e], per_row, carry)
        lax.fori_loop(0, N_EXPERTS, per_expert, 0)

    for slot in range(2):
        @pl.when(i % 2 == slot)
        def _(slot=slot):
            @pl.when(i >= 2)
            def _():
                wait_slot(slot)
            stage[slot][...] = h3_ref[...]
            for k in range(TOP_K):
                def body(jj, carry, k=k):
                    for u in range(ISSUE_UNROLL):
                        j = jj * ISSUE_UNROLL + u
                        _tile_copy(stage[slot], j, xs_hbm, dst_ref[k, j], sem.at[slot]).start()
                    return carry
                lax.fori_loop(0, tm // ISSUE_UNROLL, body, 0)

    @pl.when(i == nsteps - 1)
    def _():
        wait_slot(0)
        wait_slot(1)

        def wait_zero(r, carry):
            zero_copy(r).wait()
            return carry
        lax.fori_loop(0, pend_ref[N_EXPERTS - 1] - n_assign, wait_zero, 0)


def _dispatch_rows(h3, dest, cend, pend, rows):
    _, n = dest.shape
    tm = SCATTER_TILE
    assert n // tm >= 2
    stage = pltpu.VMEM((tm * TILE_ROWS, LANES), U32)
    grid_spec = pltpu.PrefetchScalarGridSpec(
        num_scalar_prefetch=2,
        grid=(n // tm,),
        in_specs=[pl.BlockSpec((TOP_K, tm), lambda i, ce, pe: (0, i), memory_space=pltpu.SMEM),
                  pl.BlockSpec((tm * TILE_ROWS, LANES), lambda i, ce, pe: (i, 0)),
                  pl.BlockSpec((TILE_ROWS, LANES), lambda i, ce, pe: (0, 0))],
        out_specs=pl.BlockSpec(memory_space=pl.ANY),
        scratch_shapes=[stage, stage, pltpu.SemaphoreType.DMA((2,)), pltpu.SemaphoreType.DMA],
    )
    return pl.pallas_call(
        functools.partial(_scatter_kernel, n_assign=n * TOP_K),
        grid_spec=grid_spec,
        out_shape=jax.ShapeDtypeStruct((rows * TILE_ROWS, LANES), U32),
        compiler_params=_params(("arbitrary",)),
        name="dispatch_scatter",
    )(cend, pend, dest, h3, jnp.zeros((TILE_ROWS, LANES), U32))


def _ffn_kernel(be_ref, nu_ref, x_ref, wg_ref, wu_ref, wd_ref, y_ref, wgb, wub, wdb):
    i = pl.program_id(0)
    nused = nu_ref[0]

    @pl.when((i == 0) | (be_ref[i] != be_ref[jnp.maximum(i - 1, 0)]))
    def _():
        wgb[...] = wg_ref[0].astype(BF16)
        wub[...] = wu_ref[0].astype(BF16)
        wdb[...] = wd_ref[0].astype(BF16)

    @pl.when(i < nused)
    def _():
        x = jnp.concatenate([c.astype(BF16) for c in _from_row_tiles(x_ref, FFN_BLOCK)], axis=1)
        act = _silu(jnp.dot(x, wgb[...], preferred_element_type=F32)) * jnp.dot(
            x, wub[...], preferred_element_type=F32)
        _to_row_tiles(y_ref, jnp.dot(act.astype(BF16), wdb[...], preferred_element_type=F32))

    @pl.when(i >= nused)
    def _():
        y_ref[...] = jnp.zeros(y_ref.shape, U32)


def _routed_experts(xs, block_e, nused, w_gate, w_up, w_down):
    rows = xs.shape[0] // TILE_ROWS
    nblk = rows // FFN_BLOCK
    _, d, hid = w_gate.shape
    grid_spec = pltpu.PrefetchScalarGridSpec(
        num_scalar_prefetch=2,
        grid=(nblk,),
        in_specs=[pl.BlockSpec((FFN_BLOCK * TILE_ROWS, LANES), lambda i, be, nu: (jnp.minimum(i, nu[0] - 1), 0)),
                  pl.BlockSpec((1, d, hid), lambda i, be, nu: (be[i], 0, 0)),
                  pl.BlockSpec((1, d, hid), lambda i, be, nu: (be[i], 0, 0)),
                  pl.BlockSpec((1, hid, d), lambda i, be, nu: (be[i], 0, 0))],
        out_specs=pl.BlockSpec((FFN_BLOCK * TILE_ROWS, LANES), lambda i, be, nu: (i, 0)),
        scratch_shapes=[pltpu.VMEM((d, hid), BF16), pltpu.VMEM((d, hid), BF16), pltpu.VMEM((hid, d), BF16)],
    )
    return pl.pallas_call(
        _ffn_kernel,
        grid_spec=grid_spec,
        out_shape=jax.ShapeDtypeStruct((rows * TILE_ROWS, LANES), U32),
        compiler_params=_params(("arbitrary",)),
        name="routed_experts",
    )(block_e, nused, xs, w_gate, w_up, w_down)


def _gather_start(idx_ref, src_hbm, dst_vmem, sem):
    nk, tm = idx_ref.shape
    for k in range(nk):
        def body(jj, carry, k=k):
            for u in range(ISSUE_UNROLL):
                j = jj * ISSUE_UNROLL + u
                _tile_copy(src_hbm, idx_ref[k, j], dst_vmem, k * tm + j, sem).start()
            return carry
        lax.fori_loop(0, tm // ISSUE_UNROLL, body, 0)


def _gather_wait(src_hbm, dst_vmem, sem):
    pltpu.make_async_copy(src_hbm.at[pl.ds(0, dst_vmem.shape[0]), :], dst_vmem, sem).wait()


def _combine_kernel(dst_ref, dstn_ref, wt_ref, xp_ref, g2_ref, gfin_ref, ys_hbm, o_ref, yb0, yb1, sem, *, nsteps):
    i = pl.program_id(0)
    tm = COMBINE_TILE
    bufs = (yb0, yb1)

    @pl.when(i == 0)
    def _():
        _gather_start(dst_ref, ys_hbm, yb0, sem.at[0])

    for slot in range(2):
        @pl.when((i + 1 < nsteps) & ((i + 1) % 2 == slot))
        def _(slot=slot):
            _gather_start(dstn_ref, ys_hbm, bufs[slot], sem.at[slot])

    for slot in range(2):
        @pl.when(i % 2 == slot)
        def _(slot=slot):
            _gather_wait(ys_hbm, bufs[slot], sem.at[slot])
            wt = wt_ref[...]
            cols = None
            for k in range(TOP_K):
                part = [c * wt[:, k:k + 1] for c in _from_row_tiles(bufs[slot], tm, base=k * tm * TILE_ROWS)]
                cols = part if cols is None else [a + b for a, b in zip(cols, part)]
            routed = jnp.concatenate(cols, axis=1)
            o_ref[0] = _rms(xp_ref[0] + g2_ref[0] * routed) * gfin_ref[...]


def _combine(ys, dest, wts_t, xp, gate2, g_final):
    bsz, s, d = xp.shape
    tm = COMBINE_TILE
    nt = s // tm
    nsteps = bsz * nt
    vec = pl.BlockSpec((1, 1, d), lambda i: (i // nt, 0, 0))
    return pl.pallas_call(
        functools.partial(_combine_kernel, nsteps=nsteps),
        grid=(nsteps,),
        in_specs=[pl.BlockSpec((TOP_K, tm), lambda i: (0, i), memory_space=pltpu.SMEM),
                  pl.BlockSpec((TOP_K, tm), lambda i: (0, jnp.minimum(i + 1, nsteps - 1)),
                               memory_space=pltpu.SMEM),
                  pl.BlockSpec((tm, TOP_K), lambda i: (i, 0)),
                  pl.BlockSpec((1, tm, d), lambda i: (i // nt, i % nt, 0)),
                  vec,
                  pl.BlockSpec((1, d), lambda i: (0, 0)),
                  pl.BlockSpec(memory_space=pl.ANY)],
        out_specs=pl.BlockSpec((1, tm, d), lambda i: (i // nt, i % nt, 0)),
        out_shape=jax.ShapeDtypeStruct((bsz, s, d), F32),
        scratch_shapes=[pltpu.VMEM((tm * TOP_K * TILE_ROWS, LANES), U32),
                        pltpu.VMEM((tm * TOP_K * TILE_ROWS, LANES), U32),
                        pltpu.SemaphoreType.DMA((2,))],
        compiler_params=_params(("arbitrary",)),
        name="combine_final_norm",
    )(dest, dest, wts_t, xp, gate2, g_final.reshape(1, d), ys)


def kernel(x, c, w_in, b_forget, g_fox_out, g_dil_out, w_out, w_ada, b_ada, w_router, router_bias,
           w_exp_gate, w_exp_up, w_exp_down, w_sh_gate, w_sh_up, w_sh_down, rel_bias, g_final):
    bsz, s, d = x.shape
    depth = w_in.shape[0]
    assert depth == 1 and d == 2 * TILE_ROWS * LANES and s % (DIL_BLOCK * DILATIONS[-1]) == 0
    l = 0
    mod = _modulation(c, w_ada[l], b_ada[l])
    shift1, scale1, gate1, shift2, scale2, gate2 = [m[:, None, :] for m in jnp.split(mod, 6, axis=-1)]

    qscale = HEAD_DIM ** -0.5
    o3 = 3 * WIDTH
    w = w_in[l]
    wq_t = (w[:, :WIDTH] * qscale).T.astype(BF16)
    wk = w[:, WIDTH:2 * WIDTH].astype(BF16)
    wv_t = w[:, 2 * WIDTH:o3].T.astype(BF16)
    w_flog = jnp.pad(w[:, o3:o3 + N_HEADS], ((0, 0), (0, LANES - N_HEADS))).astype(BF16)
    wd0 = o3 + N_HEADS
    w_dil = jnp.concatenate([w[:, wd0:wd0 + WIDTH] * qscale, w[:, wd0 + WIDTH:]], axis=1).astype(BF16)

    q_t, k_f, v_t, qkv_d, flog = _in_projection(x, scale1, shift1, wq_t, wk, wv_t, w_dil, w_flog)
    y_fox = _fox_attention(q_t, k_f, _forget_cumsum(flog, b_forget[l]), v_t)
    y_dil = _dilated_attention(qkv_d, _dilated_bias(rel_bias))

    wr = w_router[l].T
    wr_hi = wr.astype(BF16)
    wr_lo = (wr - wr_hi.astype(F32)).astype(BF16)
    wo = w_out[l].astype(BF16)
    xp, h3, logits_t = _out_projection(
        y_fox, y_dil, x, gate1, scale2, shift2, gate2, g_fox_out[l].reshape(1, WIDTH),
        g_dil_out[l].reshape(1, WIDTH), wo[:WIDTH], wo[WIDTH:], wr_hi, wr_lo,
        w_sh_gate[l].astype(BF16), w_sh_up[l].astype(BF16), w_sh_down[l].astype(BF16))

    eidx, wts, rank, counts = _route(logits_t, router_bias[l])
    rows = bsz * s * TOP_K + N_EXPERTS * FFN_BLOCK
    pstart, cend, pend, block_e, nused = _block_plan(counts, rows // FFN_BLOCK)
    dest = _dest_rows(eidx, rank, pstart)
    xs = _dispatch_rows(h3, dest, cend, pend, rows)
    ys = _routed_experts(xs, block_e, nused, w_exp_gate[l], w_exp_up[l], w_exp_down[l])
    return _combine(ys, dest, wts.T, xp, gate2, g_final)
```

```python
import functools

import numpy as np
import jax
import jax.numpy as jnp
from jax import lax
from jax.experimental import pallas as pl
from jax.experimental.pallas import tpu as pltpu

F32 = jnp.float32
BF16 = jnp.bfloat16

HEAD_DIM = 64
N_HEADS = 8
WIDTH = N_HEADS * HEAD_DIM
LANES = 128
TILE_ROWS = 4
U32 = jnp.uint32
DIL_UNROLL = 4
HEAD_PAIRS = WIDTH // LANES
DIL_BLOCK = 128
DILATIONS = (1, 4, 16)
T5_NUM_BUCKETS = 32
T5_MAX_DISTANCE = 2048
N_EXPERTS = 256
TOP_K = 8
N_GROUPS = 8
TOP_K_GROUPS = 4
ROUTED_SCALE = 2.5
EPS = 1e-6
NEG = -1e30
LOG2E = float(np.log2(np.e))
VMEM_LIMIT = 56 * 1024 * 1024

ROW_TILE = 512
FOX_TILE = 256
FFN_BLOCK = 256
FFN_PER_STEP = 2
COMBINE_TILE = 128
SCATTER_TILE = 128
ROUTE_TILE = 512
ISSUE_UNROLL = 8


def _params(semantics):
    return pltpu.CompilerParams(dimension_semantics=semantics, vmem_limit_bytes=VMEM_LIMIT)


def _rms(x):
    return x * lax.rsqrt(jnp.mean(x * x, axis=-1, keepdims=True) + EPS)


def _silu(x):
    return x * jax.nn.sigmoid(x)


def _dot_nt(a, b):
    return lax.dot_general(a, b, (((1,), (1,)), ((), ())), preferred_element_type=F32)


def _mod_kernel(c_ref, w_ref, b_ref, o_ref):
    o_ref[...] = jnp.dot(_silu(c_ref[...]), w_ref[...], precision=lax.Precision.HIGHEST,
                         preferred_element_type=F32) + b_ref[...]


def _modulation(c, w_ada, b_ada):
    bsz, d = c.shape
    n = w_ada.shape[1]
    tn = 1536
    return pl.pallas_call(
        _mod_kernel,
        grid=(n // tn,),
        in_specs=[pl.BlockSpec((bsz, d), lambda j: (0, 0)),
                  pl.BlockSpec((d, tn), lambda j: (0, j)),
                  pl.BlockSpec((1, tn), lambda j: (0, j))],
        out_specs=pl.BlockSpec((bsz, tn), lambda j: (0, j)),
        out_shape=jax.ShapeDtypeStruct((bsz, n), F32),
        compiler_params=_params(("arbitrary",)),
        name="adaln_mod",
    )(c, w_ada, b_ada.reshape(1, n))


def _inproj_kernel(x_ref, sc_ref, sh_ref, wq_ref, wk_ref, wv_ref, wd_ref, wl_ref,
                   oq_ref, ok_ref, ov_ref, od_ref, ol_ref):
    h = _rms(x_ref[0]) * (1.0 + sc_ref[0]) + sh_ref[0]
    hb = h.astype(BF16)
    oq_ref[0] = _dot_nt(wq_ref[...], hb).astype(BF16)
    ov_ref[0] = _dot_nt(wv_ref[...], hb).astype(BF16)
    ok_ref[0] = jnp.dot(hb, wk_ref[...], preferred_element_type=F32).astype(BF16)
    od_ref[0] = jnp.dot(hb, wd_ref[...], preferred_element_type=F32).astype(BF16)
    ol_ref[0] = jnp.dot(hb, wl_ref[...], preferred_element_type=F32)


def _in_projection(x, scale1, shift1, wq_t, wk, wv_t, w_dil, w_flog):
    bsz, s, d = x.shape
    tm = ROW_TILE
    vec = pl.BlockSpec((1, 1, d), lambda b, i: (b, 0, 0))
    full = lambda w: pl.BlockSpec(w.shape, lambda b, i: (0, 0))
    row = lambda n: pl.BlockSpec((1, tm, n), lambda b, i: (b, i, 0))
    col = pl.BlockSpec((1, WIDTH, tm), lambda b, i: (b, 0, i))
    return pl.pallas_call(
        _inproj_kernel,
        grid=(bsz, s // tm),
        in_specs=[row(d), vec, vec, full(wq_t), full(wk), full(wv_t), full(w_dil), full(w_flog)],
        out_specs=[col, row(WIDTH), col, row(3 * WIDTH), row(LANES)],
        out_shape=[jax.ShapeDtypeStruct((bsz, WIDTH, s), BF16),
                   jax.ShapeDtypeStruct((bsz, s, WIDTH), BF16),
                   jax.ShapeDtypeStruct((bsz, WIDTH, s), BF16),
                   jax.ShapeDtypeStruct((bsz, s, 3 * WIDTH), BF16),
                   jax.ShapeDtypeStruct((bsz, s, LANES), F32)],
        compiler_params=_params(("parallel", "arbitrary")),
        name="norm1_inproj",
    )(x, scale1, shift1, wq_t, wk, wv_t, w_dil, w_flog)


FORGET_PARTS = 3


def _split_bf16(x):
    parts = []
    for _ in range(FORGET_PARTS):
        p = x.astype(BF16)
        parts.append(p)
        x = x - p.astype(F32)
    return parts


def _forget_kernel(f_ref, b_ref, tri_ref, place_ref, o_ref):
    blk = tri_ref.shape[0]
    carry = jnp.zeros((1, LANES), F32)
    for c in range(f_ref.shape[1] // blk):
        rows = pl.ds(c * blk, blk)
        z = f_ref[0, rows, :] + b_ref[...]
        ls = jnp.minimum(z, 0.0) - jnp.log1p(jnp.exp(-jnp.abs(z)))
        cum = carry
        for p in _split_bf16(ls):
            cum = cum + jnp.dot(tri_ref[...], p, preferred_element_type=F32)
        carry = cum[blk - 1:blk, :]
        out = None
        for n, p in enumerate(_split_bf16(cum * LOG2E)):
            d = jnp.dot(p, place_ref[n], preferred_element_type=F32)
            out = d if out is None else out + d
        o_ref[0, rows, :] = out.astype(BF16)


def _forget_cumsum(flog, b_forget):
    bsz, s, _ = flog.shape
    blk = 512
    place = np.zeros((FORGET_PARTS, LANES, WIDTH), np.float32)
    for h in range(N_HEADS):
        for n in range(FORGET_PARTS):
            place[n, h, (h // 2) * LANES + (h % 2) * HEAD_DIM + n] = 1.0
    return pl.pallas_call(
        _forget_kernel,
        grid=(bsz,),
        in_specs=[pl.BlockSpec((1, s, LANES), lambda b: (b, 0, 0)),
                  pl.BlockSpec((1, LANES), lambda b: (0, 0)),
                  pl.BlockSpec((blk, blk), lambda b: (0, 0)),
                  pl.BlockSpec(place.shape, lambda b: (0, 0, 0))],
        out_specs=pl.BlockSpec((1, s, WIDTH), lambda b: (b, 0, 0)),
        out_shape=jax.ShapeDtypeStruct((bsz, s, WIDTH), BF16),
        compiler_params=_params(("parallel",)),
        name="forget_cumsum",
    )(flog, jnp.pad(b_forget, (0, LANES - N_HEADS)).reshape(1, LANES),
      jnp.tril(jnp.ones((blk, blk), BF16)), jnp.asarray(place, BF16))


def _stack_heads(q):
    lane = lax.broadcasted_iota(jnp.int32, q.shape, 1)
    zero = jnp.zeros_like(q)
    return jnp.concatenate([jnp.where(lane < HEAD_DIM, q, zero), jnp.where(lane >= HEAD_DIM, q, zero)], axis=0)


def _unstack_heads(o2):
    rows = o2.shape[0] // 2
    lane = lax.broadcasted_iota(jnp.int32, (rows, LANES), 1)
    return jnp.where(lane < HEAD_DIM, o2[:rows], o2[rows:])


def _fox_kernel(q_ref, k_ref, fa_ref, v_ref, o_ref, m_s, l_s, acc_s, s_a, s_b, *, t):
    i = pl.program_id(2)
    q_t = q_ref[0]
    row = lax.broadcasted_iota(jnp.int32, (LANES, 2 * t), 0)
    col = lax.broadcasted_iota(jnp.int32, (LANES, 2 * t), 1)
    head_row = jnp.where(col < t, 0, HEAD_DIM)
    q2 = jnp.concatenate([q_t, q_t], axis=1)
    q2 = jnp.where((row >= head_row) & (row < head_row + HEAD_DIM), q2, jnp.zeros_like(q2))
    minus = jnp.where((row >= head_row) & (row < head_row + FORGET_PARTS), -1.0, 0.0).astype(BF16)
    qa = jnp.concatenate([q2, minus], axis=0)
    m_s[...] = jnp.full(m_s.shape, NEG, F32)
    l_s[...] = jnp.zeros(l_s.shape, F32)
    acc_s[...] = jnp.zeros(acc_s.shape, F32)

    def scores(ks, tk):
        kk = jnp.concatenate([k_ref[0, pl.ds(ks, tk), :], fa_ref[0, pl.ds(ks, tk), :]], axis=1)
        return jnp.dot(kk, qa, preferred_element_type=F32)

    def absorb(s, ks, tk):
        m_prev = m_s[...]
        m_new = jnp.maximum(m_prev, jnp.max(s, axis=0, keepdims=True))
        alpha = jnp.exp2(m_prev - m_new)
        p = jnp.exp2(s - m_new)
        l_s[...] = alpha * l_s[...] + jnp.sum(p, axis=0, keepdims=True)
        acc_s[...] = alpha * acc_s[...] + jnp.dot(v_ref[0, :, pl.ds(ks, tk)], p.astype(BF16),
                                                  preferred_element_type=F32)
        m_s[...] = m_new

    npair = i // 2

    def tile(j):
        return pl.multiple_of(j * 2 * t, 2 * t)

    @pl.when(npair > 0)
    def _():
        s_a[...] = scores(0, 2 * t)

        def body(jj, carry):
            j = 2 * jj
            s_b[...] = scores(tile(j + 1), 2 * t)
            absorb(s_a[...], tile(j), 2 * t)
            s_a[...] = scores(tile(jnp.minimum(j + 2, npair - 1)), 2 * t)
            absorb(s_b[...], tile(j + 1), 2 * t)
            return carry

        lax.fori_loop(0, npair // 2, body, 0)

        @pl.when(npair % 2 == 1)
        def _():
            absorb(s_a[...], tile(npair - 1), 2 * t)

    @pl.when(i % 2 == 1)
    def _():
        ks = pl.multiple_of(npair * 2 * t, 2 * t)
        absorb(scores(ks, t), ks, t)

    ks = pl.multiple_of(i * t, t)
    r = lax.broadcasted_iota(jnp.int32, (t, 2 * t), 0)
    c = lax.broadcasted_iota(jnp.int32, (t, 2 * t), 1)
    absorb(jnp.where(r <= jnp.where(c >= t, c - t, c), scores(ks, t), NEG), ks, t)
    o2 = acc_s[...] / l_s[...]
    o_t = jnp.where(lax.broadcasted_iota(jnp.int32, (LANES, t), 0) < HEAD_DIM, o2[:, :t], o2[:, t:])
    o_ref[0] = o_t.T


def _fox_attention(q_t, k, f_aug, v_t):
    bsz, s, _ = k.shape
    t = FOX_TILE
    keys = pl.BlockSpec((1, s, LANES), lambda b, h, i: (b, 0, h))
    return pl.pallas_call(
        functools.partial(_fox_kernel, t=t),
        grid=(bsz, HEAD_PAIRS, s // t),
        in_specs=[pl.BlockSpec((1, LANES, t), lambda b, h, i: (b, h, i)),
                  keys, keys,
                  pl.BlockSpec((1, LANES, s), lambda b, h, i: (b, h, 0))],
        out_specs=pl.BlockSpec((1, t, LANES), lambda b, h, i: (b, i, h)),
        out_shape=jax.ShapeDtypeStruct((bsz, s, WIDTH), F32),
        scratch_shapes=[pltpu.VMEM((1, 2 * t), F32), pltpu.VMEM((1, 2 * t), F32),
                        pltpu.VMEM((LANES, 2 * t), F32), pltpu.VMEM((2 * t, 2 * t), F32),
                        pltpu.VMEM((2 * t, 2 * t), F32)],
        compiler_params=_params(("parallel", "parallel", "arbitrary")),
        name="fox_attention",
    )(q_t, k, f_aug, v_t)


def _t5_bucket(dist):
    max_exact = T5_NUM_BUCKETS // 2
    d = np.maximum(dist, 1).astype(np.float32)
    large = max_exact + (np.log(d / max_exact) / np.log(T5_MAX_DISTANCE / max_exact)
                         * (T5_NUM_BUCKETS - max_exact)).astype(np.int32)
    large = np.minimum(large, T5_NUM_BUCKETS - 1)
    return np.where(dist < max_exact, dist, large).astype(np.int32)


def _dilated_bias(rel_bias):
    blk = DIL_BLOCK
    period = 3 * blk
    m = np.arange(period)
    rel = np.where(m < 2 * blk, blk - m, blk - (m - period))
    band = (rel >= 0) & (rel <= blk)
    onehot = np.zeros((len(DILATIONS), period, T5_NUM_BUCKETS), np.float32)
    for bi, dil in enumerate(DILATIONS):
        onehot[bi, m, _t5_bucket(np.clip(rel, 0, blk) * dil)] = 1.0
    w = jnp.einsum('bmk,kh->bhm', jnp.asarray(onehot), rel_bias.astype(F32),
                   precision=lax.Precision.HIGHEST)
    w = jnp.where(jnp.asarray(band), w, NEG)
    flat = jnp.tile(w, (1, 1, blk))[:, :, :blk * (period - 1)]
    table = flat.reshape(len(DILATIONS), N_HEADS, blk, period - 1)[..., :2 * blk]
    return table.reshape(len(DILATIONS), HEAD_PAIRS, 2 * blk, 2 * blk)


def _dil_kernel(q_ref, k_ref, v_ref, bias_ref, o_ref, qf, kf, vf, ob0, ob1, ob2, ls0, ls1, ls2, *, s_len):
    blk = DIL_BLOCK
    qf[...] = q_ref[0].astype(F32)
    kf[...] = k_ref[0].astype(F32)
    vf[...] = v_ref[0].astype(F32)
    col = lax.broadcasted_iota(jnp.int32, (2 * blk, 2 * blk), 1)

    for bi, (dil, ob, ls) in enumerate(zip(DILATIONS, (ob0, ob1, ob2), (ls0, ls1, ls2))):
        span = blk * dil
        nb = s_len // span

        def rows(start, dil=dil):
            return pl.ds(start, blk) if dil == 1 else pl.ds(start, blk, stride=dil)

        def block(tix, carry, bi=bi, span=span, nb=nb, ob=ob, ls=ls, rows=rows):
            n = tix % nb
            start = n * span + tix // nb
            prev = jnp.maximum(start - span, 0)
            q2 = _stack_heads(qf[rows(start), :].astype(BF16))
            kk = jnp.concatenate([kf[rows(prev), :], kf[rows(start), :]], axis=0).astype(BF16)
            vv = jnp.concatenate([vf[rows(prev), :], vf[rows(start), :]], axis=0).astype(BF16)
            s = lax.dot_general(q2, kk, (((1,), (1,)), ((), ())), preferred_element_type=F32)
            s = s + bias_ref[bi, 0]
            s = jnp.where((col < blk) & (n == 0), NEG, s)
            m = jnp.max(s, axis=1, keepdims=True)
            p = jnp.exp(s - m)
            l = jnp.sum(p, axis=1, keepdims=True)
            o2 = jnp.dot(p.astype(BF16), vv, preferred_element_type=F32) / l
            lse = jnp.broadcast_to(m + jnp.log(l), (2 * blk, LANES))
            ob[rows(start), :] = _unstack_heads(o2)
            ls[rows(start), :] = _unstack_heads(lse)
            return carry

        def blocks(g, carry, block=block):
            for u in range(DIL_UNROLL):
                block(g * DIL_UNROLL + u, carry)
            return carry

        lax.fori_loop(0, dil * nb // DIL_UNROLL, blocks, 0)

    chunk = 512
    for c in range(s_len // chunk):
        r = pl.ds(c * chunk, chunk)
        l0, l1, l2 = ls0[r, :], ls1[r, :], ls2[r, :]
        mx = jnp.maximum(jnp.maximum(l0, l1), l2)
        e0, e1, e2 = jnp.exp(l0 - mx), jnp.exp(l1 - mx), jnp.exp(l2 - mx)
        o_ref[0, r, :] = (e0 * ob0[r, :] + e1 * ob1[r, :] + e2 * ob2[r, :]) / (e0 + e1 + e2)


def _dilated_attention(qkv, bias):
    bsz, s, _ = qkv.shape
    col = lambda off: pl.BlockSpec((1, s, LANES), lambda b, h: (b, 0, off + h))
    buf = pltpu.VMEM((s, LANES), F32)
    return pl.pallas_call(
        functools.partial(_dil_kernel, s_len=s),
        grid=(bsz, HEAD_PAIRS),
        in_specs=[col(0), col(HEAD_PAIRS), col(2 * HEAD_PAIRS),
                  pl.BlockSpec((len(DILATIONS), 1, 2 * DIL_BLOCK, 2 * DIL_BLOCK), lambda b, h: (0, h, 0, 0))],
        out_specs=pl.BlockSpec((1, s, LANES), lambda b, h: (b, 0, h)),
        out_shape=jax.ShapeDtypeStruct((bsz, s, WIDTH), F32),
        scratch_shapes=[buf] * 9,
        compiler_params=_params(("parallel", "arbitrary")),
        name="dilated_attention",
    )(qkv, qkv, qkv, bias)


def _bf16_bits(x):
    return pltpu.bitcast(x.astype(BF16).astype(F32), U32)


def _to_row_tiles(dst_ref, x, base=0):
    rows, d = x.shape
    for c in range(TILE_ROWS):
        lo = _bf16_bits(x[:, c * LANES:(c + 1) * LANES]) >> 16
        hi = _bf16_bits(x[:, d // 2 + c * LANES:d // 2 + (c + 1) * LANES])
        dst_ref[pl.ds(base + c, rows, stride=TILE_ROWS), :] = lo | hi


def _from_row_tiles(src_ref, rows, base=0):
    lo, hi = [], []
    for c in range(TILE_ROWS):
        w = src_ref[pl.ds(base + c, rows, stride=TILE_ROWS), :]
        lo.append(pltpu.bitcast(w << 16, F32))
        hi.append(pltpu.bitcast(w & jnp.uint32(0xFFFF0000), F32))
    return lo + hi


def _outproj_kernel(yf_ref, yd_ref, x_ref, g1_ref, sc_ref, sh_ref, g2_ref, gf_ref, gd_ref, wo1_ref, wo2_ref,
                    wrh_ref, wrl_ref, wsg_ref, wsu_ref, wsd_ref, xp_ref, h3_ref, lg_ref):
    nf = (_rms(yf_ref[0]) * gf_ref[...]).astype(BF16)
    nd = (_rms(yd_ref[0]) * gd_ref[...]).astype(BF16)
    mix = (jnp.dot(nf, wo1_ref[...], preferred_element_type=F32)
           + jnp.dot(nd, wo2_ref[...], preferred_element_type=F32))
    x1 = x_ref[0] + g1_ref[0] * mix
    h2 = _rms(x1) * (1.0 + sc_ref[0]) + sh_ref[0]
    hb = h2.astype(BF16)
    hl = (h2 - hb.astype(F32)).astype(BF16)
    lg_ref[...] = _dot_nt(wrh_ref[...], hb) + _dot_nt(wrh_ref[...], hl) + _dot_nt(wrl_ref[...], hb)
    act = _silu(jnp.dot(hb, wsg_ref[...], preferred_element_type=F32)) * jnp.dot(
        hb, wsu_ref[...], preferred_element_type=F32)
    shared = jnp.dot(act.astype(BF16), wsd_ref[...], preferred_element_type=F32)
    xp_ref[0] = x1 + g2_ref[0] * shared
    _to_row_tiles(h3_ref, h2)


def _out_projection(y_fox, y_dil, x, gate1, scale2, shift2, gate2, g_fox, g_dil, wo1, wo2, wr_hi, wr_lo,
                    wsg, wsu, wsd):
    bsz, s, d = x.shape
    tm = ROW_TILE
    nt = s // tm
    vec = pl.BlockSpec((1, 1, d), lambda b, i: (b, 0, 0))
    full = lambda w: pl.BlockSpec(w.shape, lambda b, i: (0,) * w.ndim)
    row = lambda n: pl.BlockSpec((1, tm, n), lambda b, i: (b, i, 0))
    return pl.pallas_call(
        _outproj_kernel,
        grid=(bsz, nt),
        in_specs=[row(WIDTH), row(WIDTH), row(d), vec, vec, vec, vec, full(g_fox), full(g_dil), full(wo1),
                  full(wo2), full(wr_hi), full(wr_lo), full(wsg), full(wsu), full(wsd)],
        out_specs=[row(d),
                   pl.BlockSpec((tm * TILE_ROWS, LANES), lambda b, i: (b * nt + i, 0)),
                   pl.BlockSpec((N_EXPERTS, tm), lambda b, i: (0, b * nt + i))],
        out_shape=[jax.ShapeDtypeStruct((bsz, s, d), F32),
                   jax.ShapeDtypeStruct((bsz * s * TILE_ROWS, LANES), U32),
                   jax.ShapeDtypeStruct((N_EXPERTS, bsz * s), F32)],
        compiler_params=_params(("parallel", "arbitrary")),
        name="outproj_norm2_router_shared",
    )(y_fox, y_dil, x, gate1, scale2, shift2, gate2, g_fox, g_dil, wo1, wo2, wr_hi, wr_lo, wsg, wsu, wsd)


def _first_argmax(v, row, size):
    m = jnp.max(v, axis=0, keepdims=True)
    return m, jnp.min(jnp.where(v == m, row, size), axis=0, keepdims=True)


def _route_kernel(lg_ref, rb_ref, tri_ref, e_ref, w_ref, rk_ref, cnt_ref, cnt_s):
    @pl.when(pl.program_id(0) == 0)
    def _():
        cnt_s[...] = jnp.zeros(cnt_s.shape, F32)

    t = lg_ref.shape[1]
    gsz = N_EXPERTS // N_GROUPS
    scores = jax.nn.sigmoid(lg_ref[...])
    sel = scores + rb_ref[...]
    row_g = lax.broadcasted_iota(jnp.int32, (gsz, t), 0)
    grp = []
    for g in range(N_GROUPS):
        v = sel[g * gsz:(g + 1) * gsz]
        m1, i1 = _first_argmax(v, row_g, gsz)
        m2 = jnp.max(jnp.where(row_g == i1, -jnp.inf, v), axis=0, keepdims=True)
        grp.append(m1 + m2)
    gv = jnp.concatenate(grp, axis=0)
    row8 = lax.broadcasted_iota(jnp.int32, (N_GROUPS, t), 0)
    pen = jnp.full((N_GROUPS, t), -jnp.inf, F32)
    for _ in range(TOP_K_GROUPS):
        _, ix = _first_argmax(gv, row8, N_GROUPS)
        pen = jnp.where(row8 == ix, 0.0, pen)
        gv = jnp.where(row8 == ix, -jnp.inf, gv)
    selm = jnp.concatenate([sel[g * gsz:(g + 1) * gsz] + pen[g:g + 1] for g in range(N_GROUPS)], axis=0)

    row = lax.broadcasted_iota(jnp.int32, (N_EXPERTS, t), 0)
    v = selm
    idxs, scs = [], []
    for _ in range(TOP_K):
        _, ix = _first_argmax(v, row, N_EXPERTS)
        hit = row == ix
        idxs.append(ix)
        scs.append(jnp.sum(jnp.where(hit, scores, 0.0), axis=0, keepdims=True))
        v = jnp.where(hit, -jnp.inf, v)
    chosen = jnp.where(v != selm, 1.0, 0.0)
    before = jnp.dot(chosen.astype(BF16), tri_ref[...], preferred_element_type=F32) + cnt_s[...]
    rks = [jnp.sum(jnp.where(row == ix, before, 0.0), axis=0, keepdims=True) for ix in idxs]
    sc = jnp.concatenate(scs, axis=0)
    e_ref[...] = jnp.concatenate(idxs, axis=0)
    w_ref[...] = sc / jnp.sum(sc, axis=0, keepdims=True) * ROUTED_SCALE
    rk_ref[...] = jnp.concatenate(rks, axis=0).astype(jnp.int32)
    cnt_s[...] = cnt_s[...] + jnp.sum(chosen, axis=1, keepdims=True)
    cnt_ref[...] = cnt_s[...]


def _route(logits_t, router_bias):
    e, n = logits_t.shape
    t = ROUTE_TILE
    tri = jnp.triu(jnp.ones((t, t), BF16), k=1)
    tile = pl.BlockSpec((TOP_K, t), lambda i: (0, i))
    return pl.pallas_call(
        _route_kernel,
        grid=(n // t,),
        in_specs=[pl.BlockSpec((e, t), lambda i: (0, i)),
                  pl.BlockSpec((e, 1), lambda i: (0, 0)),
                  pl.BlockSpec((t, t), lambda i: (0, 0))],
        out_specs=[tile, tile, tile, pl.BlockSpec((e, 1), lambda i: (0, 0))],
        out_shape=[jax.ShapeDtypeStruct((TOP_K, n), jnp.int32), jax.ShapeDtypeStruct((TOP_K, n), F32),
                   jax.ShapeDtypeStruct((TOP_K, n), jnp.int32), jax.ShapeDtypeStruct((e, 1), F32)],
        scratch_shapes=[pltpu.VMEM((e, 1), F32)],
        compiler_params=_params(("arbitrary",)),
        name="route_topk_rank",
    )(logits_t, router_bias.reshape(e, 1).astype(F32), tri)


def _dest_kernel(e_ref, rk_ref, ps_ref, d_ref):
    t = e_ref.shape[1]
    row = lax.broadcasted_iota(jnp.int32, (N_EXPERTS, t), 0)
    ps = ps_ref[...]
    base = [jnp.sum(jnp.where(row == e_ref[k:k + 1, :], ps, 0.0), axis=0, keepdims=True) for k in range(TOP_K)]
    d_ref[...] = jnp.concatenate(base, axis=0).astype(jnp.int32) + rk_ref[...]


def _dest_rows(eidx, rank, pstart):
    _, n = eidx.shape
    t = ROUTE_TILE
    tile = pl.BlockSpec((TOP_K, t), lambda i: (0, i))
    return pl.pallas_call(
        _dest_kernel,
        grid=(n // t,),
        in_specs=[tile, tile, pl.BlockSpec((N_EXPERTS, 1), lambda i: (0, 0))],
        out_specs=tile,
        out_shape=jax.ShapeDtypeStruct((TOP_K, n), jnp.int32),
        compiler_params=_params(("parallel",)),
        name="dest_rows",
    )(eidx, rank, pstart.astype(F32).reshape(N_EXPERTS, 1))


def _block_plan(counts, nblk):
    counts = counts.reshape(N_EXPERTS).astype(jnp.int32)
    padded = (counts + FFN_BLOCK - 1) // FFN_BLOCK * FFN_BLOCK
    pend = jnp.cumsum(padded).astype(jnp.int32)
    pstart = pend - padded
    first_row = jnp.arange(nblk, dtype=jnp.int32) * FFN_BLOCK
    block_e = jnp.minimum(jnp.sum(pend[None, :] <= first_row[:, None], axis=1), N_EXPERTS - 1).astype(jnp.int32)
    nused = (pend[-1:] // FFN_BLOCK).astype(jnp.int32)
    return pstart, pstart + counts, pend, block_e, nused


def _tile_copy(src_ref, src_row, dst_ref, dst_row, sem):
    return pltpu.make_async_copy(src_ref.at[pl.ds(pl.multiple_of(src_row * TILE_ROWS, TILE_ROWS), TILE_ROWS), :],
                                 dst_ref.at[pl.ds(pl.multiple_of(dst_row * TILE_ROWS, TILE_ROWS), TILE_ROWS), :],
                                 sem)


def _scatter_kernel(cend_ref, pend_ref, dst_ref, h3_ref, z_ref, xs_hbm, st0, st1, sem, zsem, *, n_assign):
    i = pl.program_id(0)
    nsteps = pl.num_programs(0)
    tm = dst_ref.shape[1]
    stage = (st0, st1)

    def zero_copy(r):
        return pltpu.make_async_copy(z_ref, xs_hbm.at[pl.ds(pl.multiple_of(r * TILE_ROWS, TILE_ROWS), TILE_ROWS), :],
                                     zsem)

    def wait_slot(slot):
        for _ in range(TOP_K):
            pltpu.make_async_copy(stage[slot], xs_hbm.at[pl.ds(0, tm * TILE_ROWS), :], sem.at[slot]).wait()

    @pl.when(i == 0)
    def _():
        def per_expert(e, carry):
            def per_row(r, c):
                zero_copy(r).start()
                return c
            return lax.fori_loop(cend_ref[e], pend_ref[e], per_row, carry)
        lax.fori_loop(0, N_EXPERTS, per_expert, 0)

    for slot in range(2):
        @pl.when(i % 2 == slot)
        def _(slot=slot):
            @pl.when(i >= 2)
            def _():
                wait_slot(slot)
            stage[slot][...] = h3_ref[...]
            for k in range(TOP_K):
                def body(jj, carry, k=k):
                    for u in range(ISSUE_UNROLL):
                        j = jj * ISSUE_UNROLL + u
                        _tile_copy(stage[slot], j, xs_hbm, dst_ref[k, j], sem.at[slot]).start()
                    return carry
                lax.fori_loop(0, tm // ISSUE_UNROLL, body, 0)

    @pl.when(i == nsteps - 1)
    def _():
        wait_slot(0)
        wait_slot(1)

        def wait_zero(r, carry):
            zero_copy(r).wait()
            return carry
        lax.fori_loop(0, pend_ref[N_EXPERTS - 1] - n_assign, wait_zero, 0)


def _dispatch_rows(h3, dest, cend, pend, rows):
    _, n = dest.shape
    tm = SCATTER_TILE
    assert n // tm >= 2
    stage = pltpu.VMEM((tm * TILE_ROWS, LANES), U32)
    grid_spec = pltpu.PrefetchScalarGridSpec(
        num_scalar_prefetch=2,
        grid=(n // tm,),
        in_specs=[pl.BlockSpec((TOP_K, tm), lambda i, ce, pe: (0, i), memory_space=pltpu.SMEM),
                  pl.BlockSpec((tm * TILE_ROWS, LANES), lambda i, ce, pe: (i, 0)),
                  pl.BlockSpec((TILE_ROWS, LANES), lambda i, ce, pe: (0, 0))],
        out_specs=pl.BlockSpec(memory_space=pl.ANY),
        scratch_shapes=[stage, stage, pltpu.SemaphoreType.DMA((2,)), pltpu.SemaphoreType.DMA],
    )
    return pl.pallas_call(
        functools.partial(_scatter_kernel, n_assign=n * TOP_K),
        grid_spec=grid_spec,
        out_shape=jax.ShapeDtypeStruct((rows * TILE_ROWS, LANES), U32),
        compiler_params=_params(("arbitrary",)),
        name="dispatch_scatter",
    )(cend, pend, dest, h3, jnp.zeros((TILE_ROWS, LANES), U32))


def _ffn_kernel(be_ref, nu_ref, x_ref, *refs):
    w_refs, (y_ref, wgb, wub, wdb) = refs[:3 * FFN_PER_STEP], refs[3 * FFN_PER_STEP:]
    nused = nu_ref[0]
    for part in range(FFN_PER_STEP):
        blk = pl.program_id(0) * FFN_PER_STEP + part
        wg_ref, wu_ref, wd_ref = w_refs[3 * part:3 * part + 3]
        base = part * FFN_BLOCK * TILE_ROWS

        @pl.when((blk == 0) | (be_ref[blk] != be_ref[jnp.maximum(blk - 1, 0)]))
        def _(wg_ref=wg_ref, wu_ref=wu_ref, wd_ref=wd_ref):
            wgb[...] = wg_ref[0].astype(BF16)
            wub[...] = wu_ref[0].astype(BF16)
            wdb[...] = wd_ref[0].astype(BF16)

        @pl.when(blk < nused)
        def _(base=base):
            x = jnp.concatenate([c.astype(BF16) for c in _from_row_tiles(x_ref, FFN_BLOCK, base)], axis=1)
            act = _silu(jnp.dot(x, wgb[...], preferred_element_type=F32)) * jnp.dot(
                x, wub[...], preferred_element_type=F32)
            _to_row_tiles(y_ref, jnp.dot(act.astype(BF16), wdb[...], preferred_element_type=F32), base)

        @pl.when(blk >= nused)
        def _(base=base):
            y_ref[pl.ds(base, FFN_BLOCK * TILE_ROWS), :] = jnp.zeros((FFN_BLOCK * TILE_ROWS, LANES), U32)


def _routed_experts(xs, block_e, nused, w_gate, w_up, w_down):
    rows = xs.shape[0] // TILE_ROWS
    nblk = rows // FFN_BLOCK
    _, d, hid = w_gate.shape
    step_rows = FFN_PER_STEP * FFN_BLOCK * TILE_ROWS
    w_specs = []
    for part in range(FFN_PER_STEP):
        expert = lambda i, be, nu, part=part: (be[i * FFN_PER_STEP + part], 0, 0)
        w_specs += [pl.BlockSpec((1, d, hid), expert), pl.BlockSpec((1, d, hid), expert),
                    pl.BlockSpec((1, hid, d), expert)]
    grid_spec = pltpu.PrefetchScalarGridSpec(
        num_scalar_prefetch=2,
        grid=(nblk // FFN_PER_STEP,),
        in_specs=[pl.BlockSpec((step_rows, LANES),
                               lambda i, be, nu: (jnp.minimum(i, (nu[0] - 1) // FFN_PER_STEP), 0))] + w_specs,
        out_specs=pl.BlockSpec((step_rows, LANES), lambda i, be, nu: (i, 0)),
        scratch_shapes=[pltpu.VMEM((d, hid), BF16), pltpu.VMEM((d, hid), BF16), pltpu.VMEM((hid, d), BF16)],
    )
    return pl.pallas_call(
        _ffn_kernel,
        grid_spec=grid_spec,
        out_shape=jax.ShapeDtypeStruct((rows * TILE_ROWS, LANES), U32),
        compiler_params=_params(("arbitrary",)),
        name="routed_experts",
    )(block_e, nused, xs, *([w_gate, w_up, w_down] * FFN_PER_STEP))


def _gather_start(idx_ref, src_hbm, dst_vmem, sem):
    nk, tm = idx_ref.shape
    for k in range(nk):
        def body(jj, carry, k=k):
            for u in range(ISSUE_UNROLL):
                j = jj * ISSUE_UNROLL + u
                _tile_copy(src_hbm, idx_ref[k, j], dst_vmem, k * tm + j, sem).start()
            return carry
        lax.fori_loop(0, tm // ISSUE_UNROLL, body, 0)


def _gather_wait(src_hbm, dst_vmem, sem):
    pltpu.make_async_copy(src_hbm.at[pl.ds(0, dst_vmem.shape[0]), :], dst_vmem, sem).wait()


def _combine_kernel(dst_ref, dstn_ref, wt_ref, xp_ref, g2_ref, gfin_ref, ys_hbm, o_ref, yb0, yb1, sem, *, nsteps):
    i = pl.program_id(0)
    tm = COMBINE_TILE
    bufs = (yb0, yb1)

    @pl.when(i == 0)
    def _():
        _gather_start(dst_ref, ys_hbm, yb0, sem.at[0])

    for slot in range(2):
        @pl.when((i + 1 < nsteps) & ((i + 1) % 2 == slot))
        def _(slot=slot):
            _gather_start(dstn_ref, ys_hbm, bufs[slot], sem.at[slot])

    for slot in range(2):
        @pl.when(i % 2 == slot)
        def _(slot=slot):
            _gather_wait(ys_hbm, bufs[slot], sem.at[slot])
            wt = wt_ref[...]
            cols = None
            for k in range(TOP_K):
                part = [c * wt[:, k:k + 1] for c in _from_row_tiles(bufs[slot], tm, base=k * tm * TILE_ROWS)]
                cols = part if cols is None else [a + b for a, b in zip(cols, part)]
            routed = jnp.concatenate(cols, axis=1)
            o_ref[0] = _rms(xp_ref[0] + g2_ref[0] * routed) * gfin_ref[...]


def _combine(ys, dest, wts_t, xp, gate2, g_final):
    bsz, s, d = xp.shape
    tm = COMBINE_TILE
    nt = s // tm
    nsteps = bsz * nt
    vec = pl.BlockSpec((1, 1, d), lambda i: (i // nt, 0, 0))
    return pl.pallas_call(
        functools.partial(_combine_kernel, nsteps=nsteps),
        grid=(nsteps,),
        in_specs=[pl.BlockSpec((TOP_K, tm), lambda i: (0, i), memory_space=pltpu.SMEM),
                  pl.BlockSpec((TOP_K, tm), lambda i: (0, jnp.minimum(i + 1, nsteps - 1)),
                               memory_space=pltpu.SMEM),
                  pl.BlockSpec((tm, TOP_K), lambda i: (i, 0)),
                  pl.BlockSpec((1, tm, d), lambda i: (i // nt, i % nt, 0)),
                  vec,
                  pl.BlockSpec((1, d), lambda i: (0, 0)),
                  pl.BlockSpec(memory_space=pl.ANY)],
        out_specs=pl.BlockSpec((1, tm, d), lambda i: (i // nt, i % nt, 0)),
        out_shape=jax.ShapeDtypeStruct((bsz, s, d), F32),
        scratch_shapes=[pltpu.VMEM((tm * TOP_K * TILE_ROWS, LANES), U32),
                        pltpu.VMEM((tm * TOP_K * TILE_ROWS, LANES), U32),
                        pltpu.SemaphoreType.DMA((2,))],
        compiler_params=_params(("arbitrary",)),
        name="combine_final_norm",
    )(dest, dest, wts_t, xp, gate2, g_final.reshape(1, d), ys)


def kernel(x, c, w_in, b_forget, g_fox_out, g_dil_out, w_out, w_ada, b_ada, w_router, router_bias,
           w_exp_gate, w_exp_up, w_exp_down, w_sh_gate, w_sh_up, w_sh_down, rel_bias, g_final):
    bsz, s, d = x.shape
    depth = w_in.shape[0]
    assert depth == 1 and d == 2 * TILE_ROWS * LANES and s % (DIL_BLOCK * DILATIONS[-1]) == 0
    l = 0
    mod = _modulation(c, w_ada[l], b_ada[l])
    shift1, scale1, gate1, shift2, scale2, gate2 = [m[:, None, :] for m in jnp.split(mod, 6, axis=-1)]

    qscale = HEAD_DIM ** -0.5
    o3 = 3 * WIDTH
    w = w_in[l]
    wq_t = (w[:, :WIDTH] * (qscale * LOG2E)).T.astype(BF16)
    wk = w[:, WIDTH:2 * WIDTH].astype(BF16)
    wv_t = w[:, 2 * WIDTH:o3].T.astype(BF16)
    w_flog = jnp.pad(w[:, o3:o3 + N_HEADS], ((0, 0), (0, LANES - N_HEADS))).astype(BF16)
    wd0 = o3 + N_HEADS
    w_dil = jnp.concatenate([w[:, wd0:wd0 + WIDTH] * qscale, w[:, wd0 + WIDTH:]], axis=1).astype(BF16)

    q_t, k_f, v_t, qkv_d, flog = _in_projection(x, scale1, shift1, wq_t, wk, wv_t, w_dil, w_flog)
    y_fox = _fox_attention(q_t, k_f, _forget_cumsum(flog, b_forget[l]), v_t)
    y_dil = _dilated_attention(qkv_d, _dilated_bias(rel_bias))

    wr = w_router[l].T
    wr_hi = wr.astype(BF16)
    wr_lo = (wr - wr_hi.astype(F32)).astype(BF16)
    wo = w_out[l].astype(BF16)
    xp, h3, logits_t = _out_projection(
        y_fox, y_dil, x, gate1, scale2, shift2, gate2, g_fox_out[l].reshape(1, WIDTH),
        g_dil_out[l].reshape(1, WIDTH), wo[:WIDTH], wo[WIDTH:], wr_hi, wr_lo,
        w_sh_gate[l].astype(BF16), w_sh_up[l].astype(BF16), w_sh_down[l].astype(BF16))

    eidx, wts, rank, counts = _route(logits_t, router_bias[l])
    rows = bsz * s * TOP_K + N_EXPERTS * FFN_BLOCK
    pstart, cend, pend, block_e, nused = _block_plan(counts, rows // FFN_BLOCK)
    dest = _dest_rows(eidx, rank, pstart)
    xs = _dispatch_rows(h3, dest, cend, pend, rows)
    ys = _routed_experts(xs, block_e, nused, w_exp_gate[l], w_exp_up[l], w_exp_down[l])
    return _combine(ys, dest, wts.T, xp, gate2, g_final)
```

```python
import functools

import numpy as np
import jax
import jax.numpy as jnp
from jax import lax
from jax.experimental import pallas as pl
from jax.experimental.pallas import tpu as pltpu

F32 = jnp.float32
BF16 = jnp.bfloat16

HEAD_DIM = 64
N_HEADS = 8
WIDTH = N_HEADS * HEAD_DIM
LANES = 128
TILE_ROWS = 4
U32 = jnp.uint32
DIL_UNROLL = 8
HEAD_PAIRS = WIDTH // LANES
DIL_BLOCK = 128
DILATIONS = (1, 4, 16)
T5_NUM_BUCKETS = 32
T5_MAX_DISTANCE = 2048
N_EXPERTS = 256
TOP_K = 8
N_GROUPS = 8
TOP_K_GROUPS = 4
ROUTED_SCALE = 2.5
EPS = 1e-6
NEG = -1e30
LOG2E = float(np.log2(np.e))
VMEM_LIMIT = 56 * 1024 * 1024

ROW_TILE = 512
FOX_TILE = 256
FFN_BLOCK = 256
FFN_PER_STEP = 4
COMBINE_TILE = 128
SCATTER_TILE = 128
ROUTE_TILE = 512
ISSUE_UNROLL = 8


def _params(semantics):
    return pltpu.CompilerParams(dimension_semantics=semantics, vmem_limit_bytes=VMEM_LIMIT)


def _rms(x):
    return x * lax.rsqrt(jnp.mean(x * x, axis=-1, keepdims=True) + EPS)


def _silu(x):
    return x * jax.nn.sigmoid(x)


def _dot_nt(a, b):
    return lax.dot_general(a, b, (((1,), (1,)), ((), ())), preferred_element_type=F32)


def _mod_kernel(c_ref, w_ref, b_ref, o_ref):
    o_ref[...] = jnp.dot(_silu(c_ref[...]), w_ref[...], precision=lax.Precision.HIGHEST,
                         preferred_element_type=F32) + b_ref[...]


def _modulation(c, w_ada, b_ada):
    bsz, d = c.shape
    n = w_ada.shape[1]
    tn = 1536
    return pl.pallas_call(
        _mod_kernel,
        grid=(n // tn,),
        in_specs=[pl.BlockSpec((bsz, d), lambda j: (0, 0)),
                  pl.BlockSpec((d, tn), lambda j: (0, j)),
                  pl.BlockSpec((1, tn), lambda j: (0, j))],
        out_specs=pl.BlockSpec((bsz, tn), lambda j: (0, j)),
        out_shape=jax.ShapeDtypeStruct((bsz, n), F32),
        compiler_params=_params(("arbitrary",)),
        name="adaln_mod",
    )(c, w_ada, b_ada.reshape(1, n))


def _inproj_kernel(x_ref, sc_ref, sh_ref, wq_ref, wk_ref, wv_ref, wd_ref, wl_ref,
                   oq_ref, ok_ref, ov_ref, od_ref, ol_ref):
    h = _rms(x_ref[0]) * (1.0 + sc_ref[0]) + sh_ref[0]
    hb = h.astype(BF16)
    oq_ref[0] = _dot_nt(wq_ref[...], hb).astype(BF16)
    ov_ref[0] = _dot_nt(wv_ref[...], hb).astype(BF16)
    ok_ref[0] = jnp.dot(hb, wk_ref[...], preferred_element_type=F32).astype(BF16)
    od_ref[0] = jnp.dot(hb, wd_ref[...], preferred_element_type=F32).astype(BF16)
    ol_ref[0] = jnp.dot(hb, wl_ref[...], preferred_element_type=F32)


def _in_projection(x, scale1, shift1, wq_t, wk, wv_t, w_dil, w_flog):
    bsz, s, d = x.shape
    tm = ROW_TILE
    vec = pl.BlockSpec((1, 1, d), lambda b, i: (b, 0, 0))
    full = lambda w: pl.BlockSpec(w.shape, lambda b, i: (0, 0))
    row = lambda n: pl.BlockSpec((1, tm, n), lambda b, i: (b, i, 0))
    col = pl.BlockSpec((1, WIDTH, tm), lambda b, i: (b, 0, i))
    return pl.pallas_call(
        _inproj_kernel,
        grid=(bsz, s // tm),
        in_specs=[row(d), vec, vec, full(wq_t), full(wk), full(wv_t), full(w_dil), full(w_flog)],
        out_specs=[col, row(WIDTH), col, row(3 * WIDTH), row(LANES)],
        out_shape=[jax.ShapeDtypeStruct((bsz, WIDTH, s), BF16),
                   jax.ShapeDtypeStruct((bsz, s, WIDTH), BF16),
                   jax.ShapeDtypeStruct((bsz, WIDTH, s), BF16),
                   jax.ShapeDtypeStruct((bsz, s, 3 * WIDTH), BF16),
                   jax.ShapeDtypeStruct((bsz, s, LANES), F32)],
        compiler_params=_params(("parallel", "arbitrary")),
        name="norm1_inproj",
    )(x, scale1, shift1, wq_t, wk, wv_t, w_dil, w_flog)


FORGET_PARTS = 3


def _split_bf16(x):
    parts = []
    for _ in range(FORGET_PARTS):
        p = x.astype(BF16)
        parts.append(p)
        x = x - p.astype(F32)
    return parts


def _forget_kernel(f_ref, b_ref, tri_ref, place_ref, o_ref):
    blk = tri_ref.shape[0]
    carry = jnp.zeros((1, LANES), F32)
    for c in range(f_ref.shape[1] // blk):
        rows = pl.ds(c * blk, blk)
        z = f_ref[0, rows, :] + b_ref[...]
        ls = jnp.minimum(z, 0.0) - jnp.log1p(jnp.exp(-jnp.abs(z)))
        cum = carry
        for p in _split_bf16(ls):
            cum = cum + jnp.dot(tri_ref[...], p, preferred_element_type=F32)
        carry = cum[blk - 1:blk, :]
        out = None
        for n, p in enumerate(_split_bf16(cum * LOG2E)):
            d = jnp.dot(p, place_ref[n], preferred_element_type=F32)
            out = d if out is None else out + d
        o_ref[0, rows, :] = out.astype(BF16)


def _forget_cumsum(flog, b_forget):
    bsz, s, _ = flog.shape
    blk = 512
    place = np.zeros((FORGET_PARTS, LANES, WIDTH), np.float32)
    for h in range(N_HEADS):
        for n in range(FORGET_PARTS):
            place[n, h, (h // 2) * LANES + (h % 2) * HEAD_DIM + n] = 1.0
    return pl.pallas_call(
        _forget_kernel,
        grid=(bsz,),
        in_specs=[pl.BlockSpec((1, s, LANES), lambda b: (b, 0, 0)),
                  pl.BlockSpec((1, LANES), lambda b: (0, 0)),
                  pl.BlockSpec((blk, blk), lambda b: (0, 0)),
                  pl.BlockSpec(place.shape, lambda b: (0, 0, 0))],
        out_specs=pl.BlockSpec((1, s, WIDTH), lambda b: (b, 0, 0)),
        out_shape=jax.ShapeDtypeStruct((bsz, s, WIDTH), BF16),
        compiler_params=_params(("parallel",)),
        name="forget_cumsum",
    )(flog, jnp.pad(b_forget, (0, LANES - N_HEADS)).reshape(1, LANES),
      jnp.tril(jnp.ones((blk, blk), BF16)), jnp.asarray(place, BF16))


def _stack_heads(q):
    lane = lax.broadcasted_iota(jnp.int32, q.shape, 1)
    zero = jnp.zeros_like(q)
    return jnp.concatenate([jnp.where(lane < HEAD_DIM, q, zero), jnp.where(lane >= HEAD_DIM, q, zero)], axis=0)


def _unstack_heads(o2):
    rows = o2.shape[0] // 2
    lane = lax.broadcasted_iota(jnp.int32, (rows, LANES), 1)
    return jnp.where(lane < HEAD_DIM, o2[:rows], o2[rows:])


def _fox_kernel(q_ref, k_ref, fa_ref, v_ref, o_ref, m_s, l_s, acc_s, s_a, s_b, *, t):
    i = pl.program_id(2)
    q_t = q_ref[0]
    row = lax.broadcasted_iota(jnp.int32, (LANES, 2 * t), 0)
    col = lax.broadcasted_iota(jnp.int32, (LANES, 2 * t), 1)
    head_row = jnp.where(col < t, 0, HEAD_DIM)
    q2 = jnp.concatenate([q_t, q_t], axis=1)
    q2 = jnp.where((row >= head_row) & (row < head_row + HEAD_DIM), q2, jnp.zeros_like(q2))
    minus = jnp.where((row >= head_row) & (row < head_row + FORGET_PARTS), -1.0, 0.0).astype(BF16)
    qa = jnp.concatenate([q2, minus], axis=0)
    m_s[...] = jnp.full(m_s.shape, NEG, F32)
    l_s[...] = jnp.zeros(l_s.shape, F32)
    acc_s[...] = jnp.zeros(acc_s.shape, F32)

    def scores(ks, tk):
        kk = jnp.concatenate([k_ref[0, pl.ds(ks, tk), :], fa_ref[0, pl.ds(ks, tk), :]], axis=1)
        return jnp.dot(kk, qa, preferred_element_type=F32)

    def absorb(s, ks, tk):
        m_prev = m_s[...]
        m_new = jnp.maximum(m_prev, jnp.max(s, axis=0, keepdims=True))
        alpha = jnp.exp2(m_prev - m_new)
        p = jnp.exp2(s - m_new)
        l_s[...] = alpha * l_s[...] + jnp.sum(p, axis=0, keepdims=True)
        acc_s[...] = alpha * acc_s[...] + jnp.dot(v_ref[0, :, pl.ds(ks, tk)], p.astype(BF16),
                                                  preferred_element_type=F32)
        m_s[...] = m_new

    npair = i // 2

    def tile(j):
        return pl.multiple_of(j * 2 * t, 2 * t)

    @pl.when(npair > 0)
    def _():
        s_a[...] = scores(0, 2 * t)

        def body(jj, carry):
            j = 2 * jj
            s_b[...] = scores(tile(j + 1), 2 * t)
            absorb(s_a[...], tile(j), 2 * t)
            s_a[...] = scores(tile(jnp.minimum(j + 2, npair - 1)), 2 * t)
            absorb(s_b[...], tile(j + 1), 2 * t)
            return carry

        lax.fori_loop(0, npair // 2, body, 0)

        @pl.when(npair % 2 == 1)
        def _():
            absorb(s_a[...], tile(npair - 1), 2 * t)

    @pl.when(i % 2 == 1)
    def _():
        ks = pl.multiple_of(npair * 2 * t, 2 * t)
        absorb(scores(ks, t), ks, t)

    ks = pl.multiple_of(i * t, t)
    r = lax.broadcasted_iota(jnp.int32, (t, 2 * t), 0)
    c = lax.broadcasted_iota(jnp.int32, (t, 2 * t), 1)
    absorb(jnp.where(r <= jnp.where(c >= t, c - t, c), scores(ks, t), NEG), ks, t)
    o2 = acc_s[...] / l_s[...]
    o_t = jnp.where(lax.broadcasted_iota(jnp.int32, (LANES, t), 0) < HEAD_DIM, o2[:, :t], o2[:, t:])
    o_ref[0] = o_t.T


def _fox_attention(q_t, k, f_aug, v_t):
    bsz, s, _ = k.shape
    t = FOX_TILE
    keys = pl.BlockSpec((1, s, LANES), lambda b, h, i: (b, 0, h))
    return pl.pallas_call(
        functools.partial(_fox_kernel, t=t),
        grid=(bsz, HEAD_PAIRS, s // t),
        in_specs=[pl.BlockSpec((1, LANES, t), lambda b, h, i: (b, h, i)),
                  keys, keys,
                  pl.BlockSpec((1, LANES, s), lambda b, h, i: (b, h, 0))],
        out_specs=pl.BlockSpec((1, t, LANES), lambda b, h, i: (b, i, h)),
        out_shape=jax.ShapeDtypeStruct((bsz, s, WIDTH), F32),
        scratch_shapes=[pltpu.VMEM((1, 2 * t), F32), pltpu.VMEM((1, 2 * t), F32),
                        pltpu.VMEM((LANES, 2 * t), F32), pltpu.VMEM((2 * t, 2 * t), F32),
                        pltpu.VMEM((2 * t, 2 * t), F32)],
        compiler_params=_params(("parallel", "parallel", "arbitrary")),
        name="fox_attention",
    )(q_t, k, f_aug, v_t)


def _t5_bucket(dist):
    max_exact = T5_NUM_BUCKETS // 2
    d = np.maximum(dist, 1).astype(np.float32)
    large = max_exact + (np.log(d / max_exact) / np.log(T5_MAX_DISTANCE / max_exact)
                         * (T5_NUM_BUCKETS - max_exact)).astype(np.int32)
    large = np.minimum(large, T5_NUM_BUCKETS - 1)
    return np.where(dist < max_exact, dist, large).astype(np.int32)


def _dilated_bias(rel_bias):
    blk = DIL_BLOCK
    period = 3 * blk
    m = np.arange(period)
    rel = np.where(m < 2 * blk, blk - m, blk - (m - period))
    band = (rel >= 0) & (rel <= blk)
    onehot = np.zeros((len(DILATIONS), period, T5_NUM_BUCKETS), np.float32)
    for bi, dil in enumerate(DILATIONS):
        onehot[bi, m, _t5_bucket(np.clip(rel, 0, blk) * dil)] = 1.0
    w = jnp.einsum('bmk,kh->bhm', jnp.asarray(onehot), rel_bias.astype(F32),
                   precision=lax.Precision.HIGHEST)
    w = jnp.where(jnp.asarray(band), w, NEG)
    flat = jnp.tile(w, (1, 1, blk))[:, :, :blk * (period - 1)]
    table = flat.reshape(len(DILATIONS), N_HEADS, blk, period - 1)[..., :2 * blk]
    table = table.reshape(len(DILATIONS), HEAD_PAIRS, 2 * blk, 2 * blk)
    first = jnp.where(jnp.arange(2 * blk) < blk, NEG, table)
    return jnp.stack([table, first], axis=1)


def _dil_kernel(q_ref, k_ref, v_ref, bias_ref, o_ref, qf, kf, vf, ob0, ob1, ob2, ls0, ls1, ls2, *, s_len):
    blk = DIL_BLOCK
    qf[...] = q_ref[0].astype(F32)
    kf[...] = k_ref[0].astype(F32)
    vf[...] = v_ref[0].astype(F32)
    for bi, (dil, ob, ls) in enumerate(zip(DILATIONS, (ob0, ob1, ob2), (ls0, ls1, ls2))):
        span = blk * dil
        nb = s_len // span

        def rows(start, dil=dil):
            return pl.ds(start, blk) if dil == 1 else pl.ds(start, blk, stride=dil)

        def block(tix, carry, bi=bi, span=span, nb=nb, ob=ob, ls=ls, rows=rows):
            n = tix % nb
            start = n * span + tix // nb
            prev = jnp.maximum(start - span, 0)
            q2 = _stack_heads(qf[rows(start), :].astype(BF16))
            kk = jnp.concatenate([kf[rows(prev), :], kf[rows(start), :]], axis=0).astype(BF16)
            vv = jnp.concatenate([vf[rows(prev), :], vf[rows(start), :]], axis=0).astype(BF16)
            s = lax.dot_general(q2, kk, (((1,), (1,)), ((), ())), preferred_element_type=F32)
            s = s + bias_ref[bi, jnp.where(n == 0, 1, 0), 0]
            m = jnp.max(s, axis=1, keepdims=True)
            p = jnp.exp(s - m)
            l = jnp.sum(p, axis=1, keepdims=True)
            o2 = jnp.dot(p.astype(BF16), vv, preferred_element_type=F32) / l
            lse = jnp.broadcast_to(m + jnp.log(l), (2 * blk, LANES))
            ob[rows(start), :] = _unstack_heads(o2)
            ls[rows(start), :] = _unstack_heads(lse)
            return carry

        def blocks(g, carry, block=block):
            for u in range(DIL_UNROLL):
                block(g * DIL_UNROLL + u, carry)
            return carry

        lax.fori_loop(0, dil * nb // DIL_UNROLL, blocks, 0)

    chunk = 512
    for c in range(s_len // chunk):
        r = pl.ds(c * chunk, chunk)
        l0, l1, l2 = ls0[r, :], ls1[r, :], ls2[r, :]
        mx = jnp.maximum(jnp.maximum(l0, l1), l2)
        e0, e1, e2 = jnp.exp(l0 - mx), jnp.exp(l1 - mx), jnp.exp(l2 - mx)
        o_ref[0, r, :] = (e0 * ob0[r, :] + e1 * ob1[r, :] + e2 * ob2[r, :]) / (e0 + e1 + e2)


def _dilated_attention(qkv, bias):
    bsz, s, _ = qkv.shape
    col = lambda off: pl.BlockSpec((1, s, LANES), lambda b, h: (b, 0, off + h))
    buf = pltpu.VMEM((s, LANES), F32)
    return pl.pallas_call(
        functools.partial(_dil_kernel, s_len=s),
        grid=(bsz, HEAD_PAIRS),
        in_specs=[col(0), col(HEAD_PAIRS), col(2 * HEAD_PAIRS),
                  pl.BlockSpec((len(DILATIONS), 2, 1, 2 * DIL_BLOCK, 2 * DIL_BLOCK), lambda b, h: (0, 0, h, 0, 0))],
        out_specs=pl.BlockSpec((1, s, LANES), lambda b, h: (b, 0, h)),
        out_shape=jax.ShapeDtypeStruct((bsz, s, WIDTH), F32),
        scratch_shapes=[buf] * 9,
        compiler_params=_params(("parallel", "arbitrary")),
        name="dilated_attention",
    )(qkv, qkv, qkv, bias)


def _bf16_bits(x):
    return pltpu.bitcast(x.astype(BF16).astype(F32), U32)


def _to_row_tiles(dst_ref, x, base=0):
    rows, d = x.shape
    for c in range(TILE_ROWS):
        lo = _bf16_bits(x[:, c * LANES:(c + 1) * LANES]) >> 16
        hi = _bf16_bits(x[:, d // 2 + c * LANES:d // 2 + (c + 1) * LANES])
        dst_ref[pl.ds(base + c, rows, stride=TILE_ROWS), :] = lo | hi


def _from_row_tiles(src_ref, rows, base=0):
    lo, hi = [], []
    for c in range(TILE_ROWS):
        w = src_ref[pl.ds(base + c, rows, stride=TILE_ROWS), :]
        lo.append(pltpu.bitcast(w << 16, F32))
        hi.append(pltpu.bitcast(w & jnp.uint32(0xFFFF0000), F32))
    return lo + hi


def _outproj_kernel(yf_ref, yd_ref, x_ref, g1_ref, sc_ref, sh_ref, g2_ref, gf_ref, gd_ref, wo1_ref, wo2_ref,
                    wrh_ref, wrl_ref, wsg_ref, wsu_ref, wsd_ref, xp_ref, h3_ref, lg_ref):
    nf = (_rms(yf_ref[0]) * gf_ref[...]).astype(BF16)
    nd = (_rms(yd_ref[0]) * gd_ref[...]).astype(BF16)
    mix = (jnp.dot(nf, wo1_ref[...], preferred_element_type=F32)
           + jnp.dot(nd, wo2_ref[...], preferred_element_type=F32))
    x1 = x_ref[0] + g1_ref[0] * mix
    h2 = _rms(x1) * (1.0 + sc_ref[0]) + sh_ref[0]
    hb = h2.astype(BF16)
    hl = (h2 - hb.astype(F32)).astype(BF16)
    lg_ref[...] = _dot_nt(wrh_ref[...], hb) + _dot_nt(wrh_ref[...], hl) + _dot_nt(wrl_ref[...], hb)
    act = _silu(jnp.dot(hb, wsg_ref[...], preferred_element_type=F32)) * jnp.dot(
        hb, wsu_ref[...], preferred_element_type=F32)
    shared = jnp.dot(act.astype(BF16), wsd_ref[...], preferred_element_type=F32)
    xp_ref[0] = x1 + g2_ref[0] * shared
    _to_row_tiles(h3_ref, h2)


def _out_projection(y_fox, y_dil, x, gate1, scale2, shift2, gate2, g_fox, g_dil, wo1, wo2, wr_hi, wr_lo,
                    wsg, wsu, wsd):
    bsz, s, d = x.shape
    tm = ROW_TILE
    nt = s // tm
    vec = pl.BlockSpec((1, 1, d), lambda b, i: (b, 0, 0))
    full = lambda w: pl.BlockSpec(w.shape, lambda b, i: (0,) * w.ndim)
    row = lambda n: pl.BlockSpec((1, tm, n), lambda b, i: (b, i, 0))
    return pl.pallas_call(
        _outproj_kernel,
        grid=(bsz, nt),
        in_specs=[row(WIDTH), row(WIDTH), row(d), vec, vec, vec, vec, full(g_fox), full(g_dil), full(wo1),
                  full(wo2), full(wr_hi), full(wr_lo), full(wsg), full(wsu), full(wsd)],
        out_specs=[row(d),
                   pl.BlockSpec((tm * TILE_ROWS, LANES), lambda b, i: (b * nt + i, 0)),
                   pl.BlockSpec((N_EXPERTS, tm), lambda b, i: (0, b * nt + i))],
        out_shape=[jax.ShapeDtypeStruct((bsz, s, d), F32),
                   jax.ShapeDtypeStruct((bsz * s * TILE_ROWS, LANES), U32),
                   jax.ShapeDtypeStruct((N_EXPERTS, bsz * s), F32)],
        compiler_params=_params(("parallel", "arbitrary")),
        name="outproj_norm2_router_shared",
    )(y_fox, y_dil, x, gate1, scale2, shift2, gate2, g_fox, g_dil, wo1, wo2, wr_hi, wr_lo, wsg, wsu, wsd)


def _first_argmax(v, row, size):
    m = jnp.max(v, axis=0, keepdims=True)
    return m, jnp.min(jnp.where(v == m, row, size), axis=0, keepdims=True)


def _route_kernel(lg_ref, rb_ref, tri_ref, e_ref, w_ref, rk_ref, cnt_ref, cnt_s):
    @pl.when(pl.program_id(0) == 0)
    def _():
        cnt_s[...] = jnp.zeros(cnt_s.shape, F32)

    t = lg_ref.shape[1]
    gsz = N_EXPERTS // N_GROUPS
    scores = jax.nn.sigmoid(lg_ref[...])
    sel = scores + rb_ref[...]
    row_g = lax.broadcasted_iota(jnp.int32, (gsz, t), 0)
    grp = []
    for g in range(N_GROUPS):
        v = sel[g * gsz:(g + 1) * gsz]
        m1, i1 = _first_argmax(v, row_g, gsz)
        m2 = jnp.max(jnp.where(row_g == i1, -jnp.inf, v), axis=0, keepdims=True)
        grp.append(m1 + m2)
    gv = jnp.concatenate(grp, axis=0)
    row8 = lax.broadcasted_iota(jnp.int32, (N_GROUPS, t), 0)
    pen = jnp.full((N_GROUPS, t), -jnp.inf, F32)
    for _ in range(TOP_K_GROUPS):
        _, ix = _first_argmax(gv, row8, N_GROUPS)
        pen = jnp.where(row8 == ix, 0.0, pen)
        gv = jnp.where(row8 == ix, -jnp.inf, gv)
    selm = jnp.concatenate([sel[g * gsz:(g + 1) * gsz] + pen[g:g + 1] for g in range(N_GROUPS)], axis=0)

    row = lax.broadcasted_iota(jnp.int32, (N_EXPERTS, t), 0)
    v = selm
    idxs, scs = [], []
    for _ in range(TOP_K):
        _, ix = _first_argmax(v, row, N_EXPERTS)
        hit = row == ix
        idxs.append(ix)
        scs.append(jnp.sum(jnp.where(hit, scores, 0.0), axis=0, keepdims=True))
        v = jnp.where(hit, -jnp.inf, v)
    chosen = jnp.where(v != selm, 1.0, 0.0)
    before = jnp.dot(chosen.astype(BF16), tri_ref[...], preferred_element_type=F32) + cnt_s[...]
    rks = [jnp.sum(jnp.where(row == ix, before, 0.0), axis=0, keepdims=True) for ix in idxs]
    sc = jnp.concatenate(scs, axis=0)
    e_ref[...] = jnp.concatenate(idxs, axis=0)
    w_ref[...] = sc / jnp.sum(sc, axis=0, keepdims=True) * ROUTED_SCALE
    rk_ref[...] = jnp.concatenate(rks, axis=0).astype(jnp.int32)
    cnt_s[...] = cnt_s[...] + jnp.sum(chosen, axis=1, keepdims=True)
    cnt_ref[...] = cnt_s[...]


def _route(logits_t, router_bias):
    e, n = logits_t.shape
    t = ROUTE_TILE
    tri = jnp.triu(jnp.ones((t, t), BF16), k=1)
    tile = pl.BlockSpec((TOP_K, t), lambda i: (0, i))
    return pl.pallas_call(
        _route_kernel,
        grid=(n // t,),
        in_specs=[pl.BlockSpec((e, t), lambda i: (0, i)),
                  pl.BlockSpec((e, 1), lambda i: (0, 0)),
                  pl.BlockSpec((t, t), lambda i: (0, 0))],
        out_specs=[tile, tile, tile, pl.BlockSpec((e, 1), lambda i: (0, 0))],
        out_shape=[jax.ShapeDtypeStruct((TOP_K, n), jnp.int32), jax.ShapeDtypeStruct((TOP_K, n), F32),
                   jax.ShapeDtypeStruct((TOP_K, n), jnp.int32), jax.ShapeDtypeStruct((e, 1), F32)],
        scratch_shapes=[pltpu.VMEM((e, 1), F32)],
        compiler_params=_params(("arbitrary",)),
        name="route_topk_rank",
    )(logits_t, router_bias.reshape(e, 1).astype(F32), tri)


def _dest_kernel(e_ref, rk_ref, ps_ref, d_ref):
    t = e_ref.shape[1]
    row = lax.broadcasted_iota(jnp.int32, (N_EXPERTS, t), 0)
    ps = ps_ref[...]
    base = [jnp.sum(jnp.where(row == e_ref[k:k + 1, :], ps, 0.0), axis=0, keepdims=True) for k in range(TOP_K)]
    d_ref[...] = jnp.concatenate(base, axis=0).astype(jnp.int32) + rk_ref[...]


def _dest_rows(eidx, rank, pstart):
    _, n = eidx.shape
    t = ROUTE_TILE
    tile = pl.BlockSpec((TOP_K, t), lambda i: (0, i))
    return pl.pallas_call(
        _dest_kernel,
        grid=(n // t,),
        in_specs=[tile, tile, pl.BlockSpec((N_EXPERTS, 1), lambda i: (0, 0))],
        out_specs=tile,
        out_shape=jax.ShapeDtypeStruct((TOP_K, n), jnp.int32),
        compiler_params=_params(("parallel",)),
        name="dest_rows",
    )(eidx, rank, pstart.astype(F32).reshape(N_EXPERTS, 1))


def _block_plan(counts, nblk):
    counts = counts.reshape(N_EXPERTS).astype(jnp.int32)
    padded = (counts + FFN_BLOCK - 1) // FFN_BLOCK * FFN_BLOCK
    pend = jnp.cumsum(padded).astype(jnp.int32)
    pstart = pend - padded
    first_row = jnp.arange(nblk, dtype=jnp.int32) * FFN_BLOCK
    block_e = jnp.minimum(jnp.sum(pend[None, :] <= first_row[:, None], axis=1), N_EXPERTS - 1).astype(jnp.int32)
    nused = (pend[-1:] // FFN_BLOCK).astype(jnp.int32)
    return pstart, pstart + counts, pend, block_e, nused


def _tile_copy(src_ref, src_row, dst_ref, dst_row, sem):
    return pltpu.make_async_copy(src_ref.at[pl.ds(pl.multiple_of(src_row * TILE_ROWS, TILE_ROWS), TILE_ROWS), :],
                                 dst_ref.at[pl.ds(pl.multiple_of(dst_row * TILE_ROWS, TILE_ROWS), TILE_ROWS), :],
                                 sem)


def _scatter_kernel(cend_ref, pend_ref, dst_ref, h3_ref, z_ref, xs_hbm, st0, st1, sem, zsem, *, n_assign):
    i = pl.program_id(0)
    nsteps = pl.num_programs(0)
    tm = dst_ref.shape[1]
    stage = (st0, st1)

    def zero_copy(r):
        return pltpu.make_async_copy(z_ref, xs_hbm.at[pl.ds(pl.multiple_of(r * TILE_ROWS, TILE_ROWS), TILE_ROWS), :],
                                     zsem)

    def wait_slot(slot):
        for _ in range(TOP_K):
            pltpu.make_async_copy(stage[slot], xs_hbm.at[pl.ds(0, tm * TILE_ROWS), :], sem.at[slot]).wait()

    @pl.when(i == 0)
    def _():
        def per_expert(e, carry):
            def per_row(r, c):
                zero_copy(r).start()
                return c
            return lax.fori_loop(cend_ref[e], pend_ref[e], per_row, carry)
        lax.fori_loop(0, N_EXPERTS, per_expert, 0)

    for slot in range(2):
        @pl.when(i % 2 == slot)
        def _(slot=slot):
            @pl.when(i >= 2)
            def _():
                wait_slot(slot)
            stage[slot][...] = h3_ref[...]
            for k in range(TOP_K):
                def body(jj, carry, k=k):
                    for u in range(ISSUE_UNROLL):
                        j = jj * ISSUE_UNROLL + u
                        _tile_copy(stage[slot], j, xs_hbm, dst_ref[k, j], sem.at[slot]).start()
                    return carry
                lax.fori_loop(0, tm // ISSUE_UNROLL, body, 0)

    @pl.when(i == nsteps - 1)
    def _():
        wait_slot(0)
        wait_slot(1)

        def wait_zero(r, carry):
            zero_copy(r).wait()
            return carry
        lax.fori_loop(0, pend_ref[N_EXPERTS - 1] - n_assign, wait_zero, 0)


def _dispatch_rows(h3, dest, cend, pend, rows):
    _, n = dest.shape
    tm = SCATTER_TILE
    assert n // tm >= 2
    stage = pltpu.VMEM((tm * TILE_ROWS, LANES), U32)
    grid_spec = pltpu.PrefetchScalarGridSpec(
        num_scalar_prefetch=2,
        grid=(n // tm,),
        in_specs=[pl.BlockSpec((TOP_K, tm), lambda i, ce, pe: (0, i), memory_space=pltpu.SMEM),
                  pl.BlockSpec((tm * TILE_ROWS, LANES), lambda i, ce, pe: (i, 0)),
                  pl.BlockSpec((TILE_ROWS, LANES), lambda i, ce, pe: (0, 0))],
        out_specs=pl.BlockSpec(memory_space=pl.ANY),
        scratch_shapes=[stage, stage, pltpu.SemaphoreType.DMA((2,)), pltpu.SemaphoreType.DMA],
    )
    return pl.pallas_call(
        functools.partial(_scatter_kernel, n_assign=n * TOP_K),
        grid_spec=grid_spec,
        out_shape=jax.ShapeDtypeStruct((rows * TILE_ROWS, LANES), U32),
        compiler_params=_params(("arbitrary",)),
        name="dispatch_scatter",
    )(cend, pend, dest, h3, jnp.zeros((TILE_ROWS, LANES), U32))


def _ffn_kernel(be_ref, nu_ref, x_ref, *refs):
    w_refs, (y_ref, wgb, wub, wdb) = refs[:3 * FFN_PER_STEP], refs[3 * FFN_PER_STEP:]
    nused = nu_ref[0]
    for part in range(FFN_PER_STEP):
        blk = pl.program_id(0) * FFN_PER_STEP + part
        wg_ref, wu_ref, wd_ref = w_refs[3 * part:3 * part + 3]
        base = part * FFN_BLOCK * TILE_ROWS

        @pl.when((blk == 0) | (be_ref[blk] != be_ref[jnp.maximum(blk - 1, 0)]))
        def _(wg_ref=wg_ref, wu_ref=wu_ref, wd_ref=wd_ref):
            wgb[...] = wg_ref[0].astype(BF16)
            wub[...] = wu_ref[0].astype(BF16)
            wdb[...] = wd_ref[0].astype(BF16)

        @pl.when(blk < nused)
        def _(base=base):
            x = jnp.concatenate([c.astype(BF16) for c in _from_row_tiles(x_ref, FFN_BLOCK, base)], axis=1)
            act = _silu(jnp.dot(x, wgb[...], preferred_element_type=F32)) * jnp.dot(
                x, wub[...], preferred_element_type=F32)
            _to_row_tiles(y_ref, jnp.dot(act.astype(BF16), wdb[...], preferred_element_type=F32), base)

        @pl.when(blk >= nused)
        def _(base=base):
            y_ref[pl.ds(base, FFN_BLOCK * TILE_ROWS), :] = jnp.zeros((FFN_BLOCK * TILE_ROWS, LANES), U32)


def _routed_experts(xs, block_e, nused, w_gate, w_up, w_down):
    rows = xs.shape[0] // TILE_ROWS
    nblk = rows // FFN_BLOCK
    _, d, hid = w_gate.shape
    step_rows = FFN_PER_STEP * FFN_BLOCK * TILE_ROWS
    w_specs = []
    for part in range(FFN_PER_STEP):
        expert = lambda i, be, nu, part=part: (be[i * FFN_PER_STEP + part], 0, 0)
        w_specs += [pl.BlockSpec((1, d, hid), expert), pl.BlockSpec((1, d, hid), expert),
                    pl.BlockSpec((1, hid, d), expert)]
    grid_spec = pltpu.PrefetchScalarGridSpec(
        num_scalar_prefetch=2,
        grid=(nblk // FFN_PER_STEP,),
        in_specs=[pl.BlockSpec((step_rows, LANES),
                               lambda i, be, nu: (jnp.minimum(i, (nu[0] - 1) // FFN_PER_STEP), 0))] + w_specs,
        out_specs=pl.BlockSpec((step_rows, LANES), lambda i, be, nu: (i, 0)),
        scratch_shapes=[pltpu.VMEM((d, hid), BF16), pltpu.VMEM((d, hid), BF16), pltpu.VMEM((hid, d), BF16)],
    )
    return pl.pallas_call(
        _ffn_kernel,
        grid_spec=grid_spec,
        out_shape=jax.ShapeDtypeStruct((rows * TILE_ROWS, LANES), U32),
        compiler_params=_params(("arbitrary",)),
        name="routed_experts",
    )(block_e, nused, xs, *([w_gate, w_up, w_down] * FFN_PER_STEP))


def _gather_start(idx_ref, src_hbm, dst_vmem, sem):
    nk, tm = idx_ref.shape
    for k in range(nk):
        def body(jj, carry, k=k):
            for u in range(ISSUE_UNROLL):
                j = jj * ISSUE_UNROLL + u
                _tile_copy(src_hbm, idx_ref[k, j], dst_vmem, k * tm + j, sem).start()
            return carry
        lax.fori_loop(0, tm // ISSUE_UNROLL, body, 0)


def _gather_wait(src_hbm, dst_vmem, sem):
    pltpu.make_async_copy(src_hbm.at[pl.ds(0, dst_vmem.shape[0]), :], dst_vmem, sem).wait()


def _combine_kernel(dst_ref, dstn_ref, wt_ref, xp_ref, g2_ref, gfin_ref, ys_hbm, o_ref, yb0, yb1, sem, *, nsteps):
    i = pl.program_id(0)
    tm = COMBINE_TILE
    bufs = (yb0, yb1)

    @pl.when(i == 0)
    def _():
        _gather_start(dst_ref, ys_hbm, yb0, sem.at[0])

    for slot in range(2):
        @pl.when((i + 1 < nsteps) & ((i + 1) % 2 == slot))
        def _(slot=slot):
            _gather_start(dstn_ref, ys_hbm, bufs[slot], sem.at[slot])

    for slot in range(2):
        @pl.when(i % 2 == slot)
        def _(slot=slot):
            _gather_wait(ys_hbm, bufs[slot], sem.at[slot])
            wt = wt_ref[...]
            cols = None
            for k in range(TOP_K):
                part = [c * wt[:, k:k + 1] for c in _from_row_tiles(bufs[slot], tm, base=k * tm * TILE_ROWS)]
                cols = part if cols is None else [a + b for a, b in zip(cols, part)]
            routed = jnp.concatenate(cols, axis=1)
            o_ref[0] = _rms(xp_ref[0] + g2_ref[0] * routed) * gfin_ref[...]


def _combine(ys, dest, wts_t, xp, gate2, g_final):
    bsz, s, d = xp.shape
    tm = COMBINE_TILE
    nt = s // tm
    nsteps = bsz * nt
    vec = pl.BlockSpec((1, 1, d), lambda i: (i // nt, 0, 0))
    return pl.pallas_call(
        functools.partial(_combine_kernel, nsteps=nsteps),
        grid=(nsteps,),
        in_specs=[pl.BlockSpec((TOP_K, tm), lambda i: (0, i), memory_space=pltpu.SMEM),
                  pl.BlockSpec((TOP_K, tm), lambda i: (0, jnp.minimum(i + 1, nsteps - 1)),
                               memory_space=pltpu.SMEM),
                  pl.BlockSpec((tm, TOP_K), lambda i: (i, 0)),
                  pl.BlockSpec((1, tm, d), lambda i: (i // nt, i % nt, 0)),
                  vec,
                  pl.BlockSpec((1, d), lambda i: (0, 0)),
                  pl.BlockSpec(memory_space=pl.ANY)],
        out_specs=pl.BlockSpec((1, tm, d), lambda i: (i // nt, i % nt, 0)),
        out_shape=jax.ShapeDtypeStruct((bsz, s, d), F32),
        scratch_shapes=[pltpu.VMEM((tm * TOP_K * TILE_ROWS, LANES), U32),
                        pltpu.VMEM((tm * TOP_K * TILE_ROWS, LANES), U32),
                        pltpu.SemaphoreType.DMA((2,))],
        compiler_params=_params(("arbitrary",)),
        name="combine_final_norm",
    )(dest, dest, wts_t, xp, gate2, g_final.reshape(1, d), ys)


def kernel(x, c, w_in, b_forget, g_fox_out, g_dil_out, w_out, w_ada, b_ada, w_router, router_bias,
           w_exp_gate, w_exp_up, w_exp_down, w_sh_gate, w_sh_up, w_sh_down, rel_bias, g_final):
    bsz, s, d = x.shape
    depth = w_in.shape[0]
    assert depth == 1 and d == 2 * TILE_ROWS * LANES and s % (DIL_BLOCK * DILATIONS[-1]) == 0
    l = 0
    mod = _modulation(c, w_ada[l], b_ada[l])
    shift1, scale1, gate1, shift2, scale2, gate2 = [m[:, None, :] for m in jnp.split(mod, 6, axis=-1)]

    qscale = HEAD_DIM ** -0.5
    o3 = 3 * WIDTH
    w = w_in[l]
    wq_t = (w[:, :WIDTH] * (qscale * LOG2E)).T.astype(BF16)
    wk = w[:, WIDTH:2 * WIDTH].astype(BF16)
    wv_t = w[:, 2 * WIDTH:o3].T.astype(BF16)
    w_flog = jnp.pad(w[:, o3:o3 + N_HEADS], ((0, 0), (0, LANES - N_HEADS))).astype(BF16)
    wd0 = o3 + N_HEADS
    w_dil = jnp.concatenate([w[:, wd0:wd0 + WIDTH] * qscale, w[:, wd0 + WIDTH:]], axis=1).astype(BF16)

    q_t, k_f, v_t, qkv_d, flog = _in_projection(x, scale1, shift1, wq_t, wk, wv_t, w_dil, w_flog)
    y_fox = _fox_attention(q_t, k_f, _forget_cumsum(flog, b_forget[l]), v_t)
    y_dil = _dilated_attention(qkv_d, _dilated_bias(rel_bias))

    wr = w_router[l].T
    wr_hi = wr.astype(BF16)
    wr_lo = (wr - wr_hi.astype(F32)).astype(BF16)
    wo = w_out[l].astype(BF16)
    xp, h3, logits_t = _out_projection(
        y_fox, y_dil, x, gate1, scale2, shift2, gate2, g_fox_out[l].reshape(1, WIDTH),
        g_dil_out[l].reshape(1, WIDTH), wo[:WIDTH], wo[WIDTH:], wr_hi, wr_lo,
        w_sh_gate[l].astype(BF16), w_sh_up[l].astype(BF16), w_sh_down[l].astype(BF16))

    eidx, wts, rank, counts = _route(logits_t, router_bias[l])
    rows = bsz * s * TOP_K + N_EXPERTS * FFN_BLOCK
    pstart, cend, pend, block_e, nused = _block_plan(counts, rows // FFN_BLOCK)
    dest = _dest_rows(eidx, rank, pstart)
    xs = _dispatch_rows(h3, dest, cend, pend, rows)
    ys = _routed_experts(xs, block_e, nused, w_exp_gate[l], w_exp_up[l], w_exp_down[l])
    return _combine(ys, dest, wts.T, xp, gate2, g_final)
```

```python
import functools

import numpy as np
import jax
import jax.numpy as jnp
from jax import lax
from jax.experimental import pallas as pl
from jax.experimental.pallas import tpu as pltpu

F32 = jnp.float32
BF16 = jnp.bfloat16

HEAD_DIM = 64
N_HEADS = 8
WIDTH = N_HEADS * HEAD_DIM
LANES = 128
TILE_ROWS = 4
U32 = jnp.uint32
DIL_UNROLL = 8
HEAD_PAIRS = WIDTH // LANES
DIL_BLOCK = 128
DILATIONS = (1, 4, 16)
T5_NUM_BUCKETS = 32
T5_MAX_DISTANCE = 2048
N_EXPERTS = 256
TOP_K = 8
N_GROUPS = 8
TOP_K_GROUPS = 4
ROUTED_SCALE = 2.5
EPS = 1e-6
NEG = -1e30
LOG2E = float(np.log2(np.e))
VMEM_LIMIT = 56 * 1024 * 1024

ROW_TILE = 512
FOX_TILE = 256
FFN_BLOCK = 256
FFN_PER_STEP = 2
COMBINE_TILE = 128
SCATTER_TILE = 128
ROUTE_TILE = 512
ISSUE_UNROLL = 8


def _params(semantics):
    return pltpu.CompilerParams(dimension_semantics=semantics, vmem_limit_bytes=VMEM_LIMIT)


def _rms(x):
    return x * lax.rsqrt(jnp.mean(x * x, axis=-1, keepdims=True) + EPS)


def _silu(x):
    return x * jax.nn.sigmoid(x)


def _dot_nt(a, b):
    return lax.dot_general(a, b, (((1,), (1,)), ((), ())), preferred_element_type=F32)


def _mod_kernel(c_ref, w_ref, b_ref, o_ref):
    o_ref[...] = jnp.dot(_silu(c_ref[...]), w_ref[...], precision=lax.Precision.HIGHEST,
                         preferred_element_type=F32) + b_ref[...]


def _modulation(c, w_ada, b_ada):
    bsz, d = c.shape
    n = w_ada.shape[1]
    tn = 1536
    return pl.pallas_call(
        _mod_kernel,
        grid=(n // tn,),
        in_specs=[pl.BlockSpec((bsz, d), lambda j: (0, 0)),
                  pl.BlockSpec((d, tn), lambda j: (0, j)),
                  pl.BlockSpec((1, tn), lambda j: (0, j))],
        out_specs=pl.BlockSpec((bsz, tn), lambda j: (0, j)),
        out_shape=jax.ShapeDtypeStruct((bsz, n), F32),
        compiler_params=_params(("arbitrary",)),
        name="adaln_mod",
    )(c, w_ada, b_ada.reshape(1, n))


def _inproj_kernel(x_ref, sc_ref, sh_ref, wq_ref, wk_ref, wv_ref, wd_ref, wl_ref,
                   oq_ref, ok_ref, ov_ref, od_ref, ol_ref):
    h = _rms(x_ref[0]) * (1.0 + sc_ref[0]) + sh_ref[0]
    hb = h.astype(BF16)
    oq_ref[0] = _dot_nt(wq_ref[...], hb).astype(BF16)
    ov_ref[0] = _dot_nt(wv_ref[...], hb).astype(BF16)
    ok_ref[0] = jnp.dot(hb, wk_ref[...], preferred_element_type=F32).astype(BF16)
    od_ref[0] = jnp.dot(hb, wd_ref[...], preferred_element_type=F32).astype(BF16)
    ol_ref[0] = jnp.dot(hb, wl_ref[...], preferred_element_type=F32)


def _in_projection(x, scale1, shift1, wq_t, wk, wv_t, w_dil, w_flog):
    bsz, s, d = x.shape
    tm = ROW_TILE
    vec = pl.BlockSpec((1, 1, d), lambda b, i: (b, 0, 0))
    full = lambda w: pl.BlockSpec(w.shape, lambda b, i: (0, 0))
    row = lambda n: pl.BlockSpec((1, tm, n), lambda b, i: (b, i, 0))
    col = pl.BlockSpec((1, WIDTH, tm), lambda b, i: (b, 0, i))
    return pl.pallas_call(
        _inproj_kernel,
        grid=(bsz, s // tm),
        in_specs=[row(d), vec, vec, full(wq_t), full(wk), full(wv_t), full(w_dil), full(w_flog)],
        out_specs=[col, row(WIDTH), col, row(3 * WIDTH), row(LANES)],
        out_shape=[jax.ShapeDtypeStruct((bsz, WIDTH, s), BF16),
                   jax.ShapeDtypeStruct((bsz, s, WIDTH), BF16),
                   jax.ShapeDtypeStruct((bsz, WIDTH, s), BF16),
                   jax.ShapeDtypeStruct((bsz, s, 3 * WIDTH), BF16),
                   jax.ShapeDtypeStruct((bsz, s, LANES), F32)],
        compiler_params=_params(("parallel", "arbitrary")),
        name="norm1_inproj",
    )(x, scale1, shift1, wq_t, wk, wv_t, w_dil, w_flog)


FORGET_PARTS = 3


def _split_bf16(x):
    parts = []
    for _ in range(FORGET_PARTS):
        p = x.astype(BF16)
        parts.append(p)
        x = x - p.astype(F32)
    return parts


def _forget_kernel(f_ref, b_ref, tri_ref, place_ref, o_ref):
    blk = tri_ref.shape[0]
    carry = jnp.zeros((1, LANES), F32)
    for c in range(f_ref.shape[1] // blk):
        rows = pl.ds(c * blk, blk)
        z = f_ref[0, rows, :] + b_ref[...]
        ls = jnp.minimum(z, 0.0) - jnp.log1p(jnp.exp(-jnp.abs(z)))
        cum = carry
        for p in _split_bf16(ls):
            cum = cum + jnp.dot(tri_ref[...], p, preferred_element_type=F32)
        carry = cum[blk - 1:blk, :]
        out = None
        for n, p in enumerate(_split_bf16(cum * LOG2E)):
            d = jnp.dot(p, place_ref[n], preferred_element_type=F32)
            out = d if out is None else out + d
        o_ref[0, rows, :] = out.astype(BF16)


def _forget_cumsum(flog, b_forget):
    bsz, s, _ = flog.shape
    blk = 512
    place = np.zeros((FORGET_PARTS, LANES, WIDTH), np.float32)
    for h in range(N_HEADS):
        for n in range(FORGET_PARTS):
            place[n, h, (h // 2) * LANES + (h % 2) * HEAD_DIM + n] = 1.0
    return pl.pallas_call(
        _forget_kernel,
        grid=(bsz,),
        in_specs=[pl.BlockSpec((1, s, LANES), lambda b: (b, 0, 0)),
                  pl.BlockSpec((1, LANES), lambda b: (0, 0)),
                  pl.BlockSpec((blk, blk), lambda b: (0, 0)),
                  pl.BlockSpec(place.shape, lambda b: (0, 0, 0))],
        out_specs=pl.BlockSpec((1, s, WIDTH), lambda b: (b, 0, 0)),
        out_shape=jax.ShapeDtypeStruct((bsz, s, WIDTH), BF16),
        compiler_params=_params(("parallel",)),
        name="forget_cumsum",
    )(flog, jnp.pad(b_forget, (0, LANES - N_HEADS)).reshape(1, LANES),
      jnp.tril(jnp.ones((blk, blk), BF16)), jnp.asarray(place, BF16))


def _stack_heads(q):
    lane = lax.broadcasted_iota(jnp.int32, q.shape, 1)
    zero = jnp.zeros_like(q)
    return jnp.concatenate([jnp.where(lane < HEAD_DIM, q, zero), jnp.where(lane >= HEAD_DIM, q, zero)], axis=0)


def _unstack_heads(o2):
    rows = o2.shape[0] // 2
    lane = lax.broadcasted_iota(jnp.int32, (rows, LANES), 1)
    return jnp.where(lane < HEAD_DIM, o2[:rows], o2[rows:])


def _fox_kernel(q_ref, k_ref, fa_ref, v_ref, o_ref, m_s, l_s, acc_s, s_a, s_b, *, t):
    i = pl.program_id(2)
    q_t = q_ref[0]
    row = lax.broadcasted_iota(jnp.int32, (LANES, 2 * t), 0)
    col = lax.broadcasted_iota(jnp.int32, (LANES, 2 * t), 1)
    head_row = jnp.where(col < t, 0, HEAD_DIM)
    q2 = jnp.concatenate([q_t, q_t], axis=1)
    q2 = jnp.where((row >= head_row) & (row < head_row + HEAD_DIM), q2, jnp.zeros_like(q2))
    minus = jnp.where((row >= head_row) & (row < head_row + FORGET_PARTS), -1.0, 0.0).astype(BF16)
    qa = jnp.concatenate([q2, minus], axis=0)
    m_s[...] = jnp.full(m_s.shape, NEG, F32)
    l_s[...] = jnp.zeros(l_s.shape, F32)
    acc_s[...] = jnp.zeros(acc_s.shape, F32)

    def scores(ks, tk):
        kk = jnp.concatenate([k_ref[0, pl.ds(ks, tk), :], fa_ref[0, pl.ds(ks, tk), :]], axis=1)
        return jnp.dot(kk, qa, preferred_element_type=F32)

    def absorb(s, ks, tk):
        m_prev = m_s[...]
        m_new = jnp.maximum(m_prev, jnp.max(s, axis=0, keepdims=True))
        alpha = jnp.exp2(m_prev - m_new)
        p = jnp.exp2(s - m_new)
        l_s[...] = alpha * l_s[...] + jnp.sum(p, axis=0, keepdims=True)
        acc_s[...] = alpha * acc_s[...] + jnp.dot(v_ref[0, :, pl.ds(ks, tk)], p.astype(BF16),
                                                  preferred_element_type=F32)
        m_s[...] = m_new

    npair = i // 2

    def tile(j):
        return pl.multiple_of(j * 2 * t, 2 * t)

    @pl.when(npair > 0)
    def _():
        s_a[...] = scores(0, 2 * t)

        def body(jj, carry):
            j = 2 * jj
            s_b[...] = scores(tile(j + 1), 2 * t)
            absorb(s_a[...], tile(j), 2 * t)
            s_a[...] = scores(tile(jnp.minimum(j + 2, npair - 1)), 2 * t)
            absorb(s_b[...], tile(j + 1), 2 * t)
            return carry

        lax.fori_loop(0, npair // 2, body, 0)

        @pl.when(npair % 2 == 1)
        def _():
            absorb(s_a[...], tile(npair - 1), 2 * t)

    @pl.when(i % 2 == 1)
    def _():
        ks = pl.multiple_of(npair * 2 * t, 2 * t)
        absorb(scores(ks, t), ks, t)

    ks = pl.multiple_of(i * t, t)
    r = lax.broadcasted_iota(jnp.int32, (t, 2 * t), 0)
    c = lax.broadcasted_iota(jnp.int32, (t, 2 * t), 1)
    absorb(jnp.where(r <= jnp.where(c >= t, c - t, c), scores(ks, t), NEG), ks, t)
    o2 = acc_s[...] / l_s[...]
    o_t = jnp.where(lax.broadcasted_iota(jnp.int32, (LANES, t), 0) < HEAD_DIM, o2[:, :t], o2[:, t:])
    o_ref[0] = o_t.T


def _fox_attention(q_t, k, f_aug, v_t):
    bsz, s, _ = k.shape
    t = FOX_TILE
    keys = pl.BlockSpec((1, s, LANES), lambda b, h, i: (b, 0, h))
    return pl.pallas_call(
        functools.partial(_fox_kernel, t=t),
        grid=(bsz, HEAD_PAIRS, s // t),
        in_specs=[pl.BlockSpec((1, LANES, t), lambda b, h, i: (b, h, i)),
                  keys, keys,
                  pl.BlockSpec((1, LANES, s), lambda b, h, i: (b, h, 0))],
        out_specs=pl.BlockSpec((1, t, LANES), lambda b, h, i: (b, i, h)),
        out_shape=jax.ShapeDtypeStruct((bsz, s, WIDTH), F32),
        scratch_shapes=[pltpu.VMEM((1, 2 * t), F32), pltpu.VMEM((1, 2 * t), F32),
                        pltpu.VMEM((LANES, 2 * t), F32), pltpu.VMEM((2 * t, 2 * t), F32),
                        pltpu.VMEM((2 * t, 2 * t), F32)],
        compiler_params=_params(("parallel", "parallel", "arbitrary")),
        name="fox_attention",
    )(q_t, k, f_aug, v_t)


def _t5_bucket(dist):
    max_exact = T5_NUM_BUCKETS // 2
    d = np.maximum(dist, 1).astype(np.float32)
    large = max_exact + (np.log(d / max_exact) / np.log(T5_MAX_DISTANCE / max_exact)
                         * (T5_NUM_BUCKETS - max_exact)).astype(np.int32)
    large = np.minimum(large, T5_NUM_BUCKETS - 1)
    return np.where(dist < max_exact, dist, large).astype(np.int32)


def _dilated_bias(rel_bias):
    blk = DIL_BLOCK
    period = 3 * blk
    m = np.arange(period)
    rel = np.where(m < 2 * blk, blk - m, blk - (m - period))
    band = (rel >= 0) & (rel <= blk)
    onehot = np.zeros((len(DILATIONS), period, T5_NUM_BUCKETS), np.float32)
    for bi, dil in enumerate(DILATIONS):
        onehot[bi, m, _t5_bucket(np.clip(rel, 0, blk) * dil)] = 1.0
    w = jnp.einsum('bmk,kh->bhm', jnp.asarray(onehot), rel_bias.astype(F32),
                   precision=lax.Precision.HIGHEST)
    w = jnp.where(jnp.asarray(band), w, NEG)
    flat = jnp.tile(w, (1, 1, blk))[:, :, :blk * (period - 1)]
    table = flat.reshape(len(DILATIONS), N_HEADS, blk, period - 1)[..., :2 * blk]
    table = table.reshape(len(DILATIONS), HEAD_PAIRS, 2 * blk, 2 * blk)
    first = jnp.where(jnp.arange(2 * blk) < blk, NEG, table)
    return jnp.stack([table, first], axis=1)


def _dil_kernel(q_ref, k_ref, v_ref, bias_ref, o_ref, qf, kf, vf, ob0, ob1, ob2, ls0, ls1, ls2, *, s_len):
    blk = DIL_BLOCK
    qf[...] = q_ref[0].astype(F32)
    kf[...] = k_ref[0].astype(F32)
    vf[...] = v_ref[0].astype(F32)
    for bi, (dil, ob, ls) in enumerate(zip(DILATIONS, (ob0, ob1, ob2), (ls0, ls1, ls2))):
        span = blk * dil
        nb = s_len // span

        def rows(start, dil=dil):
            return pl.ds(start, blk) if dil == 1 else pl.ds(start, blk, stride=dil)

        def block(tix, carry, bi=bi, span=span, nb=nb, ob=ob, ls=ls, rows=rows):
            n = tix % nb
            start = n * span + tix // nb
            prev = jnp.maximum(start - span, 0)
            q2 = _stack_heads(qf[rows(start), :].astype(BF16))
            kk = jnp.concatenate([kf[rows(prev), :], kf[rows(start), :]], axis=0).astype(BF16)
            vv = jnp.concatenate([vf[rows(prev), :], vf[rows(start), :]], axis=0).astype(BF16)
            s = lax.dot_general(q2, kk, (((1,), (1,)), ((), ())), preferred_element_type=F32)
            s = s + bias_ref[bi, jnp.where(n == 0, 1, 0), 0]
            m = jnp.max(s, axis=1, keepdims=True)
            p = jnp.exp(s - m)
            l = jnp.sum(p, axis=1, keepdims=True)
            o2 = jnp.dot(p.astype(BF16), vv, preferred_element_type=F32) / l
            lse = jnp.broadcast_to(m + jnp.log(l), (2 * blk, LANES))
            ob[rows(start), :] = _unstack_heads(o2)
            ls[rows(start), :] = _unstack_heads(lse)
            return carry

        def blocks(g, carry, block=block):
            for u in range(DIL_UNROLL):
                block(g * DIL_UNROLL + u, carry)
            return carry

        lax.fori_loop(0, dil * nb // DIL_UNROLL, blocks, 0)

    chunk = 512
    for c in range(s_len // chunk):
        r = pl.ds(c * chunk, chunk)
        l0, l1, l2 = ls0[r, :], ls1[r, :], ls2[r, :]
        mx = jnp.maximum(jnp.maximum(l0, l1), l2)
        e0, e1, e2 = jnp.exp(l0 - mx), jnp.exp(l1 - mx), jnp.exp(l2 - mx)
        o_ref[0, r, :] = (e0 * ob0[r, :] + e1 * ob1[r, :] + e2 * ob2[r, :]) / (e0 + e1 + e2)


def _dilated_attention(qkv, bias):
    bsz, s, _ = qkv.shape
    col = lambda off: pl.BlockSpec((1, s, LANES), lambda b, h: (b, 0, off + h))
    buf = pltpu.VMEM((s, LANES), F32)
    return pl.pallas_call(
        functools.partial(_dil_kernel, s_len=s),
        grid=(bsz, HEAD_PAIRS),
        in_specs=[col(0), col(HEAD_PAIRS), col(2 * HEAD_PAIRS),
                  pl.BlockSpec((len(DILATIONS), 2, 1, 2 * DIL_BLOCK, 2 * DIL_BLOCK), lambda b, h: (0, 0, h, 0, 0))],
        out_specs=pl.BlockSpec((1, s, LANES), lambda b, h: (b, 0, h)),
        out_shape=jax.ShapeDtypeStruct((bsz, s, WIDTH), F32),
        scratch_shapes=[buf] * 9,
        compiler_params=_params(("parallel", "arbitrary")),
        name="dilated_attention",
    )(qkv, qkv, qkv, bias)


def _bf16_bits(x):
    return pltpu.bitcast(x.astype(BF16).astype(F32), U32)


def _to_row_tiles(dst_ref, x, base=0):
    rows, d = x.shape
    for c in range(TILE_ROWS):
        lo = _bf16_bits(x[:, c * LANES:(c + 1) * LANES]) >> 16
        hi = _bf16_bits(x[:, d // 2 + c * LANES:d // 2 + (c + 1) * LANES])
        dst_ref[pl.ds(base + c, rows, stride=TILE_ROWS), :] = lo | hi


def _from_row_tiles(src_ref, rows, base=0):
    lo, hi = [], []
    for c in range(TILE_ROWS):
        w = src_ref[pl.ds(base + c, rows, stride=TILE_ROWS), :]
        lo.append(pltpu.bitcast(w << 16, F32))
        hi.append(pltpu.bitcast(w & jnp.uint32(0xFFFF0000), F32))
    return lo + hi


def _outproj_kernel(yf_ref, yd_ref, x_ref, g1_ref, sc_ref, sh_ref, g2_ref, gf_ref, gd_ref, wo1_ref, wo2_ref,
                    wrh_ref, wrl_ref, wsg_ref, wsu_ref, wsd_ref, xp_ref, h3_ref, lg_ref):
    nf = (_rms(yf_ref[0]) * gf_ref[...]).astype(BF16)
    nd = (_rms(yd_ref[0]) * gd_ref[...]).astype(BF16)
    mix = (jnp.dot(nf, wo1_ref[...], preferred_element_type=F32)
           + jnp.dot(nd, wo2_ref[...], preferred_element_type=F32))
    x1 = x_ref[0] + g1_ref[0] * mix
    h2 = _rms(x1) * (1.0 + sc_ref[0]) + sh_ref[0]
    hb = h2.astype(BF16)
    hl = (h2 - hb.astype(F32)).astype(BF16)
    lg_ref[...] = _dot_nt(wrh_ref[...], hb) + _dot_nt(wrh_ref[...], hl) + _dot_nt(wrl_ref[...], hb)
    act = _silu(jnp.dot(hb, wsg_ref[...], preferred_element_type=F32)) * jnp.dot(
        hb, wsu_ref[...], preferred_element_type=F32)
    shared = jnp.dot(act.astype(BF16), wsd_ref[...], preferred_element_type=F32)
    xp_ref[0] = x1 + g2_ref[0] * shared
    _to_row_tiles(h3_ref, h2)


def _out_projection(y_fox, y_dil, x, gate1, scale2, shift2, gate2, g_fox, g_dil, wo1, wo2, wr_hi, wr_lo,
                    wsg, wsu, wsd):
    bsz, s, d = x.shape
    tm = ROW_TILE
    nt = s // tm
    vec = pl.BlockSpec((1, 1, d), lambda b, i: (b, 0, 0))
    full = lambda w: pl.BlockSpec(w.shape, lambda b, i: (0,) * w.ndim)
    row = lambda n: pl.BlockSpec((1, tm, n), lambda b, i: (b, i, 0))
    return pl.pallas_call(
        _outproj_kernel,
        grid=(bsz, nt),
        in_specs=[row(WIDTH), row(WIDTH), row(d), vec, vec, vec, vec, full(g_fox), full(g_dil), full(wo1),
                  full(wo2), full(wr_hi), full(wr_lo), full(wsg), full(wsu), full(wsd)],
        out_specs=[row(d),
                   pl.BlockSpec((tm * TILE_ROWS, LANES), lambda b, i: (b * nt + i, 0)),
                   pl.BlockSpec((N_EXPERTS, tm), lambda b, i: (0, b * nt + i))],
        out_shape=[jax.ShapeDtypeStruct((bsz, s, d), F32),
                   jax.ShapeDtypeStruct((bsz * s * TILE_ROWS, LANES), U32),
                   jax.ShapeDtypeStruct((N_EXPERTS, bsz * s), F32)],
        compiler_params=_params(("parallel", "arbitrary")),
        name="outproj_norm2_router_shared",
    )(y_fox, y_dil, x, gate1, scale2, shift2, gate2, g_fox, g_dil, wo1, wo2, wr_hi, wr_lo, wsg, wsu, wsd)


def _first_argmax(v, row, size):
    m = jnp.max(v, axis=0, keepdims=True)
    return m, jnp.min(jnp.where(v == m, row, size), axis=0, keepdims=True)


def _route_kernel(lg_ref, rb_ref, tri_ref, e_ref, w_ref, rk_ref, cnt_ref, cnt_s):
    @pl.when(pl.program_id(0) == 0)
    def _():
        cnt_s[...] = jnp.zeros(cnt_s.shape, F32)

    t = lg_ref.shape[1]
    gsz = N_EXPERTS // N_GROUPS
    scores = jax.nn.sigmoid(lg_ref[...])
    sel = scores + rb_ref[...]
    row_g = lax.broadcasted_iota(jnp.int32, (gsz, t), 0)
    grp = []
    for g in range(N_GROUPS):
        v = sel[g * gsz:(g + 1) * gsz]
        m1, i1 = _first_argmax(v, row_g, gsz)
        m2 = jnp.max(jnp.where(row_g == i1, -jnp.inf, v), axis=0, keepdims=True)
        grp.append(m1 + m2)
    gv = jnp.concatenate(grp, axis=0)
    row8 = lax.broadcasted_iota(jnp.int32, (N_GROUPS, t), 0)
    pen = jnp.full((N_GROUPS, t), -jnp.inf, F32)
    for _ in range(TOP_K_GROUPS):
        _, ix = _first_argmax(gv, row8, N_GROUPS)
        pen = jnp.where(row8 == ix, 0.0, pen)
        gv = jnp.where(row8 == ix, -jnp.inf, gv)
    selm = jnp.concatenate([sel[g * gsz:(g + 1) * gsz] + pen[g:g + 1] for g in range(N_GROUPS)], axis=0)

    row = lax.broadcasted_iota(jnp.int32, (N_EXPERTS, t), 0)
    v = selm
    idxs, scs = [], []
    for _ in range(TOP_K):
        _, ix = _first_argmax(v, row, N_EXPERTS)
        hit = row == ix
        idxs.append(ix)
        scs.append(jnp.sum(jnp.where(hit, scores, 0.0), axis=0, keepdims=True))
        v = jnp.where(hit, -jnp.inf, v)
    chosen = jnp.where(v != selm, 1.0, 0.0)
    before = jnp.dot(chosen.astype(BF16), tri_ref[...], preferred_element_type=F32) + cnt_s[...]
    rks = [jnp.sum(jnp.where(row == ix, before, 0.0), axis=0, keepdims=True) for ix in idxs]
    sc = jnp.concatenate(scs, axis=0)
    e_ref[...] = jnp.concatenate(idxs, axis=0)
    w_ref[...] = sc / jnp.sum(sc, axis=0, keepdims=True) * ROUTED_SCALE
    rk_ref[...] = jnp.concatenate(rks, axis=0).astype(jnp.int32)
    cnt_s[...] = cnt_s[...] + jnp.sum(chosen, axis=1, keepdims=True)
    cnt_ref[...] = cnt_s[...]


def _route(logits_t, router_bias):
    e, n = logits_t.shape
    t = ROUTE_TILE
    tri = jnp.triu(jnp.ones((t, t), BF16), k=1)
    tile = pl.BlockSpec((TOP_K, t), lambda i: (0, i))
    return pl.pallas_call(
        _route_kernel,
        grid=(n // t,),
        in_specs=[pl.BlockSpec((e, t), lambda i: (0, i)),
                  pl.BlockSpec((e, 1), lambda i: (0, 0)),
                  pl.BlockSpec((t, t), lambda i: (0, 0))],
        out_specs=[tile, tile, tile, pl.BlockSpec((e, 1), lambda i: (0, 0))],
        out_shape=[jax.ShapeDtypeStruct((TOP_K, n), jnp.int32), jax.ShapeDtypeStruct((TOP_K, n), F32),
                   jax.ShapeDtypeStruct((TOP_K, n), jnp.int32), jax.ShapeDtypeStruct((e, 1), F32)],
        scratch_shapes=[pltpu.VMEM((e, 1), F32)],
        compiler_params=_params(("arbitrary",)),
        name="route_topk_rank",
    )(logits_t, router_bias.reshape(e, 1).astype(F32), tri)


def _dest_kernel(e_ref, rk_ref, ps_ref, d_ref):
    t = e_ref.shape[1]
    row = lax.broadcasted_iota(jnp.int32, (N_EXPERTS, t), 0)
    ps = ps_ref[...]
    base = [jnp.sum(jnp.where(row == e_ref[k:k + 1, :], ps, 0.0), axis=0, keepdims=True) for k in range(TOP_K)]
    d_ref[...] = jnp.concatenate(base, axis=0).astype(jnp.int32) + rk_ref[...]


def _dest_rows(eidx, rank, pstart):
    _, n = eidx.shape
    t = ROUTE_TILE
    tile = pl.BlockSpec((TOP_K, t), lambda i: (0, i))
    return pl.pallas_call(
        _dest_kernel,
        grid=(n // t,),
        in_specs=[tile, tile, pl.BlockSpec((N_EXPERTS, 1), lambda i: (0, 0))],
        out_specs=tile,
        out_shape=jax.ShapeDtypeStruct((TOP_K, n), jnp.int32),
        compiler_params=_params(("parallel",)),
        name="dest_rows",
    )(eidx, rank, pstart.astype(F32).reshape(N_EXPERTS, 1))


def _block_plan(counts, nblk):
    counts = counts.reshape(N_EXPERTS).astype(jnp.int32)
    padded = (counts + FFN_BLOCK - 1) // FFN_BLOCK * FFN_BLOCK
    pend = jnp.cumsum(padded).astype(jnp.int32)
    pstart = pend - padded
    first_row = jnp.arange(nblk, dtype=jnp.int32) * FFN_BLOCK
    block_e = jnp.minimum(jnp.sum(pend[None, :] <= first_row[:, None], axis=1), N_EXPERTS - 1).astype(jnp.int32)
    nused = (pend[-1:] // FFN_BLOCK).astype(jnp.int32)
    return pstart, pstart + counts, pend, block_e, nused


def _tile_copy(src_ref, src_row, dst_ref, dst_row, sem):
    return pltpu.make_async_copy(src_ref.at[pl.ds(pl.multiple_of(src_row * TILE_ROWS, TILE_ROWS), TILE_ROWS), :],
                                 dst_ref.at[pl.ds(pl.multiple_of(dst_row * TILE_ROWS, TILE_ROWS), TILE_ROWS), :],
                                 sem)


def _scatter_kernel(cend_ref, pend_ref, dst_ref, h3_ref, z_ref, xs_hbm, st0, st1, sem, zsem):
    i = pl.program_id(0)
    nsteps = pl.num_programs(0)
    tm = dst_ref.shape[1]
    stage = (st0, st1)

    def zero_copy(r):
        return pltpu.make_async_copy(z_ref, xs_hbm.at[pl.ds(pl.multiple_of(r * TILE_ROWS, TILE_ROWS), TILE_ROWS), :],
                                     zsem)

    def wait_slot(slot):
        for _ in range(TOP_K):
            pltpu.make_async_copy(stage[slot], xs_hbm.at[pl.ds(0, tm * TILE_ROWS), :], sem.at[slot]).wait()

    @pl.when(i == 0)
    def _():
        def start_row(r, c):
            zero_copy(r).start()
            return c

        def wait_row(r, c):
            zero_copy(r).wait()
            return c

        def per_expert(e, carry):
            lax.fori_loop(cend_ref[e], pend_ref[e], start_row, carry)
            prev = jnp.maximum(e - 1, 0)
            return lax.fori_loop(cend_ref[prev], jnp.where(e > 0, pend_ref[prev], cend_ref[prev]), wait_row, carry)

        lax.fori_loop(0, N_EXPERTS, per_expert, 0)
        lax.fori_loop(cend_ref[N_EXPERTS - 1], pend_ref[N_EXPERTS - 1], wait_row, 0)

    for slot in range(2):
        @pl.when(i % 2 == slot)
        def _(slot=slot):
            @pl.when(i >= 2)
            def _():
                wait_slot(slot)
            stage[slot][...] = h3_ref[...]
            for k in range(TOP_K):
                def body(jj, carry, k=k):
                    for u in range(ISSUE_UNROLL):
                        j = jj * ISSUE_UNROLL + u
                        _tile_copy(stage[slot], j, xs_hbm, dst_ref[k, j], sem.at[slot]).start()
                    return carry
                lax.fori_loop(0, tm // ISSUE_UNROLL, body, 0)

    @pl.when(i == nsteps - 1)
    def _():
        wait_slot(0)
        wait_slot(1)


def _dispatch_rows(h3, dest, cend, pend, rows):
    _, n = dest.shape
    tm = SCATTER_TILE
    assert n // tm >= 2
    stage = pltpu.VMEM((tm * TILE_ROWS, LANES), U32)
    grid_spec = pltpu.PrefetchScalarGridSpec(
        num_scalar_prefetch=2,
        grid=(n // tm,),
        in_specs=[pl.BlockSpec((TOP_K, tm), lambda i, ce, pe: (0, i), memory_space=pltpu.SMEM),
                  pl.BlockSpec((tm * TILE_ROWS, LANES), lambda i, ce, pe: (i, 0)),
                  pl.BlockSpec((TILE_ROWS, LANES), lambda i, ce, pe: (0, 0))],
        out_specs=pl.BlockSpec(memory_space=pl.ANY),
        scratch_shapes=[stage, stage, pltpu.SemaphoreType.DMA((2,)), pltpu.SemaphoreType.DMA],
    )
    return pl.pallas_call(
        _scatter_kernel,
        grid_spec=grid_spec,
        out_shape=jax.ShapeDtypeStruct((rows * TILE_ROWS, LANES), U32),
        compiler_params=_params(("arbitrary",)),
        name="dispatch_scatter",
    )(cend, pend, dest, h3, jnp.zeros((TILE_ROWS, LANES), U32))


def _ffn_kernel(be_ref, nu_ref, x_ref, *refs):
    w_refs, (y_ref, wgb, wub, wdb) = refs[:3 * FFN_PER_STEP], refs[3 * FFN_PER_STEP:]
    nused = nu_ref[0]
    for part in range(FFN_PER_STEP):
        blk = pl.program_id(0) * FFN_PER_STEP + part
        wg_ref, wu_ref, wd_ref = w_refs[3 * part:3 * part + 3]
        base = part * FFN_BLOCK * TILE_ROWS

        @pl.when((blk == 0) | (be_ref[blk] != be_ref[jnp.maximum(blk - 1, 0)]))
        def _(wg_ref=wg_ref, wu_ref=wu_ref, wd_ref=wd_ref):
            wgb[...] = wg_ref[0].astype(BF16)
            wub[...] = wu_ref[0].astype(BF16)
            wdb[...] = wd_ref[0].astype(BF16)

        @pl.when(blk < nused)
        def _(base=base):
            x = jnp.concatenate([c.astype(BF16) for c in _from_row_tiles(x_ref, FFN_BLOCK, base)], axis=1)
            act = _silu(jnp.dot(x, wgb[...], preferred_element_type=F32)) * jnp.dot(
                x, wub[...], preferred_element_type=F32)
            _to_row_tiles(y_ref, jnp.dot(act.astype(BF16), wdb[...], preferred_element_type=F32), base)

        @pl.when(blk >= nused)
        def _(base=base):
            y_ref[pl.ds(base, FFN_BLOCK * TILE_ROWS), :] = jnp.zeros((FFN_BLOCK * TILE_ROWS, LANES), U32)


def _routed_experts(xs, block_e, nused, w_gate, w_up, w_down):
    rows = xs.shape[0] // TILE_ROWS
    nblk = rows // FFN_BLOCK
    _, d, hid = w_gate.shape
    step_rows = FFN_PER_STEP * FFN_BLOCK * TILE_ROWS
    w_specs = []
    for part in range(FFN_PER_STEP):
        expert = lambda i, be, nu, part=part: (be[i * FFN_PER_STEP + part], 0, 0)
        w_specs += [pl.BlockSpec((1, d, hid), expert), pl.BlockSpec((1, d, hid), expert),
                    pl.BlockSpec((1, hid, d), expert)]
    grid_spec = pltpu.PrefetchScalarGridSpec(
        num_scalar_prefetch=2,
        grid=(nblk // FFN_PER_STEP,),
        in_specs=[pl.BlockSpec((step_rows, LANES),
                               lambda i, be, nu: (jnp.minimum(i, (nu[0] - 1) // FFN_PER_STEP), 0))] + w_specs,
        out_specs=pl.BlockSpec((step_rows, LANES), lambda i, be, nu: (i, 0)),
        scratch_shapes=[pltpu.VMEM((d, hid), BF16), pltpu.VMEM((d, hid), BF16), pltpu.VMEM((hid, d), BF16)],
    )
    return pl.pallas_call(
        _ffn_kernel,
        grid_spec=grid_spec,
        out_shape=jax.ShapeDtypeStruct((rows * TILE_ROWS, LANES), U32),
        compiler_params=_params(("arbitrary",)),
        name="routed_experts",
    )(block_e, nused, xs, *([w_gate, w_up, w_down] * FFN_PER_STEP))


def _gather_start(idx_ref, src_hbm, dst_vmem, sem):
    nk, tm = idx_ref.shape
    for k in range(nk):
        def body(jj, carry, k=k):
            for u in range(ISSUE_UNROLL):
                j = jj * ISSUE_UNROLL + u
                _tile_copy(src_hbm, idx_ref[k, j], dst_vmem, k * tm + j, sem).start()
            return carry
        lax.fori_loop(0, tm // ISSUE_UNROLL, body, 0)


def _gather_wait(src_hbm, dst_vmem, sem):
    pltpu.make_async_copy(src_hbm.at[pl.ds(0, dst_vmem.shape[0]), :], dst_vmem, sem).wait()


def _combine_kernel(dst_ref, dstn_ref, wt_ref, xp_ref, g2_ref, gfin_ref, ys_hbm, o_ref, yb0, yb1, sem, *, nsteps):
    i = pl.program_id(0)
    tm = COMBINE_TILE
    bufs = (yb0, yb1)

    @pl.when(i == 0)
    def _():
        _gather_start(dst_ref, ys_hbm, yb0, sem.at[0])

    for slot in range(2):
        @pl.when((i + 1 < nsteps) & ((i + 1) % 2 == slot))
        def _(slot=slot):
            _gather_start(dstn_ref, ys_hbm, bufs[slot], sem.at[slot])

    for slot in range(2):
        @pl.when(i % 2 == slot)
        def _(slot=slot):
            _gather_wait(ys_hbm, bufs[slot], sem.at[slot])
            wt = wt_ref[...]
            cols = None
            for k in range(TOP_K):
                part = [c * wt[:, k:k + 1] for c in _from_row_tiles(bufs[slot], tm, base=k * tm * TILE_ROWS)]
                cols = part if cols is None else [a + b for a, b in zip(cols, part)]
            routed = jnp.concatenate(cols, axis=1)
            o_ref[0] = _rms(xp_ref[0] + g2_ref[0] * routed) * gfin_ref[...]


def _combine(ys, dest, wts_t, xp, gate2, g_final):
    bsz, s, d = xp.shape
    tm = COMBINE_TILE
    nt = s // tm
    nsteps = bsz * nt
    vec = pl.BlockSpec((1, 1, d), lambda i: (i // nt, 0, 0))
    return pl.pallas_call(
        functools.partial(_combine_kernel, nsteps=nsteps),
        grid=(nsteps,),
        in_specs=[pl.BlockSpec((TOP_K, tm), lambda i: (0, i), memory_space=pltpu.SMEM),
                  pl.BlockSpec((TOP_K, tm), lambda i: (0, jnp.minimum(i + 1, nsteps - 1)),
                               memory_space=pltpu.SMEM),
                  pl.BlockSpec((tm, TOP_K), lambda i: (i, 0)),
                  pl.BlockSpec((1, tm, d), lambda i: (i // nt, i % nt, 0)),
                  vec,
                  pl.BlockSpec((1, d), lambda i: (0, 0)),
                  pl.BlockSpec(memory_space=pl.ANY)],
        out_specs=pl.BlockSpec((1, tm, d), lambda i: (i // nt, i % nt, 0)),
        out_shape=jax.ShapeDtypeStruct((bsz, s, d), F32),
        scratch_shapes=[pltpu.VMEM((tm * TOP_K * TILE_ROWS, LANES), U32),
                        pltpu.VMEM((tm * TOP_K * TILE_ROWS, LANES), U32),
                        pltpu.SemaphoreType.DMA((2,))],
        compiler_params=_params(("arbitrary",)),
        name="combine_final_norm",
    )(dest, dest, wts_t, xp, gate2, g_final.reshape(1, d), ys)


def kernel(x, c, w_in, b_forget, g_fox_out, g_dil_out, w_out, w_ada, b_ada, w_router, router_bias,
           w_exp_gate, w_exp_up, w_exp_down, w_sh_gate, w_sh_up, w_sh_down, rel_bias, g_final):
    bsz, s, d = x.shape
    depth = w_in.shape[0]
    assert depth == 1 and d == 2 * TILE_ROWS * LANES and s % (DIL_BLOCK * DILATIONS[-1]) == 0
    l = 0
    mod = _modulation(c, w_ada[l], b_ada[l])
    shift1, scale1, gate1, shift2, scale2, gate2 = [m[:, None, :] for m in jnp.split(mod, 6, axis=-1)]

    qscale = HEAD_DIM ** -0.5
    o3 = 3 * WIDTH
    w = w_in[l]
    wq_t = (w[:, :WIDTH] * (qscale * LOG2E)).T.astype(BF16)
    wk = w[:, WIDTH:2 * WIDTH].astype(BF16)
    wv_t = w[:, 2 * WIDTH:o3].T.astype(BF16)
    w_flog = jnp.pad(w[:, o3:o3 + N_HEADS], ((0, 0), (0, LANES - N_HEADS))).astype(BF16)
    wd0 = o3 + N_HEADS
    w_dil = jnp.concatenate([w[:, wd0:wd0 + WIDTH] * qscale, w[:, wd0 + WIDTH:]], axis=1).astype(BF16)

    q_t, k_f, v_t, qkv_d, flog = _in_projection(x, scale1, shift1, wq_t, wk, wv_t, w_dil, w_flog)
    y_fox = _fox_attention(q_t, k_f, _forget_cumsum(flog, b_forget[l]), v_t)
    y_dil = _dilated_attention(qkv_d, _dilated_bias(rel_bias))

    wr = w_router[l].T
    wr_hi = wr.astype(BF16)
    wr_lo = (wr - wr_hi.astype(F32)).astype(BF16)
    wo = w_out[l].astype(BF16)
    xp, h3, logits_t = _out_projection(
        y_fox, y_dil, x, gate1, scale2, shift2, gate2, g_fox_out[l].reshape(1, WIDTH),
        g_dil_out[l].reshape(1, WIDTH), wo[:WIDTH], wo[WIDTH:], wr_hi, wr_lo,
        w_sh_gate[l].astype(BF16), w_sh_up[l].astype(BF16), w_sh_down[l].astype(BF16))

    eidx, wts, rank, counts = _route(logits_t, router_bias[l])
    rows = bsz * s * TOP_K + N_EXPERTS * FFN_BLOCK
    pstart, cend, pend, block_e, nused = _block_plan(counts, rows // FFN_BLOCK)
    dest = _dest_rows(eidx, rank, pstart)
    xs = _dispatch_rows(h3, dest, cend, pend, rows)
    ys = _routed_experts(xs, block_e, nused, w_exp_gate[l], w_exp_up[l], w_exp_down[l])
    return _combine(ys, dest, wts.T, xp, gate2, g_final)
```

```python
import functools

import numpy as np
import jax
import jax.numpy as jnp
from jax import lax
from jax.experimental import pallas as pl
from jax.experimental.pallas import tpu as pltpu

F32 = jnp.float32
BF16 = jnp.bfloat16

HEAD_DIM = 64
N_HEADS = 8
WIDTH = N_HEADS * HEAD_DIM
LANES = 128
TILE_ROWS = 4
U32 = jnp.uint32
DIL_UNROLL = 8
HEAD_PAIRS = WIDTH // LANES
DIL_BLOCK = 128
DILATIONS = (1, 4, 16)
T5_NUM_BUCKETS = 32
T5_MAX_DISTANCE = 2048
N_EXPERTS = 256
TOP_K = 8
N_GROUPS = 8
TOP_K_GROUPS = 4
ROUTED_SCALE = 2.5
EPS = 1e-6
NEG = -1e30
LOG2E = float(np.log2(np.e))
VMEM_LIMIT = 56 * 1024 * 1024

ROW_TILE = 512
FOX_TILE = 256
FFN_BLOCK = 256
FFN_PER_STEP = 2
COMBINE_TILE = 256
SCATTER_TILE = 256
ROUTE_TILE = 512
ISSUE_UNROLL = 8


def _params(semantics):
    return pltpu.CompilerParams(dimension_semantics=semantics, vmem_limit_bytes=VMEM_LIMIT)


def _rms(x):
    return x * lax.rsqrt(jnp.mean(x * x, axis=-1, keepdims=True) + EPS)


def _silu(x):
    return x * jax.nn.sigmoid(x)


def _dot_nt(a, b):
    return lax.dot_general(a, b, (((1,), (1,)), ((), ())), preferred_element_type=F32)


def _mod_kernel(c_ref, w_ref, b_ref, o_ref):
    o_ref[...] = jnp.dot(_silu(c_ref[...]), w_ref[...], precision=lax.Precision.HIGHEST,
                         preferred_element_type=F32) + b_ref[...]


def _modulation(c, w_ada, b_ada):
    bsz, d = c.shape
    n = w_ada.shape[1]
    tn = 1536
    return pl.pallas_call(
        _mod_kernel,
        grid=(n // tn,),
        in_specs=[pl.BlockSpec((bsz, d), lambda j: (0, 0)),
                  pl.BlockSpec((d, tn), lambda j: (0, j)),
                  pl.BlockSpec((1, tn), lambda j: (0, j))],
        out_specs=pl.BlockSpec((bsz, tn), lambda j: (0, j)),
        out_shape=jax.ShapeDtypeStruct((bsz, n), F32),
        compiler_params=_params(("arbitrary",)),
        name="adaln_mod",
    )(c, w_ada, b_ada.reshape(1, n))


def _inproj_kernel(x_ref, sc_ref, sh_ref, wq_ref, wk_ref, wv_ref, wd_ref, wl_ref,
                   oq_ref, ok_ref, ov_ref, od_ref, ol_ref):
    h = _rms(x_ref[0]) * (1.0 + sc_ref[0]) + sh_ref[0]
    hb = h.astype(BF16)
    oq_ref[0] = _dot_nt(wq_ref[...], hb).astype(BF16)
    ov_ref[0] = _dot_nt(wv_ref[...], hb).astype(BF16)
    ok_ref[0] = jnp.dot(hb, wk_ref[...], preferred_element_type=F32).astype(BF16)
    od_ref[0] = jnp.dot(hb, wd_ref[...], preferred_element_type=F32).astype(BF16)
    ol_ref[0] = jnp.dot(hb, wl_ref[...], preferred_element_type=F32)


def _in_projection(x, scale1, shift1, wq_t, wk, wv_t, w_dil, w_flog):
    bsz, s, d = x.shape
    tm = ROW_TILE
    vec = pl.BlockSpec((1, 1, d), lambda b, i: (b, 0, 0))
    full = lambda w: pl.BlockSpec(w.shape, lambda b, i: (0, 0))
    row = lambda n: pl.BlockSpec((1, tm, n), lambda b, i: (b, i, 0))
    col = pl.BlockSpec((1, WIDTH, tm), lambda b, i: (b, 0, i))
    return pl.pallas_call(
        _inproj_kernel,
        grid=(bsz, s // tm),
        in_specs=[row(d), vec, vec, full(wq_t), full(wk), full(wv_t), full(w_dil), full(w_flog)],
        out_specs=[col, row(WIDTH), col, row(3 * WIDTH), row(LANES)],
        out_shape=[jax.ShapeDtypeStruct((bsz, WIDTH, s), BF16),
                   jax.ShapeDtypeStruct((bsz, s, WIDTH), BF16),
                   jax.ShapeDtypeStruct((bsz, WIDTH, s), BF16),
                   jax.ShapeDtypeStruct((bsz, s, 3 * WIDTH), BF16),
                   jax.ShapeDtypeStruct((bsz, s, LANES), F32)],
        compiler_params=_params(("parallel", "arbitrary")),
        name="norm1_inproj",
    )(x, scale1, shift1, wq_t, wk, wv_t, w_dil, w_flog)


FORGET_PARTS = 3


def _split_bf16(x):
    parts = []
    for _ in range(FORGET_PARTS):
        p = x.astype(BF16)
        parts.append(p)
        x = x - p.astype(F32)
    return parts


def _forget_kernel(f_ref, b_ref, tri_ref, place_ref, o_ref):
    blk = tri_ref.shape[0]
    carry = jnp.zeros((1, LANES), F32)
    for c in range(f_ref.shape[1] // blk):
        rows = pl.ds(c * blk, blk)
        z = f_ref[0, rows, :] + b_ref[...]
        ls = jnp.minimum(z, 0.0) - jnp.log1p(jnp.exp(-jnp.abs(z)))
        cum = carry
        for p in _split_bf16(ls):
            cum = cum + jnp.dot(tri_ref[...], p, preferred_element_type=F32)
        carry = cum[blk - 1:blk, :]
        out = None
        for n, p in enumerate(_split_bf16(cum * LOG2E)):
            d = jnp.dot(p, place_ref[n], preferred_element_type=F32)
            out = d if out is None else out + d
        o_ref[0, rows, :] = out.astype(BF16)


def _forget_cumsum(flog, b_forget):
    bsz, s, _ = flog.shape
    blk = 512
    place = np.zeros((FORGET_PARTS, LANES, WIDTH), np.float32)
    for h in range(N_HEADS):
        for n in range(FORGET_PARTS):
            place[n, h, (h // 2) * LANES + (h % 2) * HEAD_DIM + n] = 1.0
    return pl.pallas_call(
        _forget_kernel,
        grid=(bsz,),
        in_specs=[pl.BlockSpec((1, s, LANES), lambda b: (b, 0, 0)),
                  pl.BlockSpec((1, LANES), lambda b: (0, 0)),
                  pl.BlockSpec((blk, blk), lambda b: (0, 0)),
                  pl.BlockSpec(place.shape, lambda b: (0, 0, 0))],
        out_specs=pl.BlockSpec((1, s, WIDTH), lambda b: (b, 0, 0)),
        out_shape=jax.ShapeDtypeStruct((bsz, s, WIDTH), BF16),
        compiler_params=_params(("parallel",)),
        name="forget_cumsum",
    )(flog, jnp.pad(b_forget, (0, LANES - N_HEADS)).reshape(1, LANES),
      jnp.tril(jnp.ones((blk, blk), BF16)), jnp.asarray(place, BF16))


def _stack_heads(q):
    lane = lax.broadcasted_iota(jnp.int32, q.shape, 1)
    zero = jnp.zeros_like(q)
    return jnp.concatenate([jnp.where(lane < HEAD_DIM, q, zero), jnp.where(lane >= HEAD_DIM, q, zero)], axis=0)


def _unstack_heads(o2):
    rows = o2.shape[0] // 2
    lane = lax.broadcasted_iota(jnp.int32, (rows, LANES), 1)
    return jnp.where(lane < HEAD_DIM, o2[:rows], o2[rows:])


def _fox_kernel(q_ref, k_ref, fa_ref, v_ref, o_ref, m_s, l_s, acc_s, s_a, s_b, *, t):
    i = pl.program_id(2)
    q_t = q_ref[0]
    row = lax.broadcasted_iota(jnp.int32, (LANES, 2 * t), 0)
    col = lax.broadcasted_iota(jnp.int32, (LANES, 2 * t), 1)
    head_row = jnp.where(col < t, 0, HEAD_DIM)
    q2 = jnp.concatenate([q_t, q_t], axis=1)
    q2 = jnp.where((row >= head_row) & (row < head_row + HEAD_DIM), q2, jnp.zeros_like(q2))
    minus = jnp.where((row >= head_row) & (row < head_row + FORGET_PARTS), -1.0, 0.0).astype(BF16)
    qa = jnp.concatenate([q2, minus], axis=0)
    m_s[...] = jnp.full(m_s.shape, NEG, F32)
    l_s[...] = jnp.zeros(l_s.shape, F32)
    acc_s[...] = jnp.zeros(acc_s.shape, F32)

    def scores(ks, tk):
        kk = jnp.concatenate([k_ref[0, pl.ds(ks, tk), :], fa_ref[0, pl.ds(ks, tk), :]], axis=1)
        return jnp.dot(kk, qa, preferred_element_type=F32)

    def absorb(s, ks, tk):
        m_prev = m_s[...]
        m_new = jnp.maximum(m_prev, jnp.max(s, axis=0, keepdims=True))
        alpha = jnp.exp2(m_prev - m_new)
        p = jnp.exp2(s - m_new)
        l_s[...] = alpha * l_s[...] + jnp.sum(p, axis=0, keepdims=True)
        acc_s[...] = alpha * acc_s[...] + jnp.dot(v_ref[0, :, pl.ds(ks, tk)], p.astype(BF16),
                                                  preferred_element_type=F32)
        m_s[...] = m_new

    npair = i // 2

    def tile(j):
        return pl.multiple_of(j * 2 * t, 2 * t)

    @pl.when(npair > 0)
    def _():
        s_a[...] = scores(0, 2 * t)

        def body(jj, carry):
            j = 2 * jj
            s_b[...] = scores(tile(j + 1), 2 * t)
            absorb(s_a[...], tile(j), 2 * t)
            s_a[...] = scores(tile(jnp.minimum(j + 2, npair - 1)), 2 * t)
            absorb(s_b[...], tile(j + 1), 2 * t)
            return carry

        lax.fori_loop(0, npair // 2, body, 0)

        @pl.when(npair % 2 == 1)
        def _():
            absorb(s_a[...], tile(npair - 1), 2 * t)

    def last(ks, tk):
        r = lax.broadcasted_iota(jnp.int32, (tk, 2 * t), 0)
        c = lax.broadcasted_iota(jnp.int32, (tk, 2 * t), 1)
        absorb(jnp.where(r <= jnp.where(c >= t, c - t, c) + (tk - t), scores(ks, tk), NEG), ks, tk)

    @pl.when(i % 2 == 1)
    def _():
        last(tile(npair), 2 * t)

    @pl.when(i % 2 == 0)
    def _():
        last(pl.multiple_of(i * t, t), t)

    o2 = acc_s[...] / l_s[...]
    o_t = jnp.where(lax.broadcasted_iota(jnp.int32, (LANES, t), 0) < HEAD_DIM, o2[:, :t], o2[:, t:])
    o_ref[0] = o_t.T


def _fox_attention(q_t, k, f_aug, v_t):
    bsz, s, _ = k.shape
    t = FOX_TILE
    keys = pl.BlockSpec((1, s, LANES), lambda b, h, i: (b, 0, h))
    return pl.pallas_call(
        functools.partial(_fox_kernel, t=t),
        grid=(bsz, HEAD_PAIRS, s // t),
        in_specs=[pl.BlockSpec((1, LANES, t), lambda b, h, i: (b, h, i)),
                  keys, keys,
                  pl.BlockSpec((1, LANES, s), lambda b, h, i: (b, h, 0))],
        out_specs=pl.BlockSpec((1, t, LANES), lambda b, h, i: (b, i, h)),
        out_shape=jax.ShapeDtypeStruct((bsz, s, WIDTH), F32),
        scratch_shapes=[pltpu.VMEM((1, 2 * t), F32), pltpu.VMEM((1, 2 * t), F32),
                        pltpu.VMEM((LANES, 2 * t), F32), pltpu.VMEM((2 * t, 2 * t), F32),
                        pltpu.VMEM((2 * t, 2 * t), F32)],
        compiler_params=_params(("parallel", "parallel", "arbitrary")),
        name="fox_attention",
    )(q_t, k, f_aug, v_t)


def _t5_bucket(dist):
    max_exact = T5_NUM_BUCKETS // 2
    d = np.maximum(dist, 1).astype(np.float32)
    large = max_exact + (np.log(d / max_exact) / np.log(T5_MAX_DISTANCE / max_exact)
                         * (T5_NUM_BUCKETS - max_exact)).astype(np.int32)
    large = np.minimum(large, T5_NUM_BUCKETS - 1)
    return np.where(dist < max_exact, dist, large).astype(np.int32)


def _dilated_bias(rel_bias):
    blk = DIL_BLOCK
    period = 3 * blk
    m = np.arange(period)
    rel = np.where(m < 2 * blk, blk - m, blk - (m - period))
    band = (rel >= 0) & (rel <= blk)
    onehot = np.zeros((len(DILATIONS), period, T5_NUM_BUCKETS), np.float32)
    for bi, dil in enumerate(DILATIONS):
        onehot[bi, m, _t5_bucket(np.clip(rel, 0, blk) * dil)] = 1.0
    w = jnp.einsum('bmk,kh->bhm', jnp.asarray(onehot), rel_bias.astype(F32),
                   precision=lax.Precision.HIGHEST)
    w = jnp.where(jnp.asarray(band), w, NEG)
    flat = jnp.tile(w, (1, 1, blk))[:, :, :blk * (period - 1)]
    table = flat.reshape(len(DILATIONS), N_HEADS, blk, period - 1)[..., :2 * blk]
    table = table.reshape(len(DILATIONS), HEAD_PAIRS, 2 * blk, 2 * blk)
    first = jnp.where(jnp.arange(2 * blk) < blk, NEG, table)
    return jnp.stack([table, first], axis=1)


def _dil_kernel(q_ref, k_ref, v_ref, bias_ref, o_ref, qf, kf, vf, ob0, ob1, ob2, ls0, ls1, ls2, *, s_len):
    blk = DIL_BLOCK
    qf[...] = q_ref[0].astype(F32)
    kf[...] = k_ref[0].astype(F32)
    vf[...] = v_ref[0].astype(F32)
    for bi, (dil, ob, ls) in enumerate(zip(DILATIONS, (ob0, ob1, ob2), (ls0, ls1, ls2))):
        span = blk * dil
        nb = s_len // span

        def rows(start, dil=dil):
            return pl.ds(start, blk) if dil == 1 else pl.ds(start, blk, stride=dil)

        def block(tix, carry, bi=bi, span=span, nb=nb, ob=ob, ls=ls, rows=rows):
            n = tix % nb
            start = n * span + tix // nb
            prev = jnp.maximum(start - span, 0)
            q2 = _stack_heads(qf[rows(start), :].astype(BF16))
            kk = jnp.concatenate([kf[rows(prev), :], kf[rows(start), :]], axis=0).astype(BF16)
            vv = jnp.concatenate([vf[rows(prev), :], vf[rows(start), :]], axis=0).astype(BF16)
            s = lax.dot_general(q2, kk, (((1,), (1,)), ((), ())), preferred_element_type=F32)
            s = s + bias_ref[bi, jnp.where(n == 0, 1, 0), 0]
            m = jnp.max(s, axis=1, keepdims=True)
            p = jnp.exp(s - m)
            l = jnp.sum(p, axis=1, keepdims=True)
            o2 = jnp.dot(p.astype(BF16), vv, preferred_element_type=F32) / l
            lse = jnp.broadcast_to(m + jnp.log(l), (2 * blk, LANES))
            ob[rows(start), :] = _unstack_heads(o2)
            ls[rows(start), :] = _unstack_heads(lse)
            return carry

        def blocks(g, carry, block=block):
            for u in range(DIL_UNROLL):
                block(g * DIL_UNROLL + u, carry)
            return carry

        lax.fori_loop(0, dil * nb // DIL_UNROLL, blocks, 0)

    chunk = 512
    for c in range(s_len // chunk):
        r = pl.ds(c * chunk, chunk)
        l0, l1, l2 = ls0[r, :], ls1[r, :], ls2[r, :]
        mx = jnp.maximum(jnp.maximum(l0, l1), l2)
        e0, e1, e2 = jnp.exp(l0 - mx), jnp.exp(l1 - mx), jnp.exp(l2 - mx)
        o_ref[0, r, :] = (e0 * ob0[r, :] + e1 * ob1[r, :] + e2 * ob2[r, :]) / (e0 + e1 + e2)


def _dilated_attention(qkv, bias):
    bsz, s, _ = qkv.shape
    col = lambda off: pl.BlockSpec((1, s, LANES), lambda b, h: (b, 0, off + h))
    buf = pltpu.VMEM((s, LANES), F32)
    return pl.pallas_call(
        functools.partial(_dil_kernel, s_len=s),
        grid=(bsz, HEAD_PAIRS),
        in_specs=[col(0), col(HEAD_PAIRS), col(2 * HEAD_PAIRS),
                  pl.BlockSpec((len(DILATIONS), 2, 1, 2 * DIL_BLOCK, 2 * DIL_BLOCK), lambda b, h: (0, 0, h, 0, 0))],
        out_specs=pl.BlockSpec((1, s, LANES), lambda b, h: (b, 0, h)),
        out_shape=jax.ShapeDtypeStruct((bsz, s, WIDTH), F32),
        scratch_shapes=[buf] * 9,
        compiler_params=_params(("parallel", "arbitrary")),
        name="dilated_attention",
    )(qkv, qkv, qkv, bias)


def _bf16_bits(x):
    return pltpu.bitcast(x.astype(BF16).astype(F32), U32)


def _to_row_tiles(dst_ref, x, base=0):
    rows, d = x.shape
    for c in range(TILE_ROWS):
        lo = _bf16_bits(x[:, c * LANES:(c + 1) * LANES]) >> 16
        hi = _bf16_bits(x[:, d // 2 + c * LANES:d // 2 + (c + 1) * LANES])
        dst_ref[pl.ds(base + c, rows, stride=TILE_ROWS), :] = lo | hi


def _from_row_tiles(src_ref, rows, base=0):
    lo, hi = [], []
    for c in range(TILE_ROWS):
        w = src_ref[pl.ds(base + c, rows, stride=TILE_ROWS), :]
        lo.append(pltpu.bitcast(w << 16, F32))
        hi.append(pltpu.bitcast(w & jnp.uint32(0xFFFF0000), F32))
    return lo + hi


def _outproj_kernel(yf_ref, yd_ref, x_ref, g1_ref, sc_ref, sh_ref, g2_ref, gf_ref, gd_ref, wo1_ref, wo2_ref,
                    wrh_ref, wrl_ref, wsg_ref, wsu_ref, wsd_ref, xp_ref, h3_ref, lg_ref):
    nf = (_rms(yf_ref[0]) * gf_ref[...]).astype(BF16)
    nd = (_rms(yd_ref[0]) * gd_ref[...]).astype(BF16)
    mix = (jnp.dot(nf, wo1_ref[...], preferred_element_type=F32)
           + jnp.dot(nd, wo2_ref[...], preferred_element_type=F32))
    x1 = x_ref[0] + g1_ref[0] * mix
    h2 = _rms(x1) * (1.0 + sc_ref[0]) + sh_ref[0]
    hb = h2.astype(BF16)
    hl = (h2 - hb.astype(F32)).astype(BF16)
    lg_ref[...] = _dot_nt(wrh_ref[...], hb) + _dot_nt(wrh_ref[...], hl) + _dot_nt(wrl_ref[...], hb)
    act = _silu(jnp.dot(hb, wsg_ref[...], preferred_element_type=F32)) * jnp.dot(
        hb, wsu_ref[...], preferred_element_type=F32)
    shared = jnp.dot(act.astype(BF16), wsd_ref[...], preferred_element_type=F32)
    xp_ref[0] = x1 + g2_ref[0] * shared
    _to_row_tiles(h3_ref, h2)


def _out_projection(y_fox, y_dil, x, gate1, scale2, shift2, gate2, g_fox, g_dil, wo1, wo2, wr_hi, wr_lo,
                    wsg, wsu, wsd):
    bsz, s, d = x.shape
    tm = ROW_TILE
    nt = s // tm
    vec = pl.BlockSpec((1, 1, d), lambda b, i: (b, 0, 0))
    full = lambda w: pl.BlockSpec(w.shape, lambda b, i: (0,) * w.ndim)
    row = lambda n: pl.BlockSpec((1, tm, n), lambda b, i: (b, i, 0))
    return pl.pallas_call(
        _outproj_kernel,
        grid=(bsz, nt),
        in_specs=[row(WIDTH), row(WIDTH), row(d), vec, vec, vec, vec, full(g_fox), full(g_dil), full(wo1),
                  full(wo2), full(wr_hi), full(wr_lo), full(wsg), full(wsu), full(wsd)],
        out_specs=[row(d),
                   pl.BlockSpec((tm * TILE_ROWS, LANES), lambda b, i: (b * nt + i, 0)),
                   pl.BlockSpec((N_EXPERTS, tm), lambda b, i: (0, b * nt + i))],
        out_shape=[jax.ShapeDtypeStruct((bsz, s, d), F32),
                   jax.ShapeDtypeStruct((bsz * s * TILE_ROWS, LANES), U32),
                   jax.ShapeDtypeStruct((N_EXPERTS, bsz * s), F32)],
        compiler_params=_params(("parallel", "arbitrary")),
        name="outproj_norm2_router_shared",
    )(y_fox, y_dil, x, gate1, scale2, shift2, gate2, g_fox, g_dil, wo1, wo2, wr_hi, wr_lo, wsg, wsu, wsd)


def _first_argmax(v, row, size):
    m = jnp.max(v, axis=0, keepdims=True)
    return m, jnp.min(jnp.where(v == m, row, size), axis=0, keepdims=True)


def _route_kernel(lg_ref, rb_ref, tri_ref, e_ref, w_ref, rk_ref, cnt_ref, cnt_s):
    @pl.when(pl.program_id(0) == 0)
    def _():
        cnt_s[...] = jnp.zeros(cnt_s.shape, F32)

    t = lg_ref.shape[1]
    gsz = N_EXPERTS // N_GROUPS
    scores = jax.nn.sigmoid(lg_ref[...])
    sel = scores + rb_ref[...]
    row_g = lax.broadcasted_iota(jnp.int32, (gsz, t), 0)
    grp = []
    for g in range(N_GROUPS):
        v = sel[g * gsz:(g + 1) * gsz]
        m1, i1 = _first_argmax(v, row_g, gsz)
        m2 = jnp.max(jnp.where(row_g == i1, -jnp.inf, v), axis=0, keepdims=True)
        grp.append(m1 + m2)
    gv = jnp.concatenate(grp, axis=0)
    row8 = lax.broadcasted_iota(jnp.int32, (N_GROUPS, t), 0)
    pen = jnp.full((N_GROUPS, t), -jnp.inf, F32)
    for _ in range(TOP_K_GROUPS):
        _, ix = _first_argmax(gv, row8, N_GROUPS)
        pen = jnp.where(row8 == ix, 0.0, pen)
        gv = jnp.where(row8 == ix, -jnp.inf, gv)
    selm = jnp.concatenate([sel[g * gsz:(g + 1) * gsz] + pen[g:g + 1] for g in range(N_GROUPS)], axis=0)

    row = lax.broadcasted_iota(jnp.int32, (N_EXPERTS, t), 0)
    v = selm
    idxs, scs = [], []
    for _ in range(TOP_K):
        _, ix = _first_argmax(v, row, N_EXPERTS)
        hit = row == ix
        idxs.append(ix)
        scs.append(jnp.sum(jnp.where(hit, scores, 0.0), axis=0, keepdims=True))
        v = jnp.where(hit, -jnp.inf, v)
    chosen = jnp.where(v != selm, 1.0, 0.0)
    before = jnp.dot(chosen.astype(BF16), tri_ref[...], preferred_element_type=F32) + cnt_s[...]
    rks = [jnp.sum(jnp.where(row == ix, before, 0.0), axis=0, keepdims=True) for ix in idxs]
    sc = jnp.concatenate(scs, axis=0)
    e_ref[...] = jnp.concatenate(idxs, axis=0)
    w_ref[...] = sc / jnp.sum(sc, axis=0, keepdims=True) * ROUTED_SCALE
    rk_ref[...] = jnp.concatenate(rks, axis=0).astype(jnp.int32)
    cnt_s[...] = cnt_s[...] + jnp.sum(chosen, axis=1, keepdims=True)
    cnt_ref[...] = cnt_s[...]


def _route(logits_t, router_bias):
    e, n = logits_t.shape
    t = ROUTE_TILE
    tri = jnp.triu(jnp.ones((t, t), BF16), k=1)
    tile = pl.BlockSpec((TOP_K, t), lambda i: (0, i))
    return pl.pallas_call(
        _route_kernel,
        grid=(n // t,),
        in_specs=[pl.BlockSpec((e, t), lambda i: (0, i)),
                  pl.BlockSpec((e, 1), lambda i: (0, 0)),
                  pl.BlockSpec((t, t), lambda i: (0, 0))],
        out_specs=[tile, tile, tile, pl.BlockSpec((e, 1), lambda i: (0, 0))],
        out_shape=[jax.ShapeDtypeStruct((TOP_K, n), jnp.int32), jax.ShapeDtypeStruct((TOP_K, n), F32),
                   jax.ShapeDtypeStruct((TOP_K, n), jnp.int32), jax.ShapeDtypeStruct((e, 1), F32)],
        scratch_shapes=[pltpu.VMEM((e, 1), F32)],
        compiler_params=_params(("arbitrary",)),
        name="route_topk_rank",
    )(logits_t, router_bias.reshape(e, 1).astype(F32), tri)


def _dest_kernel(e_ref, rk_ref, ps_ref, d_ref):
    t = e_ref.shape[1]
    row = lax.broadcasted_iota(jnp.int32, (N_EXPERTS, t), 0)
    ps = ps_ref[...]
    base = [jnp.sum(jnp.where(row == e_ref[k:k + 1, :], ps, 0.0), axis=0, keepdims=True) for k in range(TOP_K)]
    d_ref[...] = jnp.concatenate(base, axis=0).astype(jnp.int32) + rk_ref[...]


def _dest_rows(eidx, rank, pstart):
    _, n = eidx.shape
    t = ROUTE_TILE
    tile = pl.BlockSpec((TOP_K, t), lambda i: (0, i))
    return pl.pallas_call(
        _dest_kernel,
        grid=(n // t,),
        in_specs=[tile, tile, pl.BlockSpec((N_EXPERTS, 1), lambda i: (0, 0))],
        out_specs=tile,
        out_shape=jax.ShapeDtypeStruct((TOP_K, n), jnp.int32),
        compiler_params=_params(("parallel",)),
        name="dest_rows",
    )(eidx, rank, pstart.astype(F32).reshape(N_EXPERTS, 1))


def _block_plan(counts, nblk):
    counts = counts.reshape(N_EXPERTS).astype(jnp.int32)
    padded = (counts + FFN_BLOCK - 1) // FFN_BLOCK * FFN_BLOCK
    pend = jnp.cumsum(padded).astype(jnp.int32)
    pstart = pend - padded
    first_row = jnp.arange(nblk, dtype=jnp.int32) * FFN_BLOCK
    block_e = jnp.minimum(jnp.sum(pend[None, :] <= first_row[:, None], axis=1), N_EXPERTS - 1).astype(jnp.int32)
    nused = (pend[-1:] // FFN_BLOCK).astype(jnp.int32)
    return pstart, pstart + counts, pend, block_e, nused


def _tile_copy(src_ref, src_row, dst_ref, dst_row, sem):
    return pltpu.make_async_copy(src_ref.at[pl.ds(pl.multiple_of(src_row * TILE_ROWS, TILE_ROWS), TILE_ROWS), :],
                                 dst_ref.at[pl.ds(pl.multiple_of(dst_row * TILE_ROWS, TILE_ROWS), TILE_ROWS), :],
                                 sem)


def _scatter_kernel(cend_ref, pend_ref, dst_ref, h3_ref, z_ref, xs_hbm, st0, st1, sem, zsem):
    i = pl.program_id(0)
    nsteps = pl.num_programs(0)
    tm = dst_ref.shape[1]
    stage = (st0, st1)

    def zero_copy(r):
        return pltpu.make_async_copy(z_ref, xs_hbm.at[pl.ds(pl.multiple_of(r * TILE_ROWS, TILE_ROWS), TILE_ROWS), :],
                                     zsem)

    def wait_slot(slot):
        for _ in range(TOP_K):
            pltpu.make_async_copy(stage[slot], xs_hbm.at[pl.ds(0, tm * TILE_ROWS), :], sem.at[slot]).wait()

    @pl.when(i == 0)
    def _():
        def start_row(r, c):
            zero_copy(r).start()
            return c

        def wait_row(r, c):
            zero_copy(r).wait()
            return c

        def per_expert(e, carry):
            lax.fori_loop(cend_ref[e], pend_ref[e], start_row, carry)
            prev = jnp.maximum(e - 1, 0)
            return lax.fori_loop(cend_ref[prev], jnp.where(e > 0, pend_ref[prev], cend_ref[prev]), wait_row, carry)

        lax.fori_loop(0, N_EXPERTS, per_expert, 0)
        lax.fori_loop(cend_ref[N_EXPERTS - 1], pend_ref[N_EXPERTS - 1], wait_row, 0)

    for slot in range(2):
        @pl.when(i % 2 == slot)
        def _(slot=slot):
            @pl.when(i >= 2)
            def _():
                wait_slot(slot)
            stage[slot][...] = h3_ref[...]
            for k in range(TOP_K):
                def body(jj, carry, k=k):
                    for u in range(ISSUE_UNROLL):
                        j = jj * ISSUE_UNROLL + u
                        _tile_copy(stage[slot], j, xs_hbm, dst_ref[k, j], sem.at[slot]).start()
                    return carry
                lax.fori_loop(0, tm // ISSUE_UNROLL, body, 0)

    @pl.when(i == nsteps - 1)
    def _():
        wait_slot(0)
        wait_slot(1)


def _dispatch_rows(h3, dest, cend, pend, rows):
    _, n = dest.shape
    tm = SCATTER_TILE
    assert n // tm >= 2
    stage = pltpu.VMEM((tm * TILE_ROWS, LANES), U32)
    grid_spec = pltpu.PrefetchScalarGridSpec(
        num_scalar_prefetch=2,
        grid=(n // tm,),
        in_specs=[pl.BlockSpec((TOP_K, tm), lambda i, ce, pe: (0, i), memory_space=pltpu.SMEM),
                  pl.BlockSpec((tm * TILE_ROWS, LANES), lambda i, ce, pe: (i, 0)),
                  pl.BlockSpec((TILE_ROWS, LANES), lambda i, ce, pe: (0, 0))],
        out_specs=pl.BlockSpec(memory_space=pl.ANY),
        scratch_shapes=[stage, stage, pltpu.SemaphoreType.DMA((2,)), pltpu.SemaphoreType.DMA],
    )
    return pl.pallas_call(
        _scatter_kernel,
        grid_spec=grid_spec,
        out_shape=jax.ShapeDtypeStruct((rows * TILE_ROWS, LANES), U32),
        compiler_params=_params(("arbitrary",)),
        name="dispatch_scatter",
    )(cend, pend, dest, h3, jnp.zeros((TILE_ROWS, LANES), U32))


def _ffn_kernel(be_ref, nu_ref, x_ref, *refs):
    w_refs, (y_ref, wgb, wub, wdb, cur) = refs[:3 * FFN_PER_STEP], refs[3 * FFN_PER_STEP:]
    first = pl.program_id(0) * FFN_PER_STEP

    @pl.when(first == 0)
    def _():
        cur[0] = 0

    slots = []
    for part in range(FFN_PER_STEP):
        blk = first + part
        wg_ref, wu_ref, wd_ref = w_refs[3 * part:3 * part + 3]

        @pl.when((blk == 0) | (be_ref[blk] != be_ref[jnp.maximum(blk - 1, 0)]))
        def _(wg_ref=wg_ref, wu_ref=wu_ref, wd_ref=wd_ref):
            slot = 1 - cur[0]
            cur[0] = slot
            wgb[slot] = wg_ref[0].astype(BF16)
            wub[slot] = wu_ref[0].astype(BF16)
            wdb[slot] = wd_ref[0].astype(BF16)

        slots.append(cur[0])

    @pl.when(first < nu_ref[0])
    def _():
        for part, slot in enumerate(slots):
            base = part * FFN_BLOCK * TILE_ROWS
            x = jnp.concatenate([c.astype(BF16) for c in _from_row_tiles(x_ref, FFN_BLOCK, base)], axis=1)
            act = _silu(jnp.dot(x, wgb[slot], preferred_element_type=F32)) * jnp.dot(
                x, wub[slot], preferred_element_type=F32)
            _to_row_tiles(y_ref, jnp.dot(act.astype(BF16), wdb[slot], preferred_element_type=F32), base)

    @pl.when(first >= nu_ref[0])
    def _():
        y_ref[...] = jnp.zeros(y_ref.shape, U32)


def _routed_experts(xs, block_e, nused, w_gate, w_up, w_down):
    rows = xs.shape[0] // TILE_ROWS
    nblk = rows // FFN_BLOCK
    _, d, hid = w_gate.shape
    step_rows = FFN_PER_STEP * FFN_BLOCK * TILE_ROWS
    w_specs = []
    for part in range(FFN_PER_STEP):
        expert = lambda i, be, nu, part=part: (be[i * FFN_PER_STEP + part], 0, 0)
        w_specs += [pl.BlockSpec((1, d, hid), expert), pl.BlockSpec((1, d, hid), expert),
                    pl.BlockSpec((1, hid, d), expert)]
    grid_spec = pltpu.PrefetchScalarGridSpec(
        num_scalar_prefetch=2,
        grid=(nblk // FFN_PER_STEP,),
        in_specs=[pl.BlockSpec((step_rows, LANES),
                               lambda i, be, nu: (jnp.minimum(i, (nu[0] - 1) // FFN_PER_STEP), 0))] + w_specs,
        out_specs=pl.BlockSpec((step_rows, LANES), lambda i, be, nu: (i, 0)),
        scratch_shapes=[pltpu.VMEM((2, d, hid), BF16), pltpu.VMEM((2, d, hid), BF16), pltpu.VMEM((2, hid, d), BF16),
                        pltpu.SMEM((1,), jnp.int32)],
    )
    return pl.pallas_call(
        _ffn_kernel,
        grid_spec=grid_spec,
        out_shape=jax.ShapeDtypeStruct((rows * TILE_ROWS, LANES), U32),
        compiler_params=_params(("arbitrary",)),
        name="routed_experts",
    )(block_e, nused, xs, *([w_gate, w_up, w_down] * FFN_PER_STEP))


def _gather_start(idx_ref, src_hbm, dst_vmem, sem):
    nk, tm = idx_ref.shape
    for k in range(nk):
        def body(jj, carry, k=k):
            for u in range(ISSUE_UNROLL):
                j = jj * ISSUE_UNROLL + u
                _tile_copy(src_hbm, idx_ref[k, j], dst_vmem, k * tm + j, sem).start()
            return carry
        lax.fori_loop(0, tm // ISSUE_UNROLL, body, 0)


def _gather_wait(src_hbm, dst_vmem, sem):
    pltpu.make_async_copy(src_hbm.at[pl.ds(0, dst_vmem.shape[0]), :], dst_vmem, sem).wait()


def _combine_kernel(dst_ref, dstn_ref, wt_ref, xp_ref, g2_ref, gfin_ref, ys_hbm, o_ref, yb0, yb1, sem, *, nsteps):
    i = pl.program_id(0)
    tm = COMBINE_TILE
    bufs = (yb0, yb1)

    @pl.when(i == 0)
    def _():
        _gather_start(dst_ref, ys_hbm, yb0, sem.at[0])

    for slot in range(2):
        @pl.when((i + 1 < nsteps) & ((i + 1) % 2 == slot))
        def _(slot=slot):
            _gather_start(dstn_ref, ys_hbm, bufs[slot], sem.at[slot])

    for slot in range(2):
        @pl.when(i % 2 == slot)
        def _(slot=slot):
            _gather_wait(ys_hbm, bufs[slot], sem.at[slot])
            wt = wt_ref[...]
            cols = None
            for k in range(TOP_K):
                part = [c * wt[:, k:k + 1] for c in _from_row_tiles(bufs[slot], tm, base=k * tm * TILE_ROWS)]
                cols = part if cols is None else [a + b for a, b in zip(cols, part)]
            routed = jnp.concatenate(cols, axis=1)
            o_ref[0] = _rms(xp_ref[0] + g2_ref[0] * routed) * gfin_ref[...]


def _combine(ys, dest, wts_t, xp, gate2, g_final):
    bsz, s, d = xp.shape
    tm = COMBINE_TILE
    nt = s // tm
    nsteps = bsz * nt
    vec = pl.BlockSpec((1, 1, d), lambda i: (i // nt, 0, 0))
    return pl.pallas_call(
        functools.partial(_combine_kernel, nsteps=nsteps),
        grid=(nsteps,),
        in_specs=[pl.BlockSpec((TOP_K, tm), lambda i: (0, i), memory_space=pltpu.SMEM),
                  pl.BlockSpec((TOP_K, tm), lambda i: (0, jnp.minimum(i + 1, nsteps - 1)),
                               memory_space=pltpu.SMEM),
                  pl.BlockSpec((tm, TOP_K), lambda i: (i, 0)),
                  pl.BlockSpec((1, tm, d), lambda i: (i // nt, i % nt, 0)),
                  vec,
                  pl.BlockSpec((1, d), lambda i: (0, 0)),
                  pl.BlockSpec(memory_space=pl.ANY)],
        out_specs=pl.BlockSpec((1, tm, d), lambda i: (i // nt, i % nt, 0)),
        out_shape=jax.ShapeDtypeStruct((bsz, s, d), F32),
        scratch_shapes=[pltpu.VMEM((tm * TOP_K * TILE_ROWS, LANES), U32),
                        pltpu.VMEM((tm * TOP_K * TILE_ROWS, LANES), U32),
                        pltpu.SemaphoreType.DMA((2,))],
        compiler_params=_params(("arbitrary",)),
        name="combine_final_norm",
    )(dest, dest, wts_t, xp, gate2, g_final.reshape(1, d), ys)


def kernel(x, c, w_in, b_forget, g_fox_out, g_dil_out, w_out, w_ada, b_ada, w_router, router_bias,
           w_exp_gate, w_exp_up, w_exp_down, w_sh_gate, w_sh_up, w_sh_down, rel_bias, g_final):
    bsz, s, d = x.shape
    depth = w_in.shape[0]
    assert depth == 1 and d == 2 * TILE_ROWS * LANES and s % (DIL_BLOCK * DILATIONS[-1]) == 0
    l = 0
    mod = _modulation(c, w_ada[l], b_ada[l])
    shift1, scale1, gate1, shift2, scale2, gate2 = [m[:, None, :] for m in jnp.split(mod, 6, axis=-1)]

    qscale = HEAD_DIM ** -0.5
    o3 = 3 * WIDTH
    w = w_in[l]
    wq_t = (w[:, :WIDTH] * (qscale * LOG2E)).T.astype(BF16)
    wk = w[:, WIDTH:2 * WIDTH].astype(BF16)
    wv_t = w[:, 2 * WIDTH:o3].T.astype(BF16)
    w_flog = jnp.pad(w[:, o3:o3 + N_HEADS], ((0, 0), (0, LANES - N_HEADS))).astype(BF16)
    wd0 = o3 + N_HEADS
    w_dil = jnp.concatenate([w[:, wd0:wd0 + WIDTH] * qscale, w[:, wd0 + WIDTH:]], axis=1).astype(BF16)

    q_t, k_f, v_t, qkv_d, flog = _in_projection(x, scale1, shift1, wq_t, wk, wv_t, w_dil, w_flog)
    y_fox = _fox_attention(q_t, k_f, _forget_cumsum(flog, b_forget[l]), v_t)
    y_dil = _dilated_attention(qkv_d, _dilated_bias(rel_bias))

    wr = w_router[l].T
    wr_hi = wr.astype(BF16)
    wr_lo = (wr - wr_hi.astype(F32)).astype(BF16)
    wo = w_out[l].astype(BF16)
    xp, h3, logits_t = _out_projection(
        y_fox, y_dil, x, gate1, scale2, shift2, gate2, g_fox_out[l].reshape(1, WIDTH),
        g_dil_out[l].reshape(1, WIDTH), wo[:WIDTH], wo[WIDTH:], wr_hi, wr_lo,
        w_sh_gate[l].astype(BF16), w_sh_up[l].astype(BF16), w_sh_down[l].astype(BF16))

    eidx, wts, rank, counts = _route(logits_t, router_bias[l])
    rows = bsz * s * TOP_K + N_EXPERTS * FFN_BLOCK
    pstart, cend, pend, block_e, nused = _block_plan(counts, rows // FFN_BLOCK)
    dest = _dest_rows(eidx, rank, pstart)
    xs = _dispatch_rows(h3, dest, cend, pend, rows)
    ys = _routed_experts(xs, block_e, nused, w_exp_gate[l], w_exp_up[l], w_exp_down[l])
    return _combine(ys, dest, wts.T, xp, gate2, g_final)
```

```python
import functools

import numpy as np
import jax
import jax.numpy as jnp
from jax import lax
from jax.experimental import pallas as pl
from jax.experimental.pallas import tpu as pltpu

F32 = jnp.float32
BF16 = jnp.bfloat16

HEAD_DIM = 64
N_HEADS = 8
WIDTH = N_HEADS * HEAD_DIM
LANES = 128
TILE_ROWS = 4
U32 = jnp.uint32
DIL_UNROLL = 8
HEAD_PAIRS = WIDTH // LANES
DIL_BLOCK = 128
DILATIONS = (1, 4, 16)
T5_NUM_BUCKETS = 32
T5_MAX_DISTANCE = 2048
N_EXPERTS = 256
TOP_K = 8
N_GROUPS = 8
TOP_K_GROUPS = 4
ROUTED_SCALE = 2.5
EPS = 1e-6
NEG = -1e30
LOG2E = float(np.log2(np.e))
VMEM_LIMIT = 56 * 1024 * 1024

ROW_TILE = 512
FOX_TILE = 512
FFN_BLOCK = 256
FFN_PER_STEP = 2
COMBINE_TILE = 256
SCATTER_TILE = 256
ROUTE_TILE = 512
ISSUE_UNROLL = 8


def _params(semantics):
    return pltpu.CompilerParams(dimension_semantics=semantics, vmem_limit_bytes=VMEM_LIMIT)


def _rms(x):
    return x * lax.rsqrt(jnp.mean(x * x, axis=-1, keepdims=True) + EPS)


def _silu(x):
    return x * jax.nn.sigmoid(x)


def _dot_nt(a, b):
    return lax.dot_general(a, b, (((1,), (1,)), ((), ())), preferred_element_type=F32)


def _mod_kernel(c_ref, w_ref, b_ref, o_ref):
    o_ref[...] = jnp.dot(_silu(c_ref[...]), w_ref[...], precision=lax.Precision.HIGHEST,
                         preferred_element_type=F32) + b_ref[...]


def _modulation(c, w_ada, b_ada):
    bsz, d = c.shape
    n = w_ada.shape[1]
    tn = 1536
    return pl.pallas_call(
        _mod_kernel,
        grid=(n // tn,),
        in_specs=[pl.BlockSpec((bsz, d), lambda j: (0, 0)),
                  pl.BlockSpec((d, tn), lambda j: (0, j)),
                  pl.BlockSpec((1, tn), lambda j: (0, j))],
        out_specs=pl.BlockSpec((bsz, tn), lambda j: (0, j)),
        out_shape=jax.ShapeDtypeStruct((bsz, n), F32),
        compiler_params=_params(("arbitrary",)),
        name="adaln_mod",
    )(c, w_ada, b_ada.reshape(1, n))


def _inproj_kernel(x_ref, sc_ref, sh_ref, wq_ref, wk_ref, wv_ref, wd_ref, wl_ref,
                   oq_ref, ok_ref, ov_ref, od_ref, ol_ref):
    h = _rms(x_ref[0]) * (1.0 + sc_ref[0]) + sh_ref[0]
    hb = h.astype(BF16)
    oq_ref[0] = _dot_nt(wq_ref[...], hb).astype(BF16)
    ov_ref[0] = _dot_nt(wv_ref[...], hb).astype(BF16)
    ok_ref[0] = jnp.dot(hb, wk_ref[...], preferred_element_type=F32).astype(BF16)
    od_ref[0] = jnp.dot(hb, wd_ref[...], preferred_element_type=F32).astype(BF16)
    ol_ref[0] = jnp.dot(hb, wl_ref[...], preferred_element_type=F32)


def _in_projection(x, scale1, shift1, wq_t, wk, wv_t, w_dil, w_flog):
    bsz, s, d = x.shape
    tm = ROW_TILE
    vec = pl.BlockSpec((1, 1, d), lambda b, i: (b, 0, 0))
    full = lambda w: pl.BlockSpec(w.shape, lambda b, i: (0, 0))
    row = lambda n: pl.BlockSpec((1, tm, n), lambda b, i: (b, i, 0))
    col = pl.BlockSpec((1, WIDTH, tm), lambda b, i: (b, 0, i))
    return pl.pallas_call(
        _inproj_kernel,
        grid=(bsz, s // tm),
        in_specs=[row(d), vec, vec, full(wq_t), full(wk), full(wv_t), full(w_dil), full(w_flog)],
        out_specs=[col, row(WIDTH), col, row(3 * WIDTH), row(LANES)],
        out_shape=[jax.ShapeDtypeStruct((bsz, WIDTH, s), BF16),
                   jax.ShapeDtypeStruct((bsz, s, WIDTH), BF16),
                   jax.ShapeDtypeStruct((bsz, WIDTH, s), BF16),
                   jax.ShapeDtypeStruct((bsz, s, 3 * WIDTH), BF16),
                   jax.ShapeDtypeStruct((bsz, s, LANES), F32)],
        compiler_params=_params(("parallel", "arbitrary")),
        name="norm1_inproj",
    )(x, scale1, shift1, wq_t, wk, wv_t, w_dil, w_flog)


FORGET_PARTS = 3


def _split_bf16(x):
    parts = []
    for _ in range(FORGET_PARTS):
        p = x.astype(BF16)
        parts.append(p)
        x = x - p.astype(F32)
    return parts


def _forget_kernel(f_ref, b_ref, tri_ref, place_ref, o_ref):
    blk = tri_ref.shape[0]
    carry = jnp.zeros((1, LANES), F32)
    for c in range(f_ref.shape[1] // blk):
        rows = pl.ds(c * blk, blk)
        z = f_ref[0, rows, :] + b_ref[...]
        ls = jnp.minimum(z, 0.0) - jnp.log1p(jnp.exp(-jnp.abs(z)))
        cum = carry
        for p in _split_bf16(ls):
            cum = cum + jnp.dot(tri_ref[...], p, preferred_element_type=F32)
        carry = cum[blk - 1:blk, :]
        out = None
        for n, p in enumerate(_split_bf16(cum * LOG2E)):
            d = jnp.dot(p, place_ref[n], preferred_element_type=F32)
            out = d if out is None else out + d
        o_ref[0, rows, :] = out.astype(BF16)


def _forget_cumsum(flog, b_forget):
    bsz, s, _ = flog.shape
    blk = 512
    place = np.zeros((FORGET_PARTS, LANES, WIDTH), np.float32)
    for h in range(N_HEADS):
        for n in range(FORGET_PARTS):
            place[n, h, (h // 2) * LANES + (h % 2) * HEAD_DIM + n] = 1.0
    return pl.pallas_call(
        _forget_kernel,
        grid=(bsz,),
        in_specs=[pl.BlockSpec((1, s, LANES), lambda b: (b, 0, 0)),
                  pl.BlockSpec((1, LANES), lambda b: (0, 0)),
                  pl.BlockSpec((blk, blk), lambda b: (0, 0)),
                  pl.BlockSpec(place.shape, lambda b: (0, 0, 0))],
        out_specs=pl.BlockSpec((1, s, WIDTH), lambda b: (b, 0, 0)),
        out_shape=jax.ShapeDtypeStruct((bsz, s, WIDTH), BF16),
        compiler_params=_params(("parallel",)),
        name="forget_cumsum",
    )(flog, jnp.pad(b_forget, (0, LANES - N_HEADS)).reshape(1, LANES),
      jnp.tril(jnp.ones((blk, blk), BF16)), jnp.asarray(place, BF16))


def _stack_heads(q):
    lane = lax.broadcasted_iota(jnp.int32, q.shape, 1)
    zero = jnp.zeros_like(q)
    return jnp.concatenate([jnp.where(lane < HEAD_DIM, q, zero), jnp.where(lane >= HEAD_DIM, q, zero)], axis=0)


def _unstack_heads(o2):
    rows = o2.shape[0] // 2
    lane = lax.broadcasted_iota(jnp.int32, (rows, LANES), 1)
    return jnp.where(lane < HEAD_DIM, o2[:rows], o2[rows:])


def _fox_kernel(q_ref, k_ref, fa_ref, v_ref, o_ref, m_s, l_s, acc_s, s_a, s_b, *, t):
    i = pl.program_id(2)
    q_t = q_ref[0]
    row = lax.broadcasted_iota(jnp.int32, (LANES, 2 * t), 0)
    col = lax.broadcasted_iota(jnp.int32, (LANES, 2 * t), 1)
    head_row = jnp.where(col < t, 0, HEAD_DIM)
    q2 = jnp.concatenate([q_t, q_t], axis=1)
    q2 = jnp.where((row >= head_row) & (row < head_row + HEAD_DIM), q2, jnp.zeros_like(q2))
    minus = jnp.where((row >= head_row) & (row < head_row + FORGET_PARTS), -1.0, 0.0).astype(BF16)
    qa = jnp.concatenate([q2, minus], axis=0)
    m_s[...] = jnp.full(m_s.shape, NEG, F32)
    l_s[...] = jnp.zeros(l_s.shape, F32)
    acc_s[...] = jnp.zeros(acc_s.shape, F32)

    def scores(ks, tk):
        kk = jnp.concatenate([k_ref[0, pl.ds(ks, tk), :], fa_ref[0, pl.ds(ks, tk), :]], axis=1)
        return jnp.dot(kk, qa, preferred_element_type=F32)

    def absorb(s, ks, tk):
        m_prev = m_s[...]
        m_new = jnp.maximum(m_prev, jnp.max(s, axis=0, keepdims=True))
        alpha = jnp.exp2(m_prev - m_new)
        p = jnp.exp2(s - m_new)
        l_s[...] = alpha * l_s[...] + jnp.sum(p, axis=0, keepdims=True)
        acc_s[...] = alpha * acc_s[...] + jnp.dot(v_ref[0, :, pl.ds(ks, tk)], p.astype(BF16),
                                                  preferred_element_type=F32)
        m_s[...] = m_new

    npair = i // 2

    def tile(j):
        return pl.multiple_of(j * 2 * t, 2 * t)

    @pl.when(npair > 0)
    def _():
        s_a[...] = scores(0, 2 * t)

        def body(jj, carry):
            j = 2 * jj
            s_b[...] = scores(tile(j + 1), 2 * t)
            absorb(s_a[...], tile(j), 2 * t)
            s_a[...] = scores(tile(jnp.minimum(j + 2, npair - 1)), 2 * t)
            absorb(s_b[...], tile(j + 1), 2 * t)
            return carry

        lax.fori_loop(0, npair // 2, body, 0)

        @pl.when(npair % 2 == 1)
        def _():
            absorb(s_a[...], tile(npair - 1), 2 * t)

    def last(ks, tk):
        r = lax.broadcasted_iota(jnp.int32, (tk, 2 * t), 0)
        c = lax.broadcasted_iota(jnp.int32, (tk, 2 * t), 1)
        absorb(jnp.where(r <= jnp.where(c >= t, c - t, c) + (tk - t), scores(ks, tk), NEG), ks, tk)

    @pl.when(i % 2 == 1)
    def _():
        last(tile(npair), 2 * t)

    @pl.when(i % 2 == 0)
    def _():
        last(pl.multiple_of(i * t, t), t)

    o2 = acc_s[...] / l_s[...]
    o_t = jnp.where(lax.broadcasted_iota(jnp.int32, (LANES, t), 0) < HEAD_DIM, o2[:, :t], o2[:, t:])
    o_ref[0] = o_t.T


def _fox_attention(q_t, k, f_aug, v_t):
    bsz, s, _ = k.shape
    t = FOX_TILE
    keys = pl.BlockSpec((1, s, LANES), lambda b, h, i: (b, 0, h))
    return pl.pallas_call(
        functools.partial(_fox_kernel, t=t),
        grid=(bsz, HEAD_PAIRS, s // t),
        in_specs=[pl.BlockSpec((1, LANES, t), lambda b, h, i: (b, h, i)),
                  keys, keys,
                  pl.BlockSpec((1, LANES, s), lambda b, h, i: (b, h, 0))],
        out_specs=pl.BlockSpec((1, t, LANES), lambda b, h, i: (b, i, h)),
        out_shape=jax.ShapeDtypeStruct((bsz, s, WIDTH), F32),
        scratch_shapes=[pltpu.VMEM((1, 2 * t), F32), pltpu.VMEM((1, 2 * t), F32),
                        pltpu.VMEM((LANES, 2 * t), F32), pltpu.VMEM((2 * t, 2 * t), F32),
                        pltpu.VMEM((2 * t, 2 * t), F32)],
        compiler_params=_params(("parallel", "parallel", "arbitrary")),
        name="fox_attention",
    )(q_t, k, f_aug, v_t)


def _t5_bucket(dist):
    max_exact = T5_NUM_BUCKETS // 2
    d = np.maximum(dist, 1).astype(np.float32)
    large = max_exact + (np.log(d / max_exact) / np.log(T5_MAX_DISTANCE / max_exact)
                         * (T5_NUM_BUCKETS - max_exact)).astype(np.int32)
    large = np.minimum(large, T5_NUM_BUCKETS - 1)
    return np.where(dist < max_exact, dist, large).astype(np.int32)


def _dilated_bias(rel_bias):
    blk = DIL_BLOCK
    period = 3 * blk
    m = np.arange(period)
    rel = np.where(m < 2 * blk, blk - m, blk - (m - period))
    band = (rel >= 0) & (rel <= blk)
    onehot = np.zeros((len(DILATIONS), period, T5_NUM_BUCKETS), np.float32)
    for bi, dil in enumerate(DILATIONS):
        onehot[bi, m, _t5_bucket(np.clip(rel, 0, blk) * dil)] = 1.0
    w = jnp.einsum('bmk,kh->bhm', jnp.asarray(onehot), rel_bias.astype(F32),
                   precision=lax.Precision.HIGHEST)
    w = jnp.where(jnp.asarray(band), w, NEG)
    flat = jnp.tile(w, (1, 1, blk))[:, :, :blk * (period - 1)]
    table = flat.reshape(len(DILATIONS), N_HEADS, blk, period - 1)[..., :2 * blk]
    table = table.reshape(len(DILATIONS), HEAD_PAIRS, 2 * blk, 2 * blk)
    first = jnp.where(jnp.arange(2 * blk) < blk, NEG, table)
    return jnp.stack([table, first], axis=1)


def _dil_kernel(q_ref, k_ref, v_ref, bias_ref, o_ref, qf, kf, vf, ob0, ob1, ob2, ls0, ls1, ls2, *, s_len):
    blk = DIL_BLOCK
    qf[...] = q_ref[0].astype(F32)
    kf[...] = k_ref[0].astype(F32)
    vf[...] = v_ref[0].astype(F32)
    for bi, (dil, ob, ls) in enumerate(zip(DILATIONS, (ob0, ob1, ob2), (ls0, ls1, ls2))):
        span = blk * dil
        nb = s_len // span

        def rows(start, dil=dil):
            return pl.ds(start, blk) if dil == 1 else pl.ds(start, blk, stride=dil)

        def block(tix, carry, bi=bi, span=span, nb=nb, ob=ob, ls=ls, rows=rows):
            n = tix % nb
            start = n * span + tix // nb
            prev = jnp.maximum(start - span, 0)
            q2 = _stack_heads(qf[rows(start), :].astype(BF16))
            kk = jnp.concatenate([kf[rows(prev), :], kf[rows(start), :]], axis=0).astype(BF16)
            vv = jnp.concatenate([vf[rows(prev), :], vf[rows(start), :]], axis=0).astype(BF16)
            s = lax.dot_general(q2, kk, (((1,), (1,)), ((), ())), preferred_element_type=F32)
            s = s + bias_ref[bi, jnp.where(n == 0, 1, 0), 0]
            m = jnp.max(s, axis=1, keepdims=True)
            p = jnp.exp(s - m)
            l = jnp.sum(p, axis=1, keepdims=True)
            o2 = jnp.dot(p.astype(BF16), vv, preferred_element_type=F32) / l
            lse = jnp.broadcast_to(m + jnp.log(l), (2 * blk, LANES))
            ob[rows(start), :] = _unstack_heads(o2)
            ls[rows(start), :] = _unstack_heads(lse)
            return carry

        def blocks(g, carry, block=block):
            for u in range(DIL_UNROLL):
                block(g * DIL_UNROLL + u, carry)
            return carry

        lax.fori_loop(0, dil * nb // DIL_UNROLL, blocks, 0)

    chunk = 512
    for c in range(s_len // chunk):
        r = pl.ds(c * chunk, chunk)
        l0, l1, l2 = ls0[r, :], ls1[r, :], ls2[r, :]
        mx = jnp.maximum(jnp.maximum(l0, l1), l2)
        e0, e1, e2 = jnp.exp(l0 - mx), jnp.exp(l1 - mx), jnp.exp(l2 - mx)
        o_ref[0, r, :] = (e0 * ob0[r, :] + e1 * ob1[r, :] + e2 * ob2[r, :]) / (e0 + e1 + e2)


def _dilated_attention(qkv, bias):
    bsz, s, _ = qkv.shape
    col = lambda off: pl.BlockSpec((1, s, LANES), lambda b, h: (b, 0, off + h))
    buf = pltpu.VMEM((s, LANES), F32)
    return pl.pallas_call(
        functools.partial(_dil_kernel, s_len=s),
        grid=(bsz, HEAD_PAIRS),
        in_specs=[col(0), col(HEAD_PAIRS), col(2 * HEAD_PAIRS),
                  pl.BlockSpec((len(DILATIONS), 2, 1, 2 * DIL_BLOCK, 2 * DIL_BLOCK), lambda b, h: (0, 0, h, 0, 0))],
        out_specs=pl.BlockSpec((1, s, LANES), lambda b, h: (b, 0, h)),
        out_shape=jax.ShapeDtypeStruct((bsz, s, WIDTH), F32),
        scratch_shapes=[buf] * 9,
        compiler_params=_params(("parallel", "arbitrary")),
        name="dilated_attention",
    )(qkv, qkv, qkv, bias)


def _bf16_bits(x):
    return pltpu.bitcast(x.astype(BF16).astype(F32), U32)


def _to_row_tiles(dst_ref, x, base=0):
    rows, d = x.shape
    for c in range(TILE_ROWS):
        lo = _bf16_bits(x[:, c * LANES:(c + 1) * LANES]) >> 16
        hi = _bf16_bits(x[:, d // 2 + c * LANES:d // 2 + (c + 1) * LANES])
        dst_ref[pl.ds(base + c, rows, stride=TILE_ROWS), :] = lo | hi


def _from_row_tiles(src_ref, rows, base=0):
    lo, hi = [], []
    for c in range(TILE_ROWS):
        w = src_ref[pl.ds(base + c, rows, stride=TILE_ROWS), :]
        lo.append(pltpu.bitcast(w << 16, F32))
        hi.append(pltpu.bitcast(w & jnp.uint32(0xFFFF0000), F32))
    return lo + hi


def _outproj_kernel(yf_ref, yd_ref, x_ref, g1_ref, sc_ref, sh_ref, g2_ref, gf_ref, gd_ref, wo1_ref, wo2_ref,
                    wrh_ref, wrl_ref, wsg_ref, wsu_ref, wsd_ref, xp_ref, h3_ref, lg_ref):
    nf = (_rms(yf_ref[0]) * gf_ref[...]).astype(BF16)
    nd = (_rms(yd_ref[0]) * gd_ref[...]).astype(BF16)
    mix = (jnp.dot(nf, wo1_ref[...], preferred_element_type=F32)
           + jnp.dot(nd, wo2_ref[...], preferred_element_type=F32))
    x1 = x_ref[0] + g1_ref[0] * mix
    h2 = _rms(x1) * (1.0 + sc_ref[0]) + sh_ref[0]
    hb = h2.astype(BF16)
    hl = (h2 - hb.astype(F32)).astype(BF16)
    lg_ref[...] = _dot_nt(wrh_ref[...], hb) + _dot_nt(wrh_ref[...], hl) + _dot_nt(wrl_ref[...], hb)
    act = _silu(jnp.dot(hb, wsg_ref[...], preferred_element_type=F32)) * jnp.dot(
        hb, wsu_ref[...], preferred_element_type=F32)
    shared = jnp.dot(act.astype(BF16), wsd_ref[...], preferred_element_type=F32)
    xp_ref[0] = x1 + g2_ref[0] * shared
    _to_row_tiles(h3_ref, h2)


def _out_projection(y_fox, y_dil, x, gate1, scale2, shift2, gate2, g_fox, g_dil, wo1, wo2, wr_hi, wr_lo,
                    wsg, wsu, wsd):
    bsz, s, d = x.shape
    tm = ROW_TILE
    nt = s // tm
    vec = pl.BlockSpec((1, 1, d), lambda b, i: (b, 0, 0))
    full = lambda w: pl.BlockSpec(w.shape, lambda b, i: (0,) * w.ndim)
    row = lambda n: pl.BlockSpec((1, tm, n), lambda b, i: (b, i, 0))
    return pl.pallas_call(
        _outproj_kernel,
        grid=(bsz, nt),
        in_specs=[row(WIDTH), row(WIDTH), row(d), vec, vec, vec, vec, full(g_fox), full(g_dil), full(wo1),
                  full(wo2), full(wr_hi), full(wr_lo), full(wsg), full(wsu), full(wsd)],
        out_specs=[row(d),
                   pl.BlockSpec((tm * TILE_ROWS, LANES), lambda b, i: (b * nt + i, 0)),
                   pl.BlockSpec((N_EXPERTS, tm), lambda b, i: (0, b * nt + i))],
        out_shape=[jax.ShapeDtypeStruct((bsz, s, d), F32),
                   jax.ShapeDtypeStruct((bsz * s * TILE_ROWS, LANES), U32),
                   jax.ShapeDtypeStruct((N_EXPERTS, bsz * s), F32)],
        compiler_params=_params(("parallel", "arbitrary")),
        name="outproj_norm2_router_shared",
    )(y_fox, y_dil, x, gate1, scale2, shift2, gate2, g_fox, g_dil, wo1, wo2, wr_hi, wr_lo, wsg, wsu, wsd)


def _first_argmax(v, row, size):
    m = jnp.max(v, axis=0, keepdims=True)
    return m, jnp.min(jnp.where(v == m, row, size), axis=0, keepdims=True)


def _route_kernel(lg_ref, rb_ref, tri_ref, e_ref, w_ref, rk_ref, cnt_ref, cnt_s):
    @pl.when(pl.program_id(0) == 0)
    def _():
        cnt_s[...] = jnp.zeros(cnt_s.shape, F32)

    t = lg_ref.shape[1]
    gsz = N_EXPERTS // N_GROUPS
    scores = jax.nn.sigmoid(lg_ref[...])
    sel = scores + rb_ref[...]
    row_g = lax.broadcasted_iota(jnp.int32, (gsz, t), 0)
    grp = []
    for g in range(N_GROUPS):
        v = sel[g * gsz:(g + 1) * gsz]
        m1, i1 = _first_argmax(v, row_g, gsz)
        m2 = jnp.max(jnp.where(row_g == i1, -jnp.inf, v), axis=0, keepdims=True)
        grp.append(m1 + m2)
    gv = jnp.concatenate(grp, axis=0)
    row8 = lax.broadcasted_iota(jnp.int32, (N_GROUPS, t), 0)
    pen = jnp.full((N_GROUPS, t), -jnp.inf, F32)
    for _ in range(TOP_K_GROUPS):
        _, ix = _first_argmax(gv, row8, N_GROUPS)
        pen = jnp.where(row8 == ix, 0.0, pen)
        gv = jnp.where(row8 == ix, -jnp.inf, gv)
    selm = jnp.concatenate([sel[g * gsz:(g + 1) * gsz] + pen[g:g + 1] for g in range(N_GROUPS)], axis=0)

    row = lax.broadcasted_iota(jnp.int32, (N_EXPERTS, t), 0)
    v = selm
    idxs, scs = [], []
    for _ in range(TOP_K):
        _, ix = _first_argmax(v, row, N_EXPERTS)
        hit = row == ix
        idxs.append(ix)
        scs.append(jnp.sum(jnp.where(hit, scores, 0.0), axis=0, keepdims=True))
        v = jnp.where(hit, -jnp.inf, v)
    chosen = jnp.where(v != selm, 1.0, 0.0)
    before = jnp.dot(chosen.astype(BF16), tri_ref[...], preferred_element_type=F32) + cnt_s[...]
    rks = [jnp.sum(jnp.where(row == ix, before, 0.0), axis=0, keepdims=True) for ix in idxs]
    sc = jnp.concatenate(scs, axis=0)
    e_ref[...] = jnp.concatenate(idxs, axis=0)
    w_ref[...] = sc / jnp.sum(sc, axis=0, keepdims=True) * ROUTED_SCALE
    rk_ref[...] = jnp.concatenate(rks, axis=0).astype(jnp.int32)
    cnt_s[...] = cnt_s[...] + jnp.sum(chosen, axis=1, keepdims=True)
    cnt_ref[...] = cnt_s[...]


def _route(logits_t, router_bias):
    e, n = logits_t.shape
    t = ROUTE_TILE
    tri = jnp.triu(jnp.ones((t, t), BF16), k=1)
    tile = pl.BlockSpec((TOP_K, t), lambda i: (0, i))
    return pl.pallas_call(
        _route_kernel,
        grid=(n // t,),
        in_specs=[pl.BlockSpec((e, t), lambda i: (0, i)),
                  pl.BlockSpec((e, 1), lambda i: (0, 0)),
                  pl.BlockSpec((t, t), lambda i: (0, 0))],
        out_specs=[tile, tile, tile, pl.BlockSpec((e, 1), lambda i: (0, 0))],
        out_shape=[jax.ShapeDtypeStruct((TOP_K, n), jnp.int32), jax.ShapeDtypeStruct((TOP_K, n), F32),
                   jax.ShapeDtypeStruct((TOP_K, n), jnp.int32), jax.ShapeDtypeStruct((e, 1), F32)],
        scratch_shapes=[pltpu.VMEM((e, 1), F32)],
        compiler_params=_params(("arbitrary",)),
        name="route_topk_rank",
    )(logits_t, router_bias.reshape(e, 1).astype(F32), tri)


def _dest_kernel(e_ref, rk_ref, ps_ref, d_ref):
    t = e_ref.shape[1]
    row = lax.broadcasted_iota(jnp.int32, (N_EXPERTS, t), 0)
    ps = ps_ref[...]
    base = [jnp.sum(jnp.where(row == e_ref[k:k + 1, :], ps, 0.0), axis=0, keepdims=True) for k in range(TOP_K)]
    d_ref[...] = jnp.concatenate(base, axis=0).astype(jnp.int32) + rk_ref[...]


def _dest_rows(eidx, rank, pstart):
    _, n = eidx.shape
    t = ROUTE_TILE
    tile = pl.BlockSpec((TOP_K, t), lambda i: (0, i))
    return pl.pallas_call(
        _dest_kernel,
        grid=(n // t,),
        in_specs=[tile, tile, pl.BlockSpec((N_EXPERTS, 1), lambda i: (0, 0))],
        out_specs=tile,
        out_shape=jax.ShapeDtypeStruct((TOP_K, n), jnp.int32),
        compiler_params=_params(("parallel",)),
        name="dest_rows",
    )(eidx, rank, pstart.astype(F32).reshape(N_EXPERTS, 1))


def _block_plan(counts, nblk):
    counts = counts.reshape(N_EXPERTS).astype(jnp.int32)
    padded = (counts + FFN_BLOCK - 1) // FFN_BLOCK * FFN_BLOCK
    pend = jnp.cumsum(padded).astype(jnp.int32)
    pstart = pend - padded
    first_row = jnp.arange(nblk, dtype=jnp.int32) * FFN_BLOCK
    block_e = jnp.minimum(jnp.sum(pend[None, :] <= first_row[:, None], axis=1), N_EXPERTS - 1).astype(jnp.int32)
    nused = (pend[-1:] // FFN_BLOCK).astype(jnp.int32)
    return pstart, pstart + counts, pend, block_e, nused


def _tile_copy(src_ref, src_row, dst_ref, dst_row, sem):
    return pltpu.make_async_copy(src_ref.at[pl.ds(pl.multiple_of(src_row * TILE_ROWS, TILE_ROWS), TILE_ROWS), :],
                                 dst_ref.at[pl.ds(pl.multiple_of(dst_row * TILE_ROWS, TILE_ROWS), TILE_ROWS), :],
                                 sem)


def _scatter_kernel(cend_ref, pend_ref, dst_ref, h3_ref, z_ref, xs_hbm, st0, st1, sem, zsem):
    i = pl.program_id(0)
    nsteps = pl.num_programs(0)
    tm = dst_ref.shape[1]
    stage = (st0, st1)

    def zero_copy(r):
        return pltpu.make_async_copy(z_ref, xs_hbm.at[pl.ds(pl.multiple_of(r * TILE_ROWS, TILE_ROWS), TILE_ROWS), :],
                                     zsem)

    def wait_slot(slot):
        for _ in range(TOP_K):
            pltpu.make_async_copy(stage[slot], xs_hbm.at[pl.ds(0, tm * TILE_ROWS), :], sem.at[slot]).wait()

    @pl.when(i == 0)
    def _():
        def start_row(r, c):
            zero_copy(r).start()
            return c

        def wait_row(r, c):
            zero_copy(r).wait()
            return c

        def per_expert(e, carry):
            lax.fori_loop(cend_ref[e], pend_ref[e], start_row, carry)
            prev = jnp.maximum(e - 1, 0)
            return lax.fori_loop(cend_ref[prev], jnp.where(e > 0, pend_ref[prev], cend_ref[prev]), wait_row, carry)

        lax.fori_loop(0, N_EXPERTS, per_expert, 0)
        lax.fori_loop(cend_ref[N_EXPERTS - 1], pend_ref[N_EXPERTS - 1], wait_row, 0)

    for slot in range(2):
        @pl.when(i % 2 == slot)
        def _(slot=slot):
            @pl.when(i >= 2)
            def _():
                wait_slot(slot)
            stage[slot][...] = h3_ref[...]
            for k in range(TOP_K):
                def body(jj, carry, k=k):
                    for u in range(ISSUE_UNROLL):
                        j = jj * ISSUE_UNROLL + u
                        _tile_copy(stage[slot], j, xs_hbm, dst_ref[k, j], sem.at[slot]).start()
                    return carry
                lax.fori_loop(0, tm // ISSUE_UNROLL, body, 0)

    @pl.when(i == nsteps - 1)
    def _():
        wait_slot(0)
        wait_slot(1)


def _dispatch_rows(h3, dest, cend, pend, rows):
    _, n = dest.shape
    tm = SCATTER_TILE
    assert n // tm >= 2
    stage = pltpu.VMEM((tm * TILE_ROWS, LANES), U32)
    grid_spec = pltpu.PrefetchScalarGridSpec(
        num_scalar_prefetch=2,
        grid=(n // tm,),
        in_specs=[pl.BlockSpec((TOP_K, tm), lambda i, ce, pe: (0, i), memory_space=pltpu.SMEM),
                  pl.BlockSpec((tm * TILE_ROWS, LANES), lambda i, ce, pe: (i, 0)),
                  pl.BlockSpec((TILE_ROWS, LANES), lambda i, ce, pe: (0, 0))],
        out_specs=pl.BlockSpec(memory_space=pl.ANY),
        scratch_shapes=[stage, stage, pltpu.SemaphoreType.DMA((2,)), pltpu.SemaphoreType.DMA],
    )
    return pl.pallas_call(
        _scatter_kernel,
        grid_spec=grid_spec,
        out_shape=jax.ShapeDtypeStruct((rows * TILE_ROWS, LANES), U32),
        compiler_params=_params(("arbitrary",)),
        name="dispatch_scatter",
    )(cend, pend, dest, h3, jnp.zeros((TILE_ROWS, LANES), U32))


def _ffn_kernel(be_ref, nu_ref, x_ref, *refs):
    w_refs, (y_ref, wgb, wub, wdb, cur) = refs[:3 * FFN_PER_STEP], refs[3 * FFN_PER_STEP:]
    first = pl.program_id(0) * FFN_PER_STEP

    @pl.when(first == 0)
    def _():
        cur[0] = 0

    slots = []
    for part in range(FFN_PER_STEP):
        blk = first + part
        wg_ref, wu_ref, wd_ref = w_refs[3 * part:3 * part + 3]

        @pl.when((blk == 0) | (be_ref[blk] != be_ref[jnp.maximum(blk - 1, 0)]))
        def _(wg_ref=wg_ref, wu_ref=wu_ref, wd_ref=wd_ref):
            slot = 1 - cur[0]
            cur[0] = slot
            wgb[slot] = wg_ref[0].astype(BF16)
            wub[slot] = wu_ref[0].astype(BF16)
            wdb[slot] = wd_ref[0].astype(BF16)

        slots.append(cur[0])

    @pl.when(first < nu_ref[0])
    def _():
        for part, slot in enumerate(slots):
            base = part * FFN_BLOCK * TILE_ROWS
            x = jnp.concatenate([c.astype(BF16) for c in _from_row_tiles(x_ref, FFN_BLOCK, base)], axis=1)
            act = _silu(jnp.dot(x, wgb[slot], preferred_element_type=F32)) * jnp.dot(
                x, wub[slot], preferred_element_type=F32)
            _to_row_tiles(y_ref, jnp.dot(act.astype(BF16), wdb[slot], preferred_element_type=F32), base)

    @pl.when(first >= nu_ref[0])
    def _():
        y_ref[...] = jnp.zeros(y_ref.shape, U32)


def _routed_experts(xs, block_e, nused, w_gate, w_up, w_down):
    rows = xs.shape[0] // TILE_ROWS
    nblk = rows // FFN_BLOCK
    _, d, hid = w_gate.shape
    step_rows = FFN_PER_STEP * FFN_BLOCK * TILE_ROWS
    w_specs = []
    for part in range(FFN_PER_STEP):
        expert = lambda i, be, nu, part=part: (be[i * FFN_PER_STEP + part], 0, 0)
        w_specs += [pl.BlockSpec((1, d, hid), expert), pl.BlockSpec((1, d, hid), expert),
                    pl.BlockSpec((1, hid, d), expert)]
    grid_spec = pltpu.PrefetchScalarGridSpec(
        num_scalar_prefetch=2,
        grid=(nblk // FFN_PER_STEP,),
        in_specs=[pl.BlockSpec((step_rows, LANES),
                               lambda i, be, nu: (jnp.minimum(i, (nu[0] - 1) // FFN_PER_STEP), 0))] + w_specs,
        out_specs=pl.BlockSpec((step_rows, LANES), lambda i, be, nu: (i, 0)),
        scratch_shapes=[pltpu.VMEM((2, d, hid), BF16), pltpu.VMEM((2, d, hid), BF16), pltpu.VMEM((2, hid, d), BF16),
                        pltpu.SMEM((1,), jnp.int32)],
    )
    return pl.pallas_call(
        _ffn_kernel,
        grid_spec=grid_spec,
        out_shape=jax.ShapeDtypeStruct((rows * TILE_ROWS, LANES), U32),
        compiler_params=_params(("arbitrary",)),
        name="routed_experts",
    )(block_e, nused, xs, *([w_gate, w_up, w_down] * FFN_PER_STEP))


def _gather_start(idx_ref, src_hbm, dst_vmem, sem):
    nk, tm = idx_ref.shape
    for k in range(nk):
        def body(jj, carry, k=k):
            for u in range(ISSUE_UNROLL):
                j = jj * ISSUE_UNROLL + u
                _tile_copy(src_hbm, idx_ref[k, j], dst_vmem, k * tm + j, sem).start()
            return carry
        lax.fori_loop(0, tm // ISSUE_UNROLL, body, 0)


def _gather_wait(src_hbm, dst_vmem, sem):
    pltpu.make_async_copy(src_hbm.at[pl.ds(0, dst_vmem.shape[0]), :], dst_vmem, sem).wait()


def _combine_kernel(dst_ref, dstn_ref, wt_ref, xp_ref, g2_ref, gfin_ref, ys_hbm, o_ref, yb0, yb1, sem, *, nsteps):
    i = pl.program_id(0)
    tm = COMBINE_TILE
    bufs = (yb0, yb1)

    @pl.when(i == 0)
    def _():
        _gather_start(dst_ref, ys_hbm, yb0, sem.at[0])

    for slot in range(2):
        @pl.when((i + 1 < nsteps) & ((i + 1) % 2 == slot))
        def _(slot=slot):
            _gather_start(dstn_ref, ys_hbm, bufs[slot], sem.at[slot])

    for slot in range(2):
        @pl.when(i % 2 == slot)
        def _(slot=slot):
            _gather_wait(ys_hbm, bufs[slot], sem.at[slot])
            wt = wt_ref[...]
            cols = None
            for k in range(TOP_K):
                part = [c * wt[:, k:k + 1] for c in _from_row_tiles(bufs[slot], tm, base=k * tm * TILE_ROWS)]
                cols = part if cols is None else [a + b for a, b in zip(cols, part)]
            routed = jnp.concatenate(cols, axis=1)
            o_ref[0] = _rms(xp_ref[0] + g2_ref[0] * routed) * gfin_ref[...]


def _combine(ys, dest, wts_t, xp, gate2, g_final):
    bsz, s, d = xp.shape
    tm = COMBINE_TILE
    nt = s // tm
    nsteps = bsz * nt
    vec = pl.BlockSpec((1, 1, d), lambda i: (i // nt, 0, 0))
    return pl.pallas_call(
        functools.partial(_combine_kernel, nsteps=nsteps),
        grid=(nsteps,),
        in_specs=[pl.BlockSpec((TOP_K, tm), lambda i: (0, i), memory_space=pltpu.SMEM),
                  pl.BlockSpec((TOP_K, tm), lambda i: (0, jnp.minimum(i + 1, nsteps - 1)),
                               memory_space=pltpu.SMEM),
                  pl.BlockSpec((tm, TOP_K), lambda i: (i, 0)),
                  pl.BlockSpec((1, tm, d), lambda i: (i // nt, i % nt, 0)),
                  vec,
                  pl.BlockSpec((1, d), lambda i: (0, 0)),
                  pl.BlockSpec(memory_space=pl.ANY)],
        out_specs=pl.BlockSpec((1, tm, d), lambda i: (i // nt, i % nt, 0)),
        out_shape=jax.ShapeDtypeStruct((bsz, s, d), F32),
        scratch_shapes=[pltpu.VMEM((tm * TOP_K * TILE_ROWS, LANES), U32),
                        pltpu.VMEM((tm * TOP_K * TILE_ROWS, LANES), U32),
                        pltpu.SemaphoreType.DMA((2,))],
        compiler_params=_params(("arbitrary",)),
        name="combine_final_norm",
    )(dest, dest, wts_t, xp, gate2, g_final.reshape(1, d), ys)


def kernel(x, c, w_in, b_forget, g_fox_out, g_dil_out, w_out, w_ada, b_ada, w_router, router_bias,
           w_exp_gate, w_exp_up, w_exp_down, w_sh_gate, w_sh_up, w_sh_down, rel_bias, g_final):
    bsz, s, d = x.shape
    depth = w_in.shape[0]
    assert depth == 1 and d == 2 * TILE_ROWS * LANES and s % (DIL_BLOCK * DILATIONS[-1]) == 0
    l = 0
    mod = _modulation(c, w_ada[l], b_ada[l])
    shift1, scale1, gate1, shift2, scale2, gate2 = [m[:, None, :] for m in jnp.split(mod, 6, axis=-1)]

    qscale = HEAD_DIM ** -0.5
    o3 = 3 * WIDTH
    w = w_in[l]
    wq_t = (w[:, :WIDTH] * (qscale * LOG2E)).T.astype(BF16)
    wk = w[:, WIDTH:2 * WIDTH].astype(BF16)
    wv_t = w[:, 2 * WIDTH:o3].T.astype(BF16)
    w_flog = jnp.pad(w[:, o3:o3 + N_HEADS], ((0, 0), (0, LANES - N_HEADS))).astype(BF16)
    wd0 = o3 + N_HEADS
    w_dil = jnp.concatenate([w[:, wd0:wd0 + WIDTH] * qscale, w[:, wd0 + WIDTH:]], axis=1).astype(BF16)

    q_t, k_f, v_t, qkv_d, flog = _in_projection(x, scale1, shift1, wq_t, wk, wv_t, w_dil, w_flog)
    y_fox = _fox_attention(q_t, k_f, _forget_cumsum(flog, b_forget[l]), v_t)
    y_dil = _dilated_attention(qkv_d, _dilated_bias(rel_bias))

    wr = w_router[l].T
    wr_hi = wr.astype(BF16)
    wr_lo = (wr - wr_hi.astype(F32)).astype(BF16)
    wo = w_out[l].astype(BF16)
    xp, h3, logits_t = _out_projection(
        y_fox, y_dil, x, gate1, scale2, shift2, gate2, g_fox_out[l].reshape(1, WIDTH),
        g_dil_out[l].reshape(1, WIDTH), wo[:WIDTH], wo[WIDTH:], wr_hi, wr_lo,
        w_sh_gate[l].astype(BF16), w_sh_up[l].astype(BF16), w_sh_down[l].astype(BF16))

    eidx, wts, rank, counts = _route(logits_t, router_bias[l])
    rows = bsz * s * TOP_K + N_EXPERTS * FFN_BLOCK
    pstart, cend, pend, block_e, nused = _block_plan(counts, rows // FFN_BLOCK)
    dest = _dest_rows(eidx, rank, pstart)
    xs = _dispatch_rows(h3, dest, cend, pend, rows)
    ys = _routed_experts(xs, block_e, nused, w_exp_gate[l], w_exp_up[l], w_exp_down[l])
    return _combine(ys, dest, wts.T, xp, gate2, g_final)
```

```python
import functools

import numpy as np
import jax
import jax.numpy as jnp
from jax import lax
from jax.experimental import pallas as pl
from jax.experimental.pallas import tpu as pltpu

F32 = jnp.float32
BF16 = jnp.bfloat16

HEAD_DIM = 64
N_HEADS = 8
WIDTH = N_HEADS * HEAD_DIM
LANES = 128
TILE_ROWS = 4
U32 = jnp.uint32
DIL_UNROLL = 8
HEAD_PAIRS = WIDTH // LANES
DIL_BLOCK = 128
DILATIONS = (1, 4, 16)
T5_NUM_BUCKETS = 32
T5_MAX_DISTANCE = 2048
N_EXPERTS = 256
TOP_K = 8
N_GROUPS = 8
TOP_K_GROUPS = 4
ROUTED_SCALE = 2.5
EPS = 1e-6
NEG = -1e30
LOG2E = float(np.log2(np.e))
VMEM_LIMIT = 56 * 1024 * 1024

ROW_TILE = 512
FOX_TILE = 512
FFN_BLOCK = 256
FFN_PER_STEP = 2
COMBINE_TILE = 256
SCATTER_TILE = 256
ROUTE_TILE = 512
ISSUE_UNROLL = 8


def _params(semantics):
    return pltpu.CompilerParams(dimension_semantics=semantics, vmem_limit_bytes=VMEM_LIMIT)


def _rms(x):
    return x * lax.rsqrt(jnp.mean(x * x, axis=-1, keepdims=True) + EPS)


def _silu(x):
    return x * jax.nn.sigmoid(x)


def _dot_nt(a, b):
    return lax.dot_general(a, b, (((1,), (1,)), ((), ())), preferred_element_type=F32)


def _mod_kernel(c_ref, w_ref, b_ref, o_ref):
    o_ref[...] = jnp.dot(_silu(c_ref[...]), w_ref[...], precision=lax.Precision.HIGHEST,
                         preferred_element_type=F32) + b_ref[...]


def _modulation(c, w_ada, b_ada):
    bsz, d = c.shape
    n = w_ada.shape[1]
    tn = 1536
    return pl.pallas_call(
        _mod_kernel,
        grid=(n // tn,),
        in_specs=[pl.BlockSpec((bsz, d), lambda j: (0, 0)),
                  pl.BlockSpec((d, tn), lambda j: (0, j)),
                  pl.BlockSpec((1, tn), lambda j: (0, j))],
        out_specs=pl.BlockSpec((bsz, tn), lambda j: (0, j)),
        out_shape=jax.ShapeDtypeStruct((bsz, n), F32),
        compiler_params=_params(("arbitrary",)),
        name="adaln_mod",
    )(c, w_ada, b_ada.reshape(1, n))


def _inproj_kernel(x_ref, sc_ref, sh_ref, wq_ref, wk_ref, wv_ref, wd_ref, wl_ref,
                   oq_ref, ok_ref, ov_ref, od_ref, ol_ref):
    h = _rms(x_ref[0]) * (1.0 + sc_ref[0]) + sh_ref[0]
    hb = h.astype(BF16)
    oq_ref[0] = _dot_nt(wq_ref[...], hb).astype(BF16)
    ov_ref[0] = _dot_nt(wv_ref[...], hb).astype(BF16)
    ok_ref[0] = jnp.dot(hb, wk_ref[...], preferred_element_type=F32).astype(BF16)
    od_ref[0] = jnp.dot(hb, wd_ref[...], preferred_element_type=F32).astype(BF16)
    ol_ref[0] = jnp.dot(hb, wl_ref[...], preferred_element_type=F32)


def _in_projection(x, scale1, shift1, wq_t, wk, wv_t, w_dil, w_flog):
    bsz, s, d = x.shape
    tm = ROW_TILE
    vec = pl.BlockSpec((1, 1, d), lambda b, i: (b, 0, 0))
    full = lambda w: pl.BlockSpec(w.shape, lambda b, i: (0, 0))
    row = lambda n: pl.BlockSpec((1, tm, n), lambda b, i: (b, i, 0))
    col = pl.BlockSpec((1, WIDTH, tm), lambda b, i: (b, 0, i))
    return pl.pallas_call(
        _inproj_kernel,
        grid=(bsz, s // tm),
        in_specs=[row(d), vec, vec, full(wq_t), full(wk), full(wv_t), full(w_dil), full(w_flog)],
        out_specs=[col, row(WIDTH), col, row(3 * WIDTH), row(LANES)],
        out_shape=[jax.ShapeDtypeStruct((bsz, WIDTH, s), BF16),
                   jax.ShapeDtypeStruct((bsz, s, WIDTH), BF16),
                   jax.ShapeDtypeStruct((bsz, WIDTH, s), BF16),
                   jax.ShapeDtypeStruct((bsz, s, 3 * WIDTH), BF16),
                   jax.ShapeDtypeStruct((bsz, s, LANES), F32)],
        compiler_params=_params(("parallel", "arbitrary")),
        name="norm1_inproj",
    )(x, scale1, shift1, wq_t, wk, wv_t, w_dil, w_flog)


FORGET_PARTS = 3


def _split_bf16(x):
    parts = []
    for _ in range(FORGET_PARTS):
        p = x.astype(BF16)
        parts.append(p)
        x = x - p.astype(F32)
    return parts


def _forget_kernel(f_ref, b_ref, tri_ref, place_ref, o_ref):
    blk = tri_ref.shape[0]
    carry = jnp.zeros((1, LANES), F32)
    for c in range(f_ref.shape[1] // blk):
        rows = pl.ds(c * blk, blk)
        z = f_ref[0, rows, :] + b_ref[...]
        ls = jnp.minimum(z, 0.0) - jnp.log1p(jnp.exp(-jnp.abs(z)))
        cum = carry
        for p in _split_bf16(ls):
            cum = cum + jnp.dot(tri_ref[...], p, preferred_element_type=F32)
        carry = cum[blk - 1:blk, :]
        out = None
        for n, p in enumerate(_split_bf16(cum * LOG2E)):
            d = jnp.dot(p, place_ref[n], preferred_element_type=F32)
            out = d if out is None else out + d
        o_ref[0, rows, :] = out.astype(BF16)


def _forget_cumsum(flog, b_forget):
    bsz, s, _ = flog.shape
    blk = 512
    place = np.zeros((FORGET_PARTS, LANES, WIDTH), np.float32)
    for h in range(N_HEADS):
        for n in range(FORGET_PARTS):
            place[n, h, (h // 2) * LANES + (h % 2) * HEAD_DIM + n] = 1.0
    return pl.pallas_call(
        _forget_kernel,
        grid=(bsz,),
        in_specs=[pl.BlockSpec((1, s, LANES), lambda b: (b, 0, 0)),
                  pl.BlockSpec((1, LANES), lambda b: (0, 0)),
                  pl.BlockSpec((blk, blk), lambda b: (0, 0)),
                  pl.BlockSpec(place.shape, lambda b: (0, 0, 0))],
        out_specs=pl.BlockSpec((1, s, WIDTH), lambda b: (b, 0, 0)),
        out_shape=jax.ShapeDtypeStruct((bsz, s, WIDTH), BF16),
        compiler_params=_params(("parallel",)),
        name="forget_cumsum",
    )(flog, jnp.pad(b_forget, (0, LANES - N_HEADS)).reshape(1, LANES),
      jnp.tril(jnp.ones((blk, blk), BF16)), jnp.asarray(place, BF16))


def _stack_heads(q):
    lane = lax.broadcasted_iota(jnp.int32, q.shape, 1)
    zero = jnp.zeros_like(q)
    return jnp.concatenate([jnp.where(lane < HEAD_DIM, q, zero), jnp.where(lane >= HEAD_DIM, q, zero)], axis=0)


def _unstack_heads(o2):
    rows = o2.shape[0] // 2
    lane = lax.broadcasted_iota(jnp.int32, (rows, LANES), 1)
    return jnp.where(lane < HEAD_DIM, o2[:rows], o2[rows:])


def _fox_kernel(q_ref, k_ref, fa_ref, v_ref, o_ref, m_s, l_s, acc_s, s_a, s_b, *, t):
    i = pl.program_id(2)
    q_t = q_ref[0]
    row = lax.broadcasted_iota(jnp.int32, (LANES, 2 * t), 0)
    col = lax.broadcasted_iota(jnp.int32, (LANES, 2 * t), 1)
    head_row = jnp.where(col < t, 0, HEAD_DIM)
    q2 = jnp.concatenate([q_t, q_t], axis=1)
    q2 = jnp.where((row >= head_row) & (row < head_row + HEAD_DIM), q2, jnp.zeros_like(q2))
    minus = jnp.where((row >= head_row) & (row < head_row + FORGET_PARTS), -1.0, 0.0).astype(BF16)
    qa = jnp.concatenate([q2, minus], axis=0)
    m_s[...] = jnp.full(m_s.shape, NEG, F32)
    l_s[...] = jnp.zeros(l_s.shape, F32)
    acc_s[...] = jnp.zeros(acc_s.shape, F32)

    def scores(ks, tk):
        kk = jnp.concatenate([k_ref[0, pl.ds(ks, tk), :], fa_ref[0, pl.ds(ks, tk), :]], axis=1)
        return jnp.dot(kk, qa, preferred_element_type=F32)

    def absorb(s, ks, tk):
        m_prev = m_s[...]
        m_new = jnp.maximum(m_prev, jnp.max(s, axis=0, keepdims=True))
        alpha = jnp.exp2(m_prev - m_new)
        p = jnp.exp2(s - m_new)
        l_s[...] = alpha * l_s[...] + jnp.sum(p, axis=0, keepdims=True)
        acc_s[...] = alpha * acc_s[...] + jnp.dot(v_ref[0, :, pl.ds(ks, tk)], p.astype(BF16),
                                                  preferred_element_type=F32)
        m_s[...] = m_new

    npair = i // 2

    def tile(j):
        return pl.multiple_of(j * 2 * t, 2 * t)

    @pl.when(npair > 0)
    def _():
        s_a[...] = scores(0, 2 * t)

        def body(jj, carry):
            j = 2 * jj
            s_b[...] = scores(tile(j + 1), 2 * t)
            absorb(s_a[...], tile(j), 2 * t)
            s_a[...] = scores(tile(jnp.minimum(j + 2, npair - 1)), 2 * t)
            absorb(s_b[...], tile(j + 1), 2 * t)
            return carry

        lax.fori_loop(0, npair // 2, body, 0)

        @pl.when(npair % 2 == 1)
        def _():
            absorb(s_a[...], tile(npair - 1), 2 * t)

    def last(ks, tk):
        r = lax.broadcasted_iota(jnp.int32, (tk, 2 * t), 0)
        c = lax.broadcasted_iota(jnp.int32, (tk, 2 * t), 1)
        absorb(jnp.where(r <= jnp.where(c >= t, c - t, c) + (tk - t), scores(ks, tk), NEG), ks, tk)

    @pl.when(i % 2 == 1)
    def _():
        last(tile(npair), 2 * t)

    @pl.when(i % 2 == 0)
    def _():
        last(pl.multiple_of(i * t, t), t)

    o2 = acc_s[...] / l_s[...]
    o_t = jnp.where(lax.broadcasted_iota(jnp.int32, (LANES, t), 0) < HEAD_DIM, o2[:, :t], o2[:, t:])
    o_ref[0] = o_t.T


def _fox_attention(q_t, k, f_aug, v_t):
    bsz, s, _ = k.shape
    t = FOX_TILE
    keys = pl.BlockSpec((1, s, LANES), lambda b, h, i: (b, 0, h))
    return pl.pallas_call(
        functools.partial(_fox_kernel, t=t),
        grid=(bsz, HEAD_PAIRS, s // t),
        in_specs=[pl.BlockSpec((1, LANES, t), lambda b, h, i: (b, h, i)),
                  keys, keys,
                  pl.BlockSpec((1, LANES, s), lambda b, h, i: (b, h, 0))],
        out_specs=pl.BlockSpec((1, t, LANES), lambda b, h, i: (b, i, h)),
        out_shape=jax.ShapeDtypeStruct((bsz, s, WIDTH), F32),
        scratch_shapes=[pltpu.VMEM((1, 2 * t), F32), pltpu.VMEM((1, 2 * t), F32),
                        pltpu.VMEM((LANES, 2 * t), F32), pltpu.VMEM((2 * t, 2 * t), F32),
                        pltpu.VMEM((2 * t, 2 * t), F32)],
        compiler_params=_params(("parallel", "parallel", "arbitrary")),
        name="fox_attention",
    )(q_t, k, f_aug, v_t)


def _t5_bucket(dist):
    max_exact = T5_NUM_BUCKETS // 2
    d = np.maximum(dist, 1).astype(np.float32)
    large = max_exact + (np.log(d / max_exact) / np.log(T5_MAX_DISTANCE / max_exact)
                         * (T5_NUM_BUCKETS - max_exact)).astype(np.int32)
    large = np.minimum(large, T5_NUM_BUCKETS - 1)
    return np.where(dist < max_exact, dist, large).astype(np.int32)


def _dilated_bias(rel_bias):
    blk = DIL_BLOCK
    period = 3 * blk
    m = np.arange(period)
    rel = np.where(m < 2 * blk, blk - m, blk - (m - period))
    band = (rel >= 0) & (rel <= blk)
    onehot = np.zeros((len(DILATIONS), period, T5_NUM_BUCKETS), np.float32)
    for bi, dil in enumerate(DILATIONS):
        onehot[bi, m, _t5_bucket(np.clip(rel, 0, blk) * dil)] = 1.0
    w = jnp.einsum('bmk,kh->bhm', jnp.asarray(onehot), rel_bias.astype(F32),
                   precision=lax.Precision.HIGHEST)
    w = jnp.where(jnp.asarray(band), w, NEG)
    flat = jnp.tile(w, (1, 1, blk))[:, :, :blk * (period - 1)]
    table = flat.reshape(len(DILATIONS), N_HEADS, blk, period - 1)[..., :2 * blk]
    table = table.reshape(len(DILATIONS), HEAD_PAIRS, 2 * blk, 2 * blk)
    first = jnp.where(jnp.arange(2 * blk) < blk, NEG, table)
    return jnp.stack([table, first], axis=1)


def _dil_kernel(q_ref, k_ref, v_ref, bias_ref, o_ref, qf, kf, vf, ob0, ob1, ob2, ls0, ls1, ls2, *, s_len):
    blk = DIL_BLOCK
    qf[...] = q_ref[0].astype(F32)
    kf[...] = k_ref[0].astype(F32)
    vf[...] = v_ref[0].astype(F32)
    for bi, (dil, ob, ls) in enumerate(zip(DILATIONS, (ob0, ob1, ob2), (ls0, ls1, ls2))):
        span = blk * dil
        nb = s_len // span

        def rows(start, dil=dil):
            return pl.ds(start, blk) if dil == 1 else pl.ds(start, blk, stride=dil)

        def block(tix, carry, bi=bi, span=span, nb=nb, ob=ob, ls=ls, rows=rows):
            n = tix % nb
            start = n * span + tix // nb
            prev = jnp.maximum(start - span, 0)
            q2 = _stack_heads(qf[rows(start), :].astype(BF16))
            kk = jnp.concatenate([kf[rows(prev), :], kf[rows(start), :]], axis=0).astype(BF16)
            vv = jnp.concatenate([vf[rows(prev), :], vf[rows(start), :]], axis=0).astype(BF16)
            s = lax.dot_general(q2, kk, (((1,), (1,)), ((), ())), preferred_element_type=F32)
            s = s + bias_ref[bi, jnp.where(n == 0, 1, 0), 0]
            m = jnp.max(s, axis=1, keepdims=True)
            p = jnp.exp(s - m)
            l = jnp.sum(p, axis=1, keepdims=True)
            o2 = jnp.dot(p.astype(BF16), vv, preferred_element_type=F32) / l
            lse = jnp.broadcast_to(m + jnp.log(l), (2 * blk, LANES))
            ob[rows(start), :] = _unstack_heads(o2)
            ls[rows(start), :] = _unstack_heads(lse)
            return carry

        def blocks(g, carry, block=block):
            for u in range(DIL_UNROLL):
                block(g * DIL_UNROLL + u, carry)
            return carry

        lax.fori_loop(0, dil * nb // DIL_UNROLL, blocks, 0)

    chunk = 512
    for c in range(s_len // chunk):
        r = pl.ds(c * chunk, chunk)
        l0, l1, l2 = ls0[r, :], ls1[r, :], ls2[r, :]
        mx = jnp.maximum(jnp.maximum(l0, l1), l2)
        e0, e1, e2 = jnp.exp(l0 - mx), jnp.exp(l1 - mx), jnp.exp(l2 - mx)
        o_ref[0, r, :] = (e0 * ob0[r, :] + e1 * ob1[r, :] + e2 * ob2[r, :]) / (e0 + e1 + e2)


def _dilated_attention(qkv, bias):
    bsz, s, _ = qkv.shape
    col = lambda off: pl.BlockSpec((1, s, LANES), lambda b, h: (b, 0, off + h))
    buf = pltpu.VMEM((s, LANES), F32)
    return pl.pallas_call(
        functools.partial(_dil_kernel, s_len=s),
        grid=(bsz, HEAD_PAIRS),
        in_specs=[col(0), col(HEAD_PAIRS), col(2 * HEAD_PAIRS),
                  pl.BlockSpec((len(DILATIONS), 2, 1, 2 * DIL_BLOCK, 2 * DIL_BLOCK), lambda b, h: (0, 0, h, 0, 0))],
        out_specs=pl.BlockSpec((1, s, LANES), lambda b, h: (b, 0, h)),
        out_shape=jax.ShapeDtypeStruct((bsz, s, WIDTH), F32),
        scratch_shapes=[buf] * 9,
        compiler_params=_params(("parallel", "arbitrary")),
        name="dilated_attention",
    )(qkv, qkv, qkv, bias)


def _bf16_bits(x):
    return pltpu.bitcast(x.astype(BF16).astype(F32), U32)


def _to_row_tiles(dst_ref, x, base=0):
    rows, d = x.shape
    for c in range(TILE_ROWS):
        lo = _bf16_bits(x[:, c * LANES:(c + 1) * LANES]) >> 16
        hi = _bf16_bits(x[:, d // 2 + c * LANES:d // 2 + (c + 1) * LANES])
        dst_ref[pl.ds(base + c, rows, stride=TILE_ROWS), :] = lo | hi


def _from_row_tiles(src_ref, rows, base=0):
    lo, hi = [], []
    for c in range(TILE_ROWS):
        w = src_ref[pl.ds(base + c, rows, stride=TILE_ROWS), :]
        lo.append(pltpu.bitcast(w << 16, F32))
        hi.append(pltpu.bitcast(w & jnp.uint32(0xFFFF0000), F32))
    return lo + hi


def _outproj_kernel(yf_ref, yd_ref, x_ref, g1_ref, sc_ref, sh_ref, g2_ref, gf_ref, gd_ref, wo1_ref, wo2_ref,
                    wrh_ref, wrl_ref, wsg_ref, wsu_ref, wsd_ref, xp_ref, h3_ref, lg_ref):
    nf = (_rms(yf_ref[0]) * gf_ref[...]).astype(BF16)
    nd = (_rms(yd_ref[0]) * gd_ref[...]).astype(BF16)
    mix = (jnp.dot(nf, wo1_ref[...], preferred_element_type=F32)
           + jnp.dot(nd, wo2_ref[...], preferred_element_type=F32))
    x1 = x_ref[0] + g1_ref[0] * mix
    h2 = _rms(x1) * (1.0 + sc_ref[0]) + sh_ref[0]
    hb = h2.astype(BF16)
    hl = (h2 - hb.astype(F32)).astype(BF16)
    lg_ref[...] = _dot_nt(wrh_ref[...], hb) + _dot_nt(wrh_ref[...], hl) + _dot_nt(wrl_ref[...], hb)
    act = _silu(jnp.dot(hb, wsg_ref[...], preferred_element_type=F32)) * jnp.dot(
        hb, wsu_ref[...], preferred_element_type=F32)
    shared = jnp.dot(act.astype(BF16), wsd_ref[...], preferred_element_type=F32)
    xp_ref[0] = x1 + g2_ref[0] * shared
    _to_row_tiles(h3_ref, h2)


def _out_projection(y_fox, y_dil, x, gate1, scale2, shift2, gate2, g_fox, g_dil, wo1, wo2, wr_hi, wr_lo,
                    wsg, wsu, wsd):
    bsz, s, d = x.shape
    tm = ROW_TILE
    nt = s // tm
    vec = pl.BlockSpec((1, 1, d), lambda b, i: (b, 0, 0))
    full = lambda w: pl.BlockSpec(w.shape, lambda b, i: (0,) * w.ndim)
    row = lambda n: pl.BlockSpec((1, tm, n), lambda b, i: (b, i, 0))
    return pl.pallas_call(
        _outproj_kernel,
        grid=(bsz, nt),
        in_specs=[row(WIDTH), row(WIDTH), row(d), vec, vec, vec, vec, full(g_fox), full(g_dil), full(wo1),
                  full(wo2), full(wr_hi), full(wr_lo), full(wsg), full(wsu), full(wsd)],
        out_specs=[row(d),
                   pl.BlockSpec((tm * TILE_ROWS, LANES), lambda b, i: (b * nt + i, 0)),
                   pl.BlockSpec((N_EXPERTS, tm), lambda b, i: (0, b * nt + i))],
        out_shape=[jax.ShapeDtypeStruct((bsz, s, d), F32),
                   jax.ShapeDtypeStruct((bsz * s * TILE_ROWS, LANES), U32),
                   jax.ShapeDtypeStruct((N_EXPERTS, bsz * s), F32)],
        compiler_params=_params(("parallel", "arbitrary")),
        name="outproj_norm2_router_shared",
    )(y_fox, y_dil, x, gate1, scale2, shift2, gate2, g_fox, g_dil, wo1, wo2, wr_hi, wr_lo, wsg, wsu, wsd)


def _first_argmax(v, row, size):
    m = jnp.max(v, axis=0, keepdims=True)
    return m, jnp.min(jnp.where(v == m, row, size), axis=0, keepdims=True)


def _route_kernel(lg_ref, rb_ref, tri_ref, e_ref, w_ref, rk_ref, cnt_ref, cnt_s):
    @pl.when(pl.program_id(0) == 0)
    def _():
        cnt_s[...] = jnp.zeros(cnt_s.shape, F32)

    t = lg_ref.shape[1]
    gsz = N_EXPERTS // N_GROUPS
    scores = jax.nn.sigmoid(lg_ref[...])
    sel = scores + rb_ref[...]
    row_g = lax.broadcasted_iota(jnp.int32, (gsz, t), 0)
    grp = []
    for g in range(N_GROUPS):
        v = sel[g * gsz:(g + 1) * gsz]
        m1, i1 = _first_argmax(v, row_g, gsz)
        m2 = jnp.max(jnp.where(row_g == i1, -jnp.inf, v), axis=0, keepdims=True)
        grp.append(m1 + m2)
    gv = jnp.concatenate(grp, axis=0)
    row8 = lax.broadcasted_iota(jnp.int32, (N_GROUPS, t), 0)
    pen = jnp.full((N_GROUPS, t), -jnp.inf, F32)
    for _ in range(TOP_K_GROUPS):
        _, ix = _first_argmax(gv, row8, N_GROUPS)
        pen = jnp.where(row8 == ix, 0.0, pen)
        gv = jnp.where(row8 == ix, -jnp.inf, gv)
    selm = jnp.concatenate([sel[g * gsz:(g + 1) * gsz] + pen[g:g + 1] for g in range(N_GROUPS)], axis=0)

    row = lax.broadcasted_iota(jnp.int32, (N_EXPERTS, t), 0)
    v = selm
    idxs, scs = [], []
    for _ in range(TOP_K):
        _, ix = _first_argmax(v, row, N_EXPERTS)
        hit = row == ix
        idxs.append(ix)
        scs.append(jnp.sum(jnp.where(hit, scores, 0.0), axis=0, keepdims=True))
        v = jnp.where(hit, -jnp.inf, v)
    chosen = jnp.where(v != selm, 1.0, 0.0)
    before = jnp.dot(chosen.astype(BF16), tri_ref[...], preferred_element_type=F32) + cnt_s[...]
    rks = [jnp.sum(jnp.where(row == ix, before, 0.0), axis=0, keepdims=True) for ix in idxs]
    sc = jnp.concatenate(scs, axis=0)
    e_ref[...] = jnp.concatenate(idxs, axis=0)
    w_ref[...] = sc / jnp.sum(sc, axis=0, keepdims=True) * ROUTED_SCALE
    rk_ref[...] = jnp.concatenate(rks, axis=0).astype(jnp.int32)
    cnt_s[...] = cnt_s[...] + jnp.sum(chosen, axis=1, keepdims=True)
    cnt_ref[...] = cnt_s[...]


def _route(logits_t, router_bias):
    e, n = logits_t.shape
    t = ROUTE_TILE
    tri = jnp.triu(jnp.ones((t, t), BF16), k=1)
    tile = pl.BlockSpec((TOP_K, t), lambda i: (0, i))
    return pl.pallas_call(
        _route_kernel,
        grid=(n // t,),
        in_specs=[pl.BlockSpec((e, t), lambda i: (0, i)),
                  pl.BlockSpec((e, 1), lambda i: (0, 0)),
                  pl.BlockSpec((t, t), lambda i: (0, 0))],
        out_specs=[tile, tile, tile, pl.BlockSpec((e, 1), lambda i: (0, 0))],
        out_shape=[jax.ShapeDtypeStruct((TOP_K, n), jnp.int32), jax.ShapeDtypeStruct((TOP_K, n), F32),
                   jax.ShapeDtypeStruct((TOP_K, n), jnp.int32), jax.ShapeDtypeStruct((e, 1), F32)],
        scratch_shapes=[pltpu.VMEM((e, 1), F32)],
        compiler_params=_params(("arbitrary",)),
        name="route_topk_rank",
    )(logits_t, router_bias.reshape(e, 1).astype(F32), tri)


def _dest_kernel(e_ref, rk_ref, ps_ref, d_ref):
    t = e_ref.shape[1]
    row = lax.broadcasted_iota(jnp.int32, (N_EXPERTS, t), 0)
    ps = ps_ref[...]
    base = [jnp.sum(jnp.where(row == e_ref[k:k + 1, :], ps, 0.0), axis=0, keepdims=True) for k in range(TOP_K)]
    d_ref[...] = jnp.concatenate(base, axis=0).astype(jnp.int32) + rk_ref[...]


def _dest_rows(eidx, rank, pstart):
    _, n = eidx.shape
    t = ROUTE_TILE
    tile = pl.BlockSpec((TOP_K, t), lambda i: (0, i))
    return pl.pallas_call(
        _dest_kernel,
        grid=(n // t,),
        in_specs=[tile, tile, pl.BlockSpec((N_EXPERTS, 1), lambda i: (0, 0))],
        out_specs=tile,
        out_shape=jax.ShapeDtypeStruct((TOP_K, n), jnp.int32),
        compiler_params=_params(("parallel",)),
        name="dest_rows",
    )(eidx, rank, pstart.astype(F32).reshape(N_EXPERTS, 1))


def _block_plan(counts, nblk):
    counts = counts.reshape(N_EXPERTS).astype(jnp.int32)
    padded = (counts + FFN_BLOCK - 1) // FFN_BLOCK * FFN_BLOCK
    pend = jnp.cumsum(padded).astype(jnp.int32)
    pstart = pend - padded
    first_row = jnp.arange(nblk, dtype=jnp.int32) * FFN_BLOCK
    block_e = jnp.minimum(jnp.sum(pend[None, :] <= first_row[:, None], axis=1), N_EXPERTS - 1).astype(jnp.int32)
    nused = (pend[-1:] // FFN_BLOCK).astype(jnp.int32)
    ids = jnp.arange(N_EXPERTS, dtype=jnp.int32)
    following = lax.cummin(jnp.where(counts > 0, ids, N_EXPERTS), reverse=True)
    following = jnp.concatenate([following[1:], jnp.full((1,), N_EXPERTS, jnp.int32)])
    following = jnp.where(following >= N_EXPERTS, -1, following)
    next_e = jnp.sum(jnp.where(block_e[:, None] == ids[None, :], following[None, :], 0), axis=1).astype(jnp.int32)
    return pstart, pstart + counts, pend, block_e, next_e, nused


def _tile_copy(src_ref, src_row, dst_ref, dst_row, sem):
    return pltpu.make_async_copy(src_ref.at[pl.ds(pl.multiple_of(src_row * TILE_ROWS, TILE_ROWS), TILE_ROWS), :],
                                 dst_ref.at[pl.ds(pl.multiple_of(dst_row * TILE_ROWS, TILE_ROWS), TILE_ROWS), :],
                                 sem)


def _scatter_kernel(cend_ref, pend_ref, dst_ref, h3_ref, z_ref, xs_hbm, st0, st1, sem, zsem):
    i = pl.program_id(0)
    nsteps = pl.num_programs(0)
    tm = dst_ref.shape[1]
    stage = (st0, st1)

    def zero_copy(r):
        return pltpu.make_async_copy(z_ref, xs_hbm.at[pl.ds(pl.multiple_of(r * TILE_ROWS, TILE_ROWS), TILE_ROWS), :],
                                     zsem)

    def wait_slot(slot):
        for _ in range(TOP_K):
            pltpu.make_async_copy(stage[slot], xs_hbm.at[pl.ds(0, tm * TILE_ROWS), :], sem.at[slot]).wait()

    @pl.when(i == 0)
    def _():
        def start_row(r, c):
            zero_copy(r).start()
            return c

        def wait_row(r, c):
            zero_copy(r).wait()
            return c

        def per_expert(e, carry):
            lax.fori_loop(cend_ref[e], pend_ref[e], start_row, carry)
            prev = jnp.maximum(e - 1, 0)
            return lax.fori_loop(cend_ref[prev], jnp.where(e > 0, pend_ref[prev], cend_ref[prev]), wait_row, carry)

        lax.fori_loop(0, N_EXPERTS, per_expert, 0)
        lax.fori_loop(cend_ref[N_EXPERTS - 1], pend_ref[N_EXPERTS - 1], wait_row, 0)

    for slot in range(2):
        @pl.when(i % 2 == slot)
        def _(slot=slot):
            @pl.when(i >= 2)
            def _():
                wait_slot(slot)
            stage[slot][...] = h3_ref[...]
            for k in range(TOP_K):
                def body(jj, carry, k=k):
                    for u in range(ISSUE_UNROLL):
                        j = jj * ISSUE_UNROLL + u
                        _tile_copy(stage[slot], j, xs_hbm, dst_ref[k, j], sem.at[slot]).start()
                    return carry
                lax.fori_loop(0, tm // ISSUE_UNROLL, body, 0)

    @pl.when(i == nsteps - 1)
    def _():
        wait_slot(0)
        wait_slot(1)


def _dispatch_rows(h3, dest, cend, pend, rows):
    _, n = dest.shape
    tm = SCATTER_TILE
    assert n // tm >= 2
    stage = pltpu.VMEM((tm * TILE_ROWS, LANES), U32)
    grid_spec = pltpu.PrefetchScalarGridSpec(
        num_scalar_prefetch=2,
        grid=(n // tm,),
        in_specs=[pl.BlockSpec((TOP_K, tm), lambda i, ce, pe: (0, i), memory_space=pltpu.SMEM),
                  pl.BlockSpec((tm * TILE_ROWS, LANES), lambda i, ce, pe: (i, 0)),
                  pl.BlockSpec((TILE_ROWS, LANES), lambda i, ce, pe: (0, 0))],
        out_specs=pl.BlockSpec(memory_space=pl.ANY),
        scratch_shapes=[stage, stage, pltpu.SemaphoreType.DMA((2,)), pltpu.SemaphoreType.DMA],
    )
    return pl.pallas_call(
        _scatter_kernel,
        grid_spec=grid_spec,
        out_shape=jax.ShapeDtypeStruct((rows * TILE_ROWS, LANES), U32),
        compiler_params=_params(("arbitrary",)),
        name="dispatch_scatter",
    )(cend, pend, dest, h3, jnp.zeros((TILE_ROWS, LANES), U32))


def _ffn_kernel(be_ref, nx_ref, nu_ref, x_ref, wg_hbm, wu_hbm, wd_hbm, y_ref,
                wgf, wuf, wdf, wgb, wub, wdb, sem, cur):
    first = pl.program_id(0) * FFN_PER_STEP
    nused = nu_ref[0]

    def fetch(e, slot):
        return [pltpu.make_async_copy(src.at[e], dst.at[slot], sem.at[slot])
                for src, dst in ((wg_hbm, wgf), (wu_hbm, wuf), (wd_hbm, wdf))]

    @pl.when(first == 0)
    def _():
        cur[0] = 1
        for c in fetch(be_ref[0], 0):
            c.start()

    slots = []
    for part in range(FFN_PER_STEP):
        blk = first + part

        @pl.when((blk < nused) & ((blk == 0) | (be_ref[blk] != be_ref[jnp.maximum(blk - 1, 0)])))
        def _(blk=blk):
            slot = 1 - cur[0]
            cur[0] = slot
            for c in fetch(be_ref[blk], slot):
                c.wait()
            wgb[slot] = wgf[slot].astype(BF16)
            wub[slot] = wuf[slot].astype(BF16)
            wdb[slot] = wdf[slot].astype(BF16)

            @pl.when(nx_ref[blk] >= 0)
            def _():
                for c in fetch(nx_ref[blk], 1 - slot):
                    c.start()

        slots.append(cur[0])

    @pl.when(first < nused)
    def _():
        for part, slot in enumerate(slots):
            base = part * FFN_BLOCK * TILE_ROWS
            x = jnp.concatenate([c.astype(BF16) for c in _from_row_tiles(x_ref, FFN_BLOCK, base)], axis=1)
            act = _silu(jnp.dot(x, wgb[slot], preferred_element_type=F32)) * jnp.dot(
                x, wub[slot], preferred_element_type=F32)
            _to_row_tiles(y_ref, jnp.dot(act.astype(BF16), wdb[slot], preferred_element_type=F32), base)

    @pl.when(first >= nused)
    def _():
        y_ref[...] = jnp.zeros(y_ref.shape, U32)


def _routed_experts(xs, block_e, next_e, nused, w_gate, w_up, w_down):
    rows = xs.shape[0] // TILE_ROWS
    nblk = rows // FFN_BLOCK
    _, d, hid = w_gate.shape
    step_rows = FFN_PER_STEP * FFN_BLOCK * TILE_ROWS
    hbm = pl.BlockSpec(memory_space=pl.ANY)
    grid_spec = pltpu.PrefetchScalarGridSpec(
        num_scalar_prefetch=3,
        grid=(nblk // FFN_PER_STEP,),
        in_specs=[pl.BlockSpec((step_rows, LANES),
                               lambda i, be, nx, nu: (jnp.minimum(i, (nu[0] - 1) // FFN_PER_STEP), 0)),
                  hbm, hbm, hbm],
        out_specs=pl.BlockSpec((step_rows, LANES), lambda i, be, nx, nu: (i, 0)),
        scratch_shapes=[pltpu.VMEM((2, d, hid), F32), pltpu.VMEM((2, d, hid), F32), pltpu.VMEM((2, hid, d), F32),
                        pltpu.VMEM((2, d, hid), BF16), pltpu.VMEM((2, d, hid), BF16), pltpu.VMEM((2, hid, d), BF16),
                        pltpu.SemaphoreType.DMA((2,)), pltpu.SMEM((1,), jnp.int32)],
    )
    return pl.pallas_call(
        _ffn_kernel,
        grid_spec=grid_spec,
        out_shape=jax.ShapeDtypeStruct((rows * TILE_ROWS, LANES), U32),
        compiler_params=_params(("arbitrary",)),
        name="routed_experts",
    )(block_e, next_e, nused, xs, w_gate, w_up, w_down)


def _gather_start(idx_ref, src_hbm, dst_vmem, sem):
    nk, tm = idx_ref.shape
    for k in range(nk):
        def body(jj, carry, k=k):
            for u in range(ISSUE_UNROLL):
                j = jj * ISSUE_UNROLL + u
                _tile_copy(src_hbm, idx_ref[k, j], dst_vmem, k * tm + j, sem).start()
            return carry
        lax.fori_loop(0, tm // ISSUE_UNROLL, body, 0)


def _gather_wait(src_hbm, dst_vmem, sem):
    pltpu.make_async_copy(src_hbm.at[pl.ds(0, dst_vmem.shape[0]), :], dst_vmem, sem).wait()


def _combine_kernel(dst_ref, dstn_ref, wt_ref, xp_ref, g2_ref, gfin_ref, ys_hbm, o_ref, yb0, yb1, sem, *, nsteps):
    i = pl.program_id(0)
    tm = COMBINE_TILE
    bufs = (yb0, yb1)

    @pl.when(i == 0)
    def _():
        _gather_start(dst_ref, ys_hbm, yb0, sem.at[0])

    for slot in range(2):
        @pl.when((i + 1 < nsteps) & ((i + 1) % 2 == slot))
        def _(slot=slot):
            _gather_start(dstn_ref, ys_hbm, bufs[slot], sem.at[slot])

    for slot in range(2):
        @pl.when(i % 2 == slot)
        def _(slot=slot):
            _gather_wait(ys_hbm, bufs[slot], sem.at[slot])
            wt = wt_ref[...]
            cols = None
            for k in range(TOP_K):
                part = [c * wt[:, k:k + 1] for c in _from_row_tiles(bufs[slot], tm, base=k * tm * TILE_ROWS)]
                cols = part if cols is None else [a + b for a, b in zip(cols, part)]
            routed = jnp.concatenate(cols, axis=1)
            o_ref[0] = _rms(xp_ref[0] + g2_ref[0] * routed) * gfin_ref[...]


def _combine(ys, dest, wts_t, xp, gate2, g_final):
    bsz, s, d = xp.shape
    tm = COMBINE_TILE
    nt = s // tm
    nsteps = bsz * nt
    vec = pl.BlockSpec((1, 1, d), lambda i: (i // nt, 0, 0))
    return pl.pallas_call(
        functools.partial(_combine_kernel, nsteps=nsteps),
        grid=(nsteps,),
        in_specs=[pl.BlockSpec((TOP_K, tm), lambda i: (0, i), memory_space=pltpu.SMEM),
                  pl.BlockSpec((TOP_K, tm), lambda i: (0, jnp.minimum(i + 1, nsteps - 1)),
                               memory_space=pltpu.SMEM),
                  pl.BlockSpec((tm, TOP_K), lambda i: (i, 0)),
                  pl.BlockSpec((1, tm, d), lambda i: (i // nt, i % nt, 0)),
                  vec,
                  pl.BlockSpec((1, d), lambda i: (0, 0)),
                  pl.BlockSpec(memory_space=pl.ANY)],
        out_specs=pl.BlockSpec((1, tm, d), lambda i: (i // nt, i % nt, 0)),
        out_shape=jax.ShapeDtypeStruct((bsz, s, d), F32),
        scratch_shapes=[pltpu.VMEM((tm * TOP_K * TILE_ROWS, LANES), U32),
                        pltpu.VMEM((tm * TOP_K * TILE_ROWS, LANES), U32),
                        pltpu.SemaphoreType.DMA((2,))],
        compiler_params=_params(("arbitrary",)),
        name="combine_final_norm",
    )(dest, dest, wts_t, xp, gate2, g_final.reshape(1, d), ys)


def kernel(x, c, w_in, b_forget, g_fox_out, g_dil_out, w_out, w_ada, b_ada, w_router, router_bias,
           w_exp_gate, w_exp_up, w_exp_down, w_sh_gate, w_sh_up, w_sh_down, rel_bias, g_final):
    bsz, s, d = x.shape
    depth = w_in.shape[0]
    assert depth == 1 and d == 2 * TILE_ROWS * LANES and s % (DIL_BLOCK * DILATIONS[-1]) == 0
    l = 0
    mod = _modulation(c, w_ada[l], b_ada[l])
    shift1, scale1, gate1, shift2, scale2, gate2 = [m[:, None, :] for m in jnp.split(mod, 6, axis=-1)]

    qscale = HEAD_DIM ** -0.5
    o3 = 3 * WIDTH
    w = w_in[l]
    wq_t = (w[:, :WIDTH] * (qscale * LOG2E)).T.astype(BF16)
    wk = w[:, WIDTH:2 * WIDTH].astype(BF16)
    wv_t = w[:, 2 * WIDTH:o3].T.astype(BF16)
    w_flog = jnp.pad(w[:, o3:o3 + N_HEADS], ((0, 0), (0, LANES - N_HEADS))).astype(BF16)
    wd0 = o3 + N_HEADS
    w_dil = jnp.concatenate([w[:, wd0:wd0 + WIDTH] * qscale, w[:, wd0 + WIDTH:]], axis=1).astype(BF16)

    q_t, k_f, v_t, qkv_d, flog = _in_projection(x, scale1, shift1, wq_t, wk, wv_t, w_dil, w_flog)
    y_fox = _fox_attention(q_t, k_f, _forget_cumsum(flog, b_forget[l]), v_t)
    y_dil = _dilated_attention(qkv_d, _dilated_bias(rel_bias))

    wr = w_router[l].T
    wr_hi = wr.astype(BF16)
    wr_lo = (wr - wr_hi.astype(F32)).astype(BF16)
    wo = w_out[l].astype(BF16)
    xp, h3, logits_t = _out_projection(
        y_fox, y_dil, x, gate1, scale2, shift2, gate2, g_fox_out[l].reshape(1, WIDTH),
        g_dil_out[l].reshape(1, WIDTH), wo[:WIDTH], wo[WIDTH:], wr_hi, wr_lo,
        w_sh_gate[l].astype(BF16), w_sh_up[l].astype(BF16), w_sh_down[l].astype(BF16))

    eidx, wts, rank, counts = _route(logits_t, router_bias[l])
    rows = bsz * s * TOP_K + N_EXPERTS * FFN_BLOCK
    pstart, cend, pend, block_e, next_e, nused = _block_plan(counts, rows // FFN_BLOCK)
    dest = _dest_rows(eidx, rank, pstart)
    xs = _dispatch_rows(h3, dest, cend, pend, rows)
    ys = _routed_experts(xs, block_e, next_e, nused, w_exp_gate[l], w_exp_up[l], w_exp_down[l])
    return _combine(ys, dest, wts.T, xp, gate2, g_final)
```

```python
import functools

import numpy as np
import jax
import jax.numpy as jnp
from jax import lax
from jax.experimental import pallas as pl
from jax.experimental.pallas import tpu as pltpu

F32 = jnp.float32
BF16 = jnp.bfloat16

HEAD_DIM = 64
N_HEADS = 8
WIDTH = N_HEADS * HEAD_DIM
LANES = 128
TILE_ROWS = 4
U32 = jnp.uint32
DIL_UNROLL = 16
HEAD_PAIRS = WIDTH // LANES
DIL_BLOCK = 128
DILATIONS = (1, 4, 16)
T5_NUM_BUCKETS = 32
T5_MAX_DISTANCE = 2048
N_EXPERTS = 256
TOP_K = 8
N_GROUPS = 8
TOP_K_GROUPS = 4
ROUTED_SCALE = 2.5
EPS = 1e-6
NEG = -1e30
LOG2E = float(np.log2(np.e))
VMEM_LIMIT = 56 * 1024 * 1024

ROW_TILE = 512
FOX_TILE = 512
FFN_BLOCK = 256
FFN_PER_STEP = 4
COMBINE_TILE = 256
SCATTER_TILE = 256
ROUTE_TILE = 512
ISSUE_UNROLL = 8


def _params(semantics):
    return pltpu.CompilerParams(dimension_semantics=semantics, vmem_limit_bytes=VMEM_LIMIT)


def _rms(x):
    return x * lax.rsqrt(jnp.mean(x * x, axis=-1, keepdims=True) + EPS)


def _silu(x):
    return x * jax.nn.sigmoid(x)


def _dot_nt(a, b):
    return lax.dot_general(a, b, (((1,), (1,)), ((), ())), preferred_element_type=F32)


def _mod_kernel(c_ref, w_ref, b_ref, o_ref):
    o_ref[...] = jnp.dot(_silu(c_ref[...]), w_ref[...], precision=lax.Precision.HIGHEST,
                         preferred_element_type=F32) + b_ref[...]


def _modulation(c, w_ada, b_ada):
    bsz, d = c.shape
    n = w_ada.shape[1]
    tn = 1536
    return pl.pallas_call(
        _mod_kernel,
        grid=(n // tn,),
        in_specs=[pl.BlockSpec((bsz, d), lambda j: (0, 0)),
                  pl.BlockSpec((d, tn), lambda j: (0, j)),
                  pl.BlockSpec((1, tn), lambda j: (0, j))],
        out_specs=pl.BlockSpec((bsz, tn), lambda j: (0, j)),
        out_shape=jax.ShapeDtypeStruct((bsz, n), F32),
        compiler_params=_params(("arbitrary",)),
        name="adaln_mod",
    )(c, w_ada, b_ada.reshape(1, n))


def _inproj_kernel(x_ref, sc_ref, sh_ref, wq_ref, wk_ref, wv_ref, wd_ref, wl_ref,
                   oq_ref, ok_ref, ov_ref, od_ref, ol_ref):
    h = _rms(x_ref[0]) * (1.0 + sc_ref[0]) + sh_ref[0]
    hb = h.astype(BF16)
    oq_ref[0] = _dot_nt(wq_ref[...], hb).astype(BF16)
    ov_ref[0] = _dot_nt(wv_ref[...], hb).astype(BF16)
    ok_ref[0] = jnp.dot(hb, wk_ref[...], preferred_element_type=F32).astype(BF16)
    od_ref[0] = jnp.dot(hb, wd_ref[...], preferred_element_type=F32).astype(BF16)
    ol_ref[0] = jnp.dot(hb, wl_ref[...], preferred_element_type=F32)


def _in_projection(x, scale1, shift1, wq_t, wk, wv_t, w_dil, w_flog):
    bsz, s, d = x.shape
    tm = ROW_TILE
    vec = pl.BlockSpec((1, 1, d), lambda b, i: (b, 0, 0))
    full = lambda w: pl.BlockSpec(w.shape, lambda b, i: (0, 0))
    row = lambda n: pl.BlockSpec((1, tm, n), lambda b, i: (b, i, 0))
    col = pl.BlockSpec((1, WIDTH, tm), lambda b, i: (b, 0, i))
    return pl.pallas_call(
        _inproj_kernel,
        grid=(bsz, s // tm),
        in_specs=[row(d), vec, vec, full(wq_t), full(wk), full(wv_t), full(w_dil), full(w_flog)],
        out_specs=[col, row(WIDTH), col, row(3 * WIDTH), row(LANES)],
        out_shape=[jax.ShapeDtypeStruct((bsz, WIDTH, s), BF16),
                   jax.ShapeDtypeStruct((bsz, s, WIDTH), BF16),
                   jax.ShapeDtypeStruct((bsz, WIDTH, s), BF16),
                   jax.ShapeDtypeStruct((bsz, s, 3 * WIDTH), BF16),
                   jax.ShapeDtypeStruct((bsz, s, LANES), F32)],
        compiler_params=_params(("parallel", "arbitrary")),
        name="norm1_inproj",
    )(x, scale1, shift1, wq_t, wk, wv_t, w_dil, w_flog)


FORGET_PARTS = 3


def _split_bf16(x):
    parts = []
    for _ in range(FORGET_PARTS):
        p = x.astype(BF16)
        parts.append(p)
        x = x - p.astype(F32)
    return parts


def _forget_kernel(f_ref, b_ref, tri_ref, place_ref, o_ref):
    blk = tri_ref.shape[0]
    carry = jnp.zeros((1, LANES), F32)
    for c in range(f_ref.shape[1] // blk):
        rows = pl.ds(c * blk, blk)
        z = f_ref[0, rows, :] + b_ref[...]
        ls = jnp.minimum(z, 0.0) - jnp.log1p(jnp.exp(-jnp.abs(z)))
        cum = carry
        for p in _split_bf16(ls):
            cum = cum + jnp.dot(tri_ref[...], p, preferred_element_type=F32)
        carry = cum[blk - 1:blk, :]
        out = None
        for n, p in enumerate(_split_bf16(cum * LOG2E)):
            d = jnp.dot(p, place_ref[n], preferred_element_type=F32)
            out = d if out is None else out + d
        o_ref[0, rows, :] = out.astype(BF16)


def _forget_cumsum(flog, b_forget):
    bsz, s, _ = flog.shape
    blk = 512
    place = np.zeros((FORGET_PARTS, LANES, WIDTH), np.float32)
    for h in range(N_HEADS):
        for n in range(FORGET_PARTS):
            place[n, h, (h // 2) * LANES + (h % 2) * HEAD_DIM + n] = 1.0
    return pl.pallas_call(
        _forget_kernel,
        grid=(bsz,),
        in_specs=[pl.BlockSpec((1, s, LANES), lambda b: (b, 0, 0)),
                  pl.BlockSpec((1, LANES), lambda b: (0, 0)),
                  pl.BlockSpec((blk, blk), lambda b: (0, 0)),
                  pl.BlockSpec(place.shape, lambda b: (0, 0, 0))],
        out_specs=pl.BlockSpec((1, s, WIDTH), lambda b: (b, 0, 0)),
        out_shape=jax.ShapeDtypeStruct((bsz, s, WIDTH), BF16),
        compiler_params=_params(("parallel",)),
        name="forget_cumsum",
    )(flog, jnp.pad(b_forget, (0, LANES - N_HEADS)).reshape(1, LANES),
      jnp.tril(jnp.ones((blk, blk), BF16)), jnp.asarray(place, BF16))


def _stack_heads(q):
    lane = lax.broadcasted_iota(jnp.int32, q.shape, 1)
    zero = jnp.zeros_like(q)
    return jnp.concatenate([jnp.where(lane < HEAD_DIM, q, zero), jnp.where(lane >= HEAD_DIM, q, zero)], axis=0)


def _unstack_heads(o2):
    rows = o2.shape[0] // 2
    lane = lax.broadcasted_iota(jnp.int32, (rows, LANES), 1)
    return jnp.where(lane < HEAD_DIM, o2[:rows], o2[rows:])


def _fox_kernel(q_ref, k_ref, fa_ref, v_ref, o_ref, m_s, l_s, acc_s, s_a, s_b, *, t):
    i = pl.program_id(2)
    q_t = q_ref[0]
    row = lax.broadcasted_iota(jnp.int32, (LANES, 2 * t), 0)
    col = lax.broadcasted_iota(jnp.int32, (LANES, 2 * t), 1)
    head_row = jnp.where(col < t, 0, HEAD_DIM)
    q2 = jnp.concatenate([q_t, q_t], axis=1)
    q2 = jnp.where((row >= head_row) & (row < head_row + HEAD_DIM), q2, jnp.zeros_like(q2))
    minus = jnp.where((row >= head_row) & (row < head_row + FORGET_PARTS), -1.0, 0.0).astype(BF16)
    qa = jnp.concatenate([q2, minus], axis=0)
    m_s[...] = jnp.full(m_s.shape, NEG, F32)
    l_s[...] = jnp.zeros(l_s.shape, F32)
    acc_s[...] = jnp.zeros(acc_s.shape, F32)

    def scores(ks, tk):
        kk = jnp.concatenate([k_ref[0, pl.ds(ks, tk), :], fa_ref[0, pl.ds(ks, tk), :]], axis=1)
        return jnp.dot(kk, qa, preferred_element_type=F32)

    def absorb(s, ks, tk):
        m_prev = m_s[...]
        m_new = jnp.maximum(m_prev, jnp.max(s, axis=0, keepdims=True))
        alpha = jnp.exp2(m_prev - m_new)
        p = jnp.exp2(s - m_new)
        l_s[...] = alpha * l_s[...] + jnp.sum(p, axis=0, keepdims=True)
        acc_s[...] = alpha * acc_s[...] + jnp.dot(v_ref[0, :, pl.ds(ks, tk)], p.astype(BF16),
                                                  preferred_element_type=F32)
        m_s[...] = m_new

    npair = i // 2

    def tile(j):
        return pl.multiple_of(j * 2 * t, 2 * t)

    @pl.when(npair > 0)
    def _():
        s_a[...] = scores(0, 2 * t)

        def body(jj, carry):
            j = 2 * jj
            s_b[...] = scores(tile(j + 1), 2 * t)
            absorb(s_a[...], tile(j), 2 * t)
            s_a[...] = scores(tile(jnp.minimum(j + 2, npair - 1)), 2 * t)
            absorb(s_b[...], tile(j + 1), 2 * t)
            return carry

        lax.fori_loop(0, npair // 2, body, 0)

        @pl.when(npair % 2 == 1)
        def _():
            absorb(s_a[...], tile(npair - 1), 2 * t)

    def last(ks, tk):
        r = lax.broadcasted_iota(jnp.int32, (tk, 2 * t), 0)
        c = lax.broadcasted_iota(jnp.int32, (tk, 2 * t), 1)
        absorb(jnp.where(r <= jnp.where(c >= t, c - t, c) + (tk - t), scores(ks, tk), NEG), ks, tk)

    @pl.when(i % 2 == 1)
    def _():
        last(tile(npair), 2 * t)

    @pl.when(i % 2 == 0)
    def _():
        last(pl.multiple_of(i * t, t), t)

    o2 = acc_s[...] / l_s[...]
    o_t = jnp.where(lax.broadcasted_iota(jnp.int32, (LANES, t), 0) < HEAD_DIM, o2[:, :t], o2[:, t:])
    o_ref[0] = o_t.T


def _fox_attention(q_t, k, f_aug, v_t):
    bsz, s, _ = k.shape
    t = FOX_TILE
    keys = pl.BlockSpec((1, s, LANES), lambda b, h, i: (b, 0, h))
    return pl.pallas_call(
        functools.partial(_fox_kernel, t=t),
        grid=(bsz, HEAD_PAIRS, s // t),
        in_specs=[pl.BlockSpec((1, LANES, t), lambda b, h, i: (b, h, i)),
                  keys, keys,
                  pl.BlockSpec((1, LANES, s), lambda b, h, i: (b, h, 0))],
        out_specs=pl.BlockSpec((1, t, LANES), lambda b, h, i: (b, i, h)),
        out_shape=jax.ShapeDtypeStruct((bsz, s, WIDTH), F32),
        scratch_shapes=[pltpu.VMEM((1, 2 * t), F32), pltpu.VMEM((1, 2 * t), F32),
                        pltpu.VMEM((LANES, 2 * t), F32), pltpu.VMEM((2 * t, 2 * t), F32),
                        pltpu.VMEM((2 * t, 2 * t), F32)],
        compiler_params=_params(("parallel", "parallel", "arbitrary")),
        name="fox_attention",
    )(q_t, k, f_aug, v_t)


def _t5_bucket(dist):
    max_exact = T5_NUM_BUCKETS // 2
    d = np.maximum(dist, 1).astype(np.float32)
    large = max_exact + (np.log(d / max_exact) / np.log(T5_MAX_DISTANCE / max_exact)
                         * (T5_NUM_BUCKETS - max_exact)).astype(np.int32)
    large = np.minimum(large, T5_NUM_BUCKETS - 1)
    return np.where(dist < max_exact, dist, large).astype(np.int32)


def _dilated_bias(rel_bias):
    blk = DIL_BLOCK
    period = 3 * blk
    m = np.arange(period)
    rel = np.where(m < 2 * blk, blk - m, blk - (m - period))
    band = (rel >= 0) & (rel <= blk)
    onehot = np.zeros((len(DILATIONS), period, T5_NUM_BUCKETS), np.float32)
    for bi, dil in enumerate(DILATIONS):
        onehot[bi, m, _t5_bucket(np.clip(rel, 0, blk) * dil)] = 1.0
    w = jnp.einsum('bmk,kh->bhm', jnp.asarray(onehot), rel_bias.astype(F32),
                   precision=lax.Precision.HIGHEST)
    w = jnp.where(jnp.asarray(band), w, NEG)
    flat = jnp.tile(w, (1, 1, blk))[:, :, :blk * (period - 1)]
    table = flat.reshape(len(DILATIONS), N_HEADS, blk, period - 1)[..., :2 * blk]
    table = table.reshape(len(DILATIONS), HEAD_PAIRS, 2 * blk, 2 * blk)
    first = jnp.where(jnp.arange(2 * blk) < blk, NEG, table)
    return jnp.stack([table, first], axis=1)


def _dil_kernel(q_ref, k_ref, v_ref, bias_ref, o_ref, qf, kf, vf, ob0, ob1, ob2, ls0, ls1, ls2, *, s_len):
    blk = DIL_BLOCK
    qf[...] = q_ref[0].astype(F32)
    kf[...] = k_ref[0].astype(F32)
    vf[...] = v_ref[0].astype(F32)
    for bi, (dil, ob, ls) in enumerate(zip(DILATIONS, (ob0, ob1, ob2), (ls0, ls1, ls2))):
        span = blk * dil
        nb = s_len // span

        def rows(start, dil=dil):
            return pl.ds(start, blk) if dil == 1 else pl.ds(start, blk, stride=dil)

        def block(tix, carry, bi=bi, span=span, nb=nb, ob=ob, ls=ls, rows=rows):
            n = tix % nb
            start = n * span + tix // nb
            prev = jnp.maximum(start - span, 0)
            q2 = _stack_heads(qf[rows(start), :].astype(BF16))
            kk = jnp.concatenate([kf[rows(prev), :], kf[rows(start), :]], axis=0).astype(BF16)
            vv = jnp.concatenate([vf[rows(prev), :], vf[rows(start), :]], axis=0).astype(BF16)
            s = lax.dot_general(q2, kk, (((1,), (1,)), ((), ())), preferred_element_type=F32)
            s = s + bias_ref[bi, jnp.where(n == 0, 1, 0), 0]
            m = jnp.max(s, axis=1, keepdims=True)
            p = jnp.exp(s - m)
            l = jnp.sum(p, axis=1, keepdims=True)
            o2 = jnp.dot(p.astype(BF16), vv, preferred_element_type=F32) / l
            lse = jnp.broadcast_to(m + jnp.log(l), (2 * blk, LANES))
            ob[rows(start), :] = _unstack_heads(o2)
            ls[rows(start), :] = _unstack_heads(lse)
            return carry

        def blocks(g, carry, block=block):
            for u in range(DIL_UNROLL):
                block(g * DIL_UNROLL + u, carry)
            return carry

        lax.fori_loop(0, dil * nb // DIL_UNROLL, blocks, 0)

    chunk = 512
    for c in range(s_len // chunk):
        r = pl.ds(c * chunk, chunk)
        l0, l1, l2 = ls0[r, :], ls1[r, :], ls2[r, :]
        mx = jnp.maximum(jnp.maximum(l0, l1), l2)
        e0, e1, e2 = jnp.exp(l0 - mx), jnp.exp(l1 - mx), jnp.exp(l2 - mx)
        o_ref[0, r, :] = (e0 * ob0[r, :] + e1 * ob1[r, :] + e2 * ob2[r, :]) / (e0 + e1 + e2)


def _dilated_attention(qkv, bias):
    bsz, s, _ = qkv.shape
    col = lambda off: pl.BlockSpec((1, s, LANES), lambda b, h: (b, 0, off + h))
    buf = pltpu.VMEM((s, LANES), F32)
    return pl.pallas_call(
        functools.partial(_dil_kernel, s_len=s),
        grid=(bsz, HEAD_PAIRS),
        in_specs=[col(0), col(HEAD_PAIRS), col(2 * HEAD_PAIRS),
                  pl.BlockSpec((len(DILATIONS), 2, 1, 2 * DIL_BLOCK, 2 * DIL_BLOCK), lambda b, h: (0, 0, h, 0, 0))],
        out_specs=pl.BlockSpec((1, s, LANES), lambda b, h: (b, 0, h)),
        out_shape=jax.ShapeDtypeStruct((bsz, s, WIDTH), F32),
        scratch_shapes=[buf] * 9,
        compiler_params=_params(("parallel", "arbitrary")),
        name="dilated_attention",
    )(qkv, qkv, qkv, bias)


def _bf16_bits(x):
    return pltpu.bitcast(x.astype(BF16).astype(F32), U32)


def _to_row_tiles(dst_ref, x, base=0):
    rows, d = x.shape
    for c in range(TILE_ROWS):
        lo = _bf16_bits(x[:, c * LANES:(c + 1) * LANES]) >> 16
        hi = _bf16_bits(x[:, d // 2 + c * LANES:d // 2 + (c + 1) * LANES])
        dst_ref[pl.ds(base + c, rows, stride=TILE_ROWS), :] = lo | hi


def _from_row_tiles(src_ref, rows, base=0):
    lo, hi = [], []
    for c in range(TILE_ROWS):
        w = src_ref[pl.ds(base + c, rows, stride=TILE_ROWS), :]
        lo.append(pltpu.bitcast(w << 16, F32))
        hi.append(pltpu.bitcast(w & jnp.uint32(0xFFFF0000), F32))
    return lo + hi


def _outproj_kernel(yf_ref, yd_ref, x_ref, g1_ref, sc_ref, sh_ref, g2_ref, gf_ref, gd_ref, wo1_ref, wo2_ref,
                    wrh_ref, wrl_ref, wsg_ref, wsu_ref, wsd_ref, xp_ref, h3_ref, lg_ref):
    nf = (_rms(yf_ref[0]) * gf_ref[...]).astype(BF16)
    nd = (_rms(yd_ref[0]) * gd_ref[...]).astype(BF16)
    mix = (jnp.dot(nf, wo1_ref[...], preferred_element_type=F32)
           + jnp.dot(nd, wo2_ref[...], preferred_element_type=F32))
    x1 = x_ref[0] + g1_ref[0] * mix
    h2 = _rms(x1) * (1.0 + sc_ref[0]) + sh_ref[0]
    hb = h2.astype(BF16)
    hl = (h2 - hb.astype(F32)).astype(BF16)
    lg_ref[...] = _dot_nt(wrh_ref[...], hb) + _dot_nt(wrh_ref[...], hl) + _dot_nt(wrl_ref[...], hb)
    act = _silu(jnp.dot(hb, wsg_ref[...], preferred_element_type=F32)) * jnp.dot(
        hb, wsu_ref[...], preferred_element_type=F32)
    shared = jnp.dot(act.astype(BF16), wsd_ref[...], preferred_element_type=F32)
    xp_ref[0] = x1 + g2_ref[0] * shared
    _to_row_tiles(h3_ref, h2)


def _out_projection(y_fox, y_dil, x, gate1, scale2, shift2, gate2, g_fox, g_dil, wo1, wo2, wr_hi, wr_lo,
                    wsg, wsu, wsd):
    bsz, s, d = x.shape
    tm = ROW_TILE
    nt = s // tm
    vec = pl.BlockSpec((1, 1, d), lambda b, i: (b, 0, 0))
    full = lambda w: pl.BlockSpec(w.shape, lambda b, i: (0,) * w.ndim)
    row = lambda n: pl.BlockSpec((1, tm, n), lambda b, i: (b, i, 0))
    return pl.pallas_call(
        _outproj_kernel,
        grid=(bsz, nt),
        in_specs=[row(WIDTH), row(WIDTH), row(d), vec, vec, vec, vec, full(g_fox), full(g_dil), full(wo1),
                  full(wo2), full(wr_hi), full(wr_lo), full(wsg), full(wsu), full(wsd)],
        out_specs=[row(d),
                   pl.BlockSpec((tm * TILE_ROWS, LANES), lambda b, i: (b * nt + i, 0)),
                   pl.BlockSpec((N_EXPERTS, tm), lambda b, i: (0, b * nt + i))],
        out_shape=[jax.ShapeDtypeStruct((bsz, s, d), F32),
                   jax.ShapeDtypeStruct((bsz * s * TILE_ROWS, LANES), U32),
                   jax.ShapeDtypeStruct((N_EXPERTS, bsz * s), F32)],
        compiler_params=_params(("parallel", "arbitrary")),
        name="outproj_norm2_router_shared",
    )(y_fox, y_dil, x, gate1, scale2, shift2, gate2, g_fox, g_dil, wo1, wo2, wr_hi, wr_lo, wsg, wsu, wsd)


def _first_argmax(v, row, size):
    m = jnp.max(v, axis=0, keepdims=True)
    return m, jnp.min(jnp.where(v == m, row, size), axis=0, keepdims=True)


def _route_kernel(lg_ref, rb_ref, tri_ref, e_ref, w_ref, rk_ref, cnt_ref, cnt_s):
    @pl.when(pl.program_id(0) == 0)
    def _():
        cnt_s[...] = jnp.zeros(cnt_s.shape, F32)

    t = lg_ref.shape[1]
    gsz = N_EXPERTS // N_GROUPS
    scores = jax.nn.sigmoid(lg_ref[...])
    sel = scores + rb_ref[...]
    row_g = lax.broadcasted_iota(jnp.int32, (gsz, t), 0)
    grp = []
    for g in range(N_GROUPS):
        v = sel[g * gsz:(g + 1) * gsz]
        m1, i1 = _first_argmax(v, row_g, gsz)
        m2 = jnp.max(jnp.where(row_g == i1, -jnp.inf, v), axis=0, keepdims=True)
        grp.append(m1 + m2)
    gv = jnp.concatenate(grp, axis=0)
    row8 = lax.broadcasted_iota(jnp.int32, (N_GROUPS, t), 0)
    pen = jnp.full((N_GROUPS, t), -jnp.inf, F32)
    for _ in range(TOP_K_GROUPS):
        _, ix = _first_argmax(gv, row8, N_GROUPS)
        pen = jnp.where(row8 == ix, 0.0, pen)
        gv = jnp.where(row8 == ix, -jnp.inf, gv)
    selm = jnp.concatenate([sel[g * gsz:(g + 1) * gsz] + pen[g:g + 1] for g in range(N_GROUPS)], axis=0)

    row = lax.broadcasted_iota(jnp.int32, (N_EXPERTS, t), 0)
    v = selm
    idxs, scs = [], []
    for _ in range(TOP_K):
        _, ix = _first_argmax(v, row, N_EXPERTS)
        hit = row == ix
        idxs.append(ix)
        scs.append(jnp.sum(jnp.where(hit, scores, 0.0), axis=0, keepdims=True))
        v = jnp.where(hit, -jnp.inf, v)
    chosen = jnp.where(v != selm, 1.0, 0.0)
    before = jnp.dot(chosen.astype(BF16), tri_ref[...], preferred_element_type=F32) + cnt_s[...]
    rks = [jnp.sum(jnp.where(row == ix, before, 0.0), axis=0, keepdims=True) for ix in idxs]
    sc = jnp.concatenate(scs, axis=0)
    e_ref[...] = jnp.concatenate(idxs, axis=0)
    w_ref[...] = sc / jnp.sum(sc, axis=0, keepdims=True) * ROUTED_SCALE
    rk_ref[...] = jnp.concatenate(rks, axis=0).astype(jnp.int32)
    cnt_s[...] = cnt_s[...] + jnp.sum(chosen, axis=1, keepdims=True)
    cnt_ref[...] = cnt_s[...]


def _route(logits_t, router_bias):
    e, n = logits_t.shape
    t = ROUTE_TILE
    tri = jnp.triu(jnp.ones((t, t), BF16), k=1)
    tile = pl.BlockSpec((TOP_K, t), lambda i: (0, i))
    return pl.pallas_call(
        _route_kernel,
        grid=(n // t,),
        in_specs=[pl.BlockSpec((e, t), lambda i: (0, i)),
                  pl.BlockSpec((e, 1), lambda i: (0, 0)),
                  pl.BlockSpec((t, t), lambda i: (0, 0))],
        out_specs=[tile, tile, tile, pl.BlockSpec((e, 1), lambda i: (0, 0))],
        out_shape=[jax.ShapeDtypeStruct((TOP_K, n), jnp.int32), jax.ShapeDtypeStruct((TOP_K, n), F32),
                   jax.ShapeDtypeStruct((TOP_K, n), jnp.int32), jax.ShapeDtypeStruct((e, 1), F32)],
        scratch_shapes=[pltpu.VMEM((e, 1), F32)],
        compiler_params=_params(("arbitrary",)),
        name="route_topk_rank",
    )(logits_t, router_bias.reshape(e, 1).astype(F32), tri)


def _dest_kernel(e_ref, rk_ref, ps_ref, d_ref):
    t = e_ref.shape[1]
    row = lax.broadcasted_iota(jnp.int32, (N_EXPERTS, t), 0)
    ps = ps_ref[...]
    base = [jnp.sum(jnp.where(row == e_ref[k:k + 1, :], ps, 0.0), axis=0, keepdims=True) for k in range(TOP_K)]
    d_ref[...] = jnp.concatenate(base, axis=0).astype(jnp.int32) + rk_ref[...]


def _dest_rows(eidx, rank, pstart):
    _, n = eidx.shape
    t = ROUTE_TILE
    tile = pl.BlockSpec((TOP_K, t), lambda i: (0, i))
    return pl.pallas_call(
        _dest_kernel,
        grid=(n // t,),
        in_specs=[tile, tile, pl.BlockSpec((N_EXPERTS, 1), lambda i: (0, 0))],
        out_specs=tile,
        out_shape=jax.ShapeDtypeStruct((TOP_K, n), jnp.int32),
        compiler_params=_params(("parallel",)),
        name="dest_rows",
    )(eidx, rank, pstart.astype(F32).reshape(N_EXPERTS, 1))


def _block_plan(counts, nblk):
    counts = counts.reshape(N_EXPERTS).astype(jnp.int32)
    padded = (counts + FFN_BLOCK - 1) // FFN_BLOCK * FFN_BLOCK
    pend = jnp.cumsum(padded).astype(jnp.int32)
    pstart = pend - padded
    first_row = jnp.arange(nblk, dtype=jnp.int32) * FFN_BLOCK
    block_e = jnp.minimum(jnp.sum(pend[None, :] <= first_row[:, None], axis=1), N_EXPERTS - 1).astype(jnp.int32)
    nused = (pend[-1:] // FFN_BLOCK).astype(jnp.int32)
    ids = jnp.arange(N_EXPERTS, dtype=jnp.int32)
    following = lax.cummin(jnp.where(counts > 0, ids, N_EXPERTS), reverse=True)
    following = jnp.concatenate([following[1:], jnp.full((1,), N_EXPERTS, jnp.int32)])
    following = jnp.where(following >= N_EXPERTS, -1, following)
    next_e = jnp.sum(jnp.where(block_e[:, None] == ids[None, :], following[None, :], 0), axis=1).astype(jnp.int32)
    return pstart, pstart + counts, pend, block_e, next_e, nused


def _tile_copy(src_ref, src_row, dst_ref, dst_row, sem):
    return pltpu.make_async_copy(src_ref.at[pl.ds(pl.multiple_of(src_row * TILE_ROWS, TILE_ROWS), TILE_ROWS), :],
                                 dst_ref.at[pl.ds(pl.multiple_of(dst_row * TILE_ROWS, TILE_ROWS), TILE_ROWS), :],
                                 sem)


def _scatter_kernel(cend_ref, pend_ref, dst_ref, h3_ref, z_ref, xs_hbm, st0, st1, sem, zsem):
    i = pl.program_id(0)
    nsteps = pl.num_programs(0)
    tm = dst_ref.shape[1]
    stage = (st0, st1)

    def zero_copy(r):
        return pltpu.make_async_copy(z_ref, xs_hbm.at[pl.ds(pl.multiple_of(r * TILE_ROWS, TILE_ROWS), TILE_ROWS), :],
                                     zsem)

    def wait_slot(slot):
        for _ in range(TOP_K):
            pltpu.make_async_copy(stage[slot], xs_hbm.at[pl.ds(0, tm * TILE_ROWS), :], sem.at[slot]).wait()

    @pl.when(i == 0)
    def _():
        def start_row(r, c):
            zero_copy(r).start()
            return c

        def wait_row(r, c):
            zero_copy(r).wait()
            return c

        def per_expert(e, carry):
            lax.fori_loop(cend_ref[e], pend_ref[e], start_row, carry)
            prev = jnp.maximum(e - 1, 0)
            return lax.fori_loop(cend_ref[prev], jnp.where(e > 0, pend_ref[prev], cend_ref[prev]), wait_row, carry)

        lax.fori_loop(0, N_EXPERTS, per_expert, 0)
        lax.fori_loop(cend_ref[N_EXPERTS - 1], pend_ref[N_EXPERTS - 1], wait_row, 0)

    for slot in range(2):
        @pl.when(i % 2 == slot)
        def _(slot=slot):
            @pl.when(i >= 2)
            def _():
                wait_slot(slot)
            stage[slot][...] = h3_ref[...]
            for k in range(TOP_K):
                def body(jj, carry, k=k):
                    for u in range(ISSUE_UNROLL):
                        j = jj * ISSUE_UNROLL + u
                        _tile_copy(stage[slot], j, xs_hbm, dst_ref[k, j], sem.at[slot]).start()
                    return carry
                lax.fori_loop(0, tm // ISSUE_UNROLL, body, 0)

    @pl.when(i == nsteps - 1)
    def _():
        wait_slot(0)
        wait_slot(1)


def _dispatch_rows(h3, dest, cend, pend, rows):
    _, n = dest.shape
    tm = SCATTER_TILE
    assert n // tm >= 2
    stage = pltpu.VMEM((tm * TILE_ROWS, LANES), U32)
    grid_spec = pltpu.PrefetchScalarGridSpec(
        num_scalar_prefetch=2,
        grid=(n // tm,),
        in_specs=[pl.BlockSpec((TOP_K, tm), lambda i, ce, pe: (0, i), memory_space=pltpu.SMEM),
                  pl.BlockSpec((tm * TILE_ROWS, LANES), lambda i, ce, pe: (i, 0)),
                  pl.BlockSpec((TILE_ROWS, LANES), lambda i, ce, pe: (0, 0))],
        out_specs=pl.BlockSpec(memory_space=pl.ANY),
        scratch_shapes=[stage, stage, pltpu.SemaphoreType.DMA((2,)), pltpu.SemaphoreType.DMA],
    )
    return pl.pallas_call(
        _scatter_kernel,
        grid_spec=grid_spec,
        out_shape=jax.ShapeDtypeStruct((rows * TILE_ROWS, LANES), U32),
        compiler_params=_params(("arbitrary",)),
        name="dispatch_scatter",
    )(cend, pend, dest, h3, jnp.zeros((TILE_ROWS, LANES), U32))


def _ffn_kernel(be_ref, nx_ref, nu_ref, x_ref, wg_hbm, wu_hbm, wd_hbm, y_ref,
                wgf, wuf, wdf, wgb, wub, wdb, sem, run):
    first = pl.program_id(0) * FFN_PER_STEP
    nused = nu_ref[0]

    def fetch(e, slot):
        return [pltpu.make_async_copy(src.at[e], dst.at[slot], sem.at[slot])
                for src, dst in ((wg_hbm, wgf), (wu_hbm, wuf), (wd_hbm, wdf))]

    @pl.when(first == 0)
    def _():
        run[0] = -1
        for c in fetch(be_ref[0], 0):
            c.start()

    slots = []
    for part in range(FFN_PER_STEP):
        blk = first + part

        @pl.when((blk < nused) & ((blk == 0) | (be_ref[blk] != be_ref[jnp.maximum(blk - 1, 0)])))
        def _(blk=blk):
            r = run[0] + 1
            run[0] = r
            fslot, slot = r % 2, r % FFN_PER_STEP
            for c in fetch(be_ref[blk], fslot):
                c.wait()
            wgb[slot] = wgf[fslot].astype(BF16)
            wub[slot] = wuf[fslot].astype(BF16)
            wdb[slot] = wdf[fslot].astype(BF16)

            @pl.when(nx_ref[blk] >= 0)
            def _():
                for c in fetch(nx_ref[blk], 1 - fslot):
                    c.start()

        slots.append(run[0] % FFN_PER_STEP)

    @pl.when(first < nused)
    def _():
        for part, slot in enumerate(slots):
            base = part * FFN_BLOCK * TILE_ROWS
            x = jnp.concatenate([c.astype(BF16) for c in _from_row_tiles(x_ref, FFN_BLOCK, base)], axis=1)
            act = _silu(jnp.dot(x, wgb[slot], preferred_element_type=F32)) * jnp.dot(
                x, wub[slot], preferred_element_type=F32)
            _to_row_tiles(y_ref, jnp.dot(act.astype(BF16), wdb[slot], preferred_element_type=F32), base)

    @pl.when(first >= nused)
    def _():
        y_ref[...] = jnp.zeros(y_ref.shape, U32)


def _routed_experts(xs, block_e, next_e, nused, w_gate, w_up, w_down):
    rows = xs.shape[0] // TILE_ROWS
    nblk = rows // FFN_BLOCK
    _, d, hid = w_gate.shape
    step_rows = FFN_PER_STEP * FFN_BLOCK * TILE_ROWS
    hbm = pl.BlockSpec(memory_space=pl.ANY)
    grid_spec = pltpu.PrefetchScalarGridSpec(
        num_scalar_prefetch=3,
        grid=(nblk // FFN_PER_STEP,),
        in_specs=[pl.BlockSpec((step_rows, LANES),
                               lambda i, be, nx, nu: (jnp.minimum(i, (nu[0] - 1) // FFN_PER_STEP), 0)),
                  hbm, hbm, hbm],
        out_specs=pl.BlockSpec((step_rows, LANES), lambda i, be, nx, nu: (i, 0)),
        scratch_shapes=[pltpu.VMEM((2, d, hid), F32), pltpu.VMEM((2, d, hid), F32), pltpu.VMEM((2, hid, d), F32),
                        pltpu.VMEM((FFN_PER_STEP, d, hid), BF16), pltpu.VMEM((FFN_PER_STEP, d, hid), BF16),
                        pltpu.VMEM((FFN_PER_STEP, hid, d), BF16),
                        pltpu.SemaphoreType.DMA((2,)), pltpu.SMEM((1,), jnp.int32)],
    )
    return pl.pallas_call(
        _ffn_kernel,
        grid_spec=grid_spec,
        out_shape=jax.ShapeDtypeStruct((rows * TILE_ROWS, LANES), U32),
        compiler_params=_params(("arbitrary",)),
        name="routed_experts",
    )(block_e, next_e, nused, xs, w_gate, w_up, w_down)


def _gather_start(idx_ref, src_hbm, dst_vmem, sem):
    nk, tm = idx_ref.shape
    for k in range(nk):
        def body(jj, carry, k=k):
            for u in range(ISSUE_UNROLL):
                j = jj * ISSUE_UNROLL + u
                _tile_copy(src_hbm, idx_ref[k, j], dst_vmem, k * tm + j, sem).start()
            return carry
        lax.fori_loop(0, tm // ISSUE_UNROLL, body, 0)


def _gather_wait(src_hbm, dst_vmem, sem):
    pltpu.make_async_copy(src_hbm.at[pl.ds(0, dst_vmem.shape[0]), :], dst_vmem, sem).wait()


def _combine_kernel(dst_ref, dstn_ref, wt_ref, xp_ref, g2_ref, gfin_ref, ys_hbm, o_ref, yb0, yb1, sem, *, nsteps):
    i = pl.program_id(0)
    tm = COMBINE_TILE
    bufs = (yb0, yb1)

    @pl.when(i == 0)
    def _():
        _gather_start(dst_ref, ys_hbm, yb0, sem.at[0])

    for slot in range(2):
        @pl.when((i + 1 < nsteps) & ((i + 1) % 2 == slot))
        def _(slot=slot):
            _gather_start(dstn_ref, ys_hbm, bufs[slot], sem.at[slot])

    for slot in range(2):
        @pl.when(i % 2 == slot)
        def _(slot=slot):
            _gather_wait(ys_hbm, bufs[slot], sem.at[slot])
            wt = wt_ref[...]
            cols = None
            for k in range(TOP_K):
                part = [c * wt[:, k:k + 1] for c in _from_row_tiles(bufs[slot], tm, base=k * tm * TILE_ROWS)]
                cols = part if cols is None else [a + b for a, b in zip(cols, part)]
            routed = jnp.concatenate(cols, axis=1)
            o_ref[0] = _rms(xp_ref[0] + g2_ref[0] * routed) * gfin_ref[...]


def _combine(ys, dest, wts_t, xp, gate2, g_final):
    bsz, s, d = xp.shape
    tm = COMBINE_TILE
    nt = s // tm
    nsteps = bsz * nt
    vec = pl.BlockSpec((1, 1, d), lambda i: (i // nt, 0, 0))
    return pl.pallas_call(
        functools.partial(_combine_kernel, nsteps=nsteps),
        grid=(nsteps,),
        in_specs=[pl.BlockSpec((TOP_K, tm), lambda i: (0, i), memory_space=pltpu.SMEM),
                  pl.BlockSpec((TOP_K, tm), lambda i: (0, jnp.minimum(i + 1, nsteps - 1)),
                               memory_space=pltpu.SMEM),
                  pl.BlockSpec((tm, TOP_K), lambda i: (i, 0)),
                  pl.BlockSpec((1, tm, d), lambda i: (i // nt, i % nt, 0)),
                  vec,
                  pl.BlockSpec((1, d), lambda i: (0, 0)),
                  pl.BlockSpec(memory_space=pl.ANY)],
        out_specs=pl.BlockSpec((1, tm, d), lambda i: (i // nt, i % nt, 0)),
        out_shape=jax.ShapeDtypeStruct((bsz, s, d), F32),
        scratch_shapes=[pltpu.VMEM((tm * TOP_K * TILE_ROWS, LANES), U32),
                        pltpu.VMEM((tm * TOP_K * TILE_ROWS, LANES), U32),
                        pltpu.SemaphoreType.DMA((2,))],
        compiler_params=_params(("arbitrary",)),
        name="combine_final_norm",
    )(dest, dest, wts_t, xp, gate2, g_final.reshape(1, d), ys)


def kernel(x, c, w_in, b_forget, g_fox_out, g_dil_out, w_out, w_ada, b_ada, w_router, router_bias,
           w_exp_gate, w_exp_up, w_exp_down, w_sh_gate, w_sh_up, w_sh_down, rel_bias, g_final):
    bsz, s, d = x.shape
    depth = w_in.shape[0]
    assert depth == 1 and d == 2 * TILE_ROWS * LANES and s % (DIL_BLOCK * DILATIONS[-1]) == 0
    l = 0
    mod = _modulation(c, w_ada[l], b_ada[l])
    shift1, scale1, gate1, shift2, scale2, gate2 = [m[:, None, :] for m in jnp.split(mod, 6, axis=-1)]

    qscale = HEAD_DIM ** -0.5
    o3 = 3 * WIDTH
    w = w_in[l]
    wq_t = (w[:, :WIDTH] * (qscale * LOG2E)).T.astype(BF16)
    wk = w[:, WIDTH:2 * WIDTH].astype(BF16)
    wv_t = w[:, 2 * WIDTH:o3].T.astype(BF16)
    w_flog = jnp.pad(w[:, o3:o3 + N_HEADS], ((0, 0), (0, LANES - N_HEADS))).astype(BF16)
    wd0 = o3 + N_HEADS
    w_dil = jnp.concatenate([w[:, wd0:wd0 + WIDTH] * qscale, w[:, wd0 + WIDTH:]], axis=1).astype(BF16)

    q_t, k_f, v_t, qkv_d, flog = _in_projection(x, scale1, shift1, wq_t, wk, wv_t, w_dil, w_flog)
    y_fox = _fox_attention(q_t, k_f, _forget_cumsum(flog, b_forget[l]), v_t)
    y_dil = _dilated_attention(qkv_d, _dilated_bias(rel_bias))

    wr = w_router[l].T
    wr_hi = wr.astype(BF16)
    wr_lo = (wr - wr_hi.astype(F32)).astype(BF16)
    wo = w_out[l].astype(BF16)
    xp, h3, logits_t = _out_projection(
        y_fox, y_dil, x, gate1, scale2, shift2, gate2, g_fox_out[l].reshape(1, WIDTH),
        g_dil_out[l].reshape(1, WIDTH), wo[:WIDTH], wo[WIDTH:], wr_hi, wr_lo,
        w_sh_gate[l].astype(BF16), w_sh_up[l].astype(BF16), w_sh_down[l].astype(BF16))

    eidx, wts, rank, counts = _route(logits_t, router_bias[l])
    rows = bsz * s * TOP_K + N_EXPERTS * FFN_BLOCK
    pstart, cend, pend, block_e, next_e, nused = _block_plan(counts, rows // FFN_BLOCK)
    dest = _dest_rows(eidx, rank, pstart)
    xs = _dispatch_rows(h3, dest, cend, pend, rows)
    ys = _routed_experts(xs, block_e, next_e, nused, w_exp_gate[l], w_exp_up[l], w_exp_down[l])
    return _combine(ys, dest, wts.T, xp, gate2, g_final)
```

```python
import functools

import numpy as np
import jax
import jax.numpy as jnp
from jax import lax
from jax.experimental import pallas as pl
from jax.experimental.pallas import tpu as pltpu

F32 = jnp.float32
BF16 = jnp.bfloat16

HEAD_DIM = 64
N_HEADS = 8
WIDTH = N_HEADS * HEAD_DIM
LANES = 128
TILE_ROWS = 4
U32 = jnp.uint32
DIL_UNROLL = 16
HEAD_PAIRS = WIDTH // LANES
DIL_BLOCK = 128
DILATIONS = (1, 4, 16)
T5_NUM_BUCKETS = 32
T5_MAX_DISTANCE = 2048
N_EXPERTS = 256
TOP_K = 8
N_GROUPS = 8
TOP_K_GROUPS = 4
ROUTED_SCALE = 2.5
EPS = 1e-6
NEG = -1e30
LOG2E = float(np.log2(np.e))
VMEM_LIMIT = 56 * 1024 * 1024

ROW_TILE = 512
FOX_TILE = 512
FFN_BLOCK = 256
FFN_PER_STEP = 8
COMBINE_TILE = 256
SCATTER_TILE = 256
ROUTE_TILE = 512
ISSUE_UNROLL = 8


def _params(semantics):
    return pltpu.CompilerParams(dimension_semantics=semantics, vmem_limit_bytes=VMEM_LIMIT)


def _rms(x):
    return x * lax.rsqrt(jnp.mean(x * x, axis=-1, keepdims=True) + EPS)


def _silu(x):
    return x * jax.nn.sigmoid(x)


def _dot_nt(a, b):
    return lax.dot_general(a, b, (((1,), (1,)), ((), ())), preferred_element_type=F32)


def _mod_kernel(c_ref, w_ref, b_ref, o_ref):
    o_ref[...] = jnp.dot(_silu(c_ref[...]), w_ref[...], precision=lax.Precision.HIGHEST,
                         preferred_element_type=F32) + b_ref[...]


def _modulation(c, w_ada, b_ada):
    bsz, d = c.shape
    n = w_ada.shape[1]
    tn = 1536
    return pl.pallas_call(
        _mod_kernel,
        grid=(n // tn,),
        in_specs=[pl.BlockSpec((bsz, d), lambda j: (0, 0)),
                  pl.BlockSpec((d, tn), lambda j: (0, j)),
                  pl.BlockSpec((1, tn), lambda j: (0, j))],
        out_specs=pl.BlockSpec((bsz, tn), lambda j: (0, j)),
        out_shape=jax.ShapeDtypeStruct((bsz, n), F32),
        compiler_params=_params(("arbitrary",)),
        name="adaln_mod",
    )(c, w_ada, b_ada.reshape(1, n))


def _inproj_kernel(x_ref, sc_ref, sh_ref, wq_ref, wk_ref, wv_ref, wd_ref, wl_ref,
                   oq_ref, ok_ref, ov_ref, od_ref, ol_ref):
    h = _rms(x_ref[0]) * (1.0 + sc_ref[0]) + sh_ref[0]
    hb = h.astype(BF16)
    oq_ref[0] = _dot_nt(wq_ref[...], hb).astype(BF16)
    ov_ref[0] = _dot_nt(wv_ref[...], hb).astype(BF16)
    ok_ref[0] = jnp.dot(hb, wk_ref[...], preferred_element_type=F32).astype(BF16)
    od_ref[0] = jnp.dot(hb, wd_ref[...], preferred_element_type=F32).astype(BF16)
    ol_ref[0] = jnp.dot(hb, wl_ref[...], preferred_element_type=F32)


def _in_projection(x, scale1, shift1, wq_t, wk, wv_t, w_dil, w_flog):
    bsz, s, d = x.shape
    tm = ROW_TILE
    vec = pl.BlockSpec((1, 1, d), lambda b, i: (b, 0, 0))
    full = lambda w: pl.BlockSpec(w.shape, lambda b, i: (0, 0))
    row = lambda n: pl.BlockSpec((1, tm, n), lambda b, i: (b, i, 0))
    col = pl.BlockSpec((1, WIDTH, tm), lambda b, i: (b, 0, i))
    return pl.pallas_call(
        _inproj_kernel,
        grid=(bsz, s // tm),
        in_specs=[row(d), vec, vec, full(wq_t), full(wk), full(wv_t), full(w_dil), full(w_flog)],
        out_specs=[col, row(WIDTH), col, row(3 * WIDTH), row(LANES)],
        out_shape=[jax.ShapeDtypeStruct((bsz, WIDTH, s), BF16),
                   jax.ShapeDtypeStruct((bsz, s, WIDTH), BF16),
                   jax.ShapeDtypeStruct((bsz, WIDTH, s), BF16),
                   jax.ShapeDtypeStruct((bsz, s, 3 * WIDTH), BF16),
                   jax.ShapeDtypeStruct((bsz, s, LANES), F32)],
        compiler_params=_params(("parallel", "arbitrary")),
        name="norm1_inproj",
    )(x, scale1, shift1, wq_t, wk, wv_t, w_dil, w_flog)


FORGET_PARTS = 3


def _split_bf16(x):
    parts = []
    for _ in range(FORGET_PARTS):
        p = x.astype(BF16)
        parts.append(p)
        x = x - p.astype(F32)
    return parts


def _forget_kernel(f_ref, b_ref, tri_ref, place_ref, o_ref):
    blk = tri_ref.shape[0]
    carry = jnp.zeros((1, LANES), F32)
    for c in range(f_ref.shape[1] // blk):
        rows = pl.ds(c * blk, blk)
        z = f_ref[0, rows, :] + b_ref[...]
        ls = jnp.minimum(z, 0.0) - jnp.log1p(jnp.exp(-jnp.abs(z)))
        cum = carry
        for p in _split_bf16(ls):
            cum = cum + jnp.dot(tri_ref[...], p, preferred_element_type=F32)
        carry = cum[blk - 1:blk, :]
        out = None
        for n, p in enumerate(_split_bf16(cum * LOG2E)):
            d = jnp.dot(p, place_ref[n], preferred_element_type=F32)
            out = d if out is None else out + d
        o_ref[0, rows, :] = out.astype(BF16)


def _forget_cumsum(flog, b_forget):
    bsz, s, _ = flog.shape
    blk = 512
    place = np.zeros((FORGET_PARTS, LANES, WIDTH), np.float32)
    for h in range(N_HEADS):
        for n in range(FORGET_PARTS):
            place[n, h, (h // 2) * LANES + (h % 2) * HEAD_DIM + n] = 1.0
    return pl.pallas_call(
        _forget_kernel,
        grid=(bsz,),
        in_specs=[pl.BlockSpec((1, s, LANES), lambda b: (b, 0, 0)),
                  pl.BlockSpec((1, LANES), lambda b: (0, 0)),
                  pl.BlockSpec((blk, blk), lambda b: (0, 0)),
                  pl.BlockSpec(place.shape, lambda b: (0, 0, 0))],
        out_specs=pl.BlockSpec((1, s, WIDTH), lambda b: (b, 0, 0)),
        out_shape=jax.ShapeDtypeStruct((bsz, s, WIDTH), BF16),
        compiler_params=_params(("parallel",)),
        name="forget_cumsum",
    )(flog, jnp.pad(b_forget, (0, LANES - N_HEADS)).reshape(1, LANES),
      jnp.tril(jnp.ones((blk, blk), BF16)), jnp.asarray(place, BF16))


def _stack_heads(q):
    lane = lax.broadcasted_iota(jnp.int32, q.shape, 1)
    zero = jnp.zeros_like(q)
    return jnp.concatenate([jnp.where(lane < HEAD_DIM, q, zero), jnp.where(lane >= HEAD_DIM, q, zero)], axis=0)


def _unstack_heads(o2):
    rows = o2.shape[0] // 2
    lane = lax.broadcasted_iota(jnp.int32, (rows, LANES), 1)
    return jnp.where(lane < HEAD_DIM, o2[:rows], o2[rows:])


def _fox_kernel(q_ref, k_ref, fa_ref, v_ref, o_ref, m_s, l_s, acc_s, s_a, s_b, *, t):
    i = pl.program_id(2)
    q_t = q_ref[0]
    row = lax.broadcasted_iota(jnp.int32, (LANES, 2 * t), 0)
    col = lax.broadcasted_iota(jnp.int32, (LANES, 2 * t), 1)
    head_row = jnp.where(col < t, 0, HEAD_DIM)
    q2 = jnp.concatenate([q_t, q_t], axis=1)
    q2 = jnp.where((row >= head_row) & (row < head_row + HEAD_DIM), q2, jnp.zeros_like(q2))
    minus = jnp.where((row >= head_row) & (row < head_row + FORGET_PARTS), -1.0, 0.0).astype(BF16)
    qa = jnp.concatenate([q2, minus], axis=0)
    m_s[...] = jnp.full(m_s.shape, NEG, F32)
    l_s[...] = jnp.zeros(l_s.shape, F32)
    acc_s[...] = jnp.zeros(acc_s.shape, F32)

    def scores(ks, tk):
        kk = jnp.concatenate([k_ref[0, pl.ds(ks, tk), :], fa_ref[0, pl.ds(ks, tk), :]], axis=1)
        return jnp.dot(kk, qa, preferred_element_type=F32)

    def absorb(s, ks, tk):
        m_prev = m_s[...]
        m_new = jnp.maximum(m_prev, jnp.max(s, axis=0, keepdims=True))
        alpha = jnp.exp2(m_prev - m_new)
        p = jnp.exp2(s - m_new)
        l_s[...] = alpha * l_s[...] + jnp.sum(p, axis=0, keepdims=True)
        acc_s[...] = alpha * acc_s[...] + jnp.dot(v_ref[0, :, pl.ds(ks, tk)], p.astype(BF16),
                                                  preferred_element_type=F32)
        m_s[...] = m_new

    npair = i // 2

    def tile(j):
        return pl.multiple_of(j * 2 * t, 2 * t)

    @pl.when(npair > 0)
    def _():
        s_a[...] = scores(0, 2 * t)

        def body(jj, carry):
            j = 2 * jj
            s_b[...] = scores(tile(j + 1), 2 * t)
            absorb(s_a[...], tile(j), 2 * t)
            s_a[...] = scores(tile(jnp.minimum(j + 2, npair - 1)), 2 * t)
            absorb(s_b[...], tile(j + 1), 2 * t)
            return carry

        lax.fori_loop(0, npair // 2, body, 0)

        @pl.when(npair % 2 == 1)
        def _():
            absorb(s_a[...], tile(npair - 1), 2 * t)

    def last(ks, tk):
        r = lax.broadcasted_iota(jnp.int32, (tk, 2 * t), 0)
        c = lax.broadcasted_iota(jnp.int32, (tk, 2 * t), 1)
        absorb(jnp.where(r <= jnp.where(c >= t, c - t, c) + (tk - t), scores(ks, tk), NEG), ks, tk)

    @pl.when(i % 2 == 1)
    def _():
        last(tile(npair), 2 * t)

    @pl.when(i % 2 == 0)
    def _():
        last(pl.multiple_of(i * t, t), t)

    o2 = acc_s[...] / l_s[...]
    o_t = jnp.where(lax.broadcasted_iota(jnp.int32, (LANES, t), 0) < HEAD_DIM, o2[:, :t], o2[:, t:])
    o_ref[0] = o_t.T


def _fox_attention(q_t, k, f_aug, v_t):
    bsz, s, _ = k.shape
    t = FOX_TILE
    keys = pl.BlockSpec((1, s, LANES), lambda b, h, i: (b, 0, h))
    return pl.pallas_call(
        functools.partial(_fox_kernel, t=t),
        grid=(bsz, HEAD_PAIRS, s // t),
        in_specs=[pl.BlockSpec((1, LANES, t), lambda b, h, i: (b, h, i)),
                  keys, keys,
                  pl.BlockSpec((1, LANES, s), lambda b, h, i: (b, h, 0))],
        out_specs=pl.BlockSpec((1, t, LANES), lambda b, h, i: (b, i, h)),
        out_shape=jax.ShapeDtypeStruct((bsz, s, WIDTH), F32),
        scratch_shapes=[pltpu.VMEM((1, 2 * t), F32), pltpu.VMEM((1, 2 * t), F32),
                        pltpu.VMEM((LANES, 2 * t), F32), pltpu.VMEM((2 * t, 2 * t), F32),
                        pltpu.VMEM((2 * t, 2 * t), F32)],
        compiler_params=_params(("parallel", "parallel", "arbitrary")),
        name="fox_attention",
    )(q_t, k, f_aug, v_t)


def _t5_bucket(dist):
    max_exact = T5_NUM_BUCKETS // 2
    d = np.maximum(dist, 1).astype(np.float32)
    large = max_exact + (np.log(d / max_exact) / np.log(T5_MAX_DISTANCE / max_exact)
                         * (T5_NUM_BUCKETS - max_exact)).astype(np.int32)
    large = np.minimum(large, T5_NUM_BUCKETS - 1)
    return np.where(dist < max_exact, dist, large).astype(np.int32)


def _dilated_bias(rel_bias):
    blk = DIL_BLOCK
    period = 3 * blk
    m = np.arange(period)
    rel = np.where(m < 2 * blk, blk - m, blk - (m - period))
    band = (rel >= 0) & (rel <= blk)
    onehot = np.zeros((len(DILATIONS), period, T5_NUM_BUCKETS), np.float32)
    for bi, dil in enumerate(DILATIONS):
        onehot[bi, m, _t5_bucket(np.clip(rel, 0, blk) * dil)] = 1.0
    w = jnp.einsum('bmk,kh->bhm', jnp.asarray(onehot), rel_bias.astype(F32),
                   precision=lax.Precision.HIGHEST)
    w = jnp.where(jnp.asarray(band), w, NEG)
    flat = jnp.tile(w, (1, 1, blk))[:, :, :blk * (period - 1)]
    table = flat.reshape(len(DILATIONS), N_HEADS, blk, period - 1)[..., :2 * blk]
    table = table.reshape(len(DILATIONS), HEAD_PAIRS, 2 * blk, 2 * blk)
    first = jnp.where(jnp.arange(2 * blk) < blk, NEG, table)
    return jnp.stack([table, first], axis=1)


def _dil_kernel(q_ref, k_ref, v_ref, bias_ref, o_ref, qf, kf, vf, ob0, ob1, ob2, ls0, ls1, ls2, *, s_len):
    blk = DIL_BLOCK
    qf[...] = q_ref[0].astype(F32)
    kf[...] = k_ref[0].astype(F32)
    vf[...] = v_ref[0].astype(F32)
    for bi, (dil, ob, ls) in enumerate(zip(DILATIONS, (ob0, ob1, ob2), (ls0, ls1, ls2))):
        span = blk * dil
        nb = s_len // span

        def rows(start, dil=dil):
            return pl.ds(start, blk) if dil == 1 else pl.ds(start, blk, stride=dil)

        def block(tix, carry, bi=bi, span=span, nb=nb, ob=ob, ls=ls, rows=rows):
            n = tix % nb
            start = n * span + tix // nb
            prev = jnp.maximum(start - span, 0)
            q2 = _stack_heads(qf[rows(start), :].astype(BF16))
            kk = jnp.concatenate([kf[rows(prev), :], kf[rows(start), :]], axis=0).astype(BF16)
            vv = jnp.concatenate([vf[rows(prev), :], vf[rows(start), :]], axis=0).astype(BF16)
            s = lax.dot_general(q2, kk, (((1,), (1,)), ((), ())), preferred_element_type=F32)
            s = s + bias_ref[bi, jnp.where(n == 0, 1, 0), 0]
            m = jnp.max(s, axis=1, keepdims=True)
            p = jnp.exp(s - m)
            l = jnp.sum(p, axis=1, keepdims=True)
            o2 = jnp.dot(p.astype(BF16), vv, preferred_element_type=F32) / l
            lse = jnp.broadcast_to(m + jnp.log(l), (2 * blk, LANES))
            ob[rows(start), :] = _unstack_heads(o2)
            ls[rows(start), :] = _unstack_heads(lse)
            return carry

        def blocks(g, carry, block=block):
            for u in range(DIL_UNROLL):
                block(g * DIL_UNROLL + u, carry)
            return carry

        lax.fori_loop(0, dil * nb // DIL_UNROLL, blocks, 0)

    chunk = 512
    for c in range(s_len // chunk):
        r = pl.ds(c * chunk, chunk)
        l0, l1, l2 = ls0[r, :], ls1[r, :], ls2[r, :]
        mx = jnp.maximum(jnp.maximum(l0, l1), l2)
        e0, e1, e2 = jnp.exp(l0 - mx), jnp.exp(l1 - mx), jnp.exp(l2 - mx)
        o_ref[0, r, :] = (e0 * ob0[r, :] + e1 * ob1[r, :] + e2 * ob2[r, :]) / (e0 + e1 + e2)


def _dilated_attention(qkv, bias):
    bsz, s, _ = qkv.shape
    col = lambda off: pl.BlockSpec((1, s, LANES), lambda b, h: (b, 0, off + h))
    buf = pltpu.VMEM((s, LANES), F32)
    return pl.pallas_call(
        functools.partial(_dil_kernel, s_len=s),
        grid=(bsz, HEAD_PAIRS),
        in_specs=[col(0), col(HEAD_PAIRS), col(2 * HEAD_PAIRS),
                  pl.BlockSpec((len(DILATIONS), 2, 1, 2 * DIL_BLOCK, 2 * DIL_BLOCK), lambda b, h: (0, 0, h, 0, 0))],
        out_specs=pl.BlockSpec((1, s, LANES), lambda b, h: (b, 0, h)),
        out_shape=jax.ShapeDtypeStruct((bsz, s, WIDTH), F32),
        scratch_shapes=[buf] * 9,
        compiler_params=_params(("parallel", "arbitrary")),
        name="dilated_attention",
    )(qkv, qkv, qkv, bias)


def _bf16_bits(x):
    return pltpu.bitcast(x.astype(BF16).astype(F32), U32)


def _to_row_tiles(dst_ref, x, base=0):
    rows, d = x.shape
    for c in range(TILE_ROWS):
        lo = _bf16_bits(x[:, c * LANES:(c + 1) * LANES]) >> 16
        hi = _bf16_bits(x[:, d // 2 + c * LANES:d // 2 + (c + 1) * LANES])
        dst_ref[pl.ds(base + c, rows, stride=TILE_ROWS), :] = lo | hi


def _from_row_tiles(src_ref, rows, base=0):
    lo, hi = [], []
    for c in range(TILE_ROWS):
        w = src_ref[pl.ds(base + c, rows, stride=TILE_ROWS), :]
        lo.append(pltpu.bitcast(w << 16, F32))
        hi.append(pltpu.bitcast(w & jnp.uint32(0xFFFF0000), F32))
    return lo + hi


def _outproj_kernel(yf_ref, yd_ref, x_ref, g1_ref, sc_ref, sh_ref, g2_ref, gf_ref, gd_ref, wo1_ref, wo2_ref,
                    wrh_ref, wrl_ref, wsg_ref, wsu_ref, wsd_ref, xp_ref, h3_ref, lg_ref):
    nf = (_rms(yf_ref[0]) * gf_ref[...]).astype(BF16)
    nd = (_rms(yd_ref[0]) * gd_ref[...]).astype(BF16)
    mix = (jnp.dot(nf, wo1_ref[...], preferred_element_type=F32)
           + jnp.dot(nd, wo2_ref[...], preferred_element_type=F32))
    x1 = x_ref[0] + g1_ref[0] * mix
    h2 = _rms(x1) * (1.0 + sc_ref[0]) + sh_ref[0]
    hb = h2.astype(BF16)
    hl = (h2 - hb.astype(F32)).astype(BF16)
    lg_ref[...] = _dot_nt(wrh_ref[...], hb) + _dot_nt(wrh_ref[...], hl) + _dot_nt(wrl_ref[...], hb)
    act = _silu(jnp.dot(hb, wsg_ref[...], preferred_element_type=F32)) * jnp.dot(
        hb, wsu_ref[...], preferred_element_type=F32)
    shared = jnp.dot(act.astype(BF16), wsd_ref[...], preferred_element_type=F32)
    xp_ref[0] = x1 + g2_ref[0] * shared
    _to_row_tiles(h3_ref, h2)


def _out_projection(y_fox, y_dil, x, gate1, scale2, shift2, gate2, g_fox, g_dil, wo1, wo2, wr_hi, wr_lo,
                    wsg, wsu, wsd):
    bsz, s, d = x.shape
    tm = ROW_TILE
    nt = s // tm
    vec = pl.BlockSpec((1, 1, d), lambda b, i: (b, 0, 0))
    full = lambda w: pl.BlockSpec(w.shape, lambda b, i: (0,) * w.ndim)
    row = lambda n: pl.BlockSpec((1, tm, n), lambda b, i: (b, i, 0))
    return pl.pallas_call(
        _outproj_kernel,
        grid=(bsz, nt),
        in_specs=[row(WIDTH), row(WIDTH), row(d), vec, vec, vec, vec, full(g_fox), full(g_dil), full(wo1),
                  full(wo2), full(wr_hi), full(wr_lo), full(wsg), full(wsu), full(wsd)],
        out_specs=[row(d),
                   pl.BlockSpec((tm * TILE_ROWS, LANES), lambda b, i: (b * nt + i, 0)),
                   pl.BlockSpec((N_EXPERTS, tm), lambda b, i: (0, b * nt + i))],
        out_shape=[jax.ShapeDtypeStruct((bsz, s, d), F32),
                   jax.ShapeDtypeStruct((bsz * s * TILE_ROWS, LANES), U32),
                   jax.ShapeDtypeStruct((N_EXPERTS, bsz * s), F32)],
        compiler_params=_params(("parallel", "arbitrary")),
        name="outproj_norm2_router_shared",
    )(y_fox, y_dil, x, gate1, scale2, shift2, gate2, g_fox, g_dil, wo1, wo2, wr_hi, wr_lo, wsg, wsu, wsd)


def _first_argmax(v, row, size):
    m = jnp.max(v, axis=0, keepdims=True)
    return m, jnp.min(jnp.where(v == m, row, size), axis=0, keepdims=True)


def _route_kernel(lg_ref, rb_ref, tri_ref, e_ref, w_ref, rk_ref, cnt_ref, cnt_s):
    @pl.when(pl.program_id(0) == 0)
    def _():
        cnt_s[...] = jnp.zeros(cnt_s.shape, F32)

    t = lg_ref.shape[1]
    gsz = N_EXPERTS // N_GROUPS
    scores = jax.nn.sigmoid(lg_ref[...])
    sel = scores + rb_ref[...]
    row_g = lax.broadcasted_iota(jnp.int32, (gsz, t), 0)
    grp = []
    for g in range(N_GROUPS):
        v = sel[g * gsz:(g + 1) * gsz]
        m1, i1 = _first_argmax(v, row_g, gsz)
        m2 = jnp.max(jnp.where(row_g == i1, -jnp.inf, v), axis=0, keepdims=True)
        grp.append(m1 + m2)
    gv = jnp.concatenate(grp, axis=0)
    row8 = lax.broadcasted_iota(jnp.int32, (N_GROUPS, t), 0)
    pen = jnp.full((N_GROUPS, t), -jnp.inf, F32)
    for _ in range(TOP_K_GROUPS):
        _, ix = _first_argmax(gv, row8, N_GROUPS)
        pen = jnp.where(row8 == ix, 0.0, pen)
        gv = jnp.where(row8 == ix, -jnp.inf, gv)
    selm = jnp.concatenate([sel[g * gsz:(g + 1) * gsz] + pen[g:g + 1] for g in range(N_GROUPS)], axis=0)

    row = lax.broadcasted_iota(jnp.int32, (N_EXPERTS, t), 0)
    v = selm
    idxs, scs = [], []
    for _ in range(TOP_K):
        _, ix = _first_argmax(v, row, N_EXPERTS)
        hit = row == ix
        idxs.append(ix)
        scs.append(jnp.sum(jnp.where(hit, scores, 0.0), axis=0, keepdims=True))
        v = jnp.where(hit, -jnp.inf, v)
    chosen = jnp.where(v != selm, 1.0, 0.0)
    before = jnp.dot(chosen.astype(BF16), tri_ref[...], preferred_element_type=F32) + cnt_s[...]
    rks = [jnp.sum(jnp.where(row == ix, before, 0.0), axis=0, keepdims=True) for ix in idxs]
    sc = jnp.concatenate(scs, axis=0)
    e_ref[...] = jnp.concatenate(idxs, axis=0)
    w_ref[...] = sc / jnp.sum(sc, axis=0, keepdims=True) * ROUTED_SCALE
    rk_ref[...] = jnp.concatenate(rks, axis=0).astype(jnp.int32)
    cnt_s[...] = cnt_s[...] + jnp.sum(chosen, axis=1, keepdims=True)
    cnt_ref[...] = cnt_s[...]


def _route(logits_t, router_bias):
    e, n = logits_t.shape
    t = ROUTE_TILE
    tri = jnp.triu(jnp.ones((t, t), BF16), k=1)
    tile = pl.BlockSpec((TOP_K, t), lambda i: (0, i))
    return pl.pallas_call(
        _route_kernel,
        grid=(n // t,),
        in_specs=[pl.BlockSpec((e, t), lambda i: (0, i)),
                  pl.BlockSpec((e, 1), lambda i: (0, 0)),
                  pl.BlockSpec((t, t), lambda i: (0, 0))],
        out_specs=[tile, tile, tile, pl.BlockSpec((e, 1), lambda i: (0, 0))],
        out_shape=[jax.ShapeDtypeStruct((TOP_K, n), jnp.int32), jax.ShapeDtypeStruct((TOP_K, n), F32),
                   jax.ShapeDtypeStruct((TOP_K, n), jnp.int32), jax.ShapeDtypeStruct((e, 1), F32)],
        scratch_shapes=[pltpu.VMEM((e, 1), F32)],
        compiler_params=_params(("arbitrary",)),
        name="route_topk_rank",
    )(logits_t, router_bias.reshape(e, 1).astype(F32), tri)


def _dest_kernel(e_ref, rk_ref, ps_ref, d_ref):
    t = e_ref.shape[1]
    row = lax.broadcasted_iota(jnp.int32, (N_EXPERTS, t), 0)
    ps = ps_ref[...]
    base = [jnp.sum(jnp.where(row == e_ref[k:k + 1, :], ps, 0.0), axis=0, keepdims=True) for k in range(TOP_K)]
    d_ref[...] = jnp.concatenate(base, axis=0).astype(jnp.int32) + rk_ref[...]


def _dest_rows(eidx, rank, pstart):
    _, n = eidx.shape
    t = ROUTE_TILE
    tile = pl.BlockSpec((TOP_K, t), lambda i: (0, i))
    return pl.pallas_call(
        _dest_kernel,
        grid=(n // t,),
        in_specs=[tile, tile, pl.BlockSpec((N_EXPERTS, 1), lambda i: (0, 0))],
        out_specs=tile,
        out_shape=jax.ShapeDtypeStruct((TOP_K, n), jnp.int32),
        compiler_params=_params(("parallel",)),
        name="dest_rows",
    )(eidx, rank, pstart.astype(F32).reshape(N_EXPERTS, 1))


def _block_plan(counts, nblk):
    counts = counts.reshape(N_EXPERTS).astype(jnp.int32)
    padded = (counts + FFN_BLOCK - 1) // FFN_BLOCK * FFN_BLOCK
    pend = jnp.cumsum(padded).astype(jnp.int32)
    pstart = pend - padded
    first_row = jnp.arange(nblk, dtype=jnp.int32) * FFN_BLOCK
    block_e = jnp.minimum(jnp.sum(pend[None, :] <= first_row[:, None], axis=1), N_EXPERTS - 1).astype(jnp.int32)
    nused = (pend[-1:] // FFN_BLOCK).astype(jnp.int32)
    ids = jnp.arange(N_EXPERTS, dtype=jnp.int32)
    following = lax.cummin(jnp.where(counts > 0, ids, N_EXPERTS), reverse=True)
    following = jnp.concatenate([following[1:], jnp.full((1,), N_EXPERTS, jnp.int32)])
    following = jnp.where(following >= N_EXPERTS, -1, following)
    next_e = jnp.sum(jnp.where(block_e[:, None] == ids[None, :], following[None, :], 0), axis=1).astype(jnp.int32)
    return pstart, pstart + counts, pend, block_e, next_e, nused


def _tile_copy(src_ref, src_row, dst_ref, dst_row, sem):
    return pltpu.make_async_copy(src_ref.at[pl.ds(pl.multiple_of(src_row * TILE_ROWS, TILE_ROWS), TILE_ROWS), :],
                                 dst_ref.at[pl.ds(pl.multiple_of(dst_row * TILE_ROWS, TILE_ROWS), TILE_ROWS), :],
                                 sem)


def _scatter_kernel(cend_ref, pend_ref, dst_ref, h3_ref, z_ref, xs_hbm, st0, st1, sem, zsem):
    i = pl.program_id(0)
    nsteps = pl.num_programs(0)
    tm = dst_ref.shape[1]
    stage = (st0, st1)

    def zero_copy(r):
        return pltpu.make_async_copy(z_ref, xs_hbm.at[pl.ds(pl.multiple_of(r * TILE_ROWS, TILE_ROWS), TILE_ROWS), :],
                                     zsem)

    def wait_slot(slot):
        for _ in range(TOP_K):
            pltpu.make_async_copy(stage[slot], xs_hbm.at[pl.ds(0, tm * TILE_ROWS), :], sem.at[slot]).wait()

    @pl.when(i == 0)
    def _():
        def start_row(r, c):
            zero_copy(r).start()
            return c

        def wait_row(r, c):
            zero_copy(r).wait()
            return c

        def per_expert(e, carry):
            lax.fori_loop(cend_ref[e], pend_ref[e], start_row, carry)
            prev = jnp.maximum(e - 1, 0)
            return lax.fori_loop(cend_ref[prev], jnp.where(e > 0, pend_ref[prev], cend_ref[prev]), wait_row, carry)

        lax.fori_loop(0, N_EXPERTS, per_expert, 0)
        lax.fori_loop(cend_ref[N_EXPERTS - 1], pend_ref[N_EXPERTS - 1], wait_row, 0)

    for slot in range(2):
        @pl.when(i % 2 == slot)
        def _(slot=slot):
            @pl.when(i >= 2)
            def _():
                wait_slot(slot)
            stage[slot][...] = h3_ref[...]
            for k in range(TOP_K):
                def body(jj, carry, k=k):
                    for u in range(ISSUE_UNROLL):
                        j = jj * ISSUE_UNROLL + u
                        _tile_copy(stage[slot], j, xs_hbm, dst_ref[k, j], sem.at[slot]).start()
                    return carry
                lax.fori_loop(0, tm // ISSUE_UNROLL, body, 0)

    @pl.when(i == nsteps - 1)
    def _():
        wait_slot(0)
        wait_slot(1)


def _dispatch_rows(h3, dest, cend, pend, rows):
    _, n = dest.shape
    tm = SCATTER_TILE
    assert n // tm >= 2
    stage = pltpu.VMEM((tm * TILE_ROWS, LANES), U32)
    grid_spec = pltpu.PrefetchScalarGridSpec(
        num_scalar_prefetch=2,
        grid=(n // tm,),
        in_specs=[pl.BlockSpec((TOP_K, tm), lambda i, ce, pe: (0, i), memory_space=pltpu.SMEM),
                  pl.BlockSpec((tm * TILE_ROWS, LANES), lambda i, ce, pe: (i, 0)),
                  pl.BlockSpec((TILE_ROWS, LANES), lambda i, ce, pe: (0, 0))],
        out_specs=pl.BlockSpec(memory_space=pl.ANY),
        scratch_shapes=[stage, stage, pltpu.SemaphoreType.DMA((2,)), pltpu.SemaphoreType.DMA],
    )
    return pl.pallas_call(
        _scatter_kernel,
        grid_spec=grid_spec,
        out_shape=jax.ShapeDtypeStruct((rows * TILE_ROWS, LANES), U32),
        compiler_params=_params(("arbitrary",)),
        name="dispatch_scatter",
    )(cend, pend, dest, h3, jnp.zeros((TILE_ROWS, LANES), U32))


def _ffn_kernel(be_ref, nx_ref, nu_ref, x_ref, wg_hbm, wu_hbm, wd_hbm, y_ref,
                wgf, wuf, wdf, wgb, wub, wdb, sem, run):
    first = pl.program_id(0) * FFN_PER_STEP
    nused = nu_ref[0]

    def fetch(e, slot):
        return [pltpu.make_async_copy(src.at[e], dst.at[slot], sem.at[slot])
                for src, dst in ((wg_hbm, wgf), (wu_hbm, wuf), (wd_hbm, wdf))]

    @pl.when(first == 0)
    def _():
        run[0] = -1
        for c in fetch(be_ref[0], 0):
            c.start()

    slots = []
    for part in range(FFN_PER_STEP):
        blk = first + part

        @pl.when((blk < nused) & ((blk == 0) | (be_ref[blk] != be_ref[jnp.maximum(blk - 1, 0)])))
        def _(blk=blk):
            r = run[0] + 1
            run[0] = r
            fslot, slot = r % 2, r % FFN_PER_STEP
            for c in fetch(be_ref[blk], fslot):
                c.wait()
            wgb[slot] = wgf[fslot].astype(BF16)
            wub[slot] = wuf[fslot].astype(BF16)
            wdb[slot] = wdf[fslot].astype(BF16)

            @pl.when(nx_ref[blk] >= 0)
            def _():
                for c in fetch(nx_ref[blk], 1 - fslot):
                    c.start()

        slots.append(run[0] % FFN_PER_STEP)

    @pl.when(first < nused)
    def _():
        for part, slot in enumerate(slots):
            base = part * FFN_BLOCK * TILE_ROWS
            x = jnp.concatenate([c.astype(BF16) for c in _from_row_tiles(x_ref, FFN_BLOCK, base)], axis=1)
            act = _silu(jnp.dot(x, wgb[slot], preferred_element_type=F32)) * jnp.dot(
                x, wub[slot], preferred_element_type=F32)
            _to_row_tiles(y_ref, jnp.dot(act.astype(BF16), wdb[slot], preferred_element_type=F32), base)

    @pl.when(first >= nused)
    def _():
        y_ref[...] = jnp.zeros(y_ref.shape, U32)


def _routed_experts(xs, block_e, next_e, nused, w_gate, w_up, w_down):
    rows = xs.shape[0] // TILE_ROWS
    nblk = rows // FFN_BLOCK
    _, d, hid = w_gate.shape
    step_rows = FFN_PER_STEP * FFN_BLOCK * TILE_ROWS
    hbm = pl.BlockSpec(memory_space=pl.ANY)
    grid_spec = pltpu.PrefetchScalarGridSpec(
        num_scalar_prefetch=3,
        grid=(nblk // FFN_PER_STEP,),
        in_specs=[pl.BlockSpec((step_rows, LANES),
                               lambda i, be, nx, nu: (jnp.minimum(i, (nu[0] - 1) // FFN_PER_STEP), 0)),
                  hbm, hbm, hbm],
        out_specs=pl.BlockSpec((step_rows, LANES), lambda i, be, nx, nu: (i, 0)),
        scratch_shapes=[pltpu.VMEM((2, d, hid), F32), pltpu.VMEM((2, d, hid), F32), pltpu.VMEM((2, hid, d), F32),
                        pltpu.VMEM((FFN_PER_STEP, d, hid), BF16), pltpu.VMEM((FFN_PER_STEP, d, hid), BF16),
                        pltpu.VMEM((FFN_PER_STEP, hid, d), BF16),
                        pltpu.SemaphoreType.DMA((2,)), pltpu.SMEM((1,), jnp.int32)],
    )
    return pl.pallas_call(
        _ffn_kernel,
        grid_spec=grid_spec,
        out_shape=jax.ShapeDtypeStruct((rows * TILE_ROWS, LANES), U32),
        compiler_params=_params(("arbitrary",)),
        name="routed_experts",
    )(block_e, next_e, nused, xs, w_gate, w_up, w_down)


def _gather_start(idx_ref, src_hbm, dst_vmem, sem):
    nk, tm = idx_ref.shape
    for k in range(nk):
        def body(jj, carry, k=k):
            for u in range(ISSUE_UNROLL):
                j = jj * ISSUE_UNROLL + u
                _tile_copy(src_hbm, idx_ref[k, j], dst_vmem, k * tm + j, sem).start()
            return carry
        lax.fori_loop(0, tm // ISSUE_UNROLL, body, 0)


def _gather_wait(src_hbm, dst_vmem, sem):
    pltpu.make_async_copy(src_hbm.at[pl.ds(0, dst_vmem.shape[0]), :], dst_vmem, sem).wait()


def _combine_kernel(dst_ref, dstn_ref, wt_ref, xp_ref, g2_ref, gfin_ref, ys_hbm, o_ref, yb0, yb1, sem, *, nsteps):
    i = pl.program_id(0)
    tm = COMBINE_TILE
    bufs = (yb0, yb1)

    @pl.when(i == 0)
    def _():
        _gather_start(dst_ref, ys_hbm, yb0, sem.at[0])

    for slot in range(2):
        @pl.when((i + 1 < nsteps) & ((i + 1) % 2 == slot))
        def _(slot=slot):
            _gather_start(dstn_ref, ys_hbm, bufs[slot], sem.at[slot])

    for slot in range(2):
        @pl.when(i % 2 == slot)
        def _(slot=slot):
            _gather_wait(ys_hbm, bufs[slot], sem.at[slot])
            wt = wt_ref[...]
            cols = None
            for k in range(TOP_K):
                part = [c * wt[:, k:k + 1] for c in _from_row_tiles(bufs[slot], tm, base=k * tm * TILE_ROWS)]
                cols = part if cols is None else [a + b for a, b in zip(cols, part)]
            routed = jnp.concatenate(cols, axis=1)
            o_ref[0] = _rms(xp_ref[0] + g2_ref[0] * routed) * gfin_ref[...]


def _combine(ys, dest, wts_t, xp, gate2, g_final):
    bsz, s, d = xp.shape
    tm = COMBINE_TILE
    nt = s // tm
    nsteps = bsz * nt
    vec = pl.BlockSpec((1, 1, d), lambda i: (i // nt, 0, 0))
    return pl.pallas_call(
        functools.partial(_combine_kernel, nsteps=nsteps),
        grid=(nsteps,),
        in_specs=[pl.BlockSpec((TOP_K, tm), lambda i: (0, i), memory_space=pltpu.SMEM),
                  pl.BlockSpec((TOP_K, tm), lambda i: (0, jnp.minimum(i + 1, nsteps - 1)),
                               memory_space=pltpu.SMEM),
                  pl.BlockSpec((tm, TOP_K), lambda i: (i, 0)),
                  pl.BlockSpec((1, tm, d), lambda i: (i // nt, i % nt, 0)),
                  vec,
                  pl.BlockSpec((1, d), lambda i: (0, 0)),
                  pl.BlockSpec(memory_space=pl.ANY)],
        out_specs=pl.BlockSpec((1, tm, d), lambda i: (i // nt, i % nt, 0)),
        out_shape=jax.ShapeDtypeStruct((bsz, s, d), F32),
        scratch_shapes=[pltpu.VMEM((tm * TOP_K * TILE_ROWS, LANES), U32),
                        pltpu.VMEM((tm * TOP_K * TILE_ROWS, LANES), U32),
                        pltpu.SemaphoreType.DMA((2,))],
        compiler_params=_params(("arbitrary",)),
        name="combine_final_norm",
    )(dest, dest, wts_t, xp, gate2, g_final.reshape(1, d), ys)


def kernel(x, c, w_in, b_forget, g_fox_out, g_dil_out, w_out, w_ada, b_ada, w_router, router_bias,
           w_exp_gate, w_exp_up, w_exp_down, w_sh_gate, w_sh_up, w_sh_down, rel_bias, g_final):
    bsz, s, d = x.shape
    depth = w_in.shape[0]
    assert depth == 1 and d == 2 * TILE_ROWS * LANES and s % (DIL_BLOCK * DILATIONS[-1]) == 0
    l = 0
    mod = _modulation(c, w_ada[l], b_ada[l])
    shift1, scale1, gate1, shift2, scale2, gate2 = [m[:, None, :] for m in jnp.split(mod, 6, axis=-1)]

    qscale = HEAD_DIM ** -0.5
    o3 = 3 * WIDTH
    w = w_in[l]
    wq_t = (w[:, :WIDTH] * (qscale * LOG2E)).T.astype(BF16)
    wk = w[:, WIDTH:2 * WIDTH].astype(BF16)
    wv_t = w[:, 2 * WIDTH:o3].T.astype(BF16)
    w_flog = jnp.pad(w[:, o3:o3 + N_HEADS], ((0, 0), (0, LANES - N_HEADS))).astype(BF16)
    wd0 = o3 + N_HEADS
    w_dil = jnp.concatenate([w[:, wd0:wd0 + WIDTH] * qscale, w[:, wd0 + WIDTH:]], axis=1).astype(BF16)

    q_t, k_f, v_t, qkv_d, flog = _in_projection(x, scale1, shift1, wq_t, wk, wv_t, w_dil, w_flog)
    y_fox = _fox_attention(q_t, k_f, _forget_cumsum(flog, b_forget[l]), v_t)
    y_dil = _dilated_attention(qkv_d, _dilated_bias(rel_bias))

    wr = w_router[l].T
    wr_hi = wr.astype(BF16)
    wr_lo = (wr - wr_hi.astype(F32)).astype(BF16)
    wo = w_out[l].astype(BF16)
    xp, h3, logits_t = _out_projection(
        y_fox, y_dil, x, gate1, scale2, shift2, gate2, g_fox_out[l].reshape(1, WIDTH),
        g_dil_out[l].reshape(1, WIDTH), wo[:WIDTH], wo[WIDTH:], wr_hi, wr_lo,
        w_sh_gate[l].astype(BF16), w_sh_up[l].astype(BF16), w_sh_down[l].astype(BF16))

    eidx, wts, rank, counts = _route(logits_t, router_bias[l])
    rows = bsz * s * TOP_K + N_EXPERTS * FFN_BLOCK
    pstart, cend, pend, block_e, next_e, nused = _block_plan(counts, rows // FFN_BLOCK)
    dest = _dest_rows(eidx, rank, pstart)
    xs = _dispatch_rows(h3, dest, cend, pend, rows)
    ys = _routed_experts(xs, block_e, next_e, nused, w_exp_gate[l], w_exp_up[l], w_exp_down[l])
    return _combine(ys, dest, wts.T, xp, gate2, g_final)
```

```python
import functools

import numpy as np
import jax
import jax.numpy as jnp
from jax import lax
from jax.experimental import pallas as pl
from jax.experimental.pallas import tpu as pltpu

F32 = jnp.float32
BF16 = jnp.bfloat16
U32 = jnp.uint32

HEAD_DIM = 64
N_HEADS = 8
WIDTH = N_HEADS * HEAD_DIM
DIL_BLOCK = 128
DILATIONS = (1, 4, 16)
T5_NUM_BUCKETS = 32
T5_MAX_DISTANCE = 2048
N_EXPERTS = 256
TOP_K = 8
N_GROUPS = 8
TOP_K_GROUPS = 4
ROUTED_SCALE = 2.5
EPS = 1e-6

LANES = 128
VMEM_BYTES = 64 * 1024 * 1024
VMEM_LIMIT = VMEM_BYTES * 7 // 8

HEAD_PAIRS = WIDTH // LANES
TILE_ROWS = 4
NEG = -1e30
LOG2E = float(np.log2(np.e))

MOD_TILE = 1536
ROW_TILE = 512
FORGET_BLOCK = 512
FOX_TILE = 512
DIL_UNROLL = 16
DIL_CHUNK = 512
FFN_BLOCK = 256
FFN_PER_STEP = 4
COMBINE_TILE = 256
SCATTER_TILE = 256
ROUTE_TILE = 512
ISSUE_UNROLL = 8


def _params(semantics):
    return pltpu.CompilerParams(dimension_semantics=semantics, vmem_limit_bytes=VMEM_LIMIT)


def _rms(x):
    return x * lax.rsqrt(jnp.mean(x * x, axis=-1, keepdims=True) + EPS)


def _silu(x):
    return x * jax.nn.sigmoid(x)


def _dot_nt(a, b):
    return lax.dot_general(a, b, (((1,), (1,)), ((), ())), preferred_element_type=F32)


def _mod_kernel(c_ref, w_ref, b_ref, o_ref):
    o_ref[...] = jnp.dot(_silu(c_ref[...]), w_ref[...], precision=lax.Precision.HIGHEST,
                         preferred_element_type=F32) + b_ref[...]


def _modulation(c, w_ada, b_ada):
    bsz, d = c.shape
    n = w_ada.shape[1]
    tn = MOD_TILE
    return pl.pallas_call(
        _mod_kernel,
        grid=(n // tn,),
        in_specs=[pl.BlockSpec((bsz, d), lambda j: (0, 0)),
                  pl.BlockSpec((d, tn), lambda j: (0, j)),
                  pl.BlockSpec((1, tn), lambda j: (0, j))],
        out_specs=pl.BlockSpec((bsz, tn), lambda j: (0, j)),
        out_shape=jax.ShapeDtypeStruct((bsz, n), F32),
        compiler_params=_params(("arbitrary",)),
        name="adaln_mod",
    )(c, w_ada, b_ada.reshape(1, n))


def _inproj_kernel(x_ref, sc_ref, sh_ref, wq_ref, wk_ref, wv_ref, wd_ref, wl_ref,
                   oq_ref, ok_ref, ov_ref, od_ref, ol_ref):
    h = _rms(x_ref[0]) * (1.0 + sc_ref[0]) + sh_ref[0]
    hb = h.astype(BF16)
    oq_ref[0] = _dot_nt(wq_ref[...], hb).astype(BF16)
    ov_ref[0] = _dot_nt(wv_ref[...], hb).astype(BF16)
    ok_ref[0] = jnp.dot(hb, wk_ref[...], preferred_element_type=F32).astype(BF16)
    od_ref[0] = jnp.dot(hb, wd_ref[...], preferred_element_type=F32).astype(BF16)
    ol_ref[0] = jnp.dot(hb, wl_ref[...], preferred_element_type=F32)


def _in_projection(x, scale1, shift1, wq_t, wk, wv_t, w_dil, w_flog):
    bsz, s, d = x.shape
    tm = ROW_TILE
    vec = pl.BlockSpec((1, 1, d), lambda b, i: (b, 0, 0))
    full = lambda w: pl.BlockSpec(w.shape, lambda b, i: (0, 0))
    row = lambda n: pl.BlockSpec((1, tm, n), lambda b, i: (b, i, 0))
    col = pl.BlockSpec((1, WIDTH, tm), lambda b, i: (b, 0, i))
    return pl.pallas_call(
        _inproj_kernel,
        grid=(bsz, s // tm),
        in_specs=[row(d), vec, vec, full(wq_t), full(wk), full(wv_t), full(w_dil), full(w_flog)],
        out_specs=[col, row(WIDTH), col, row(3 * WIDTH), row(LANES)],
        out_shape=[jax.ShapeDtypeStruct((bsz, WIDTH, s), BF16),
                   jax.ShapeDtypeStruct((bsz, s, WIDTH), BF16),
                   jax.ShapeDtypeStruct((bsz, WIDTH, s), BF16),
                   jax.ShapeDtypeStruct((bsz, s, 3 * WIDTH), BF16),
                   jax.ShapeDtypeStruct((bsz, s, LANES), F32)],
        compiler_params=_params(("parallel", "arbitrary")),
        name="norm1_inproj",
    )(x, scale1, shift1, wq_t, wk, wv_t, w_dil, w_flog)


FORGET_PARTS = 3


def _split_bf16(x):
    parts = []
    for _ in range(FORGET_PARTS):
        p = x.astype(BF16)
        parts.append(p)
        x = x - p.astype(F32)
    return parts


def _forget_kernel(f_ref, b_ref, tri_ref, place_ref, o_ref):
    blk = tri_ref.shape[0]
    carry = jnp.zeros((1, LANES), F32)
    for c in range(f_ref.shape[1] // blk):
        rows = pl.ds(c * blk, blk)
        z = f_ref[0, rows, :] + b_ref[...]
        ls = jnp.minimum(z, 0.0) - jnp.log1p(jnp.exp(-jnp.abs(z)))
        cum = carry
        for p in _split_bf16(ls):
            cum = cum + jnp.dot(tri_ref[...], p, preferred_element_type=F32)
        carry = cum[blk - 1:blk, :]
        out = None
        for n, p in enumerate(_split_bf16(cum * LOG2E)):
            d = jnp.dot(p, place_ref[n], preferred_element_type=F32)
            out = d if out is None else out + d
        o_ref[0, rows, :] = out.astype(BF16)


def _forget_cumsum(flog, b_forget):
    bsz, s, _ = flog.shape
    blk = FORGET_BLOCK
    place = np.zeros((FORGET_PARTS, LANES, WIDTH), np.float32)
    for h in range(N_HEADS):
        for n in range(FORGET_PARTS):
            place[n, h, (h // 2) * LANES + (h % 2) * HEAD_DIM + n] = 1.0
    return pl.pallas_call(
        _forget_kernel,
        grid=(bsz,),
        in_specs=[pl.BlockSpec((1, s, LANES), lambda b: (b, 0, 0)),
                  pl.BlockSpec((1, LANES), lambda b: (0, 0)),
                  pl.BlockSpec((blk, blk), lambda b: (0, 0)),
                  pl.BlockSpec(place.shape, lambda b: (0, 0, 0))],
        out_specs=pl.BlockSpec((1, s, WIDTH), lambda b: (b, 0, 0)),
        out_shape=jax.ShapeDtypeStruct((bsz, s, WIDTH), BF16),
        compiler_params=_params(("parallel",)),
        name="forget_cumsum",
    )(flog, jnp.pad(b_forget, (0, LANES - N_HEADS)).reshape(1, LANES),
      jnp.tril(jnp.ones((blk, blk), BF16)), jnp.asarray(place, BF16))


def _stack_heads(q):
    lane = lax.broadcasted_iota(jnp.int32, q.shape, 1)
    zero = jnp.zeros_like(q)
    return jnp.concatenate([jnp.where(lane < HEAD_DIM, q, zero), jnp.where(lane >= HEAD_DIM, q, zero)], axis=0)


def _unstack_heads(o2):
    rows = o2.shape[0] // 2
    lane = lax.broadcasted_iota(jnp.int32, (rows, LANES), 1)
    return jnp.where(lane < HEAD_DIM, o2[:rows], o2[rows:])


def _fox_kernel(q_ref, k_ref, fa_ref, v_ref, o_ref, m_s, l_s, acc_s, s_a, s_b, *, t):
    i = pl.program_id(2)
    q_t = q_ref[0]
    row = lax.broadcasted_iota(jnp.int32, (LANES, 2 * t), 0)
    col = lax.broadcasted_iota(jnp.int32, (LANES, 2 * t), 1)
    head_row = jnp.where(col < t, 0, HEAD_DIM)
    q2 = jnp.concatenate([q_t, q_t], axis=1)
    q2 = jnp.where((row >= head_row) & (row < head_row + HEAD_DIM), q2, jnp.zeros_like(q2))
    minus = jnp.where((row >= head_row) & (row < head_row + FORGET_PARTS), -1.0, 0.0).astype(BF16)
    qa = jnp.concatenate([q2, minus], axis=0)
    m_s[...] = jnp.full(m_s.shape, NEG, F32)
    l_s[...] = jnp.zeros(l_s.shape, F32)
    acc_s[...] = jnp.zeros(acc_s.shape, F32)

    def scores(ks, tk):
        kk = jnp.concatenate([k_ref[0, pl.ds(ks, tk), :], fa_ref[0, pl.ds(ks, tk), :]], axis=1)
        return jnp.dot(kk, qa, preferred_element_type=F32)

    def absorb(s, ks, tk):
        m_prev = m_s[...]
        m_new = jnp.maximum(m_prev, jnp.max(s, axis=0, keepdims=True))
        alpha = jnp.exp2(m_prev - m_new)
        p = jnp.exp2(s - m_new)
        l_s[...] = alpha * l_s[...] + jnp.sum(p, axis=0, keepdims=True)
        acc_s[...] = alpha * acc_s[...] + jnp.dot(v_ref[0, :, pl.ds(ks, tk)], p.astype(BF16),
                                                  preferred_element_type=F32)
        m_s[...] = m_new

    npair = i // 2

    def tile(j):
        return pl.multiple_of(j * 2 * t, 2 * t)

    @pl.when(npair > 0)
    def _():
        s_a[...] = scores(0, 2 * t)

        def body(jj, carry):
            j = 2 * jj
            s_b[...] = scores(tile(j + 1), 2 * t)
            absorb(s_a[...], tile(j), 2 * t)
            s_a[...] = scores(tile(jnp.minimum(j + 2, npair - 1)), 2 * t)
            absorb(s_b[...], tile(j + 1), 2 * t)
            return carry

        lax.fori_loop(0, npair // 2, body, 0)

        @pl.when(npair % 2 == 1)
        def _():
            absorb(s_a[...], tile(npair - 1), 2 * t)

    def last(ks, tk):
        r = lax.broadcasted_iota(jnp.int32, (tk, 2 * t), 0)
        c = lax.broadcasted_iota(jnp.int32, (tk, 2 * t), 1)
        absorb(jnp.where(r <= jnp.where(c >= t, c - t, c) + (tk - t), scores(ks, tk), NEG), ks, tk)

    @pl.when(i % 2 == 1)
    def _():
        last(tile(npair), 2 * t)

    @pl.when(i % 2 == 0)
    def _():
        last(pl.multiple_of(i * t, t), t)

    o2 = acc_s[...] / l_s[...]
    o_t = jnp.where(lax.broadcasted_iota(jnp.int32, (LANES, t), 0) < HEAD_DIM, o2[:, :t], o2[:, t:])
    o_ref[0] = o_t.T


def _fox_attention(q_t, k, f_aug, v_t):
    bsz, s, _ = k.shape
    t = FOX_TILE
    keys = pl.BlockSpec((1, s, LANES), lambda b, h, i: (b, 0, h))
    return pl.pallas_call(
        functools.partial(_fox_kernel, t=t),
        grid=(bsz, HEAD_PAIRS, s // t),
        in_specs=[pl.BlockSpec((1, LANES, t), lambda b, h, i: (b, h, i)),
                  keys, keys,
                  pl.BlockSpec((1, LANES, s), lambda b, h, i: (b, h, 0))],
        out_specs=pl.BlockSpec((1, t, LANES), lambda b, h, i: (b, i, h)),
        out_shape=jax.ShapeDtypeStruct((bsz, s, WIDTH), F32),
        scratch_shapes=[pltpu.VMEM((1, 2 * t), F32), pltpu.VMEM((1, 2 * t), F32),
                        pltpu.VMEM((LANES, 2 * t), F32), pltpu.VMEM((2 * t, 2 * t), F32),
                        pltpu.VMEM((2 * t, 2 * t), F32)],
        compiler_params=_params(("parallel", "parallel", "arbitrary")),
        name="fox_attention",
    )(q_t, k, f_aug, v_t)


def _t5_bucket(dist):
    max_exact = T5_NUM_BUCKETS // 2
    d = np.maximum(dist, 1).astype(np.float32)
    large = max_exact + (np.log(d / max_exact) / np.log(T5_MAX_DISTANCE / max_exact)
                         * (T5_NUM_BUCKETS - max_exact)).astype(np.int32)
    large = np.minimum(large, T5_NUM_BUCKETS - 1)
    return np.where(dist < max_exact, dist, large).astype(np.int32)


def _dilated_bias(rel_bias):
    blk = DIL_BLOCK
    period = 3 * blk
    m = np.arange(period)
    rel = np.where(m < 2 * blk, blk - m, blk - (m - period))
    band = (rel >= 0) & (rel <= blk)
    onehot = np.zeros((len(DILATIONS), period, T5_NUM_BUCKETS), np.float32)
    for bi, dil in enumerate(DILATIONS):
        onehot[bi, m, _t5_bucket(np.clip(rel, 0, blk) * dil)] = 1.0
    w = jnp.einsum('bmk,kh->bhm', jnp.asarray(onehot), rel_bias.astype(F32),
                   precision=lax.Precision.HIGHEST)
    w = jnp.where(jnp.asarray(band), w, NEG)
    flat = jnp.tile(w, (1, 1, blk))[:, :, :blk * (period - 1)]
    table = flat.reshape(len(DILATIONS), N_HEADS, blk, period - 1)[..., :2 * blk]
    table = table.reshape(len(DILATIONS), HEAD_PAIRS, 2 * blk, 2 * blk)
    first = jnp.where(jnp.arange(2 * blk) < blk, NEG, table)
    return jnp.stack([table, first], axis=1)


def _dil_kernel(q_ref, k_ref, v_ref, bias_ref, o_ref, qf, kf, vf, ob0, ob1, ob2, ls0, ls1, ls2, *, s_len):
    blk = DIL_BLOCK
    qf[...] = q_ref[0].astype(F32)
    kf[...] = k_ref[0].astype(F32)
    vf[...] = v_ref[0].astype(F32)
    for bi, (dil, ob, ls) in enumerate(zip(DILATIONS, (ob0, ob1, ob2), (ls0, ls1, ls2))):
        span = blk * dil
        nb = s_len // span

        def rows(start, dil=dil):
            return pl.ds(start, blk) if dil == 1 else pl.ds(start, blk, stride=dil)

        def block(tix, carry, bi=bi, span=span, nb=nb, ob=ob, ls=ls, rows=rows):
            n = tix % nb
            start = n * span + tix // nb
            prev = jnp.maximum(start - span, 0)
            q2 = _stack_heads(qf[rows(start), :].astype(BF16))
            kk = jnp.concatenate([kf[rows(prev), :], kf[rows(start), :]], axis=0).astype(BF16)
            vv = jnp.concatenate([vf[rows(prev), :], vf[rows(start), :]], axis=0).astype(BF16)
            s = lax.dot_general(q2, kk, (((1,), (1,)), ((), ())), preferred_element_type=F32)
            s = s + bias_ref[bi, jnp.where(n == 0, 1, 0), 0]
            m = jnp.max(s, axis=1, keepdims=True)
            p = jnp.exp(s - m)
            l = jnp.sum(p, axis=1, keepdims=True)
            o2 = jnp.dot(p.astype(BF16), vv, preferred_element_type=F32) / l
            lse = jnp.broadcast_to(m + jnp.log(l), (2 * blk, LANES))
            ob[rows(start), :] = _unstack_heads(o2)
            ls[rows(start), :] = _unstack_heads(lse)
            return carry

        def blocks(g, carry, block=block):
            for u in range(DIL_UNROLL):
                block(g * DIL_UNROLL + u, carry)
            return carry

        lax.fori_loop(0, dil * nb // DIL_UNROLL, blocks, 0)

    chunk = DIL_CHUNK
    for c in range(s_len // chunk):
        r = pl.ds(c * chunk, chunk)
        l0, l1, l2 = ls0[r, :], ls1[r, :], ls2[r, :]
        mx = jnp.maximum(jnp.maximum(l0, l1), l2)
        e0, e1, e2 = jnp.exp(l0 - mx), jnp.exp(l1 - mx), jnp.exp(l2 - mx)
        o_ref[0, r, :] = (e0 * ob0[r, :] + e1 * ob1[r, :] + e2 * ob2[r, :]) / (e0 + e1 + e2)


def _dilated_attention(qkv, bias):
    bsz, s, _ = qkv.shape
    col = lambda off: pl.BlockSpec((1, s, LANES), lambda b, h: (b, 0, off + h))
    buf = pltpu.VMEM((s, LANES), F32)
    return pl.pallas_call(
        functools.partial(_dil_kernel, s_len=s),
        grid=(bsz, HEAD_PAIRS),
        in_specs=[col(0), col(HEAD_PAIRS), col(2 * HEAD_PAIRS),
                  pl.BlockSpec((len(DILATIONS), 2, 1, 2 * DIL_BLOCK, 2 * DIL_BLOCK), lambda b, h: (0, 0, h, 0, 0))],
        out_specs=pl.BlockSpec((1, s, LANES), lambda b, h: (b, 0, h)),
        out_shape=jax.ShapeDtypeStruct((bsz, s, WIDTH), F32),
        scratch_shapes=[buf] * 9,
        compiler_params=_params(("parallel", "arbitrary")),
        name="dilated_attention",
    )(qkv, qkv, qkv, bias)


def _bf16_bits(x):
    return pltpu.bitcast(x.astype(BF16).astype(F32), U32)


def _to_row_tiles(dst_ref, x, base=0):
    rows, d = x.shape
    for c in range(TILE_ROWS):
        lo = _bf16_bits(x[:, c * LANES:(c + 1) * LANES]) >> 16
        hi = _bf16_bits(x[:, d // 2 + c * LANES:d // 2 + (c + 1) * LANES])
        dst_ref[pl.ds(base + c, rows, stride=TILE_ROWS), :] = lo | hi


def _from_row_tiles(src_ref, rows, base=0):
    lo, hi = [], []
    for c in range(TILE_ROWS):
        w = src_ref[pl.ds(base + c, rows, stride=TILE_ROWS), :]
        lo.append(pltpu.bitcast(w << 16, F32))
        hi.append(pltpu.bitcast(w & jnp.uint32(0xFFFF0000), F32))
    return lo + hi


def _outproj_kernel(yf_ref, yd_ref, x_ref, g1_ref, sc_ref, sh_ref, g2_ref, gf_ref, gd_ref, wo1_ref, wo2_ref,
                    wrh_ref, wrl_ref, wsg_ref, wsu_ref, wsd_ref, xp_ref, h3_ref, lg_ref):
    nf = (_rms(yf_ref[0]) * gf_ref[...]).astype(BF16)
    nd = (_rms(yd_ref[0]) * gd_ref[...]).astype(BF16)
    mix = (jnp.dot(nf, wo1_ref[...], preferred_element_type=F32)
           + jnp.dot(nd, wo2_ref[...], preferred_element_type=F32))
    x1 = x_ref[0] + g1_ref[0] * mix
    h2 = _rms(x1) * (1.0 + sc_ref[0]) + sh_ref[0]
    hb = h2.astype(BF16)
    hl = (h2 - hb.astype(F32)).astype(BF16)
    lg_ref[...] = _dot_nt(wrh_ref[...], hb) + _dot_nt(wrh_ref[...], hl) + _dot_nt(wrl_ref[...], hb)
    act = _silu(jnp.dot(hb, wsg_ref[...], preferred_element_type=F32)) * jnp.dot(
        hb, wsu_ref[...], preferred_element_type=F32)
    shared = jnp.dot(act.astype(BF16), wsd_ref[...], preferred_element_type=F32)
    xp_ref[0] = x1 + g2_ref[0] * shared
    _to_row_tiles(h3_ref, h2)


def _out_projection(y_fox, y_dil, x, gate1, scale2, shift2, gate2, g_fox, g_dil, wo1, wo2, wr_hi, wr_lo,
                    wsg, wsu, wsd):
    bsz, s, d = x.shape
    tm = ROW_TILE
    nt = s // tm
    vec = pl.BlockSpec((1, 1, d), lambda b, i: (b, 0, 0))
    full = lambda w: pl.BlockSpec(w.shape, lambda b, i: (0,) * w.ndim)
    row = lambda n: pl.BlockSpec((1, tm, n), lambda b, i: (b, i, 0))
    return pl.pallas_call(
        _outproj_kernel,
        grid=(bsz, nt),
        in_specs=[row(WIDTH), row(WIDTH), row(d), vec, vec, vec, vec, full(g_fox), full(g_dil), full(wo1),
                  full(wo2), full(wr_hi), full(wr_lo), full(wsg), full(wsu), full(wsd)],
        out_specs=[row(d),
                   pl.BlockSpec((tm * TILE_ROWS, LANES), lambda b, i: (b * nt + i, 0)),
                   pl.BlockSpec((N_EXPERTS, tm), lambda b, i: (0, b * nt + i))],
        out_shape=[jax.ShapeDtypeStruct((bsz, s, d), F32),
                   jax.ShapeDtypeStruct((bsz * s * TILE_ROWS, LANES), U32),
                   jax.ShapeDtypeStruct((N_EXPERTS, bsz * s), F32)],
        compiler_params=_params(("parallel", "arbitrary")),
        name="outproj_norm2_router_shared",
    )(y_fox, y_dil, x, gate1, scale2, shift2, gate2, g_fox, g_dil, wo1, wo2, wr_hi, wr_lo, wsg, wsu, wsd)


def _first_argmax(v, row, size):
    m = jnp.max(v, axis=0, keepdims=True)
    return m, jnp.min(jnp.where(v == m, row, size), axis=0, keepdims=True)


def _route_kernel(lg_ref, rb_ref, tri_ref, e_ref, w_ref, rk_ref, cnt_ref, cnt_s):
    @pl.when(pl.program_id(0) == 0)
    def _():
        cnt_s[...] = jnp.zeros(cnt_s.shape, F32)

    t = lg_ref.shape[1]
    gsz = N_EXPERTS // N_GROUPS
    scores = jax.nn.sigmoid(lg_ref[...])
    sel = scores + rb_ref[...]
    row_g = lax.broadcasted_iota(jnp.int32, (gsz, t), 0)
    grp = []
    for g in range(N_GROUPS):
        v = sel[g * gsz:(g + 1) * gsz]
        m1, i1 = _first_argmax(v, row_g, gsz)
        m2 = jnp.max(jnp.where(row_g == i1, -jnp.inf, v), axis=0, keepdims=True)
        grp.append(m1 + m2)
    gv = jnp.concatenate(grp, axis=0)
    row8 = lax.broadcasted_iota(jnp.int32, (N_GROUPS, t), 0)
    pen = jnp.full((N_GROUPS, t), -jnp.inf, F32)
    for _ in range(TOP_K_GROUPS):
        _, ix = _first_argmax(gv, row8, N_GROUPS)
        pen = jnp.where(row8 == ix, 0.0, pen)
        gv = jnp.where(row8 == ix, -jnp.inf, gv)
    selm = jnp.concatenate([sel[g * gsz:(g + 1) * gsz] + pen[g:g + 1] for g in range(N_GROUPS)], axis=0)

    row = lax.broadcasted_iota(jnp.int32, (N_EXPERTS, t), 0)
    v = selm
    idxs, scs = [], []
    for _ in range(TOP_K):
        _, ix = _first_argmax(v, row, N_EXPERTS)
        hit = row == ix
        idxs.append(ix)
        scs.append(jnp.sum(jnp.where(hit, scores, 0.0), axis=0, keepdims=True))
        v = jnp.where(hit, -jnp.inf, v)
    chosen = jnp.where(v != selm, 1.0, 0.0)
    before = jnp.dot(chosen.astype(BF16), tri_ref[...], preferred_element_type=F32) + cnt_s[...]
    rks = [jnp.sum(jnp.where(row == ix, before, 0.0), axis=0, keepdims=True) for ix in idxs]
    sc = jnp.concatenate(scs, axis=0)
    e_ref[...] = jnp.concatenate(idxs, axis=0)
    w_ref[...] = sc / jnp.sum(sc, axis=0, keepdims=True) * ROUTED_SCALE
    rk_ref[...] = jnp.concatenate(rks, axis=0).astype(jnp.int32)
    cnt_s[...] = cnt_s[...] + jnp.sum(chosen, axis=1, keepdims=True)
    cnt_ref[...] = cnt_s[...]


def _route(logits_t, router_bias):
    e, n = logits_t.shape
    t = ROUTE_TILE
    tri = jnp.triu(jnp.ones((t, t), BF16), k=1)
    tile = pl.BlockSpec((TOP_K, t), lambda i: (0, i))
    return pl.pallas_call(
        _route_kernel,
        grid=(n // t,),
        in_specs=[pl.BlockSpec((e, t), lambda i: (0, i)),
                  pl.BlockSpec((e, 1), lambda i: (0, 0)),
                  pl.BlockSpec((t, t), lambda i: (0, 0))],
        out_specs=[tile, tile, tile, pl.BlockSpec((e, 1), lambda i: (0, 0))],
        out_shape=[jax.ShapeDtypeStruct((TOP_K, n), jnp.int32), jax.ShapeDtypeStruct((TOP_K, n), F32),
                   jax.ShapeDtypeStruct((TOP_K, n), jnp.int32), jax.ShapeDtypeStruct((e, 1), F32)],
        scratch_shapes=[pltpu.VMEM((e, 1), F32)],
        compiler_params=_params(("arbitrary",)),
        name="route_topk_rank",
    )(logits_t, router_bias.reshape(e, 1).astype(F32), tri)


def _dest_kernel(e_ref, rk_ref, ps_ref, d_ref):
    t = e_ref.shape[1]
    row = lax.broadcasted_iota(jnp.int32, (N_EXPERTS, t), 0)
    ps = ps_ref[...]
    base = [jnp.sum(jnp.where(row == e_ref[k:k + 1, :], ps, 0.0), axis=0, keepdims=True) for k in range(TOP_K)]
    d_ref[...] = jnp.concatenate(base, axis=0).astype(jnp.int32) + rk_ref[...]


def _dest_rows(eidx, rank, pstart):
    _, n = eidx.shape
    t = ROUTE_TILE
    tile = pl.BlockSpec((TOP_K, t), lambda i: (0, i))
    return pl.pallas_call(
        _dest_kernel,
        grid=(n // t,),
        in_specs=[tile, tile, pl.BlockSpec((N_EXPERTS, 1), lambda i: (0, 0))],
        out_specs=tile,
        out_shape=jax.ShapeDtypeStruct((TOP_K, n), jnp.int32),
        compiler_params=_params(("parallel",)),
        name="dest_rows",
    )(eidx, rank, pstart.astype(F32).reshape(N_EXPERTS, 1))


def _block_plan(counts, nblk):
    counts = counts.reshape(N_EXPERTS).astype(jnp.int32)
    padded = (counts + FFN_BLOCK - 1) // FFN_BLOCK * FFN_BLOCK
    pend = jnp.cumsum(padded).astype(jnp.int32)
    pstart = pend - padded
    first_row = jnp.arange(nblk, dtype=jnp.int32) * FFN_BLOCK
    block_e = jnp.minimum(jnp.sum(pend[None, :] <= first_row[:, None], axis=1), N_EXPERTS - 1).astype(jnp.int32)
    nused = (pend[-1:] // FFN_BLOCK).astype(jnp.int32)
    ids = jnp.arange(N_EXPERTS, dtype=jnp.int32)
    following = lax.cummin(jnp.where(counts > 0, ids, N_EXPERTS), reverse=True)
    following = jnp.concatenate([following[1:], jnp.full((1,), N_EXPERTS, jnp.int32)])
    following = jnp.where(following >= N_EXPERTS, -1, following)
    next_e = jnp.sum(jnp.where(block_e[:, None] == ids[None, :], following[None, :], 0), axis=1).astype(jnp.int32)
    return pstart, pstart + counts, pend, block_e, next_e, nused


def _tile_copy(src_ref, src_row, dst_ref, dst_row, sem):
    return pltpu.make_async_copy(src_ref.at[pl.ds(pl.multiple_of(src_row * TILE_ROWS, TILE_ROWS), TILE_ROWS), :],
                                 dst_ref.at[pl.ds(pl.multiple_of(dst_row * TILE_ROWS, TILE_ROWS), TILE_ROWS), :],
                                 sem)


def _scatter_kernel(cend_ref, pend_ref, dst_ref, h3_ref, z_ref, xs_hbm, st0, st1, sem, zsem):
    i = pl.program_id(0)
    nsteps = pl.num_programs(0)
    tm = dst_ref.shape[1]
    stage = (st0, st1)

    def zero_copy(r):
        return pltpu.make_async_copy(z_ref, xs_hbm.at[pl.ds(pl.multiple_of(r * TILE_ROWS, TILE_ROWS), TILE_ROWS), :],
                                     zsem)

    def wait_slot(slot):
        for _ in range(TOP_K):
            pltpu.make_async_copy(stage[slot], xs_hbm.at[pl.ds(0, tm * TILE_ROWS), :], sem.at[slot]).wait()

    @pl.when(i == 0)
    def _():
        def start_row(r, c):
            zero_copy(r).start()
            return c

        def wait_row(r, c):
            zero_copy(r).wait()
            return c

        def per_expert(e, carry):
            lax.fori_loop(cend_ref[e], pend_ref[e], start_row, carry)
            prev = jnp.maximum(e - 1, 0)
            return lax.fori_loop(cend_ref[prev], jnp.where(e > 0, pend_ref[prev], cend_ref[prev]), wait_row, carry)

        lax.fori_loop(0, N_EXPERTS, per_expert, 0)
        lax.fori_loop(cend_ref[N_EXPERTS - 1], pend_ref[N_EXPERTS - 1], wait_row, 0)

    for slot in range(2):
        @pl.when(i % 2 == slot)
        def _(slot=slot):
            @pl.when(i >= 2)
            def _():
                wait_slot(slot)
            stage[slot][...] = h3_ref[...]
            for k in range(TOP_K):
                def body(jj, carry, k=k):
                    for u in range(ISSUE_UNROLL):
                        j = jj * ISSUE_UNROLL + u
                        _tile_copy(stage[slot], j, xs_hbm, dst_ref[k, j], sem.at[slot]).start()
                    return carry
                lax.fori_loop(0, tm // ISSUE_UNROLL, body, 0)

    @pl.when(i == nsteps - 1)
    def _():
        wait_slot(0)
        wait_slot(1)


def _dispatch_rows(h3, dest, cend, pend, rows):
    _, n = dest.shape
    tm = SCATTER_TILE
    assert n // tm >= 2
    stage = pltpu.VMEM((tm * TILE_ROWS, LANES), U32)
    grid_spec = pltpu.PrefetchScalarGridSpec(
        num_scalar_prefetch=2,
        grid=(n // tm,),
        in_specs=[pl.BlockSpec((TOP_K, tm), lambda i, ce, pe: (0, i), memory_space=pltpu.SMEM),
                  pl.BlockSpec((tm * TILE_ROWS, LANES), lambda i, ce, pe: (i, 0)),
                  pl.BlockSpec((TILE_ROWS, LANES), lambda i, ce, pe: (0, 0))],
        out_specs=pl.BlockSpec(memory_space=pl.ANY),
        scratch_shapes=[stage, stage, pltpu.SemaphoreType.DMA((2,)), pltpu.SemaphoreType.DMA],
    )
    return pl.pallas_call(
        _scatter_kernel,
        grid_spec=grid_spec,
        out_shape=jax.ShapeDtypeStruct((rows * TILE_ROWS, LANES), U32),
        compiler_params=_params(("arbitrary",)),
        name="dispatch_scatter",
    )(cend, pend, dest, h3, jnp.zeros((TILE_ROWS, LANES), U32))


def _ffn_kernel(be_ref, nx_ref, nu_ref, x_ref, wg_hbm, wu_hbm, wd_hbm, y_ref,
                wgf, wuf, wdf, wgb, wub, wdb, sem, run):
    first = pl.program_id(0) * FFN_PER_STEP
    nused = nu_ref[0]

    def fetch(e, slot):
        return [pltpu.make_async_copy(src.at[e], dst.at[slot], sem.at[slot])
                for src, dst in ((wg_hbm, wgf), (wu_hbm, wuf), (wd_hbm, wdf))]

    @pl.when(first == 0)
    def _():
        run[0] = -1
        for c in fetch(be_ref[0], 0):
            c.start()

    slots = []
    for part in range(FFN_PER_STEP):
        blk = first + part

        @pl.when((blk < nused) & ((blk == 0) | (be_ref[blk] != be_ref[jnp.maximum(blk - 1, 0)])))
        def _(blk=blk):
            r = run[0] + 1
            run[0] = r
            fslot, slot = r % 2, r % FFN_PER_STEP
            for c in fetch(be_ref[blk], fslot):
                c.wait()
            wgb[slot] = wgf[fslot].astype(BF16)
            wub[slot] = wuf[fslot].astype(BF16)
            wdb[slot] = wdf[fslot].astype(BF16)

            @pl.when(nx_ref[blk] >= 0)
            def _():
                for c in fetch(nx_ref[blk], 1 - fslot):
                    c.start()

        slots.append(run[0] % FFN_PER_STEP)

    @pl.when(first < nused)
    def _():
        for part, slot in enumerate(slots):
            base = part * FFN_BLOCK * TILE_ROWS
            x = jnp.concatenate([c.astype(BF16) for c in _from_row_tiles(x_ref, FFN_BLOCK, base)], axis=1)
            act = _silu(jnp.dot(x, wgb[slot], preferred_element_type=F32)) * jnp.dot(
                x, wub[slot], preferred_element_type=F32)
            _to_row_tiles(y_ref, jnp.dot(act.astype(BF16), wdb[slot], preferred_element_type=F32), base)

    @pl.when(first >= nused)
    def _():
        y_ref[...] = jnp.zeros(y_ref.shape, U32)


def _routed_experts(xs, block_e, next_e, nused, w_gate, w_up, w_down):
    rows = xs.shape[0] // TILE_ROWS
    nblk = rows // FFN_BLOCK
    _, d, hid = w_gate.shape
    step_rows = FFN_PER_STEP * FFN_BLOCK * TILE_ROWS
    hbm = pl.BlockSpec(memory_space=pl.ANY)
    grid_spec = pltpu.PrefetchScalarGridSpec(
        num_scalar_prefetch=3,
        grid=(nblk // FFN_PER_STEP,),
        in_specs=[pl.BlockSpec((step_rows, LANES),
                               lambda i, be, nx, nu: (jnp.minimum(i, (nu[0] - 1) // FFN_PER_STEP), 0)),
                  hbm, hbm, hbm],
        out_specs=pl.BlockSpec((step_rows, LANES), lambda i, be, nx, nu: (i, 0)),
        scratch_shapes=[pltpu.VMEM((2, d, hid), F32), pltpu.VMEM((2, d, hid), F32), pltpu.VMEM((2, hid, d), F32),
                        pltpu.VMEM((FFN_PER_STEP, d, hid), BF16), pltpu.VMEM((FFN_PER_STEP, d, hid), BF16),
                        pltpu.VMEM((FFN_PER_STEP, hid, d), BF16),
                        pltpu.SemaphoreType.DMA((2,)), pltpu.SMEM((1,), jnp.int32)],
    )
    return pl.pallas_call(
        _ffn_kernel,
        grid_spec=grid_spec,
        out_shape=jax.ShapeDtypeStruct((rows * TILE_ROWS, LANES), U32),
        compiler_params=_params(("arbitrary",)),
        name="routed_experts",
    )(block_e, next_e, nused, xs, w_gate, w_up, w_down)


def _gather_start(idx_ref, src_hbm, dst_vmem, sem):
    nk, tm = idx_ref.shape
    for k in range(nk):
        def body(jj, carry, k=k):
            for u in range(ISSUE_UNROLL):
                j = jj * ISSUE_UNROLL + u
                _tile_copy(src_hbm, idx_ref[k, j], dst_vmem, k * tm + j, sem).start()
            return carry
        lax.fori_loop(0, tm // ISSUE_UNROLL, body, 0)


def _gather_wait(src_hbm, dst_vmem, sem):
    pltpu.make_async_copy(src_hbm.at[pl.ds(0, dst_vmem.shape[0]), :], dst_vmem, sem).wait()


def _combine_kernel(dst_ref, dstn_ref, wt_ref, xp_ref, g2_ref, gfin_ref, ys_hbm, o_ref, yb0, yb1, sem, *, nsteps):
    i = pl.program_id(0)
    tm = COMBINE_TILE
    bufs = (yb0, yb1)

    @pl.when(i == 0)
    def _():
        _gather_start(dst_ref, ys_hbm, yb0, sem.at[0])

    for slot in range(2):
        @pl.when((i + 1 < nsteps) & ((i + 1) % 2 == slot))
        def _(slot=slot):
            _gather_start(dstn_ref, ys_hbm, bufs[slot], sem.at[slot])

    for slot in range(2):
        @pl.when(i % 2 == slot)
        def _(slot=slot):
            _gather_wait(ys_hbm, bufs[slot], sem.at[slot])
            wt = wt_ref[...]
            cols = None
            for k in range(TOP_K):
                part = [c * wt[:, k:k + 1] for c in _from_row_tiles(bufs[slot], tm, base=k * tm * TILE_ROWS)]
                cols = part if cols is None else [a + b for a, b in zip(cols, part)]
            routed = jnp.concatenate(cols, axis=1)
            o_ref[0] = _rms(xp_ref[0] + g2_ref[0] * routed) * gfin_ref[...]


def _combine(ys, dest, wts_t, xp, gate2, g_final):
    bsz, s, d = xp.shape
    tm = COMBINE_TILE
    nt = s // tm
    nsteps = bsz * nt
    vec = pl.BlockSpec((1, 1, d), lambda i: (i // nt, 0, 0))
    return pl.pallas_call(
        functools.partial(_combine_kernel, nsteps=nsteps),
        grid=(nsteps,),
        in_specs=[pl.BlockSpec((TOP_K, tm), lambda i: (0, i), memory_space=pltpu.SMEM),
                  pl.BlockSpec((TOP_K, tm), lambda i: (0, jnp.minimum(i + 1, nsteps - 1)),
                               memory_space=pltpu.SMEM),
                  pl.BlockSpec((tm, TOP_K), lambda i: (i, 0)),
                  pl.BlockSpec((1, tm, d), lambda i: (i // nt, i % nt, 0)),
                  vec,
                  pl.BlockSpec((1, d), lambda i: (0, 0)),
                  pl.BlockSpec(memory_space=pl.ANY)],
        out_specs=pl.BlockSpec((1, tm, d), lambda i: (i // nt, i % nt, 0)),
        out_shape=jax.ShapeDtypeStruct((bsz, s, d), F32),
        scratch_shapes=[pltpu.VMEM((tm * TOP_K * TILE_ROWS, LANES), U32),
                        pltpu.VMEM((tm * TOP_K * TILE_ROWS, LANES), U32),
                        pltpu.SemaphoreType.DMA((2,))],
        compiler_params=_params(("arbitrary",)),
        name="combine_final_norm",
    )(dest, dest, wts_t, xp, gate2, g_final.reshape(1, d), ys)


def kernel(x, c, w_in, b_forget, g_fox_out, g_dil_out, w_out, w_ada, b_ada, w_router, router_bias,
           w_exp_gate, w_exp_up, w_exp_down, w_sh_gate, w_sh_up, w_sh_down, rel_bias, g_final):
    bsz, s, d = x.shape
    depth = w_in.shape[0]
    assert depth == 1 and d == 2 * TILE_ROWS * LANES and s % (DIL_BLOCK * DILATIONS[-1]) == 0
    assert all(s % t == 0 for t in (ROW_TILE, FORGET_BLOCK, FOX_TILE, DIL_CHUNK, COMBINE_TILE, SCATTER_TILE))
    assert (bsz * s) % ROUTE_TILE == 0 and (bsz * s * TOP_K // FFN_BLOCK + N_EXPERTS) % FFN_PER_STEP == 0
    l = 0
    mod = _modulation(c, w_ada[l], b_ada[l])
    shift1, scale1, gate1, shift2, scale2, gate2 = [m[:, None, :] for m in jnp.split(mod, 6, axis=-1)]

    qscale = HEAD_DIM ** -0.5
    o3 = 3 * WIDTH
    w = w_in[l]
    wq_t = (w[:, :WIDTH] * (qscale * LOG2E)).T.astype(BF16)
    wk = w[:, WIDTH:2 * WIDTH].astype(BF16)
    wv_t = w[:, 2 * WIDTH:o3].T.astype(BF16)
    w_flog = jnp.pad(w[:, o3:o3 + N_HEADS], ((0, 0), (0, LANES - N_HEADS))).astype(BF16)
    wd0 = o3 + N_HEADS
    w_dil = jnp.concatenate([w[:, wd0:wd0 + WIDTH] * qscale, w[:, wd0 + WIDTH:]], axis=1).astype(BF16)

    q_t, k_f, v_t, qkv_d, flog = _in_projection(x, scale1, shift1, wq_t, wk, wv_t, w_dil, w_flog)
    y_fox = _fox_attention(q_t, k_f, _forget_cumsum(flog, b_forget[l]), v_t)
    y_dil = _dilated_attention(qkv_d, _dilated_bias(rel_bias))

    wr = w_router[l].T
    wr_hi = wr.astype(BF16)
    wr_lo = (wr - wr_hi.astype(F32)).astype(BF16)
    wo = w_out[l].astype(BF16)
    xp, h3, logits_t = _out_projection(
        y_fox, y_dil, x, gate1, scale2, shift2, gate2, g_fox_out[l].reshape(1, WIDTH),
        g_dil_out[l].reshape(1, WIDTH), wo[:WIDTH], wo[WIDTH:], wr_hi, wr_lo,
        w_sh_gate[l].astype(BF16), w_sh_up[l].astype(BF16), w_sh_down[l].astype(BF16))

    eidx, wts, rank, counts = _route(logits_t, router_bias[l])
    rows = bsz * s * TOP_K + N_EXPERTS * FFN_BLOCK
    pstart, cend, pend, block_e, next_e, nused = _block_plan(counts, rows // FFN_BLOCK)
    dest = _dest_rows(eidx, rank, pstart)
    xs = _dispatch_rows(h3, dest, cend, pend, rows)
    ys = _routed_experts(xs, block_e, next_e, nused, w_exp_gate[l], w_exp_up[l], w_exp_down[l])
    return _combine(ys, dest, wts.T, xp, gate2, g_final)
```

```python
import functools

import numpy as np
import jax
import jax.numpy as jnp
from jax import lax
from jax.experimental import pallas as pl
from jax.experimental.pallas import tpu as pltpu

F32 = jnp.float32
BF16 = jnp.bfloat16
U32 = jnp.uint32

HEAD_DIM = 64
N_HEADS = 8
WIDTH = N_HEADS * HEAD_DIM
DIL_BLOCK = 128
DILATIONS = (1, 4, 16)
T5_NUM_BUCKETS = 32
T5_MAX_DISTANCE = 2048
N_EXPERTS = 256
TOP_K = 8
N_GROUPS = 8
TOP_K_GROUPS = 4
ROUTED_SCALE = 2.5
EPS = 1e-6

LANES = 128
VMEM_BYTES = 64 * 1024 * 1024
VMEM_LIMIT = VMEM_BYTES * 7 // 8

HEAD_PAIRS = WIDTH // LANES
TILE_ROWS = 4
NEG = -1e30
LOG2E = float(np.log2(np.e))

MOD_TILE = 1536
ROW_TILE = 512
FORGET_BLOCK = 512
FOX_TILE = 512
DIL_UNROLL = 16
DIL_CHUNK = 512
FFN_BLOCK = 256
FFN_PER_STEP = 4
ZERO_CHUNKS = tuple(1 << b for b in reversed(range(FFN_BLOCK.bit_length() - 1)))
COMBINE_TILE = 256
SCATTER_TILE = 256
ROUTE_TILE = 512
ISSUE_UNROLL = 8


def _params(semantics):
    return pltpu.CompilerParams(dimension_semantics=semantics, vmem_limit_bytes=VMEM_LIMIT)


def _rms(x):
    return x * lax.rsqrt(jnp.mean(x * x, axis=-1, keepdims=True) + EPS)


def _silu(x):
    return x * jax.nn.sigmoid(x)


def _dot_nt(a, b):
    return lax.dot_general(a, b, (((1,), (1,)), ((), ())), preferred_element_type=F32)


def _mod_kernel(c_ref, w_ref, b_ref, o_ref):
    o_ref[...] = jnp.dot(_silu(c_ref[...]), w_ref[...], precision=lax.Precision.HIGHEST,
                         preferred_element_type=F32) + b_ref[...]


def _modulation(c, w_ada, b_ada):
    bsz, d = c.shape
    n = w_ada.shape[1]
    tn = MOD_TILE
    return pl.pallas_call(
        _mod_kernel,
        grid=(n // tn,),
        in_specs=[pl.BlockSpec((bsz, d), lambda j: (0, 0)),
                  pl.BlockSpec((d, tn), lambda j: (0, j)),
                  pl.BlockSpec((1, tn), lambda j: (0, j))],
        out_specs=pl.BlockSpec((bsz, tn), lambda j: (0, j)),
        out_shape=jax.ShapeDtypeStruct((bsz, n), F32),
        compiler_params=_params(("arbitrary",)),
        name="adaln_mod",
    )(c, w_ada, b_ada.reshape(1, n))


def _inproj_kernel(x_ref, sc_ref, sh_ref, wq_ref, wk_ref, wv_ref, wd_ref, wl_ref,
                   oq_ref, ok_ref, ov_ref, od_ref, ol_ref):
    h = _rms(x_ref[0]) * (1.0 + sc_ref[0]) + sh_ref[0]
    hb = h.astype(BF16)
    oq_ref[0] = _dot_nt(wq_ref[...], hb).astype(BF16)
    ov_ref[0] = _dot_nt(wv_ref[...], hb).astype(BF16)
    ok_ref[0] = jnp.dot(hb, wk_ref[...], preferred_element_type=F32).astype(BF16)
    od_ref[0] = jnp.dot(hb, wd_ref[...], preferred_element_type=F32).astype(BF16)
    ol_ref[0] = jnp.dot(hb, wl_ref[...], preferred_element_type=F32)


def _in_projection(x, scale1, shift1, wq_t, wk, wv_t, w_dil, w_flog):
    bsz, s, d = x.shape
    tm = ROW_TILE
    vec = pl.BlockSpec((1, 1, d), lambda b, i: (b, 0, 0))
    full = lambda w: pl.BlockSpec(w.shape, lambda b, i: (0, 0))
    row = lambda n: pl.BlockSpec((1, tm, n), lambda b, i: (b, i, 0))
    col = pl.BlockSpec((1, WIDTH, tm), lambda b, i: (b, 0, i))
    return pl.pallas_call(
        _inproj_kernel,
        grid=(bsz, s // tm),
        in_specs=[row(d), vec, vec, full(wq_t), full(wk), full(wv_t), full(w_dil), full(w_flog)],
        out_specs=[col, row(WIDTH), col, row(3 * WIDTH), row(LANES)],
        out_shape=[jax.ShapeDtypeStruct((bsz, WIDTH, s), BF16),
                   jax.ShapeDtypeStruct((bsz, s, WIDTH), BF16),
                   jax.ShapeDtypeStruct((bsz, WIDTH, s), BF16),
                   jax.ShapeDtypeStruct((bsz, s, 3 * WIDTH), BF16),
                   jax.ShapeDtypeStruct((bsz, s, LANES), F32)],
        compiler_params=_params(("parallel", "arbitrary")),
        name="norm1_inproj",
    )(x, scale1, shift1, wq_t, wk, wv_t, w_dil, w_flog)


FORGET_PARTS = 3


def _split_bf16(x):
    parts = []
    for _ in range(FORGET_PARTS):
        p = x.astype(BF16)
        parts.append(p)
        x = x - p.astype(F32)
    return parts


def _forget_kernel(f_ref, b_ref, tri_ref, place_ref, o_ref):
    blk = tri_ref.shape[0]
    carry = jnp.zeros((1, LANES), F32)
    for c in range(f_ref.shape[1] // blk):
        rows = pl.ds(c * blk, blk)
        z = f_ref[0, rows, :] + b_ref[...]
        ls = jnp.minimum(z, 0.0) - jnp.log1p(jnp.exp(-jnp.abs(z)))
        cum = carry
        for p in _split_bf16(ls):
            cum = cum + jnp.dot(tri_ref[...], p, preferred_element_type=F32)
        carry = cum[blk - 1:blk, :]
        out = None
        for n, p in enumerate(_split_bf16(cum * LOG2E)):
            d = jnp.dot(p, place_ref[n], preferred_element_type=F32)
            out = d if out is None else out + d
        o_ref[0, rows, :] = out.astype(BF16)


def _forget_cumsum(flog, b_forget):
    bsz, s, _ = flog.shape
    blk = FORGET_BLOCK
    place = np.zeros((FORGET_PARTS, LANES, WIDTH), np.float32)
    for h in range(N_HEADS):
        for n in range(FORGET_PARTS):
            place[n, h, (h // 2) * LANES + (h % 2) * HEAD_DIM + n] = 1.0
    return pl.pallas_call(
        _forget_kernel,
        grid=(bsz,),
        in_specs=[pl.BlockSpec((1, s, LANES), lambda b: (b, 0, 0)),
                  pl.BlockSpec((1, LANES), lambda b: (0, 0)),
                  pl.BlockSpec((blk, blk), lambda b: (0, 0)),
                  pl.BlockSpec(place.shape, lambda b: (0, 0, 0))],
        out_specs=pl.BlockSpec((1, s, WIDTH), lambda b: (b, 0, 0)),
        out_shape=jax.ShapeDtypeStruct((bsz, s, WIDTH), BF16),
        compiler_params=_params(("parallel",)),
        name="forget_cumsum",
    )(flog, jnp.pad(b_forget, (0, LANES - N_HEADS)).reshape(1, LANES),
      jnp.tril(jnp.ones((blk, blk), BF16)), jnp.asarray(place, BF16))


def _stack_heads(q):
    lane = lax.broadcasted_iota(jnp.int32, q.shape, 1)
    zero = jnp.zeros_like(q)
    return jnp.concatenate([jnp.where(lane < HEAD_DIM, q, zero), jnp.where(lane >= HEAD_DIM, q, zero)], axis=0)


def _unstack_heads(o2):
    rows = o2.shape[0] // 2
    lane = lax.broadcasted_iota(jnp.int32, (rows, LANES), 1)
    return jnp.where(lane < HEAD_DIM, o2[:rows], o2[rows:])


def _fox_kernel(q_ref, k_ref, fa_ref, v_ref, o_ref, m_s, l_s, acc_s, s_a, s_b, *, t):
    i = pl.program_id(2)
    q_t = q_ref[0]
    row = lax.broadcasted_iota(jnp.int32, (LANES, 2 * t), 0)
    col = lax.broadcasted_iota(jnp.int32, (LANES, 2 * t), 1)
    head_row = jnp.where(col < t, 0, HEAD_DIM)
    q2 = jnp.concatenate([q_t, q_t], axis=1)
    q2 = jnp.where((row >= head_row) & (row < head_row + HEAD_DIM), q2, jnp.zeros_like(q2))
    minus = jnp.where((row >= head_row) & (row < head_row + FORGET_PARTS), -1.0, 0.0).astype(BF16)
    qa = jnp.concatenate([q2, minus], axis=0)
    m_s[...] = jnp.full(m_s.shape, NEG, F32)
    l_s[...] = jnp.zeros(l_s.shape, F32)
    acc_s[...] = jnp.zeros(acc_s.shape, F32)

    def scores(ks, tk):
        kk = jnp.concatenate([k_ref[0, pl.ds(ks, tk), :], fa_ref[0, pl.ds(ks, tk), :]], axis=1)
        return jnp.dot(kk, qa, preferred_element_type=F32)

    def absorb(s, ks, tk):
        m_prev = m_s[...]
        m_new = jnp.maximum(m_prev, jnp.max(s, axis=0, keepdims=True))
        alpha = jnp.exp2(m_prev - m_new)
        p = jnp.exp2(s - m_new)
        l_s[...] = alpha * l_s[...] + jnp.sum(p, axis=0, keepdims=True)
        acc_s[...] = alpha * acc_s[...] + jnp.dot(v_ref[0, :, pl.ds(ks, tk)], p.astype(BF16),
                                                  preferred_element_type=F32)
        m_s[...] = m_new

    npair = i // 2

    def tile(j):
        return pl.multiple_of(j * 2 * t, 2 * t)

    @pl.when(npair > 0)
    def _():
        s_a[...] = scores(0, 2 * t)

        def body(jj, carry):
            j = 2 * jj
            s_b[...] = scores(tile(j + 1), 2 * t)
            absorb(s_a[...], tile(j), 2 * t)
            s_a[...] = scores(tile(jnp.minimum(j + 2, npair - 1)), 2 * t)
            absorb(s_b[...], tile(j + 1), 2 * t)
            return carry

        lax.fori_loop(0, npair // 2, body, 0)

        @pl.when(npair % 2 == 1)
        def _():
            absorb(s_a[...], tile(npair - 1), 2 * t)

    def last(ks, tk):
        r = lax.broadcasted_iota(jnp.int32, (tk, 2 * t), 0)
        c = lax.broadcasted_iota(jnp.int32, (tk, 2 * t), 1)
        absorb(jnp.where(r <= jnp.where(c >= t, c - t, c) + (tk - t), scores(ks, tk), NEG), ks, tk)

    @pl.when(i % 2 == 1)
    def _():
        last(tile(npair), 2 * t)

    @pl.when(i % 2 == 0)
    def _():
        last(pl.multiple_of(i * t, t), t)

    o2 = acc_s[...] / l_s[...]
    o_t = jnp.where(lax.broadcasted_iota(jnp.int32, (LANES, t), 0) < HEAD_DIM, o2[:, :t], o2[:, t:])
    o_ref[0] = o_t.T


def _fox_attention(q_t, k, f_aug, v_t):
    bsz, s, _ = k.shape
    t = FOX_TILE
    keys = pl.BlockSpec((1, s, LANES), lambda b, h, i: (b, 0, h))
    return pl.pallas_call(
        functools.partial(_fox_kernel, t=t),
        grid=(bsz, HEAD_PAIRS, s // t),
        in_specs=[pl.BlockSpec((1, LANES, t), lambda b, h, i: (b, h, i)),
                  keys, keys,
                  pl.BlockSpec((1, LANES, s), lambda b, h, i: (b, h, 0))],
        out_specs=pl.BlockSpec((1, t, LANES), lambda b, h, i: (b, i, h)),
        out_shape=jax.ShapeDtypeStruct((bsz, s, WIDTH), F32),
        scratch_shapes=[pltpu.VMEM((1, 2 * t), F32), pltpu.VMEM((1, 2 * t), F32),
                        pltpu.VMEM((LANES, 2 * t), F32), pltpu.VMEM((2 * t, 2 * t), F32),
                        pltpu.VMEM((2 * t, 2 * t), F32)],
        compiler_params=_params(("parallel", "parallel", "arbitrary")),
        name="fox_attention",
    )(q_t, k, f_aug, v_t)


def _t5_bucket(dist):
    max_exact = T5_NUM_BUCKETS // 2
    d = np.maximum(dist, 1).astype(np.float32)
    large = max_exact + (np.log(d / max_exact) / np.log(T5_MAX_DISTANCE / max_exact)
                         * (T5_NUM_BUCKETS - max_exact)).astype(np.int32)
    large = np.minimum(large, T5_NUM_BUCKETS - 1)
    return np.where(dist < max_exact, dist, large).astype(np.int32)


def _dilated_bias(rel_bias):
    blk = DIL_BLOCK
    period = 3 * blk
    m = np.arange(period)
    rel = np.where(m < 2 * blk, blk - m, blk - (m - period))
    band = (rel >= 0) & (rel <= blk)
    onehot = np.zeros((len(DILATIONS), period, T5_NUM_BUCKETS), np.float32)
    for bi, dil in enumerate(DILATIONS):
        onehot[bi, m, _t5_bucket(np.clip(rel, 0, blk) * dil)] = 1.0
    w = jnp.einsum('bmk,kh->bhm', jnp.asarray(onehot), rel_bias.astype(F32),
                   precision=lax.Precision.HIGHEST)
    w = jnp.where(jnp.asarray(band), w, NEG)
    flat = jnp.tile(w, (1, 1, blk))[:, :, :blk * (period - 1)]
    table = flat.reshape(len(DILATIONS), N_HEADS, blk, period - 1)[..., :2 * blk]
    table = table.reshape(len(DILATIONS), HEAD_PAIRS, 2 * blk, 2 * blk)
    first = jnp.where(jnp.arange(2 * blk) < blk, NEG, table)
    return jnp.stack([table, first], axis=1)


def _dil_kernel(q_ref, k_ref, v_ref, bias_ref, o_ref, qf, kf, vf, ob0, ob1, ob2, ls0, ls1, ls2, *, s_len):
    blk = DIL_BLOCK
    qf[...] = q_ref[0].astype(F32)
    kf[...] = k_ref[0].astype(F32)
    vf[...] = v_ref[0].astype(F32)
    for bi, (dil, ob, ls) in enumerate(zip(DILATIONS, (ob0, ob1, ob2), (ls0, ls1, ls2))):
        span = blk * dil
        nb = s_len // span

        def rows(start, dil=dil):
            return pl.ds(start, blk) if dil == 1 else pl.ds(start, blk, stride=dil)

        def block(tix, carry, bi=bi, span=span, nb=nb, ob=ob, ls=ls, rows=rows):
            n = tix % nb
            start = n * span + tix // nb
            prev = jnp.maximum(start - span, 0)
            q2 = _stack_heads(qf[rows(start), :].astype(BF16))
            kk = jnp.concatenate([kf[rows(prev), :], kf[rows(start), :]], axis=0).astype(BF16)
            vv = jnp.concatenate([vf[rows(prev), :], vf[rows(start), :]], axis=0).astype(BF16)
            s = lax.dot_general(q2, kk, (((1,), (1,)), ((), ())), preferred_element_type=F32)
            s = s + bias_ref[bi, jnp.where(n == 0, 1, 0), 0]
            m = jnp.max(s, axis=1, keepdims=True)
            p = jnp.exp(s - m)
            l = jnp.sum(p, axis=1, keepdims=True)
            o2 = jnp.dot(p.astype(BF16), vv, preferred_element_type=F32) / l
            lse = jnp.broadcast_to(m + jnp.log(l), (2 * blk, LANES))
            ob[rows(start), :] = _unstack_heads(o2)
            ls[rows(start), :] = _unstack_heads(lse)
            return carry

        def blocks(g, carry, block=block):
            for u in range(DIL_UNROLL):
                block(g * DIL_UNROLL + u, carry)
            return carry

        lax.fori_loop(0, dil * nb // DIL_UNROLL, blocks, 0)

    chunk = DIL_CHUNK
    for c in range(s_len // chunk):
        r = pl.ds(c * chunk, chunk)
        l0, l1, l2 = ls0[r, :], ls1[r, :], ls2[r, :]
        mx = jnp.maximum(jnp.maximum(l0, l1), l2)
        e0, e1, e2 = jnp.exp(l0 - mx), jnp.exp(l1 - mx), jnp.exp(l2 - mx)
        o_ref[0, r, :] = (e0 * ob0[r, :] + e1 * ob1[r, :] + e2 * ob2[r, :]) / (e0 + e1 + e2)


def _dilated_attention(qkv, bias):
    bsz, s, _ = qkv.shape
    col = lambda off: pl.BlockSpec((1, s, LANES), lambda b, h: (b, 0, off + h))
    buf = pltpu.VMEM((s, LANES), F32)
    return pl.pallas_call(
        functools.partial(_dil_kernel, s_len=s),
        grid=(bsz, HEAD_PAIRS),
        in_specs=[col(0), col(HEAD_PAIRS), col(2 * HEAD_PAIRS),
                  pl.BlockSpec((len(DILATIONS), 2, 1, 2 * DIL_BLOCK, 2 * DIL_BLOCK), lambda b, h: (0, 0, h, 0, 0))],
        out_specs=pl.BlockSpec((1, s, LANES), lambda b, h: (b, 0, h)),
        out_shape=jax.ShapeDtypeStruct((bsz, s, WIDTH), F32),
        scratch_shapes=[buf] * 9,
        compiler_params=_params(("parallel", "arbitrary")),
        name="dilated_attention",
    )(qkv, qkv, qkv, bias)


def _bf16_bits(x):
    return pltpu.bitcast(x.astype(BF16).astype(F32), U32)


def _to_row_tiles(dst_ref, x, base=0):
    rows, d = x.shape
    for c in range(TILE_ROWS):
        lo = _bf16_bits(x[:, c * LANES:(c + 1) * LANES]) >> 16
        hi = _bf16_bits(x[:, d // 2 + c * LANES:d // 2 + (c + 1) * LANES])
        dst_ref[pl.ds(base + c, rows, stride=TILE_ROWS), :] = lo | hi


def _from_row_tiles(src_ref, rows, base=0):
    lo, hi = [], []
    for c in range(TILE_ROWS):
        w = src_ref[pl.ds(base + c, rows, stride=TILE_ROWS), :]
        lo.append(pltpu.bitcast(w << 16, F32))
        hi.append(pltpu.bitcast(w & jnp.uint32(0xFFFF0000), F32))
    return lo + hi


def _outproj_kernel(yf_ref, yd_ref, x_ref, g1_ref, sc_ref, sh_ref, g2_ref, gf_ref, gd_ref, wo1_ref, wo2_ref,
                    wrh_ref, wrl_ref, wsg_ref, wsu_ref, wsd_ref, xp_ref, h3_ref, lg_ref):
    nf = (_rms(yf_ref[0]) * gf_ref[...]).astype(BF16)
    nd = (_rms(yd_ref[0]) * gd_ref[...]).astype(BF16)
    mix = (jnp.dot(nf, wo1_ref[...], preferred_element_type=F32)
           + jnp.dot(nd, wo2_ref[...], preferred_element_type=F32))
    x1 = x_ref[0] + g1_ref[0] * mix
    h2 = _rms(x1) * (1.0 + sc_ref[0]) + sh_ref[0]
    hb = h2.astype(BF16)
    hl = (h2 - hb.astype(F32)).astype(BF16)
    lg_ref[...] = _dot_nt(wrh_ref[...], hb) + _dot_nt(wrh_ref[...], hl) + _dot_nt(wrl_ref[...], hb)
    act = _silu(jnp.dot(hb, wsg_ref[...], preferred_element_type=F32)) * jnp.dot(
        hb, wsu_ref[...], preferred_element_type=F32)
    shared = jnp.dot(act.astype(BF16), wsd_ref[...], preferred_element_type=F32)
    xp_ref[0] = x1 + g2_ref[0] * shared
    _to_row_tiles(h3_ref, h2)


def _out_projection(y_fox, y_dil, x, gate1, scale2, shift2, gate2, g_fox, g_dil, wo1, wo2, wr_hi, wr_lo,
                    wsg, wsu, wsd):
    bsz, s, d = x.shape
    tm = ROW_TILE
    nt = s // tm
    vec = pl.BlockSpec((1, 1, d), lambda b, i: (b, 0, 0))
    full = lambda w: pl.BlockSpec(w.shape, lambda b, i: (0,) * w.ndim)
    row = lambda n: pl.BlockSpec((1, tm, n), lambda b, i: (b, i, 0))
    return pl.pallas_call(
        _outproj_kernel,
        grid=(bsz, nt),
        in_specs=[row(WIDTH), row(WIDTH), row(d), vec, vec, vec, vec, full(g_fox), full(g_dil), full(wo1),
                  full(wo2), full(wr_hi), full(wr_lo), full(wsg), full(wsu), full(wsd)],
        out_specs=[row(d),
                   pl.BlockSpec((tm * TILE_ROWS, LANES), lambda b, i: (b * nt + i, 0)),
                   pl.BlockSpec((N_EXPERTS, tm), lambda b, i: (0, b * nt + i))],
        out_shape=[jax.ShapeDtypeStruct((bsz, s, d), F32),
                   jax.ShapeDtypeStruct((bsz * s * TILE_ROWS, LANES), U32),
                   jax.ShapeDtypeStruct((N_EXPERTS, bsz * s), F32)],
        compiler_params=_params(("parallel", "arbitrary")),
        name="outproj_norm2_router_shared",
    )(y_fox, y_dil, x, gate1, scale2, shift2, gate2, g_fox, g_dil, wo1, wo2, wr_hi, wr_lo, wsg, wsu, wsd)


def _first_argmax(v, row, size):
    m = jnp.max(v, axis=0, keepdims=True)
    return m, jnp.min(jnp.where(v == m, row, size), axis=0, keepdims=True)


def _route_kernel(lg_ref, rb_ref, tri_ref, e_ref, w_ref, rk_ref, cnt_ref, cnt_s):
    @pl.when(pl.program_id(0) == 0)
    def _():
        cnt_s[...] = jnp.zeros(cnt_s.shape, F32)

    t = lg_ref.shape[1]
    gsz = N_EXPERTS // N_GROUPS
    scores = jax.nn.sigmoid(lg_ref[...])
    sel = scores + rb_ref[...]
    row_g = lax.broadcasted_iota(jnp.int32, (gsz, t), 0)
    grp = []
    for g in range(N_GROUPS):
        v = sel[g * gsz:(g + 1) * gsz]
        m1, i1 = _first_argmax(v, row_g, gsz)
        m2 = jnp.max(jnp.where(row_g == i1, -jnp.inf, v), axis=0, keepdims=True)
        grp.append(m1 + m2)
    gv = jnp.concatenate(grp, axis=0)
    row8 = lax.broadcasted_iota(jnp.int32, (N_GROUPS, t), 0)
    pen = jnp.full((N_GROUPS, t), -jnp.inf, F32)
    for _ in range(TOP_K_GROUPS):
        _, ix = _first_argmax(gv, row8, N_GROUPS)
        pen = jnp.where(row8 == ix, 0.0, pen)
        gv = jnp.where(row8 == ix, -jnp.inf, gv)
    selm = jnp.concatenate([sel[g * gsz:(g + 1) * gsz] + pen[g:g + 1] for g in range(N_GROUPS)], axis=0)

    row = lax.broadcasted_iota(jnp.int32, (N_EXPERTS, t), 0)
    v = selm
    idxs, scs = [], []
    for _ in range(TOP_K):
        _, ix = _first_argmax(v, row, N_EXPERTS)
        hit = row == ix
        idxs.append(ix)
        scs.append(jnp.sum(jnp.where(hit, scores, 0.0), axis=0, keepdims=True))
        v = jnp.where(hit, -jnp.inf, v)
    chosen = jnp.where(v != selm, 1.0, 0.0)
    before = jnp.dot(chosen.astype(BF16), tri_ref[...], preferred_element_type=F32) + cnt_s[...]
    rks = [jnp.sum(jnp.where(row == ix, before, 0.0), axis=0, keepdims=True) for ix in idxs]
    sc = jnp.concatenate(scs, axis=0)
    e_ref[...] = jnp.concatenate(idxs, axis=0)
    w_ref[...] = sc / jnp.sum(sc, axis=0, keepdims=True) * ROUTED_SCALE
    rk_ref[...] = jnp.concatenate(rks, axis=0).astype(jnp.int32)
    cnt_s[...] = cnt_s[...] + jnp.sum(chosen, axis=1, keepdims=True)
    cnt_ref[...] = cnt_s[...]


def _route(logits_t, router_bias):
    e, n = logits_t.shape
    t = ROUTE_TILE
    tri = jnp.triu(jnp.ones((t, t), BF16), k=1)
    tile = pl.BlockSpec((TOP_K, t), lambda i: (0, i))
    return pl.pallas_call(
        _route_kernel,
        grid=(n // t,),
        in_specs=[pl.BlockSpec((e, t), lambda i: (0, i)),
                  pl.BlockSpec((e, 1), lambda i: (0, 0)),
                  pl.BlockSpec((t, t), lambda i: (0, 0))],
        out_specs=[tile, tile, tile, pl.BlockSpec((e, 1), lambda i: (0, 0))],
        out_shape=[jax.ShapeDtypeStruct((TOP_K, n), jnp.int32), jax.ShapeDtypeStruct((TOP_K, n), F32),
                   jax.ShapeDtypeStruct((TOP_K, n), jnp.int32), jax.ShapeDtypeStruct((e, 1), F32)],
        scratch_shapes=[pltpu.VMEM((e, 1), F32)],
        compiler_params=_params(("arbitrary",)),
        name="route_topk_rank",
    )(logits_t, router_bias.reshape(e, 1).astype(F32), tri)


def _dest_kernel(e_ref, rk_ref, ps_ref, d_ref):
    t = e_ref.shape[1]
    row = lax.broadcasted_iota(jnp.int32, (N_EXPERTS, t), 0)
    ps = ps_ref[...]
    base = [jnp.sum(jnp.where(row == e_ref[k:k + 1, :], ps, 0.0), axis=0, keepdims=True) for k in range(TOP_K)]
    d_ref[...] = jnp.concatenate(base, axis=0).astype(jnp.int32) + rk_ref[...]


def _dest_rows(eidx, rank, pstart):
    _, n = eidx.shape
    t = ROUTE_TILE
    tile = pl.BlockSpec((TOP_K, t), lambda i: (0, i))
    return pl.pallas_call(
        _dest_kernel,
        grid=(n // t,),
        in_specs=[tile, tile, pl.BlockSpec((N_EXPERTS, 1), lambda i: (0, 0))],
        out_specs=tile,
        out_shape=jax.ShapeDtypeStruct((TOP_K, n), jnp.int32),
        compiler_params=_params(("parallel",)),
        name="dest_rows",
    )(eidx, rank, pstart.astype(F32).reshape(N_EXPERTS, 1))


def _block_plan(counts, nblk):
    counts = counts.reshape(N_EXPERTS).astype(jnp.int32)
    padded = (counts + FFN_BLOCK - 1) // FFN_BLOCK * FFN_BLOCK
    pend = jnp.cumsum(padded).astype(jnp.int32)
    pstart = pend - padded
    first_row = jnp.arange(nblk, dtype=jnp.int32) * FFN_BLOCK
    block_e = jnp.minimum(jnp.sum(pend[None, :] <= first_row[:, None], axis=1), N_EXPERTS - 1).astype(jnp.int32)
    nused = (pend[-1:] // FFN_BLOCK).astype(jnp.int32)
    ids = jnp.arange(N_EXPERTS, dtype=jnp.int32)
    following = lax.cummin(jnp.where(counts > 0, ids, N_EXPERTS), reverse=True)
    following = jnp.concatenate([following[1:], jnp.full((1,), N_EXPERTS, jnp.int32)])
    following = jnp.where(following >= N_EXPERTS, -1, following)
    next_e = jnp.sum(jnp.where(block_e[:, None] == ids[None, :], following[None, :], 0), axis=1).astype(jnp.int32)
    return pstart, pstart + counts, pend, block_e, next_e, nused


def _tile_copy(src_ref, src_row, dst_ref, dst_row, sem):
    return pltpu.make_async_copy(src_ref.at[pl.ds(pl.multiple_of(src_row * TILE_ROWS, TILE_ROWS), TILE_ROWS), :],
                                 dst_ref.at[pl.ds(pl.multiple_of(dst_row * TILE_ROWS, TILE_ROWS), TILE_ROWS), :],
                                 sem)


def _scatter_kernel(cend_ref, pend_ref, dst_ref, h3_ref, z_ref, xs_hbm, st0, st1, sem, zsem):
    i = pl.program_id(0)
    nsteps = pl.num_programs(0)
    tm = dst_ref.shape[1]
    stage = (st0, st1)

    def zero_fill(e, wait):
        pad = pend_ref[e] - cend_ref[e]
        for p in ZERO_CHUNKS:
            @pl.when((pad & p) != 0)
            def _(p=p):
                row = cend_ref[e] + (pad & (-2 * p))
                dst = xs_hbm.at[pl.ds(pl.multiple_of(row * TILE_ROWS, TILE_ROWS), p * TILE_ROWS), :]
                copy = pltpu.make_async_copy(z_ref.at[pl.ds(0, p * TILE_ROWS), :], dst, zsem)
                copy.wait() if wait else copy.start()

    def wait_slot(slot):
        for _ in range(TOP_K):
            pltpu.make_async_copy(stage[slot], xs_hbm.at[pl.ds(0, tm * TILE_ROWS), :], sem.at[slot]).wait()

    @pl.when(i == 0)
    def _():
        def per_expert(e, carry):
            zero_fill(e, False)

            @pl.when(e > 0)
            def _():
                zero_fill(e - 1, True)
            return carry

        lax.fori_loop(0, N_EXPERTS, per_expert, 0)
        zero_fill(N_EXPERTS - 1, True)

    for slot in range(2):
        @pl.when(i % 2 == slot)
        def _(slot=slot):
            @pl.when(i >= 2)
            def _():
                wait_slot(slot)
            stage[slot][...] = h3_ref[...]
            for k in range(TOP_K):
                def body(jj, carry, k=k):
                    for u in range(ISSUE_UNROLL):
                        j = jj * ISSUE_UNROLL + u
                        _tile_copy(stage[slot], j, xs_hbm, dst_ref[k, j], sem.at[slot]).start()
                    return carry
                lax.fori_loop(0, tm // ISSUE_UNROLL, body, 0)

    @pl.when(i == nsteps - 1)
    def _():
        wait_slot(0)
        wait_slot(1)


def _dispatch_rows(h3, dest, cend, pend, rows):
    _, n = dest.shape
    tm = SCATTER_TILE
    assert n // tm >= 2
    stage = pltpu.VMEM((tm * TILE_ROWS, LANES), U32)
    grid_spec = pltpu.PrefetchScalarGridSpec(
        num_scalar_prefetch=2,
        grid=(n // tm,),
        in_specs=[pl.BlockSpec((TOP_K, tm), lambda i, ce, pe: (0, i), memory_space=pltpu.SMEM),
                  pl.BlockSpec((tm * TILE_ROWS, LANES), lambda i, ce, pe: (i, 0)),
                  pl.BlockSpec((ZERO_CHUNKS[0] * TILE_ROWS, LANES), lambda i, ce, pe: (0, 0))],
        out_specs=pl.BlockSpec(memory_space=pl.ANY),
        scratch_shapes=[stage, stage, pltpu.SemaphoreType.DMA((2,)), pltpu.SemaphoreType.DMA],
    )
    return pl.pallas_call(
        _scatter_kernel,
        grid_spec=grid_spec,
        out_shape=jax.ShapeDtypeStruct((rows * TILE_ROWS, LANES), U32),
        compiler_params=_params(("arbitrary",)),
        name="dispatch_scatter",
    )(cend, pend, dest, h3, jnp.zeros((ZERO_CHUNKS[0] * TILE_ROWS, LANES), U32))


def _ffn_kernel(be_ref, nx_ref, nu_ref, x_ref, wg_hbm, wu_hbm, wd_hbm, y_ref,
                wgf, wuf, wdf, wgb, wub, wdb, sem, run):
    first = pl.program_id(0) * FFN_PER_STEP
    nused = nu_ref[0]

    def fetch(e, slot):
        return [pltpu.make_async_copy(src.at[e], dst.at[slot], sem.at[slot])
                for src, dst in ((wg_hbm, wgf), (wu_hbm, wuf), (wd_hbm, wdf))]

    @pl.when(first == 0)
    def _():
        run[0] = -1
        for c in fetch(be_ref[0], 0):
            c.start()

    slots = []
    for part in range(FFN_PER_STEP):
        blk = first + part

        @pl.when((blk < nused) & ((blk == 0) | (be_ref[blk] != be_ref[jnp.maximum(blk - 1, 0)])))
        def _(blk=blk):
            r = run[0] + 1
            run[0] = r
            fslot, slot = r % 2, r % FFN_PER_STEP
            for c in fetch(be_ref[blk], fslot):
                c.wait()
            wgb[slot] = wgf[fslot].astype(BF16)
            wub[slot] = wuf[fslot].astype(BF16)
            wdb[slot] = wdf[fslot].astype(BF16)

            @pl.when(nx_ref[blk] >= 0)
            def _():
                for c in fetch(nx_ref[blk], 1 - fslot):
                    c.start()

        slots.append(run[0] % FFN_PER_STEP)

    @pl.when(first < nused)
    def _():
        for part, slot in enumerate(slots):
            base = part * FFN_BLOCK * TILE_ROWS
            x = jnp.concatenate([c.astype(BF16) for c in _from_row_tiles(x_ref, FFN_BLOCK, base)], axis=1)
            act = _silu(jnp.dot(x, wgb[slot], preferred_element_type=F32)) * jnp.dot(
                x, wub[slot], preferred_element_type=F32)
            _to_row_tiles(y_ref, jnp.dot(act.astype(BF16), wdb[slot], preferred_element_type=F32), base)

    @pl.when(first >= nused)
    def _():
        y_ref[...] = jnp.zeros(y_ref.shape, U32)


def _routed_experts(xs, block_e, next_e, nused, w_gate, w_up, w_down):
    rows = xs.shape[0] // TILE_ROWS
    nblk = rows // FFN_BLOCK
    _, d, hid = w_gate.shape
    step_rows = FFN_PER_STEP * FFN_BLOCK * TILE_ROWS
    hbm = pl.BlockSpec(memory_space=pl.ANY)
    grid_spec = pltpu.PrefetchScalarGridSpec(
        num_scalar_prefetch=3,
        grid=(nblk // FFN_PER_STEP,),
        in_specs=[pl.BlockSpec((step_rows, LANES),
                               lambda i, be, nx, nu: (jnp.minimum(i, (nu[0] - 1) // FFN_PER_STEP), 0)),
                  hbm, hbm, hbm],
        out_specs=pl.BlockSpec((step_rows, LANES), lambda i, be, nx, nu: (i, 0)),
        scratch_shapes=[pltpu.VMEM((2, d, hid), F32), pltpu.VMEM((2, d, hid), F32), pltpu.VMEM((2, hid, d), F32),
                        pltpu.VMEM((FFN_PER_STEP, d, hid), BF16), pltpu.VMEM((FFN_PER_STEP, d, hid), BF16),
                        pltpu.VMEM((FFN_PER_STEP, hid, d), BF16),
                        pltpu.SemaphoreType.DMA((2,)), pltpu.SMEM((1,), jnp.int32)],
    )
    return pl.pallas_call(
        _ffn_kernel,
        grid_spec=grid_spec,
        out_shape=jax.ShapeDtypeStruct((rows * TILE_ROWS, LANES), U32),
        compiler_params=_params(("arbitrary",)),
        name="routed_experts",
    )(block_e, next_e, nused, xs, w_gate, w_up, w_down)


def _gather_start(idx_ref, src_hbm, dst_vmem, sem):
    nk, tm = idx_ref.shape
    for k in range(nk):
        def body(jj, carry, k=k):
            for u in range(ISSUE_UNROLL):
                j = jj * ISSUE_UNROLL + u
                _tile_copy(src_hbm, idx_ref[k, j], dst_vmem, k * tm + j, sem).start()
            return carry
        lax.fori_loop(0, tm // ISSUE_UNROLL, body, 0)


def _gather_wait(src_hbm, dst_vmem, sem):
    pltpu.make_async_copy(src_hbm.at[pl.ds(0, dst_vmem.shape[0]), :], dst_vmem, sem).wait()


def _combine_kernel(dst_ref, dstn_ref, wt_ref, xp_ref, g2_ref, gfin_ref, ys_hbm, o_ref, yb0, yb1, sem, *, nsteps):
    i = pl.program_id(0)
    tm = COMBINE_TILE
    bufs = (yb0, yb1)

    @pl.when(i == 0)
    def _():
        _gather_start(dst_ref, ys_hbm, yb0, sem.at[0])

    for slot in range(2):
        @pl.when((i + 1 < nsteps) & ((i + 1) % 2 == slot))
        def _(slot=slot):
            _gather_start(dstn_ref, ys_hbm, bufs[slot], sem.at[slot])

    for slot in range(2):
        @pl.when(i % 2 == slot)
        def _(slot=slot):
            _gather_wait(ys_hbm, bufs[slot], sem.at[slot])
            wt = wt_ref[...]
            cols = None
            for k in range(TOP_K):
                part = [c * wt[:, k:k + 1] for c in _from_row_tiles(bufs[slot], tm, base=k * tm * TILE_ROWS)]
                cols = part if cols is None else [a + b for a, b in zip(cols, part)]
            routed = jnp.concatenate(cols, axis=1)
            o_ref[0] = _rms(xp_ref[0] + g2_ref[0] * routed) * gfin_ref[...]


def _combine(ys, dest, wts_t, xp, gate2, g_final):
    bsz, s, d = xp.shape
    tm = COMBINE_TILE
    nt = s // tm
    nsteps = bsz * nt
    vec = pl.BlockSpec((1, 1, d), lambda i: (i // nt, 0, 0))
    return pl.pallas_call(
        functools.partial(_combine_kernel, nsteps=nsteps),
        grid=(nsteps,),
        in_specs=[pl.BlockSpec((TOP_K, tm), lambda i: (0, i), memory_space=pltpu.SMEM),
                  pl.BlockSpec((TOP_K, tm), lambda i: (0, jnp.minimum(i + 1, nsteps - 1)),
                               memory_space=pltpu.SMEM),
                  pl.BlockSpec((tm, TOP_K), lambda i: (i, 0)),
                  pl.BlockSpec((1, tm, d), lambda i: (i // nt, i % nt, 0)),
                  vec,
                  pl.BlockSpec((1, d), lambda i: (0, 0)),
                  pl.BlockSpec(memory_space=pl.ANY)],
        out_specs=pl.BlockSpec((1, tm, d), lambda i: (i // nt, i % nt, 0)),
        out_shape=jax.ShapeDtypeStruct((bsz, s, d), F32),
        scratch_shapes=[pltpu.VMEM((tm * TOP_K * TILE_ROWS, LANES), U32),
                        pltpu.VMEM((tm * TOP_K * TILE_ROWS, LANES), U32),
                        pltpu.SemaphoreType.DMA((2,))],
        compiler_params=_params(("arbitrary",)),
        name="combine_final_norm",
    )(dest, dest, wts_t, xp, gate2, g_final.reshape(1, d), ys)


def kernel(x, c, w_in, b_forget, g_fox_out, g_dil_out, w_out, w_ada, b_ada, w_router, router_bias,
           w_exp_gate, w_exp_up, w_exp_down, w_sh_gate, w_sh_up, w_sh_down, rel_bias, g_final):
    bsz, s, d = x.shape
    depth = w_in.shape[0]
    assert depth == 1 and d == 2 * TILE_ROWS * LANES and s % (DIL_BLOCK * DILATIONS[-1]) == 0
    assert all(s % t == 0 for t in (ROW_TILE, FORGET_BLOCK, FOX_TILE, DIL_CHUNK, COMBINE_TILE, SCATTER_TILE))
    assert (bsz * s) % ROUTE_TILE == 0 and (bsz * s * TOP_K // FFN_BLOCK + N_EXPERTS) % FFN_PER_STEP == 0
    l = 0
    mod = _modulation(c, w_ada[l], b_ada[l])
    shift1, scale1, gate1, shift2, scale2, gate2 = [m[:, None, :] for m in jnp.split(mod, 6, axis=-1)]

    qscale = HEAD_DIM ** -0.5
    o3 = 3 * WIDTH
    w = w_in[l]
    wq_t = (w[:, :WIDTH] * (qscale * LOG2E)).T.astype(BF16)
    wk = w[:, WIDTH:2 * WIDTH].astype(BF16)
    wv_t = w[:, 2 * WIDTH:o3].T.astype(BF16)
    w_flog = jnp.pad(w[:, o3:o3 + N_HEADS], ((0, 0), (0, LANES - N_HEADS))).astype(BF16)
    wd0 = o3 + N_HEADS
    w_dil = jnp.concatenate([w[:, wd0:wd0 + WIDTH] * qscale, w[:, wd0 + WIDTH:]], axis=1).astype(BF16)

    q_t, k_f, v_t, qkv_d, flog = _in_projection(x, scale1, shift1, wq_t, wk, wv_t, w_dil, w_flog)
    y_fox = _fox_attention(q_t, k_f, _forget_cumsum(flog, b_forget[l]), v_t)
    y_dil = _dilated_attention(qkv_d, _dilated_bias(rel_bias))

    wr = w_router[l].T
    wr_hi = wr.astype(BF16)
    wr_lo = (wr - wr_hi.astype(F32)).astype(BF16)
    wo = w_out[l].astype(BF16)
    xp, h3, logits_t = _out_projection(
        y_fox, y_dil, x, gate1, scale2, shift2, gate2, g_fox_out[l].reshape(1, WIDTH),
        g_dil_out[l].reshape(1, WIDTH), wo[:WIDTH], wo[WIDTH:], wr_hi, wr_lo,
        w_sh_gate[l].astype(BF16), w_sh_up[l].astype(BF16), w_sh_down[l].astype(BF16))

    eidx, wts, rank, counts = _route(logits_t, router_bias[l])
    rows = bsz * s * TOP_K + N_EXPERTS * FFN_BLOCK
    pstart, cend, pend, block_e, next_e, nused = _block_plan(counts, rows // FFN_BLOCK)
    dest = _dest_rows(eidx, rank, pstart)
    xs = _dispatch_rows(h3, dest, cend, pend, rows)
    ys = _routed_experts(xs, block_e, next_e, nused, w_exp_gate[l], w_exp_up[l], w_exp_down[l])
    return _combine(ys, dest, wts.T, xp, gate2, g_final)
```

```python
import functools

import numpy as np
import jax
import jax.numpy as jnp
from jax import lax
from jax.experimental import pallas as pl
from jax.experimental.pallas import tpu as pltpu

F32 = jnp.float32
BF16 = jnp.bfloat16
U32 = jnp.uint32

HEAD_DIM = 64
N_HEADS = 8
WIDTH = N_HEADS * HEAD_DIM
DIL_BLOCK = 128
DILATIONS = (1, 4, 16)
T5_NUM_BUCKETS = 32
T5_MAX_DISTANCE = 2048
N_EXPERTS = 256
TOP_K = 8
N_GROUPS = 8
TOP_K_GROUPS = 4
ROUTED_SCALE = 2.5
EPS = 1e-6

LANES = 128
VMEM_BYTES = 64 * 1024 * 1024
VMEM_LIMIT = VMEM_BYTES * 7 // 8

HEAD_PAIRS = WIDTH // LANES
TILE_ROWS = 4
NEG = -1e30
LOG2E = float(np.log2(np.e))

MOD_TILE = 1536
ROW_TILE = 512
FORGET_BLOCK = 512
FOX_TILE = 512
DIL_UNROLL = 16
DIL_CHUNK = 512
FFN_BLOCK = 256
FFN_PER_STEP = 4
ZERO_CHUNKS = tuple(1 << b for b in reversed(range(FFN_BLOCK.bit_length() - 1)))
COMBINE_TILE = 256
SCATTER_TILE = 256
ROUTE_TILE = 512
ISSUE_UNROLL = 8


def _params(semantics):
    return pltpu.CompilerParams(dimension_semantics=semantics, vmem_limit_bytes=VMEM_LIMIT)


def _rms(x):
    return x * lax.rsqrt(jnp.mean(x * x, axis=-1, keepdims=True) + EPS)


def _silu(x):
    return x * jax.nn.sigmoid(x)


def _dot_nt(a, b):
    return lax.dot_general(a, b, (((1,), (1,)), ((), ())), preferred_element_type=F32)


def _mod_kernel(c_ref, w_ref, b_ref, o_ref):
    o_ref[...] = jnp.dot(_silu(c_ref[...]), w_ref[...], precision=lax.Precision.HIGHEST,
                         preferred_element_type=F32) + b_ref[...]


def _modulation(c, w_ada, b_ada):
    bsz, d = c.shape
    n = w_ada.shape[1]
    tn = MOD_TILE
    return pl.pallas_call(
        _mod_kernel,
        grid=(n // tn,),
        in_specs=[pl.BlockSpec((bsz, d), lambda j: (0, 0)),
                  pl.BlockSpec((d, tn), lambda j: (0, j)),
                  pl.BlockSpec((1, tn), lambda j: (0, j))],
        out_specs=pl.BlockSpec((bsz, tn), lambda j: (0, j)),
        out_shape=jax.ShapeDtypeStruct((bsz, n), F32),
        compiler_params=_params(("arbitrary",)),
        name="adaln_mod",
    )(c, w_ada, b_ada.reshape(1, n))


def _inproj_kernel(x_ref, sc_ref, sh_ref, wq_ref, wk_ref, wv_ref, wd_ref, wl_ref,
                   oq_ref, ok_ref, ov_ref, od_ref, ol_ref):
    h = _rms(x_ref[0]) * (1.0 + sc_ref[0]) + sh_ref[0]
    hb = h.astype(BF16)
    oq_ref[0] = _dot_nt(wq_ref[...], hb).astype(BF16)
    ov_ref[0] = _dot_nt(wv_ref[...], hb).astype(BF16)
    ok_ref[0] = jnp.dot(hb, wk_ref[...], preferred_element_type=F32).astype(BF16)
    od_ref[0] = jnp.dot(hb, wd_ref[...], preferred_element_type=F32).astype(BF16)
    ol_ref[0] = jnp.dot(hb, wl_ref[...], preferred_element_type=F32)


def _in_projection(x, scale1, shift1, wq_t, wk, wv_t, w_dil, w_flog):
    bsz, s, d = x.shape
    tm = ROW_TILE
    vec = pl.BlockSpec((1, 1, d), lambda b, i: (b, 0, 0))
    full = lambda w: pl.BlockSpec(w.shape, lambda b, i: (0, 0))
    row = lambda n: pl.BlockSpec((1, tm, n), lambda b, i: (b, i, 0))
    col = pl.BlockSpec((1, WIDTH, tm), lambda b, i: (b, 0, i))
    return pl.pallas_call(
        _inproj_kernel,
        grid=(bsz, s // tm),
        in_specs=[row(d), vec, vec, full(wq_t), full(wk), full(wv_t), full(w_dil), full(w_flog)],
        out_specs=[col, row(WIDTH), col, row(3 * WIDTH), row(LANES)],
        out_shape=[jax.ShapeDtypeStruct((bsz, WIDTH, s), BF16),
                   jax.ShapeDtypeStruct((bsz, s, WIDTH), BF16),
                   jax.ShapeDtypeStruct((bsz, WIDTH, s), BF16),
                   jax.ShapeDtypeStruct((bsz, s, 3 * WIDTH), BF16),
                   jax.ShapeDtypeStruct((bsz, s, LANES), F32)],
        compiler_params=_params(("parallel", "arbitrary")),
        name="norm1_inproj",
    )(x, scale1, shift1, wq_t, wk, wv_t, w_dil, w_flog)


FORGET_PARTS = 3


def _split_bf16(x):
    parts = []
    for _ in range(FORGET_PARTS):
        p = x.astype(BF16)
        parts.append(p)
        x = x - p.astype(F32)
    return parts


def _forget_kernel(f_ref, b_ref, tri_ref, place_ref, o_ref):
    blk = tri_ref.shape[0]
    carry = jnp.zeros((1, LANES), F32)
    for c in range(f_ref.shape[1] // blk):
        rows = pl.ds(c * blk, blk)
        z = f_ref[0, rows, :] + b_ref[...]
        ls = jnp.minimum(z, 0.0) - jnp.log1p(jnp.exp(-jnp.abs(z)))
        cum = carry
        for p in _split_bf16(ls):
            cum = cum + jnp.dot(tri_ref[...], p, preferred_element_type=F32)
        carry = cum[blk - 1:blk, :]
        out = None
        for n, p in enumerate(_split_bf16(cum * LOG2E)):
            d = jnp.dot(p, place_ref[n], preferred_element_type=F32)
            out = d if out is None else out + d
        o_ref[0, rows, :] = out.astype(BF16)


def _forget_cumsum(flog, b_forget):
    bsz, s, _ = flog.shape
    blk = FORGET_BLOCK
    place = np.zeros((FORGET_PARTS, LANES, WIDTH), np.float32)
    for h in range(N_HEADS):
        for n in range(FORGET_PARTS):
            place[n, h, (h // 2) * LANES + (h % 2) * HEAD_DIM + n] = 1.0
    return pl.pallas_call(
        _forget_kernel,
        grid=(bsz,),
        in_specs=[pl.BlockSpec((1, s, LANES), lambda b: (b, 0, 0)),
                  pl.BlockSpec((1, LANES), lambda b: (0, 0)),
                  pl.BlockSpec((blk, blk), lambda b: (0, 0)),
                  pl.BlockSpec(place.shape, lambda b: (0, 0, 0))],
        out_specs=pl.BlockSpec((1, s, WIDTH), lambda b: (b, 0, 0)),
        out_shape=jax.ShapeDtypeStruct((bsz, s, WIDTH), BF16),
        compiler_params=_params(("parallel",)),
        name="forget_cumsum",
    )(flog, jnp.pad(b_forget, (0, LANES - N_HEADS)).reshape(1, LANES),
      jnp.tril(jnp.ones((blk, blk), BF16)), jnp.asarray(place, BF16))


def _stack_heads(q):
    lane = lax.broadcasted_iota(jnp.int32, q.shape, 1)
    zero = jnp.zeros_like(q)
    return jnp.concatenate([jnp.where(lane < HEAD_DIM, q, zero), jnp.where(lane >= HEAD_DIM, q, zero)], axis=0)


def _unstack_heads(o2):
    rows = o2.shape[0] // 2
    lane = lax.broadcasted_iota(jnp.int32, (rows, LANES), 1)
    return jnp.where(lane < HEAD_DIM, o2[:rows], o2[rows:])


def _fox_kernel(q_ref, k_ref, fa_ref, v_ref, o_ref, m_s, l_s, acc_s, s_a, s_b, *, t):
    i = pl.program_id(2)
    q_t = q_ref[0]
    row = lax.broadcasted_iota(jnp.int32, (LANES, 2 * t), 0)
    col = lax.broadcasted_iota(jnp.int32, (LANES, 2 * t), 1)
    head_row = jnp.where(col < t, 0, HEAD_DIM)
    q2 = jnp.concatenate([q_t, q_t], axis=1)
    q2 = jnp.where((row >= head_row) & (row < head_row + HEAD_DIM), q2, jnp.zeros_like(q2))
    minus = jnp.where((row >= head_row) & (row < head_row + FORGET_PARTS), -1.0, 0.0).astype(BF16)
    qa = jnp.concatenate([q2, minus], axis=0)
    m_s[...] = jnp.full(m_s.shape, NEG, F32)
    l_s[...] = jnp.zeros(l_s.shape, F32)
    acc_s[...] = jnp.zeros(acc_s.shape, F32)

    def scores(ks, tk):
        kk = jnp.concatenate([k_ref[0, pl.ds(ks, tk), :], fa_ref[0, pl.ds(ks, tk), :]], axis=1)
        return jnp.dot(kk, qa, preferred_element_type=F32)

    def absorb(s, ks, tk):
        m_prev = m_s[...]
        m_new = jnp.maximum(m_prev, jnp.max(s, axis=0, keepdims=True))
        alpha = jnp.exp2(m_prev - m_new)
        p = jnp.exp2(s - m_new)
        l_s[...] = alpha * l_s[...] + jnp.sum(p, axis=0, keepdims=True)
        acc_s[...] = alpha * acc_s[...] + jnp.dot(v_ref[0, :, pl.ds(ks, tk)], p.astype(BF16),
                                                  preferred_element_type=F32)
        m_s[...] = m_new

    npair = i // 2

    def tile(j):
        return pl.multiple_of(j * 2 * t, 2 * t)

    @pl.when(npair > 0)
    def _():
        s_a[...] = scores(0, 2 * t)

        def body(jj, carry):
            j = 2 * jj
            s_b[...] = scores(tile(j + 1), 2 * t)
            absorb(s_a[...], tile(j), 2 * t)
            s_a[...] = scores(tile(jnp.minimum(j + 2, npair - 1)), 2 * t)
            absorb(s_b[...], tile(j + 1), 2 * t)
            return carry

        lax.fori_loop(0, npair // 2, body, 0)

        @pl.when(npair % 2 == 1)
        def _():
            absorb(s_a[...], tile(npair - 1), 2 * t)

    def last(ks, tk):
        r = lax.broadcasted_iota(jnp.int32, (tk, 2 * t), 0)
        c = lax.broadcasted_iota(jnp.int32, (tk, 2 * t), 1)
        absorb(jnp.where(r <= jnp.where(c >= t, c - t, c) + (tk - t), scores(ks, tk), NEG), ks, tk)

    @pl.when(i % 2 == 1)
    def _():
        last(tile(npair), 2 * t)

    @pl.when(i % 2 == 0)
    def _():
        last(pl.multiple_of(i * t, t), t)

    o2 = acc_s[...] / l_s[...]
    o_t = jnp.where(lax.broadcasted_iota(jnp.int32, (LANES, t), 0) < HEAD_DIM, o2[:, :t], o2[:, t:])
    o_ref[0] = o_t.T


def _fox_attention(q_t, k, f_aug, v_t):
    bsz, s, _ = k.shape
    t = FOX_TILE
    keys = pl.BlockSpec((1, s, LANES), lambda b, h, i: (b, 0, h))
    return pl.pallas_call(
        functools.partial(_fox_kernel, t=t),
        grid=(bsz, HEAD_PAIRS, s // t),
        in_specs=[pl.BlockSpec((1, LANES, t), lambda b, h, i: (b, h, i)),
                  keys, keys,
                  pl.BlockSpec((1, LANES, s), lambda b, h, i: (b, h, 0))],
        out_specs=pl.BlockSpec((1, t, LANES), lambda b, h, i: (b, i, h)),
        out_shape=jax.ShapeDtypeStruct((bsz, s, WIDTH), F32),
        scratch_shapes=[pltpu.VMEM((1, 2 * t), F32), pltpu.VMEM((1, 2 * t), F32),
                        pltpu.VMEM((LANES, 2 * t), F32), pltpu.VMEM((2 * t, 2 * t), F32),
                        pltpu.VMEM((2 * t, 2 * t), F32)],
        compiler_params=_params(("parallel", "parallel", "arbitrary")),
        name="fox_attention",
    )(q_t, k, f_aug, v_t)


def _t5_bucket(dist):
    max_exact = T5_NUM_BUCKETS // 2
    d = np.maximum(dist, 1).astype(np.float32)
    large = max_exact + (np.log(d / max_exact) / np.log(T5_MAX_DISTANCE / max_exact)
                         * (T5_NUM_BUCKETS - max_exact)).astype(np.int32)
    large = np.minimum(large, T5_NUM_BUCKETS - 1)
    return np.where(dist < max_exact, dist, large).astype(np.int32)


def _dilated_bias(rel_bias):
    blk = DIL_BLOCK
    period = 3 * blk
    m = np.arange(period)
    rel = np.where(m < 2 * blk, blk - m, blk - (m - period))
    band = (rel >= 0) & (rel <= blk)
    onehot = np.zeros((len(DILATIONS), period, T5_NUM_BUCKETS), np.float32)
    for bi, dil in enumerate(DILATIONS):
        onehot[bi, m, _t5_bucket(np.clip(rel, 0, blk) * dil)] = 1.0
    w = jnp.einsum('bmk,kh->bhm', jnp.asarray(onehot), rel_bias.astype(F32),
                   precision=lax.Precision.HIGHEST)
    w = jnp.where(jnp.asarray(band), w, NEG)
    flat = jnp.tile(w, (1, 1, blk))[:, :, :blk * (period - 1)]
    table = flat.reshape(len(DILATIONS), N_HEADS, blk, period - 1)[..., :2 * blk]
    table = table.reshape(len(DILATIONS), HEAD_PAIRS, 2 * blk, 2 * blk)
    first = jnp.where(jnp.arange(2 * blk) < blk, NEG, table)
    return jnp.stack([table, first], axis=1)


def _dil_kernel(q_ref, k_ref, v_ref, bias_ref, o_ref, qf, kf, vf, ob0, ob1, ob2, ls0, ls1, ls2, *, s_len):
    blk = DIL_BLOCK
    qf[...] = q_ref[0].astype(F32)
    kf[...] = k_ref[0].astype(F32)
    vf[...] = v_ref[0].astype(F32)
    for bi, (dil, ob, ls) in enumerate(zip(DILATIONS, (ob0, ob1, ob2), (ls0, ls1, ls2))):
        span = blk * dil
        nb = s_len // span

        def rows(start, dil=dil):
            return pl.ds(start, blk) if dil == 1 else pl.ds(start, blk, stride=dil)

        def block(tix, carry, bi=bi, span=span, nb=nb, ob=ob, ls=ls, rows=rows):
            n = tix % nb
            start = n * span + tix // nb
            prev = jnp.maximum(start - span, 0)
            q2 = _stack_heads(qf[rows(start), :].astype(BF16))
            kk = jnp.concatenate([kf[rows(prev), :], kf[rows(start), :]], axis=0).astype(BF16)
            vv = jnp.concatenate([vf[rows(prev), :], vf[rows(start), :]], axis=0).astype(BF16)
            s = lax.dot_general(q2, kk, (((1,), (1,)), ((), ())), preferred_element_type=F32)
            s = s + bias_ref[bi, jnp.where(n == 0, 1, 0), 0]
            m = jnp.max(s, axis=1, keepdims=True)
            p = jnp.exp(s - m)
            l = jnp.sum(p, axis=1, keepdims=True)
            o2 = jnp.dot(p.astype(BF16), vv, preferred_element_type=F32) / l
            lse = jnp.broadcast_to(m + jnp.log(l), (2 * blk, LANES))
            ob[rows(start), :] = _unstack_heads(o2)
            ls[rows(start), :] = _unstack_heads(lse)
            return carry

        def blocks(g, carry, block=block):
            for u in range(DIL_UNROLL):
                block(g * DIL_UNROLL + u, carry)
            return carry

        lax.fori_loop(0, dil * nb // DIL_UNROLL, blocks, 0)

    chunk = DIL_CHUNK
    for c in range(s_len // chunk):
        r = pl.ds(c * chunk, chunk)
        l0, l1, l2 = ls0[r, :], ls1[r, :], ls2[r, :]
        mx = jnp.maximum(jnp.maximum(l0, l1), l2)
        e0, e1, e2 = jnp.exp(l0 - mx), jnp.exp(l1 - mx), jnp.exp(l2 - mx)
        o_ref[0, r, :] = (e0 * ob0[r, :] + e1 * ob1[r, :] + e2 * ob2[r, :]) / (e0 + e1 + e2)


def _dilated_attention(qkv, bias):
    bsz, s, _ = qkv.shape
    col = lambda off: pl.BlockSpec((1, s, LANES), lambda b, h: (b, 0, off + h))
    buf = pltpu.VMEM((s, LANES), F32)
    return pl.pallas_call(
        functools.partial(_dil_kernel, s_len=s),
        grid=(bsz, HEAD_PAIRS),
        in_specs=[col(0), col(HEAD_PAIRS), col(2 * HEAD_PAIRS),
                  pl.BlockSpec((len(DILATIONS), 2, 1, 2 * DIL_BLOCK, 2 * DIL_BLOCK), lambda b, h: (0, 0, h, 0, 0))],
        out_specs=pl.BlockSpec((1, s, LANES), lambda b, h: (b, 0, h)),
        out_shape=jax.ShapeDtypeStruct((bsz, s, WIDTH), F32),
        scratch_shapes=[buf] * 9,
        compiler_params=_params(("parallel", "arbitrary")),
        name="dilated_attention",
    )(qkv, qkv, qkv, bias)


def _bf16_bits(x):
    return pltpu.bitcast(x.astype(BF16).astype(F32), U32)


def _to_row_tiles(dst_ref, x, base=0):
    rows, d = x.shape
    for c in range(TILE_ROWS):
        lo = _bf16_bits(x[:, c * LANES:(c + 1) * LANES]) >> 16
        hi = _bf16_bits(x[:, d // 2 + c * LANES:d // 2 + (c + 1) * LANES])
        dst_ref[pl.ds(base + c, rows, stride=TILE_ROWS), :] = lo | hi


def _from_row_tiles(src_ref, rows, base=0):
    lo, hi = [], []
    for c in range(TILE_ROWS):
        w = src_ref[pl.ds(base + c, rows, stride=TILE_ROWS), :]
        lo.append(pltpu.bitcast(w << 16, F32))
        hi.append(pltpu.bitcast(w & jnp.uint32(0xFFFF0000), F32))
    return lo + hi


def _outproj_kernel(yf_ref, yd_ref, x_ref, g1_ref, sc_ref, sh_ref, gf_ref, gd_ref, wo1_ref, wo2_ref,
                    wrh_ref, wrl_ref, x1_ref, h3_ref, lg_ref):
    nf = (_rms(yf_ref[0]) * gf_ref[...]).astype(BF16)
    nd = (_rms(yd_ref[0]) * gd_ref[...]).astype(BF16)
    mix = (jnp.dot(nf, wo1_ref[...], preferred_element_type=F32)
           + jnp.dot(nd, wo2_ref[...], preferred_element_type=F32))
    x1 = x_ref[0] + g1_ref[0] * mix
    h2 = _rms(x1) * (1.0 + sc_ref[0]) + sh_ref[0]
    hb = h2.astype(BF16)
    hl = (h2 - hb.astype(F32)).astype(BF16)
    lg_ref[...] = _dot_nt(wrh_ref[...], hb) + _dot_nt(wrh_ref[...], hl) + _dot_nt(wrl_ref[...], hb)
    x1_ref[0] = x1
    _to_row_tiles(h3_ref, h2)


def _out_projection(y_fox, y_dil, x, gate1, scale2, shift2, g_fox, g_dil, wo1, wo2, wr_hi, wr_lo):
    bsz, s, d = x.shape
    tm = ROW_TILE
    nt = s // tm
    vec = pl.BlockSpec((1, 1, d), lambda b, i: (b, 0, 0))
    full = lambda w: pl.BlockSpec(w.shape, lambda b, i: (0,) * w.ndim)
    row = lambda n: pl.BlockSpec((1, tm, n), lambda b, i: (b, i, 0))
    return pl.pallas_call(
        _outproj_kernel,
        grid=(bsz, nt),
        in_specs=[row(WIDTH), row(WIDTH), row(d), vec, vec, vec, full(g_fox), full(g_dil), full(wo1),
                  full(wo2), full(wr_hi), full(wr_lo)],
        out_specs=[row(d),
                   pl.BlockSpec((tm * TILE_ROWS, LANES), lambda b, i: (b * nt + i, 0)),
                   pl.BlockSpec((N_EXPERTS, tm), lambda b, i: (0, b * nt + i))],
        out_shape=[jax.ShapeDtypeStruct((bsz, s, d), F32),
                   jax.ShapeDtypeStruct((bsz * s * TILE_ROWS, LANES), U32),
                   jax.ShapeDtypeStruct((N_EXPERTS, bsz * s), F32)],
        compiler_params=_params(("parallel", "arbitrary")),
        name="outproj_norm2_router",
    )(y_fox, y_dil, x, gate1, scale2, shift2, g_fox, g_dil, wo1, wo2, wr_hi, wr_lo)


def _first_argmax(v, row, size):
    m = jnp.max(v, axis=0, keepdims=True)
    return m, jnp.min(jnp.where(v == m, row, size), axis=0, keepdims=True)


def _route_kernel(lg_ref, rb_ref, tri_ref, e_ref, w_ref, rk_ref, cnt_ref, cnt_s):
    @pl.when(pl.program_id(0) == 0)
    def _():
        cnt_s[...] = jnp.zeros(cnt_s.shape, F32)

    t = lg_ref.shape[1]
    gsz = N_EXPERTS // N_GROUPS
    scores = jax.nn.sigmoid(lg_ref[...])
    sel = scores + rb_ref[...]
    row_g = lax.broadcasted_iota(jnp.int32, (gsz, t), 0)
    grp = []
    for g in range(N_GROUPS):
        v = sel[g * gsz:(g + 1) * gsz]
        m1, i1 = _first_argmax(v, row_g, gsz)
        m2 = jnp.max(jnp.where(row_g == i1, -jnp.inf, v), axis=0, keepdims=True)
        grp.append(m1 + m2)
    gv = jnp.concatenate(grp, axis=0)
    row8 = lax.broadcasted_iota(jnp.int32, (N_GROUPS, t), 0)
    pen = jnp.full((N_GROUPS, t), -jnp.inf, F32)
    for _ in range(TOP_K_GROUPS):
        _, ix = _first_argmax(gv, row8, N_GROUPS)
        pen = jnp.where(row8 == ix, 0.0, pen)
        gv = jnp.where(row8 == ix, -jnp.inf, gv)
    selm = jnp.concatenate([sel[g * gsz:(g + 1) * gsz] + pen[g:g + 1] for g in range(N_GROUPS)], axis=0)

    row = lax.broadcasted_iota(jnp.int32, (N_EXPERTS, t), 0)
    v = selm
    idxs, scs = [], []
    for _ in range(TOP_K):
        _, ix = _first_argmax(v, row, N_EXPERTS)
        hit = row == ix
        idxs.append(ix)
        scs.append(jnp.sum(jnp.where(hit, scores, 0.0), axis=0, keepdims=True))
        v = jnp.where(hit, -jnp.inf, v)
    chosen = jnp.where(v != selm, 1.0, 0.0)
    before = jnp.dot(chosen.astype(BF16), tri_ref[...], preferred_element_type=F32) + cnt_s[...]
    rks = [jnp.sum(jnp.where(row == ix, before, 0.0), axis=0, keepdims=True) for ix in idxs]
    sc = jnp.concatenate(scs, axis=0)
    e_ref[...] = jnp.concatenate(idxs, axis=0)
    w_ref[...] = sc / jnp.sum(sc, axis=0, keepdims=True) * ROUTED_SCALE
    rk_ref[...] = jnp.concatenate(rks, axis=0).astype(jnp.int32)
    cnt_s[...] = cnt_s[...] + jnp.sum(chosen, axis=1, keepdims=True)
    cnt_ref[...] = cnt_s[...]


def _route(logits_t, router_bias):
    e, n = logits_t.shape
    t = ROUTE_TILE
    tri = jnp.triu(jnp.ones((t, t), BF16), k=1)
    tile = pl.BlockSpec((TOP_K, t), lambda i: (0, i))
    return pl.pallas_call(
        _route_kernel,
        grid=(n // t,),
        in_specs=[pl.BlockSpec((e, t), lambda i: (0, i)),
                  pl.BlockSpec((e, 1), lambda i: (0, 0)),
                  pl.BlockSpec((t, t), lambda i: (0, 0))],
        out_specs=[tile, tile, tile, pl.BlockSpec((e, 1), lambda i: (0, 0))],
        out_shape=[jax.ShapeDtypeStruct((TOP_K, n), jnp.int32), jax.ShapeDtypeStruct((TOP_K, n), F32),
                   jax.ShapeDtypeStruct((TOP_K, n), jnp.int32), jax.ShapeDtypeStruct((e, 1), F32)],
        scratch_shapes=[pltpu.VMEM((e, 1), F32)],
        compiler_params=_params(("arbitrary",)),
        name="route_topk_rank",
    )(logits_t, router_bias.reshape(e, 1).astype(F32), tri)


def _dest_kernel(e_ref, rk_ref, ps_ref, d_ref):
    t = e_ref.shape[1]
    row = lax.broadcasted_iota(jnp.int32, (N_EXPERTS, t), 0)
    ps = ps_ref[...]
    base = [jnp.sum(jnp.where(row == e_ref[k:k + 1, :], ps, 0.0), axis=0, keepdims=True) for k in range(TOP_K)]
    d_ref[...] = jnp.concatenate(base, axis=0).astype(jnp.int32) + rk_ref[...]


def _dest_rows(eidx, rank, pstart):
    _, n = eidx.shape
    t = ROUTE_TILE
    tile = pl.BlockSpec((TOP_K, t), lambda i: (0, i))
    return pl.pallas_call(
        _dest_kernel,
        grid=(n // t,),
        in_specs=[tile, tile, pl.BlockSpec((N_EXPERTS, 1), lambda i: (0, 0))],
        out_specs=tile,
        out_shape=jax.ShapeDtypeStruct((TOP_K, n), jnp.int32),
        compiler_params=_params(("parallel",)),
        name="dest_rows",
    )(eidx, rank, pstart.astype(F32).reshape(N_EXPERTS, 1))


def _block_plan(counts, nblk):
    counts = counts.reshape(N_EXPERTS).astype(jnp.int32)
    padded = (counts + FFN_BLOCK - 1) // FFN_BLOCK * FFN_BLOCK
    pend = jnp.cumsum(padded).astype(jnp.int32)
    pstart = pend - padded
    first_row = jnp.arange(nblk, dtype=jnp.int32) * FFN_BLOCK
    block_e = jnp.minimum(jnp.sum(pend[None, :] <= first_row[:, None], axis=1), N_EXPERTS - 1).astype(jnp.int32)
    nused = (pend[-1:] // FFN_BLOCK).astype(jnp.int32)
    ids = jnp.arange(N_EXPERTS, dtype=jnp.int32)
    following = lax.cummin(jnp.where(counts > 0, ids, N_EXPERTS), reverse=True)
    following = jnp.concatenate([following[1:], jnp.full((1,), N_EXPERTS, jnp.int32)])
    following = jnp.where(following >= N_EXPERTS, -1, following)
    next_e = jnp.sum(jnp.where(block_e[:, None] == ids[None, :], following[None, :], 0), axis=1).astype(jnp.int32)
    return pstart, pstart + counts, pend, block_e, next_e, nused


def _tile_copy(src_ref, src_row, dst_ref, dst_row, sem):
    return pltpu.make_async_copy(src_ref.at[pl.ds(pl.multiple_of(src_row * TILE_ROWS, TILE_ROWS), TILE_ROWS), :],
                                 dst_ref.at[pl.ds(pl.multiple_of(dst_row * TILE_ROWS, TILE_ROWS), TILE_ROWS), :],
                                 sem)


def _scatter_kernel(cend_ref, pend_ref, dst_ref, h3_ref, z_ref, xs_hbm, st0, st1, sem, zsem):
    i = pl.program_id(0)
    nsteps = pl.num_programs(0)
    tm = dst_ref.shape[1]
    stage = (st0, st1)

    def zero_fill(e, wait):
        pad = pend_ref[e] - cend_ref[e]
        for p in ZERO_CHUNKS:
            @pl.when((pad & p) != 0)
            def _(p=p):
                row = cend_ref[e] + (pad & (-2 * p))
                dst = xs_hbm.at[pl.ds(pl.multiple_of(row * TILE_ROWS, TILE_ROWS), p * TILE_ROWS), :]
                copy = pltpu.make_async_copy(z_ref.at[pl.ds(0, p * TILE_ROWS), :], dst, zsem)
                copy.wait() if wait else copy.start()

    def wait_slot(slot):
        for _ in range(TOP_K):
            pltpu.make_async_copy(stage[slot], xs_hbm.at[pl.ds(0, tm * TILE_ROWS), :], sem.at[slot]).wait()

    @pl.when(i == 0)
    def _():
        def per_expert(e, carry):
            zero_fill(e, False)

            @pl.when(e > 0)
            def _():
                zero_fill(e - 1, True)
            return carry

        lax.fori_loop(0, N_EXPERTS, per_expert, 0)
        zero_fill(N_EXPERTS - 1, True)

    for slot in range(2):
        @pl.when(i % 2 == slot)
        def _(slot=slot):
            @pl.when(i >= 2)
            def _():
                wait_slot(slot)
            stage[slot][...] = h3_ref[...]
            for k in range(TOP_K):
                def body(jj, carry, k=k):
                    for u in range(ISSUE_UNROLL):
                        j = jj * ISSUE_UNROLL + u
                        _tile_copy(stage[slot], j, xs_hbm, dst_ref[k, j], sem.at[slot]).start()
                    return carry
                lax.fori_loop(0, tm // ISSUE_UNROLL, body, 0)

    @pl.when(i == nsteps - 1)
    def _():
        wait_slot(0)
        wait_slot(1)


def _dispatch_rows(h3, dest, cend, pend, rows):
    _, n = dest.shape
    tm = SCATTER_TILE
    assert n // tm >= 2
    stage = pltpu.VMEM((tm * TILE_ROWS, LANES), U32)
    grid_spec = pltpu.PrefetchScalarGridSpec(
        num_scalar_prefetch=2,
        grid=(n // tm,),
        in_specs=[pl.BlockSpec((TOP_K, tm), lambda i, ce, pe: (0, i), memory_space=pltpu.SMEM),
                  pl.BlockSpec((tm * TILE_ROWS, LANES), lambda i, ce, pe: (i, 0)),
                  pl.BlockSpec((ZERO_CHUNKS[0] * TILE_ROWS, LANES), lambda i, ce, pe: (0, 0))],
        out_specs=pl.BlockSpec(memory_space=pl.ANY),
        scratch_shapes=[stage, stage, pltpu.SemaphoreType.DMA((2,)), pltpu.SemaphoreType.DMA],
    )
    return pl.pallas_call(
        _scatter_kernel,
        grid_spec=grid_spec,
        out_shape=jax.ShapeDtypeStruct((rows * TILE_ROWS, LANES), U32),
        compiler_params=_params(("arbitrary",)),
        name="dispatch_scatter",
    )(cend, pend, dest, h3, jnp.zeros((ZERO_CHUNKS[0] * TILE_ROWS, LANES), U32))


def _ffn_kernel(be_ref, nx_ref, nu_ref, x_ref, wg_hbm, wu_hbm, wd_hbm, y_ref,
                wgf, wuf, wdf, wgb, wub, wdb, sem, run):
    first = pl.program_id(0) * FFN_PER_STEP
    nused = nu_ref[0]

    def fetch(e, slot):
        return [pltpu.make_async_copy(src.at[e], dst.at[slot], sem.at[slot])
                for src, dst in ((wg_hbm, wgf), (wu_hbm, wuf), (wd_hbm, wdf))]

    @pl.when(first == 0)
    def _():
        run[0] = -1
        for c in fetch(be_ref[0], 0):
            c.start()

    slots = []
    for part in range(FFN_PER_STEP):
        blk = first + part

        @pl.when((blk < nused) & ((blk == 0) | (be_ref[blk] != be_ref[jnp.maximum(blk - 1, 0)])))
        def _(blk=blk):
            r = run[0] + 1
            run[0] = r
            fslot, slot = r % 2, r % FFN_PER_STEP
            for c in fetch(be_ref[blk], fslot):
                c.wait()
            wgb[slot] = wgf[fslot].astype(BF16)
            wub[slot] = wuf[fslot].astype(BF16)
            wdb[slot] = wdf[fslot].astype(BF16)

            @pl.when(nx_ref[blk] >= 0)
            def _():
                for c in fetch(nx_ref[blk], 1 - fslot):
                    c.start()

        slots.append(run[0] % FFN_PER_STEP)

    @pl.when(first < nused)
    def _():
        for part, slot in enumerate(slots):
            base = part * FFN_BLOCK * TILE_ROWS
            x = jnp.concatenate([c.astype(BF16) for c in _from_row_tiles(x_ref, FFN_BLOCK, base)], axis=1)
            act = _silu(jnp.dot(x, wgb[slot], preferred_element_type=F32)) * jnp.dot(
                x, wub[slot], preferred_element_type=F32)
            _to_row_tiles(y_ref, jnp.dot(act.astype(BF16), wdb[slot], preferred_element_type=F32), base)

    @pl.when(first >= nused)
    def _():
        y_ref[...] = jnp.zeros(y_ref.shape, U32)


def _routed_experts(xs, block_e, next_e, nused, w_gate, w_up, w_down):
    rows = xs.shape[0] // TILE_ROWS
    nblk = rows // FFN_BLOCK
    _, d, hid = w_gate.shape
    step_rows = FFN_PER_STEP * FFN_BLOCK * TILE_ROWS
    hbm = pl.BlockSpec(memory_space=pl.ANY)
    grid_spec = pltpu.PrefetchScalarGridSpec(
        num_scalar_prefetch=3,
        grid=(nblk // FFN_PER_STEP,),
        in_specs=[pl.BlockSpec((step_rows, LANES),
                               lambda i, be, nx, nu: (jnp.minimum(i, (nu[0] - 1) // FFN_PER_STEP), 0)),
                  hbm, hbm, hbm],
        out_specs=pl.BlockSpec((step_rows, LANES), lambda i, be, nx, nu: (i, 0)),
        scratch_shapes=[pltpu.VMEM((2, d, hid), F32), pltpu.VMEM((2, d, hid), F32), pltpu.VMEM((2, hid, d), F32),
                        pltpu.VMEM((FFN_PER_STEP, d, hid), BF16), pltpu.VMEM((FFN_PER_STEP, d, hid), BF16),
                        pltpu.VMEM((FFN_PER_STEP, hid, d), BF16),
                        pltpu.SemaphoreType.DMA((2,)), pltpu.SMEM((1,), jnp.int32)],
    )
    return pl.pallas_call(
        _ffn_kernel,
        grid_spec=grid_spec,
        out_shape=jax.ShapeDtypeStruct((rows * TILE_ROWS, LANES), U32),
        compiler_params=_params(("arbitrary",)),
        name="routed_experts",
    )(block_e, next_e, nused, xs, w_gate, w_up, w_down)


def _gather_start(idx_ref, src_hbm, dst_vmem, sem):
    nk, tm = idx_ref.shape
    for k in range(nk):
        def body(jj, carry, k=k):
            for u in range(ISSUE_UNROLL):
                j = jj * ISSUE_UNROLL + u
                _tile_copy(src_hbm, idx_ref[k, j], dst_vmem, k * tm + j, sem).start()
            return carry
        lax.fori_loop(0, tm // ISSUE_UNROLL, body, 0)


def _gather_wait(src_hbm, dst_vmem, sem):
    pltpu.make_async_copy(src_hbm.at[pl.ds(0, dst_vmem.shape[0]), :], dst_vmem, sem).wait()


def _combine_kernel(dst_ref, dstn_ref, wt_ref, x1_ref, h3_ref, g2_ref, gfin_ref, wsg_ref, wsu_ref, wsd_ref, ys_hbm,
                    o_ref, yb0, yb1, sem, moe_s, *, nsteps):
    i = pl.program_id(0)
    tm = COMBINE_TILE
    bufs = (yb0, yb1)

    @pl.when(i == 0)
    def _():
        _gather_start(dst_ref, ys_hbm, yb0, sem.at[0])

    for slot in range(2):
        @pl.when((i + 1 < nsteps) & ((i + 1) % 2 == slot))
        def _(slot=slot):
            _gather_start(dstn_ref, ys_hbm, bufs[slot], sem.at[slot])

    hb = jnp.concatenate([c.astype(BF16) for c in _from_row_tiles(h3_ref, tm)], axis=1)
    act = _silu(jnp.dot(hb, wsg_ref[...], preferred_element_type=F32)) * jnp.dot(
        hb, wsu_ref[...], preferred_element_type=F32)
    moe_s[...] = jnp.dot(act.astype(BF16), wsd_ref[...], preferred_element_type=F32)

    for slot in range(2):
        @pl.when(i % 2 == slot)
        def _(slot=slot):
            _gather_wait(ys_hbm, bufs[slot], sem.at[slot])
            wt = wt_ref[...]
            cols = None
            for k in range(TOP_K):
                part = [c * wt[:, k:k + 1] for c in _from_row_tiles(bufs[slot], tm, base=k * tm * TILE_ROWS)]
                cols = part if cols is None else [a + b for a, b in zip(cols, part)]
            moe = jnp.concatenate(cols, axis=1) + moe_s[...]
            o_ref[0] = _rms(x1_ref[0] + g2_ref[0] * moe) * gfin_ref[...]


def _combine(ys, dest, wts_t, x1, h3, gate2, g_final, wsg, wsu, wsd):
    bsz, s, d = x1.shape
    tm = COMBINE_TILE
    nt = s // tm
    nsteps = bsz * nt
    vec = pl.BlockSpec((1, 1, d), lambda i: (i // nt, 0, 0))
    full = lambda w: pl.BlockSpec(w.shape, lambda i: (0, 0))
    return pl.pallas_call(
        functools.partial(_combine_kernel, nsteps=nsteps),
        grid=(nsteps,),
        in_specs=[pl.BlockSpec((TOP_K, tm), lambda i: (0, i), memory_space=pltpu.SMEM),
                  pl.BlockSpec((TOP_K, tm), lambda i: (0, jnp.minimum(i + 1, nsteps - 1)),
                               memory_space=pltpu.SMEM),
                  pl.BlockSpec((tm, TOP_K), lambda i: (i, 0)),
                  pl.BlockSpec((1, tm, d), lambda i: (i // nt, i % nt, 0)),
                  pl.BlockSpec((tm * TILE_ROWS, LANES), lambda i: (i, 0)),
                  vec,
                  pl.BlockSpec((1, d), lambda i: (0, 0)),
                  full(wsg), full(wsu), full(wsd),
                  pl.BlockSpec(memory_space=pl.ANY)],
        out_specs=pl.BlockSpec((1, tm, d), lambda i: (i // nt, i % nt, 0)),
        out_shape=jax.ShapeDtypeStruct((bsz, s, d), F32),
        scratch_shapes=[pltpu.VMEM((tm * TOP_K * TILE_ROWS, LANES), U32),
                        pltpu.VMEM((tm * TOP_K * TILE_ROWS, LANES), U32),
                        pltpu.SemaphoreType.DMA((2,)),
                        pltpu.VMEM((tm, d), F32)],
        compiler_params=_params(("arbitrary",)),
        name="combine_shared_final_norm",
    )(dest, dest, wts_t, x1, h3, gate2, g_final.reshape(1, d), wsg, wsu, wsd, ys)


def kernel(x, c, w_in, b_forget, g_fox_out, g_dil_out, w_out, w_ada, b_ada, w_router, router_bias,
           w_exp_gate, w_exp_up, w_exp_down, w_sh_gate, w_sh_up, w_sh_down, rel_bias, g_final):
    bsz, s, d = x.shape
    depth = w_in.shape[0]
    assert depth == 1 and d == 2 * TILE_ROWS * LANES and s % (DIL_BLOCK * DILATIONS[-1]) == 0
    assert all(s % t == 0 for t in (ROW_TILE, FORGET_BLOCK, FOX_TILE, DIL_CHUNK, COMBINE_TILE, SCATTER_TILE))
    assert (bsz * s) % ROUTE_TILE == 0 and (bsz * s * TOP_K // FFN_BLOCK + N_EXPERTS) % FFN_PER_STEP == 0
    l = 0
    mod = _modulation(c, w_ada[l], b_ada[l])
    shift1, scale1, gate1, shift2, scale2, gate2 = [m[:, None, :] for m in jnp.split(mod, 6, axis=-1)]

    qscale = HEAD_DIM ** -0.5
    o3 = 3 * WIDTH
    w = w_in[l]
    wq_t = (w[:, :WIDTH] * (qscale * LOG2E)).T.astype(BF16)
    wk = w[:, WIDTH:2 * WIDTH].astype(BF16)
    wv_t = w[:, 2 * WIDTH:o3].T.astype(BF16)
    w_flog = jnp.pad(w[:, o3:o3 + N_HEADS], ((0, 0), (0, LANES - N_HEADS))).astype(BF16)
    wd0 = o3 + N_HEADS
    w_dil = jnp.concatenate([w[:, wd0:wd0 + WIDTH] * qscale, w[:, wd0 + WIDTH:]], axis=1).astype(BF16)

    q_t, k_f, v_t, qkv_d, flog = _in_projection(x, scale1, shift1, wq_t, wk, wv_t, w_dil, w_flog)
    y_fox = _fox_attention(q_t, k_f, _forget_cumsum(flog, b_forget[l]), v_t)
    y_dil = _dilated_attention(qkv_d, _dilated_bias(rel_bias))

    wr = w_router[l].T
    wr_hi = wr.astype(BF16)
    wr_lo = (wr - wr_hi.astype(F32)).astype(BF16)
    wo = w_out[l].astype(BF16)
    x1, h3, logits_t = _out_projection(
        y_fox, y_dil, x, gate1, scale2, shift2, g_fox_out[l].reshape(1, WIDTH),
        g_dil_out[l].reshape(1, WIDTH), wo[:WIDTH], wo[WIDTH:], wr_hi, wr_lo)

    eidx, wts, rank, counts = _route(logits_t, router_bias[l])
    rows = bsz * s * TOP_K + N_EXPERTS * FFN_BLOCK
    pstart, cend, pend, block_e, next_e, nused = _block_plan(counts, rows // FFN_BLOCK)
    dest = _dest_rows(eidx, rank, pstart)
    xs = _dispatch_rows(h3, dest, cend, pend, rows)
    ys = _routed_experts(xs, block_e, next_e, nused, w_exp_gate[l], w_exp_up[l], w_exp_down[l])
    return _combine(ys, dest, wts.T, x1, h3, gate2, g_final,
                    w_sh_gate[l].astype(BF16), w_sh_up[l].astype(BF16), w_sh_down[l].astype(BF16))
```

```python
import functools

import numpy as np
import jax
import jax.numpy as jnp
from jax import lax
from jax.experimental import pallas as pl
from jax.experimental.pallas import tpu as pltpu

F32 = jnp.float32
BF16 = jnp.bfloat16
U32 = jnp.uint32

HEAD_DIM = 64
N_HEADS = 8
WIDTH = N_HEADS * HEAD_DIM
DIL_BLOCK = 128
DILATIONS = (1, 4, 16)
T5_NUM_BUCKETS = 32
T5_MAX_DISTANCE = 2048
N_EXPERTS = 256
TOP_K = 8
N_GROUPS = 8
TOP_K_GROUPS = 4
ROUTED_SCALE = 2.5
EPS = 1e-6

LANES = 128
VMEM_BYTES = 64 * 1024 * 1024
VMEM_LIMIT = VMEM_BYTES * 7 // 8

HEAD_PAIRS = WIDTH // LANES
TILE_ROWS = 4
NEG = -1e30
LOG2E = float(np.log2(np.e))

MOD_TILE = 1536
ROW_TILE = 512
FORGET_BLOCK = 512
FOX_TILE = 512
DIL_UNROLL = 16
DIL_CHUNK = 512
FFN_BLOCK = 512
FFN_PER_STEP = 2
ZERO_CHUNKS = tuple(1 << b for b in reversed(range(FFN_BLOCK.bit_length() - 1)))
COMBINE_TILE = 256
SCATTER_TILE = 256
ROUTE_TILE = 512
ISSUE_UNROLL = 8


def _params(semantics):
    return pltpu.CompilerParams(dimension_semantics=semantics, vmem_limit_bytes=VMEM_LIMIT)


def _rms(x):
    return x * lax.rsqrt(jnp.mean(x * x, axis=-1, keepdims=True) + EPS)


def _silu(x):
    return x * jax.nn.sigmoid(x)


def _dot_nt(a, b):
    return lax.dot_general(a, b, (((1,), (1,)), ((), ())), preferred_element_type=F32)


def _mod_kernel(c_ref, w_ref, b_ref, o_ref):
    o_ref[...] = jnp.dot(_silu(c_ref[...]), w_ref[...], precision=lax.Precision.HIGHEST,
                         preferred_element_type=F32) + b_ref[...]


def _modulation(c, w_ada, b_ada):
    bsz, d = c.shape
    n = w_ada.shape[1]
    tn = MOD_TILE
    return pl.pallas_call(
        _mod_kernel,
        grid=(n // tn,),
        in_specs=[pl.BlockSpec((bsz, d), lambda j: (0, 0)),
                  pl.BlockSpec((d, tn), lambda j: (0, j)),
                  pl.BlockSpec((1, tn), lambda j: (0, j))],
        out_specs=pl.BlockSpec((bsz, tn), lambda j: (0, j)),
        out_shape=jax.ShapeDtypeStruct((bsz, n), F32),
        compiler_params=_params(("arbitrary",)),
        name="adaln_mod",
    )(c, w_ada, b_ada.reshape(1, n))


def _inproj_kernel(x_ref, sc_ref, sh_ref, wq_ref, wk_ref, wv_ref, wd_ref, wl_ref,
                   oq_ref, ok_ref, ov_ref, od_ref, ol_ref):
    h = _rms(x_ref[0]) * (1.0 + sc_ref[0]) + sh_ref[0]
    hb = h.astype(BF16)
    oq_ref[0] = _dot_nt(wq_ref[...], hb).astype(BF16)
    ov_ref[0] = _dot_nt(wv_ref[...], hb).astype(BF16)
    ok_ref[0] = jnp.dot(hb, wk_ref[...], preferred_element_type=F32).astype(BF16)
    od_ref[0] = jnp.dot(hb, wd_ref[...], preferred_element_type=F32).astype(BF16)
    ol_ref[0] = jnp.dot(hb, wl_ref[...], preferred_element_type=F32)


def _in_projection(x, scale1, shift1, wq_t, wk, wv_t, w_dil, w_flog):
    bsz, s, d = x.shape
    tm = ROW_TILE
    vec = pl.BlockSpec((1, 1, d), lambda b, i: (b, 0, 0))
    full = lambda w: pl.BlockSpec(w.shape, lambda b, i: (0, 0))
    row = lambda n: pl.BlockSpec((1, tm, n), lambda b, i: (b, i, 0))
    col = pl.BlockSpec((1, WIDTH, tm), lambda b, i: (b, 0, i))
    return pl.pallas_call(
        _inproj_kernel,
        grid=(bsz, s // tm),
        in_specs=[row(d), vec, vec, full(wq_t), full(wk), full(wv_t), full(w_dil), full(w_flog)],
        out_specs=[col, row(WIDTH), col, row(3 * WIDTH), row(LANES)],
        out_shape=[jax.ShapeDtypeStruct((bsz, WIDTH, s), BF16),
                   jax.ShapeDtypeStruct((bsz, s, WIDTH), BF16),
                   jax.ShapeDtypeStruct((bsz, WIDTH, s), BF16),
                   jax.ShapeDtypeStruct((bsz, s, 3 * WIDTH), BF16),
                   jax.ShapeDtypeStruct((bsz, s, LANES), F32)],
        compiler_params=_params(("parallel", "arbitrary")),
        name="norm1_inproj",
    )(x, scale1, shift1, wq_t, wk, wv_t, w_dil, w_flog)


FORGET_PARTS = 3


def _split_bf16(x):
    parts = []
    for _ in range(FORGET_PARTS):
        p = x.astype(BF16)
        parts.append(p)
        x = x - p.astype(F32)
    return parts


def _forget_kernel(f_ref, b_ref, tri_ref, place_ref, o_ref):
    blk = tri_ref.shape[0]
    carry = jnp.zeros((1, LANES), F32)
    for c in range(f_ref.shape[1] // blk):
        rows = pl.ds(c * blk, blk)
        z = f_ref[0, rows, :] + b_ref[...]
        ls = jnp.minimum(z, 0.0) - jnp.log1p(jnp.exp(-jnp.abs(z)))
        cum = carry
        for p in _split_bf16(ls):
            cum = cum + jnp.dot(tri_ref[...], p, preferred_element_type=F32)
        carry = cum[blk - 1:blk, :]
        out = None
        for n, p in enumerate(_split_bf16(cum * LOG2E)):
            d = jnp.dot(p, place_ref[n], preferred_element_type=F32)
            out = d if out is None else out + d
        o_ref[0, rows, :] = out.astype(BF16)


def _forget_cumsum(flog, b_forget):
    bsz, s, _ = flog.shape
    blk = FORGET_BLOCK
    place = np.zeros((FORGET_PARTS, LANES, WIDTH), np.float32)
    for h in range(N_HEADS):
        for n in range(FORGET_PARTS):
            place[n, h, (h // 2) * LANES + (h % 2) * HEAD_DIM + n] = 1.0
    return pl.pallas_call(
        _forget_kernel,
        grid=(bsz,),
        in_specs=[pl.BlockSpec((1, s, LANES), lambda b: (b, 0, 0)),
                  pl.BlockSpec((1, LANES), lambda b: (0, 0)),
                  pl.BlockSpec((blk, blk), lambda b: (0, 0)),
                  pl.BlockSpec(place.shape, lambda b: (0, 0, 0))],
        out_specs=pl.BlockSpec((1, s, WIDTH), lambda b: (b, 0, 0)),
        out_shape=jax.ShapeDtypeStruct((bsz, s, WIDTH), BF16),
        compiler_params=_params(("parallel",)),
        name="forget_cumsum",
    )(flog, jnp.pad(b_forget, (0, LANES - N_HEADS)).reshape(1, LANES),
      jnp.tril(jnp.ones((blk, blk), BF16)), jnp.asarray(place, BF16))


def _stack_heads(q):
    lane = lax.broadcasted_iota(jnp.int32, q.shape, 1)
    zero = jnp.zeros_like(q)
    return jnp.concatenate([jnp.where(lane < HEAD_DIM, q, zero), jnp.where(lane >= HEAD_DIM, q, zero)], axis=0)


def _unstack_heads(o2):
    rows = o2.shape[0] // 2
    lane = lax.broadcasted_iota(jnp.int32, (rows, LANES), 1)
    return jnp.where(lane < HEAD_DIM, o2[:rows], o2[rows:])


def _fox_kernel(q_ref, k_ref, fa_ref, v_ref, o_ref, m_s, l_s, acc_s, s_a, s_b, *, t):
    i = pl.program_id(2)
    q_t = q_ref[0]
    row = lax.broadcasted_iota(jnp.int32, (LANES, 2 * t), 0)
    col = lax.broadcasted_iota(jnp.int32, (LANES, 2 * t), 1)
    head_row = jnp.where(col < t, 0, HEAD_DIM)
    q2 = jnp.concatenate([q_t, q_t], axis=1)
    q2 = jnp.where((row >= head_row) & (row < head_row + HEAD_DIM), q2, jnp.zeros_like(q2))
    minus = jnp.where((row >= head_row) & (row < head_row + FORGET_PARTS), -1.0, 0.0).astype(BF16)
    qa = jnp.concatenate([q2, minus], axis=0)
    m_s[...] = jnp.full(m_s.shape, NEG, F32)
    l_s[...] = jnp.zeros(l_s.shape, F32)
    acc_s[...] = jnp.zeros(acc_s.shape, F32)

    def scores(ks, tk):
        kk = jnp.concatenate([k_ref[0, pl.ds(ks, tk), :], fa_ref[0, pl.ds(ks, tk), :]], axis=1)
        return jnp.dot(kk, qa, preferred_element_type=F32)

    def absorb(s, ks, tk):
        m_prev = m_s[...]
        m_new = jnp.maximum(m_prev, jnp.max(s, axis=0, keepdims=True))
        alpha = jnp.exp2(m_prev - m_new)
        p = jnp.exp2(s - m_new)
        l_s[...] = alpha * l_s[...] + jnp.sum(p, axis=0, keepdims=True)
        acc_s[...] = alpha * acc_s[...] + jnp.dot(v_ref[0, :, pl.ds(ks, tk)], p.astype(BF16),
                                                  preferred_element_type=F32)
        m_s[...] = m_new

    npair = i // 2

    def tile(j):
        return pl.multiple_of(j * 2 * t, 2 * t)

    @pl.when(npair > 0)
    def _():
        s_a[...] = scores(0, 2 * t)

        def body(jj, carry):
            j = 2 * jj
            s_b[...] = scores(tile(j + 1), 2 * t)
            absorb(s_a[...], tile(j), 2 * t)
            s_a[...] = scores(tile(jnp.minimum(j + 2, npair - 1)), 2 * t)
            absorb(s_b[...], tile(j + 1), 2 * t)
            return carry

        lax.fori_loop(0, npair // 2, body, 0)

        @pl.when(npair % 2 == 1)
        def _():
            absorb(s_a[...], tile(npair - 1), 2 * t)

    def last(ks, tk):
        r = lax.broadcasted_iota(jnp.int32, (tk, 2 * t), 0)
        c = lax.broadcasted_iota(jnp.int32, (tk, 2 * t), 1)
        absorb(jnp.where(r <= jnp.where(c >= t, c - t, c) + (tk - t), scores(ks, tk), NEG), ks, tk)

    @pl.when(i % 2 == 1)
    def _():
        last(tile(npair), 2 * t)

    @pl.when(i % 2 == 0)
    def _():
        last(pl.multiple_of(i * t, t), t)

    o2 = acc_s[...] / l_s[...]
    o_t = jnp.where(lax.broadcasted_iota(jnp.int32, (LANES, t), 0) < HEAD_DIM, o2[:, :t], o2[:, t:])
    o_ref[0] = o_t.T


def _fox_attention(q_t, k, f_aug, v_t):
    bsz, s, _ = k.shape
    t = FOX_TILE
    keys = pl.BlockSpec((1, s, LANES), lambda b, h, i: (b, 0, h))
    return pl.pallas_call(
        functools.partial(_fox_kernel, t=t),
        grid=(bsz, HEAD_PAIRS, s // t),
        in_specs=[pl.BlockSpec((1, LANES, t), lambda b, h, i: (b, h, i)),
                  keys, keys,
                  pl.BlockSpec((1, LANES, s), lambda b, h, i: (b, h, 0))],
        out_specs=pl.BlockSpec((1, t, LANES), lambda b, h, i: (b, i, h)),
        out_shape=jax.ShapeDtypeStruct((bsz, s, WIDTH), F32),
        scratch_shapes=[pltpu.VMEM((1, 2 * t), F32), pltpu.VMEM((1, 2 * t), F32),
                        pltpu.VMEM((LANES, 2 * t), F32), pltpu.VMEM((2 * t, 2 * t), F32),
                        pltpu.VMEM((2 * t, 2 * t), F32)],
        compiler_params=_params(("parallel", "parallel", "arbitrary")),
        name="fox_attention",
    )(q_t, k, f_aug, v_t)


def _t5_bucket(dist):
    max_exact = T5_NUM_BUCKETS // 2
    d = np.maximum(dist, 1).astype(np.float32)
    large = max_exact + (np.log(d / max_exact) / np.log(T5_MAX_DISTANCE / max_exact)
                         * (T5_NUM_BUCKETS - max_exact)).astype(np.int32)
    large = np.minimum(large, T5_NUM_BUCKETS - 1)
    return np.where(dist < max_exact, dist, large).astype(np.int32)


def _dilated_bias(rel_bias):
    blk = DIL_BLOCK
    period = 3 * blk
    m = np.arange(period)
    rel = np.where(m < 2 * blk, blk - m, blk - (m - period))
    band = (rel >= 0) & (rel <= blk)
    onehot = np.zeros((len(DILATIONS), period, T5_NUM_BUCKETS), np.float32)
    for bi, dil in enumerate(DILATIONS):
        onehot[bi, m, _t5_bucket(np.clip(rel, 0, blk) * dil)] = 1.0
    w = jnp.einsum('bmk,kh->bhm', jnp.asarray(onehot), rel_bias.astype(F32),
                   precision=lax.Precision.HIGHEST)
    w = jnp.where(jnp.asarray(band), w, NEG)
    flat = jnp.tile(w, (1, 1, blk))[:, :, :blk * (period - 1)]
    table = flat.reshape(len(DILATIONS), N_HEADS, blk, period - 1)[..., :2 * blk]
    table = table.reshape(len(DILATIONS), HEAD_PAIRS, 2 * blk, 2 * blk)
    first = jnp.where(jnp.arange(2 * blk) < blk, NEG, table)
    return jnp.stack([table, first], axis=1)


def _dil_kernel(q_ref, k_ref, v_ref, bias_ref, o_ref, qf, kf, vf, ob0, ob1, ob2, ls0, ls1, ls2, *, s_len):
    blk = DIL_BLOCK
    qf[...] = q_ref[0].astype(F32)
    kf[...] = k_ref[0].astype(F32)
    vf[...] = v_ref[0].astype(F32)
    for bi, (dil, ob, ls) in enumerate(zip(DILATIONS, (ob0, ob1, ob2), (ls0, ls1, ls2))):
        span = blk * dil
        nb = s_len // span

        def rows(start, dil=dil):
            return pl.ds(start, blk) if dil == 1 else pl.ds(start, blk, stride=dil)

        def block(tix, carry, bi=bi, span=span, nb=nb, ob=ob, ls=ls, rows=rows):
            n = tix % nb
            start = n * span + tix // nb
            prev = jnp.maximum(start - span, 0)
            q2 = _stack_heads(qf[rows(start), :].astype(BF16))
            kk = jnp.concatenate([kf[rows(prev), :], kf[rows(start), :]], axis=0).astype(BF16)
            vv = jnp.concatenate([vf[rows(prev), :], vf[rows(start), :]], axis=0).astype(BF16)
            s = lax.dot_general(q2, kk, (((1,), (1,)), ((), ())), preferred_element_type=F32)
            s = s + bias_ref[bi, jnp.where(n == 0, 1, 0), 0]
            m = jnp.max(s, axis=1, keepdims=True)
            p = jnp.exp(s - m)
            l = jnp.sum(p, axis=1, keepdims=True)
            o2 = jnp.dot(p.astype(BF16), vv, preferred_element_type=F32) / l
            lse = jnp.broadcast_to(m + jnp.log(l), (2 * blk, LANES))
            ob[rows(start), :] = _unstack_heads(o2)
            ls[rows(start), :] = _unstack_heads(lse)
            return carry

        def blocks(g, carry, block=block):
            for u in range(DIL_UNROLL):
                block(g * DIL_UNROLL + u, carry)
            return carry

        lax.fori_loop(0, dil * nb // DIL_UNROLL, blocks, 0)

    chunk = DIL_CHUNK
    for c in range(s_len // chunk):
        r = pl.ds(c * chunk, chunk)
        l0, l1, l2 = ls0[r, :], ls1[r, :], ls2[r, :]
        mx = jnp.maximum(jnp.maximum(l0, l1), l2)
        e0, e1, e2 = jnp.exp(l0 - mx), jnp.exp(l1 - mx), jnp.exp(l2 - mx)
        o_ref[0, r, :] = (e0 * ob0[r, :] + e1 * ob1[r, :] + e2 * ob2[r, :]) / (e0 + e1 + e2)


def _dilated_attention(qkv, bias):
    bsz, s, _ = qkv.shape
    col = lambda off: pl.BlockSpec((1, s, LANES), lambda b, h: (b, 0, off + h))
    buf = pltpu.VMEM((s, LANES), F32)
    return pl.pallas_call(
        functools.partial(_dil_kernel, s_len=s),
        grid=(bsz, HEAD_PAIRS),
        in_specs=[col(0), col(HEAD_PAIRS), col(2 * HEAD_PAIRS),
                  pl.BlockSpec((len(DILATIONS), 2, 1, 2 * DIL_BLOCK, 2 * DIL_BLOCK), lambda b, h: (0, 0, h, 0, 0))],
        out_specs=pl.BlockSpec((1, s, LANES), lambda b, h: (b, 0, h)),
        out_shape=jax.ShapeDtypeStruct((bsz, s, WIDTH), F32),
        scratch_shapes=[buf] * 9,
        compiler_params=_params(("parallel", "arbitrary")),
        name="dilated_attention",
    )(qkv, qkv, qkv, bias)


def _bf16_bits(x):
    return pltpu.bitcast(x.astype(BF16).astype(F32), U32)


def _to_row_tiles(dst_ref, x, base=0):
    rows, d = x.shape
    for c in range(TILE_ROWS):
        lo = _bf16_bits(x[:, c * LANES:(c + 1) * LANES]) >> 16
        hi = _bf16_bits(x[:, d // 2 + c * LANES:d // 2 + (c + 1) * LANES])
        dst_ref[pl.ds(base + c, rows, stride=TILE_ROWS), :] = lo | hi


def _from_row_tiles(src_ref, rows, base=0):
    lo, hi = [], []
    for c in range(TILE_ROWS):
        w = src_ref[pl.ds(base + c, rows, stride=TILE_ROWS), :]
        lo.append(pltpu.bitcast(w << 16, F32))
        hi.append(pltpu.bitcast(w & jnp.uint32(0xFFFF0000), F32))
    return lo + hi


def _outproj_kernel(yf_ref, yd_ref, x_ref, g1_ref, sc_ref, sh_ref, g2_ref, gf_ref, gd_ref, wo1_ref, wo2_ref,
                    wrh_ref, wrl_ref, wsg_ref, wsu_ref, wsd_ref, xp_ref, h3_ref, lg_ref):
    nf = (_rms(yf_ref[0]) * gf_ref[...]).astype(BF16)
    nd = (_rms(yd_ref[0]) * gd_ref[...]).astype(BF16)
    mix = (jnp.dot(nf, wo1_ref[...], preferred_element_type=F32)
           + jnp.dot(nd, wo2_ref[...], preferred_element_type=F32))
    x1 = x_ref[0] + g1_ref[0] * mix
    h2 = _rms(x1) * (1.0 + sc_ref[0]) + sh_ref[0]
    hb = h2.astype(BF16)
    hl = (h2 - hb.astype(F32)).astype(BF16)
    lg_ref[...] = _dot_nt(wrh_ref[...], hb) + _dot_nt(wrh_ref[...], hl) + _dot_nt(wrl_ref[...], hb)
    act = _silu(jnp.dot(hb, wsg_ref[...], preferred_element_type=F32)) * jnp.dot(
        hb, wsu_ref[...], preferred_element_type=F32)
    shared = jnp.dot(act.astype(BF16), wsd_ref[...], preferred_element_type=F32)
    xp_ref[0] = x1 + g2_ref[0] * shared
    _to_row_tiles(h3_ref, h2)


def _out_projection(y_fox, y_dil, x, gate1, scale2, shift2, gate2, g_fox, g_dil, wo1, wo2, wr_hi, wr_lo,
                    wsg, wsu, wsd):
    bsz, s, d = x.shape
    tm = ROW_TILE
    nt = s // tm
    vec = pl.BlockSpec((1, 1, d), lambda b, i: (b, 0, 0))
    full = lambda w: pl.BlockSpec(w.shape, lambda b, i: (0,) * w.ndim)
    row = lambda n: pl.BlockSpec((1, tm, n), lambda b, i: (b, i, 0))
    return pl.pallas_call(
        _outproj_kernel,
        grid=(bsz, nt),
        in_specs=[row(WIDTH), row(WIDTH), row(d), vec, vec, vec, vec, full(g_fox), full(g_dil), full(wo1),
                  full(wo2), full(wr_hi), full(wr_lo), full(wsg), full(wsu), full(wsd)],
        out_specs=[row(d),
                   pl.BlockSpec((tm * TILE_ROWS, LANES), lambda b, i: (b * nt + i, 0)),
                   pl.BlockSpec((N_EXPERTS, tm), lambda b, i: (0, b * nt + i))],
        out_shape=[jax.ShapeDtypeStruct((bsz, s, d), F32),
                   jax.ShapeDtypeStruct((bsz * s * TILE_ROWS, LANES), U32),
                   jax.ShapeDtypeStruct((N_EXPERTS, bsz * s), F32)],
        compiler_params=_params(("parallel", "arbitrary")),
        name="outproj_norm2_router_shared",
    )(y_fox, y_dil, x, gate1, scale2, shift2, gate2, g_fox, g_dil, wo1, wo2, wr_hi, wr_lo, wsg, wsu, wsd)


def _first_argmax(v, row, size):
    m = jnp.max(v, axis=0, keepdims=True)
    return m, jnp.min(jnp.where(v == m, row, size), axis=0, keepdims=True)


def _route_kernel(lg_ref, rb_ref, tri_ref, e_ref, w_ref, rk_ref, cnt_ref, cnt_s):
    @pl.when(pl.program_id(0) == 0)
    def _():
        cnt_s[...] = jnp.zeros(cnt_s.shape, F32)

    t = lg_ref.shape[1]
    gsz = N_EXPERTS // N_GROUPS
    scores = jax.nn.sigmoid(lg_ref[...])
    sel = scores + rb_ref[...]
    row_g = lax.broadcasted_iota(jnp.int32, (gsz, t), 0)
    grp = []
    for g in range(N_GROUPS):
        v = sel[g * gsz:(g + 1) * gsz]
        m1, i1 = _first_argmax(v, row_g, gsz)
        m2 = jnp.max(jnp.where(row_g == i1, -jnp.inf, v), axis=0, keepdims=True)
        grp.append(m1 + m2)
    gv = jnp.concatenate(grp, axis=0)
    row8 = lax.broadcasted_iota(jnp.int32, (N_GROUPS, t), 0)
    pen = jnp.full((N_GROUPS, t), -jnp.inf, F32)
    for _ in range(TOP_K_GROUPS):
        _, ix = _first_argmax(gv, row8, N_GROUPS)
        pen = jnp.where(row8 == ix, 0.0, pen)
        gv = jnp.where(row8 == ix, -jnp.inf, gv)
    selm = jnp.concatenate([sel[g * gsz:(g + 1) * gsz] + pen[g:g + 1] for g in range(N_GROUPS)], axis=0)

    row = lax.broadcasted_iota(jnp.int32, (N_EXPERTS, t), 0)
    v = selm
    idxs, scs = [], []
    for _ in range(TOP_K):
        _, ix = _first_argmax(v, row, N_EXPERTS)
        hit = row == ix
        idxs.append(ix)
        scs.append(jnp.sum(jnp.where(hit, scores, 0.0), axis=0, keepdims=True))
        v = jnp.where(hit, -jnp.inf, v)
    chosen = jnp.where(v != selm, 1.0, 0.0)
    before = jnp.dot(chosen.astype(BF16), tri_ref[...], preferred_element_type=F32) + cnt_s[...]
    rks = [jnp.sum(jnp.where(row == ix, before, 0.0), axis=0, keepdims=True) for ix in idxs]
    sc = jnp.concatenate(scs, axis=0)
    e_ref[...] = jnp.concatenate(idxs, axis=0)
    w_ref[...] = sc / jnp.sum(sc, axis=0, keepdims=True) * ROUTED_SCALE
    rk_ref[...] = jnp.concatenate(rks, axis=0).astype(jnp.int32)
    cnt_s[...] = cnt_s[...] + jnp.sum(chosen, axis=1, keepdims=True)
    cnt_ref[...] = cnt_s[...]


def _route(logits_t, router_bias):
    e, n = logits_t.shape
    t = ROUTE_TILE
    tri = jnp.triu(jnp.ones((t, t), BF16), k=1)
    tile = pl.BlockSpec((TOP_K, t), lambda i: (0, i))
    return pl.pallas_call(
        _route_kernel,
        grid=(n // t,),
        in_specs=[pl.BlockSpec((e, t), lambda i: (0, i)),
                  pl.BlockSpec((e, 1), lambda i: (0, 0)),
                  pl.BlockSpec((t, t), lambda i: (0, 0))],
        out_specs=[tile, tile, tile, pl.BlockSpec((e, 1), lambda i: (0, 0))],
        out_shape=[jax.ShapeDtypeStruct((TOP_K, n), jnp.int32), jax.ShapeDtypeStruct((TOP_K, n), F32),
                   jax.ShapeDtypeStruct((TOP_K, n), jnp.int32), jax.ShapeDtypeStruct((e, 1), F32)],
        scratch_shapes=[pltpu.VMEM((e, 1), F32)],
        compiler_params=_params(("arbitrary",)),
        name="route_topk_rank",
    )(logits_t, router_bias.reshape(e, 1).astype(F32), tri)


def _dest_kernel(e_ref, rk_ref, ps_ref, d_ref):
    t = e_ref.shape[1]
    row = lax.broadcasted_iota(jnp.int32, (N_EXPERTS, t), 0)
    ps = ps_ref[...]
    base = [jnp.sum(jnp.where(row == e_ref[k:k + 1, :], ps, 0.0), axis=0, keepdims=True) for k in range(TOP_K)]
    d_ref[...] = jnp.concatenate(base, axis=0).astype(jnp.int32) + rk_ref[...]


def _dest_rows(eidx, rank, pstart):
    _, n = eidx.shape
    t = ROUTE_TILE
    tile = pl.BlockSpec((TOP_K, t), lambda i: (0, i))
    return pl.pallas_call(
        _dest_kernel,
        grid=(n // t,),
        in_specs=[tile, tile, pl.BlockSpec((N_EXPERTS, 1), lambda i: (0, 0))],
        out_specs=tile,
        out_shape=jax.ShapeDtypeStruct((TOP_K, n), jnp.int32),
        compiler_params=_params(("parallel",)),
        name="dest_rows",
    )(eidx, rank, pstart.astype(F32).reshape(N_EXPERTS, 1))


def _block_plan(counts, nblk):
    counts = counts.reshape(N_EXPERTS).astype(jnp.int32)
    padded = (counts + FFN_BLOCK - 1) // FFN_BLOCK * FFN_BLOCK
    pend = jnp.cumsum(padded).astype(jnp.int32)
    pstart = pend - padded
    first_row = jnp.arange(nblk, dtype=jnp.int32) * FFN_BLOCK
    block_e = jnp.minimum(jnp.sum(pend[None, :] <= first_row[:, None], axis=1), N_EXPERTS - 1).astype(jnp.int32)
    nused = (pend[-1:] // FFN_BLOCK).astype(jnp.int32)
    ids = jnp.arange(N_EXPERTS, dtype=jnp.int32)
    following = lax.cummin(jnp.where(counts > 0, ids, N_EXPERTS), reverse=True)
    following = jnp.concatenate([following[1:], jnp.full((1,), N_EXPERTS, jnp.int32)])
    following = jnp.where(following >= N_EXPERTS, -1, following)
    next_e = jnp.sum(jnp.where(block_e[:, None] == ids[None, :], following[None, :], 0), axis=1).astype(jnp.int32)
    return pstart, pstart + counts, pend, block_e, next_e, nused


def _tile_copy(src_ref, src_row, dst_ref, dst_row, sem):
    return pltpu.make_async_copy(src_ref.at[pl.ds(pl.multiple_of(src_row * TILE_ROWS, TILE_ROWS), TILE_ROWS), :],
                                 dst_ref.at[pl.ds(pl.multiple_of(dst_row * TILE_ROWS, TILE_ROWS), TILE_ROWS), :],
                                 sem)


def _scatter_kernel(cend_ref, pend_ref, dst_ref, h3_ref, z_ref, xs_hbm, st0, st1, sem, zsem):
    i = pl.program_id(0)
    nsteps = pl.num_programs(0)
    tm = dst_ref.shape[1]
    stage = (st0, st1)

    def zero_fill(e, wait):
        pad = pend_ref[e] - cend_ref[e]
        for p in ZERO_CHUNKS:
            @pl.when((pad & p) != 0)
            def _(p=p):
                row = cend_ref[e] + (pad & (-2 * p))
                dst = xs_hbm.at[pl.ds(pl.multiple_of(row * TILE_ROWS, TILE_ROWS), p * TILE_ROWS), :]
                copy = pltpu.make_async_copy(z_ref.at[pl.ds(0, p * TILE_ROWS), :], dst, zsem)
                copy.wait() if wait else copy.start()

    def wait_slot(slot):
        for _ in range(TOP_K):
            pltpu.make_async_copy(stage[slot], xs_hbm.at[pl.ds(0, tm * TILE_ROWS), :], sem.at[slot]).wait()

    @pl.when(i == 0)
    def _():
        def per_expert(e, carry):
            zero_fill(e, False)

            @pl.when(e > 0)
            def _():
                zero_fill(e - 1, True)
            return carry

        lax.fori_loop(0, N_EXPERTS, per_expert, 0)
        zero_fill(N_EXPERTS - 1, True)

    for slot in range(2):
        @pl.when(i % 2 == slot)
        def _(slot=slot):
            @pl.when(i >= 2)
            def _():
                wait_slot(slot)
            stage[slot][...] = h3_ref[...]
            for k in range(TOP_K):
                def body(jj, carry, k=k):
                    for u in range(ISSUE_UNROLL):
                        j = jj * ISSUE_UNROLL + u
                        _tile_copy(stage[slot], j, xs_hbm, dst_ref[k, j], sem.at[slot]).start()
                    return carry
                lax.fori_loop(0, tm // ISSUE_UNROLL, body, 0)

    @pl.when(i == nsteps - 1)
    def _():
        wait_slot(0)
        wait_slot(1)


def _dispatch_rows(h3, dest, cend, pend, rows):
    _, n = dest.shape
    tm = SCATTER_TILE
    assert n // tm >= 2
    stage = pltpu.VMEM((tm * TILE_ROWS, LANES), U32)
    grid_spec = pltpu.PrefetchScalarGridSpec(
        num_scalar_prefetch=2,
        grid=(n // tm,),
        in_specs=[pl.BlockSpec((TOP_K, tm), lambda i, ce, pe: (0, i), memory_space=pltpu.SMEM),
                  pl.BlockSpec((tm * TILE_ROWS, LANES), lambda i, ce, pe: (i, 0)),
                  pl.BlockSpec((ZERO_CHUNKS[0] * TILE_ROWS, LANES), lambda i, ce, pe: (0, 0))],
        out_specs=pl.BlockSpec(memory_space=pl.ANY),
        scratch_shapes=[stage, stage, pltpu.SemaphoreType.DMA((2,)), pltpu.SemaphoreType.DMA],
    )
    return pl.pallas_call(
        _scatter_kernel,
        grid_spec=grid_spec,
        out_shape=jax.ShapeDtypeStruct((rows * TILE_ROWS, LANES), U32),
        compiler_params=_params(("arbitrary",)),
        name="dispatch_scatter",
    )(cend, pend, dest, h3, jnp.zeros((ZERO_CHUNKS[0] * TILE_ROWS, LANES), U32))


def _ffn_kernel(be_ref, nx_ref, nu_ref, x_ref, wg_hbm, wu_hbm, wd_hbm, y_ref,
                wgf, wuf, wdf, wgb, wub, wdb, sem, run):
    first = pl.program_id(0) * FFN_PER_STEP
    nused = nu_ref[0]

    def fetch(e, slot):
        return [pltpu.make_async_copy(src.at[e], dst.at[slot], sem.at[slot])
                for src, dst in ((wg_hbm, wgf), (wu_hbm, wuf), (wd_hbm, wdf))]

    @pl.when(first == 0)
    def _():
        run[0] = -1
        for c in fetch(be_ref[0], 0):
            c.start()

    slots = []
    for part in range(FFN_PER_STEP):
        blk = first + part

        @pl.when((blk < nused) & ((blk == 0) | (be_ref[blk] != be_ref[jnp.maximum(blk - 1, 0)])))
        def _(blk=blk):
            r = run[0] + 1
            run[0] = r
            fslot, slot = r % 2, r % FFN_PER_STEP
            for c in fetch(be_ref[blk], fslot):
                c.wait()
            wgb[slot] = wgf[fslot].astype(BF16)
            wub[slot] = wuf[fslot].astype(BF16)
            wdb[slot] = wdf[fslot].astype(BF16)

            @pl.when(nx_ref[blk] >= 0)
            def _():
                for c in fetch(nx_ref[blk], 1 - fslot):
                    c.start()

        slots.append(run[0] % FFN_PER_STEP)

    @pl.when(first < nused)
    def _():
        for part, slot in enumerate(slots):
            base = part * FFN_BLOCK * TILE_ROWS
            x = jnp.concatenate([c.astype(BF16) for c in _from_row_tiles(x_ref, FFN_BLOCK, base)], axis=1)
            act = _silu(jnp.dot(x, wgb[slot], preferred_element_type=F32)) * jnp.dot(
                x, wub[slot], preferred_element_type=F32)
            _to_row_tiles(y_ref, jnp.dot(act.astype(BF16), wdb[slot], preferred_element_type=F32), base)

    @pl.when(first >= nused)
    def _():
        y_ref[...] = jnp.zeros(y_ref.shape, U32)


def _routed_experts(xs, block_e, next_e, nused, w_gate, w_up, w_down):
    rows = xs.shape[0] // TILE_ROWS
    nblk = rows // FFN_BLOCK
    _, d, hid = w_gate.shape
    step_rows = FFN_PER_STEP * FFN_BLOCK * TILE_ROWS
    hbm = pl.BlockSpec(memory_space=pl.ANY)
    grid_spec = pltpu.PrefetchScalarGridSpec(
        num_scalar_prefetch=3,
        grid=(nblk // FFN_PER_STEP,),
        in_specs=[pl.BlockSpec((step_rows, LANES),
                               lambda i, be, nx, nu: (jnp.minimum(i, (nu[0] - 1) // FFN_PER_STEP), 0)),
                  hbm, hbm, hbm],
        out_specs=pl.BlockSpec((step_rows, LANES), lambda i, be, nx, nu: (i, 0)),
        scratch_shapes=[pltpu.VMEM((2, d, hid), F32), pltpu.VMEM((2, d, hid), F32), pltpu.VMEM((2, hid, d), F32),
                        pltpu.VMEM((FFN_PER_STEP, d, hid), BF16), pltpu.VMEM((FFN_PER_STEP, d, hid), BF16),
                        pltpu.VMEM((FFN_PER_STEP, hid, d), BF16),
                        pltpu.SemaphoreType.DMA((2,)), pltpu.SMEM((1,), jnp.int32)],
    )
    return pl.pallas_call(
        _ffn_kernel,
        grid_spec=grid_spec,
        out_shape=jax.ShapeDtypeStruct((rows * TILE_ROWS, LANES), U32),
        compiler_params=_params(("arbitrary",)),
        name="routed_experts",
    )(block_e, next_e, nused, xs, w_gate, w_up, w_down)


def _gather_start(idx_ref, src_hbm, dst_vmem, sem):
    nk, tm = idx_ref.shape
    for k in range(nk):
        def body(jj, carry, k=k):
            for u in range(ISSUE_UNROLL):
                j = jj * ISSUE_UNROLL + u
                _tile_copy(src_hbm, idx_ref[k, j], dst_vmem, k * tm + j, sem).start()
            return carry
        lax.fori_loop(0, tm // ISSUE_UNROLL, body, 0)


def _gather_wait(src_hbm, dst_vmem, sem):
    pltpu.make_async_copy(src_hbm.at[pl.ds(0, dst_vmem.shape[0]), :], dst_vmem, sem).wait()


def _combine_kernel(dst_ref, dstn_ref, wt_ref, xp_ref, g2_ref, gfin_ref, ys_hbm, o_ref, yb0, yb1, sem, *, nsteps):
    i = pl.program_id(0)
    tm = COMBINE_TILE
    bufs = (yb0, yb1)

    @pl.when(i == 0)
    def _():
        _gather_start(dst_ref, ys_hbm, yb0, sem.at[0])

    for slot in range(2):
        @pl.when((i + 1 < nsteps) & ((i + 1) % 2 == slot))
        def _(slot=slot):
            _gather_start(dstn_ref, ys_hbm, bufs[slot], sem.at[slot])

    for slot in range(2):
        @pl.when(i % 2 == slot)
        def _(slot=slot):
            _gather_wait(ys_hbm, bufs[slot], sem.at[slot])
            wt = wt_ref[...]
            cols = None
            for k in range(TOP_K):
                part = [c * wt[:, k:k + 1] for c in _from_row_tiles(bufs[slot], tm, base=k * tm * TILE_ROWS)]
                cols = part if cols is None else [a + b for a, b in zip(cols, part)]
            routed = jnp.concatenate(cols, axis=1)
            o_ref[0] = _rms(xp_ref[0] + g2_ref[0] * routed) * gfin_ref[...]


def _combine(ys, dest, wts_t, xp, gate2, g_final):
    bsz, s, d = xp.shape
    tm = COMBINE_TILE
    nt = s // tm
    nsteps = bsz * nt
    vec = pl.BlockSpec((1, 1, d), lambda i: (i // nt, 0, 0))
    return pl.pallas_call(
        functools.partial(_combine_kernel, nsteps=nsteps),
        grid=(nsteps,),
        in_specs=[pl.BlockSpec((TOP_K, tm), lambda i: (0, i), memory_space=pltpu.SMEM),
                  pl.BlockSpec((TOP_K, tm), lambda i: (0, jnp.minimum(i + 1, nsteps - 1)),
                               memory_space=pltpu.SMEM),
                  pl.BlockSpec((tm, TOP_K), lambda i: (i, 0)),
                  pl.BlockSpec((1, tm, d), lambda i: (i // nt, i % nt, 0)),
                  vec,
                  pl.BlockSpec((1, d), lambda i: (0, 0)),
                  pl.BlockSpec(memory_space=pl.ANY)],
        out_specs=pl.BlockSpec((1, tm, d), lambda i: (i // nt, i % nt, 0)),
        out_shape=jax.ShapeDtypeStruct((bsz, s, d), F32),
        scratch_shapes=[pltpu.VMEM((tm * TOP_K * TILE_ROWS, LANES), U32),
                        pltpu.VMEM((tm * TOP_K * TILE_ROWS, LANES), U32),
                        pltpu.SemaphoreType.DMA((2,))],
        compiler_params=_params(("arbitrary",)),
        name="combine_final_norm",
    )(dest, dest, wts_t, xp, gate2, g_final.reshape(1, d), ys)


def kernel(x, c, w_in, b_forget, g_fox_out, g_dil_out, w_out, w_ada, b_ada, w_router, router_bias,
           w_exp_gate, w_exp_up, w_exp_down, w_sh_gate, w_sh_up, w_sh_down, rel_bias, g_final):
    bsz, s, d = x.shape
    depth = w_in.shape[0]
    assert depth == 1 and d == 2 * TILE_ROWS * LANES and s % (DIL_BLOCK * DILATIONS[-1]) == 0
    assert all(s % t == 0 for t in (ROW_TILE, FORGET_BLOCK, FOX_TILE, DIL_CHUNK, COMBINE_TILE, SCATTER_TILE))
    assert (bsz * s) % ROUTE_TILE == 0 and (bsz * s * TOP_K // FFN_BLOCK + N_EXPERTS) % FFN_PER_STEP == 0
    l = 0
    mod = _modulation(c, w_ada[l], b_ada[l])
    shift1, scale1, gate1, shift2, scale2, gate2 = [m[:, None, :] for m in jnp.split(mod, 6, axis=-1)]

    qscale = HEAD_DIM ** -0.5
    o3 = 3 * WIDTH
    w = w_in[l]
    wq_t = (w[:, :WIDTH] * (qscale * LOG2E)).T.astype(BF16)
    wk = w[:, WIDTH:2 * WIDTH].astype(BF16)
    wv_t = w[:, 2 * WIDTH:o3].T.astype(BF16)
    w_flog = jnp.pad(w[:, o3:o3 + N_HEADS], ((0, 0), (0, LANES - N_HEADS))).astype(BF16)
    wd0 = o3 + N_HEADS
    w_dil = jnp.concatenate([w[:, wd0:wd0 + WIDTH] * qscale, w[:, wd0 + WIDTH:]], axis=1).astype(BF16)

    q_t, k_f, v_t, qkv_d, flog = _in_projection(x, scale1, shift1, wq_t, wk, wv_t, w_dil, w_flog)
    y_fox = _fox_attention(q_t, k_f, _forget_cumsum(flog, b_forget[l]), v_t)
    y_dil = _dilated_attention(qkv_d, _dilated_bias(rel_bias))

    wr = w_router[l].T
    wr_hi = wr.astype(BF16)
    wr_lo = (wr - wr_hi.astype(F32)).astype(BF16)
    wo = w_out[l].astype(BF16)
    xp, h3, logits_t = _out_projection(
        y_fox, y_dil, x, gate1, scale2, shift2, gate2, g_fox_out[l].reshape(1, WIDTH),
        g_dil_out[l].reshape(1, WIDTH), wo[:WIDTH], wo[WIDTH:], wr_hi, wr_lo,
        w_sh_gate[l].astype(BF16), w_sh_up[l].astype(BF16), w_sh_down[l].astype(BF16))

    eidx, wts, rank, counts = _route(logits_t, router_bias[l])
    rows = bsz * s * TOP_K + N_EXPERTS * FFN_BLOCK
    pstart, cend, pend, block_e, next_e, nused = _block_plan(counts, rows // FFN_BLOCK)
    dest = _dest_rows(eidx, rank, pstart)
    xs = _dispatch_rows(h3, dest, cend, pend, rows)
    ys = _routed_experts(xs, block_e, next_e, nused, w_exp_gate[l], w_exp_up[l], w_exp_down[l])
    return _combine(ys, dest, wts.T, xp, gate2, g_final)
```

```python
import functools

import numpy as np
import jax
import jax.numpy as jnp
from jax import lax
from jax.experimental import pallas as pl
from jax.experimental.pallas import tpu as pltpu

F32 = jnp.float32
BF16 = jnp.bfloat16
U32 = jnp.uint32

HEAD_DIM = 64
N_HEADS = 8
WIDTH = N_HEADS * HEAD_DIM
DIL_BLOCK = 128
DILATIONS = (1, 4, 16)
T5_NUM_BUCKETS = 32
T5_MAX_DISTANCE = 2048
N_EXPERTS = 256
TOP_K = 8
N_GROUPS = 8
TOP_K_GROUPS = 4
ROUTED_SCALE = 2.5
EPS = 1e-6

LANES = 128
VMEM_BYTES = 64 * 1024 * 1024
VMEM_LIMIT = VMEM_BYTES * 7 // 8

HEAD_PAIRS = WIDTH // LANES
TILE_ROWS = 4
NEG = -1e30
LOG2E = float(np.log2(np.e))

MOD_TILE = 1536
ROW_TILE = 512
FORGET_BLOCK = 512
FOX_TILE = 512
DIL_UNROLL = 16
DIL_CHUNK = 512
FFN_BLOCK = 512
FFN_PER_STEP = 2
ZERO_CHUNKS = tuple(1 << b for b in reversed(range(FFN_BLOCK.bit_length() - 1)))
COMBINE_TILE = 256
SCATTER_TILE = 256
ROUTE_TILE = 512
ISSUE_UNROLL = 8


def _params(semantics):
    return pltpu.CompilerParams(dimension_semantics=semantics, vmem_limit_bytes=VMEM_LIMIT)


def _rms(x):
    return x * lax.rsqrt(jnp.mean(x * x, axis=-1, keepdims=True) + EPS)


def _silu(x):
    return x * jax.nn.sigmoid(x)


def _dot_nt(a, b):
    return lax.dot_general(a, b, (((1,), (1,)), ((), ())), preferred_element_type=F32)


def _mod_kernel(c_ref, w_ref, b_ref, o_ref):
    o_ref[...] = jnp.dot(_silu(c_ref[...]), w_ref[...], precision=lax.Precision.HIGHEST,
                         preferred_element_type=F32) + b_ref[...]


def _modulation(c, w_ada, b_ada):
    bsz, d = c.shape
    n = w_ada.shape[1]
    tn = MOD_TILE
    return pl.pallas_call(
        _mod_kernel,
        grid=(n // tn,),
        in_specs=[pl.BlockSpec((bsz, d), lambda j: (0, 0)),
                  pl.BlockSpec((d, tn), lambda j: (0, j)),
                  pl.BlockSpec((1, tn), lambda j: (0, j))],
        out_specs=pl.BlockSpec((bsz, tn), lambda j: (0, j)),
        out_shape=jax.ShapeDtypeStruct((bsz, n), F32),
        compiler_params=_params(("arbitrary",)),
        name="adaln_mod",
    )(c, w_ada, b_ada.reshape(1, n))


def _inproj_kernel(x_ref, sc_ref, sh_ref, wq_ref, wk_ref, wv_ref, wd_ref, wl_ref,
                   oq_ref, ok_ref, ov_ref, od_ref, ol_ref):
    h = _rms(x_ref[0]) * (1.0 + sc_ref[0]) + sh_ref[0]
    hb = h.astype(BF16)
    oq_ref[0] = _dot_nt(wq_ref[...], hb).astype(BF16)
    ov_ref[0] = _dot_nt(wv_ref[...], hb).astype(BF16)
    ok_ref[0] = jnp.dot(hb, wk_ref[...], preferred_element_type=F32).astype(BF16)
    od_ref[0] = jnp.dot(hb, wd_ref[...], preferred_element_type=F32).astype(BF16)
    ol_ref[0] = jnp.dot(hb, wl_ref[...], preferred_element_type=F32)


def _in_projection(x, scale1, shift1, wq_t, wk, wv_t, w_dil, w_flog):
    bsz, s, d = x.shape
    tm = ROW_TILE
    vec = pl.BlockSpec((1, 1, d), lambda b, i: (b, 0, 0))
    full = lambda w: pl.BlockSpec(w.shape, lambda b, i: (0, 0))
    row = lambda n: pl.BlockSpec((1, tm, n), lambda b, i: (b, i, 0))
    col = pl.BlockSpec((1, WIDTH, tm), lambda b, i: (b, 0, i))
    return pl.pallas_call(
        _inproj_kernel,
        grid=(bsz, s // tm),
        in_specs=[row(d), vec, vec, full(wq_t), full(wk), full(wv_t), full(w_dil), full(w_flog)],
        out_specs=[col, row(WIDTH), col, row(3 * WIDTH), row(LANES)],
        out_shape=[jax.ShapeDtypeStruct((bsz, WIDTH, s), BF16),
                   jax.ShapeDtypeStruct((bsz, s, WIDTH), BF16),
                   jax.ShapeDtypeStruct((bsz, WIDTH, s), BF16),
                   jax.ShapeDtypeStruct((bsz, s, 3 * WIDTH), BF16),
                   jax.ShapeDtypeStruct((bsz, s, LANES), F32)],
        compiler_params=_params(("parallel", "arbitrary")),
        name="norm1_inproj",
    )(x, scale1, shift1, wq_t, wk, wv_t, w_dil, w_flog)


FORGET_PARTS = 3


def _split_bf16(x):
    parts = []
    for _ in range(FORGET_PARTS):
        p = x.astype(BF16)
        parts.append(p)
        x = x - p.astype(F32)
    return parts


def _forget_kernel(f_ref, b_ref, tri_ref, place_ref, o_ref):
    blk = tri_ref.shape[0]
    carry = jnp.zeros((1, LANES), F32)
    for c in range(f_ref.shape[1] // blk):
        rows = pl.ds(c * blk, blk)
        z = f_ref[0, rows, :] + b_ref[...]
        ls = jnp.minimum(z, 0.0) - jnp.log1p(jnp.exp(-jnp.abs(z)))
        cum = carry
        for p in _split_bf16(ls):
            cum = cum + jnp.dot(tri_ref[...], p, preferred_element_type=F32)
        carry = cum[blk - 1:blk, :]
        out = None
        for n, p in enumerate(_split_bf16(cum * LOG2E)):
            d = jnp.dot(p, place_ref[n], preferred_element_type=F32)
            out = d if out is None else out + d
        o_ref[0, rows, :] = out.astype(BF16)


def _forget_cumsum(flog, b_forget):
    bsz, s, _ = flog.shape
    blk = FORGET_BLOCK
    place = np.zeros((FORGET_PARTS, LANES, WIDTH), np.float32)
    for h in range(N_HEADS):
        for n in range(FORGET_PARTS):
            place[n, h, (h // 2) * LANES + (h % 2) * HEAD_DIM + n] = 1.0
    return pl.pallas_call(
        _forget_kernel,
        grid=(bsz,),
        in_specs=[pl.BlockSpec((1, s, LANES), lambda b: (b, 0, 0)),
                  pl.BlockSpec((1, LANES), lambda b: (0, 0)),
                  pl.BlockSpec((blk, blk), lambda b: (0, 0)),
                  pl.BlockSpec(place.shape, lambda b: (0, 0, 0))],
        out_specs=pl.BlockSpec((1, s, WIDTH), lambda b: (b, 0, 0)),
        out_shape=jax.ShapeDtypeStruct((bsz, s, WIDTH), BF16),
        compiler_params=_params(("parallel",)),
        name="forget_cumsum",
    )(flog, jnp.pad(b_forget, (0, LANES - N_HEADS)).reshape(1, LANES),
      jnp.tril(jnp.ones((blk, blk), BF16)), jnp.asarray(place, BF16))


def _stack_heads(q):
    lane = lax.broadcasted_iota(jnp.int32, q.shape, 1)
    zero = jnp.zeros_like(q)
    return jnp.concatenate([jnp.where(lane < HEAD_DIM, q, zero), jnp.where(lane >= HEAD_DIM, q, zero)], axis=0)


def _unstack_heads(o2):
    rows = o2.shape[0] // 2
    lane = lax.broadcasted_iota(jnp.int32, (rows, LANES), 1)
    return jnp.where(lane < HEAD_DIM, o2[:rows], o2[rows:])


def _fox_kernel(q_ref, k_ref, fa_ref, v_ref, o_ref, m_s, l_s, acc_s, s_a, s_b, *, t):
    i = pl.program_id(2)
    q_t = q_ref[0]
    row = lax.broadcasted_iota(jnp.int32, (LANES, 2 * t), 0)
    col = lax.broadcasted_iota(jnp.int32, (LANES, 2 * t), 1)
    head_row = jnp.where(col < t, 0, HEAD_DIM)
    q2 = jnp.concatenate([q_t, q_t], axis=1)
    q2 = jnp.where((row >= head_row) & (row < head_row + HEAD_DIM), q2, jnp.zeros_like(q2))
    minus = jnp.where((row >= head_row) & (row < head_row + FORGET_PARTS), -1.0, 0.0).astype(BF16)
    qa = jnp.concatenate([q2, minus], axis=0)
    m_s[...] = jnp.full(m_s.shape, NEG, F32)
    l_s[...] = jnp.zeros(l_s.shape, F32)
    acc_s[...] = jnp.zeros(acc_s.shape, F32)

    def scores(ks, tk):
        kk = jnp.concatenate([k_ref[0, pl.ds(ks, tk), :], fa_ref[0, pl.ds(ks, tk), :]], axis=1)
        return jnp.dot(kk, qa, preferred_element_type=F32)

    def absorb(s, ks, tk):
        m_prev = m_s[...]
        m_new = jnp.maximum(m_prev, jnp.max(s, axis=0, keepdims=True))
        alpha = jnp.exp2(m_prev - m_new)
        p = jnp.exp2(s - m_new)
        l_s[...] = alpha * l_s[...] + jnp.sum(p, axis=0, keepdims=True)
        acc_s[...] = alpha * acc_s[...] + jnp.dot(v_ref[0, :, pl.ds(ks, tk)], p.astype(BF16),
                                                  preferred_element_type=F32)
        m_s[...] = m_new

    npair = i // 2

    def tile(j):
        return pl.multiple_of(j * 2 * t, 2 * t)

    @pl.when(npair > 0)
    def _():
        s_a[...] = scores(0, 2 * t)

        def body(jj, carry):
            j = 2 * jj
            s_b[...] = scores(tile(j + 1), 2 * t)
            absorb(s_a[...], tile(j), 2 * t)
            s_a[...] = scores(tile(jnp.minimum(j + 2, npair - 1)), 2 * t)
            absorb(s_b[...], tile(j + 1), 2 * t)
            return carry

        lax.fori_loop(0, npair // 2, body, 0)

        @pl.when(npair % 2 == 1)
        def _():
            absorb(s_a[...], tile(npair - 1), 2 * t)

    def last(ks, tk):
        r = lax.broadcasted_iota(jnp.int32, (tk, 2 * t), 0)
        c = lax.broadcasted_iota(jnp.int32, (tk, 2 * t), 1)
        absorb(jnp.where(r <= jnp.where(c >= t, c - t, c) + (tk - t), scores(ks, tk), NEG), ks, tk)

    @pl.when(i % 2 == 1)
    def _():
        last(tile(npair), 2 * t)

    @pl.when(i % 2 == 0)
    def _():
        last(pl.multiple_of(i * t, t), t)

    o2 = acc_s[...] / l_s[...]
    o_t = jnp.where(lax.broadcasted_iota(jnp.int32, (LANES, t), 0) < HEAD_DIM, o2[:, :t], o2[:, t:])
    o_ref[0] = o_t.T


def _fox_attention(q_t, k, f_aug, v_t):
    bsz, s, _ = k.shape
    t = FOX_TILE
    keys = pl.BlockSpec((1, s, LANES), lambda b, h, i: (b, 0, h))
    return pl.pallas_call(
        functools.partial(_fox_kernel, t=t),
        grid=(bsz, HEAD_PAIRS, s // t),
        in_specs=[pl.BlockSpec((1, LANES, t), lambda b, h, i: (b, h, i)),
                  keys, keys,
                  pl.BlockSpec((1, LANES, s), lambda b, h, i: (b, h, 0))],
        out_specs=pl.BlockSpec((1, t, LANES), lambda b, h, i: (b, i, h)),
        out_shape=jax.ShapeDtypeStruct((bsz, s, WIDTH), F32),
        scratch_shapes=[pltpu.VMEM((1, 2 * t), F32), pltpu.VMEM((1, 2 * t), F32),
                        pltpu.VMEM((LANES, 2 * t), F32), pltpu.VMEM((2 * t, 2 * t), F32),
                        pltpu.VMEM((2 * t, 2 * t), F32)],
        compiler_params=_params(("parallel", "parallel", "arbitrary")),
        name="fox_attention",
    )(q_t, k, f_aug, v_t)


def _t5_bucket(dist):
    max_exact = T5_NUM_BUCKETS // 2
    d = np.maximum(dist, 1).astype(np.float32)
    large = max_exact + (np.log(d / max_exact) / np.log(T5_MAX_DISTANCE / max_exact)
                         * (T5_NUM_BUCKETS - max_exact)).astype(np.int32)
    large = np.minimum(large, T5_NUM_BUCKETS - 1)
    return np.where(dist < max_exact, dist, large).astype(np.int32)


def _dilated_bias(rel_bias):
    blk = DIL_BLOCK
    period = 3 * blk
    m = np.arange(period)
    rel = np.where(m < 2 * blk, blk - m, blk - (m - period))
    band = (rel >= 0) & (rel <= blk)
    onehot = np.zeros((len(DILATIONS), period, T5_NUM_BUCKETS), np.float32)
    for bi, dil in enumerate(DILATIONS):
        onehot[bi, m, _t5_bucket(np.clip(rel, 0, blk) * dil)] = 1.0
    w = jnp.einsum('bmk,kh->bhm', jnp.asarray(onehot), rel_bias.astype(F32),
                   precision=lax.Precision.HIGHEST)
    w = jnp.where(jnp.asarray(band), w, NEG)
    flat = jnp.tile(w, (1, 1, blk))[:, :, :blk * (period - 1)]
    table = flat.reshape(len(DILATIONS), N_HEADS, blk, period - 1)[..., :2 * blk]
    table = table.reshape(len(DILATIONS), HEAD_PAIRS, 2 * blk, 2 * blk)
    first = jnp.where(jnp.arange(2 * blk) < blk, NEG, table)
    return jnp.stack([table, first], axis=1)


def _dil_kernel(q_ref, k_ref, v_ref, bias_ref, o_ref, qf, kf, vf, ob0, ob1, ob2, ls0, ls1, ls2, *, s_len):
    blk = DIL_BLOCK
    qf[...] = q_ref[0].astype(F32)
    kf[...] = k_ref[0].astype(F32)
    vf[...] = v_ref[0].astype(F32)
    for bi, (dil, ob, ls) in enumerate(zip(DILATIONS, (ob0, ob1, ob2), (ls0, ls1, ls2))):
        span = blk * dil
        nb = s_len // span

        def rows(start, dil=dil):
            return pl.ds(start, blk) if dil == 1 else pl.ds(start, blk, stride=dil)

        def block(tix, carry, bi=bi, span=span, nb=nb, ob=ob, ls=ls, rows=rows):
            n = tix % nb
            start = n * span + tix // nb
            prev = jnp.maximum(start - span, 0)
            q2 = _stack_heads(qf[rows(start), :].astype(BF16))
            kk = jnp.concatenate([kf[rows(prev), :], kf[rows(start), :]], axis=0).astype(BF16)
            vv = jnp.concatenate([vf[rows(prev), :], vf[rows(start), :]], axis=0).astype(BF16)
            s = lax.dot_general(q2, kk, (((1,), (1,)), ((), ())), preferred_element_type=F32)
            s = s + bias_ref[bi, jnp.where(n == 0, 1, 0), 0]
            m = jnp.max(s, axis=1, keepdims=True)
            p = jnp.exp(s - m)
            l = jnp.sum(p, axis=1, keepdims=True)
            o2 = jnp.dot(p.astype(BF16), vv, preferred_element_type=F32) / l
            lse = jnp.broadcast_to(m + jnp.log(l), (2 * blk, LANES))
            ob[rows(start), :] = _unstack_heads(o2)
            ls[rows(start), :] = _unstack_heads(lse)
            return carry

        def blocks(g, carry, block=block):
            for u in range(DIL_UNROLL):
                block(g * DIL_UNROLL + u, carry)
            return carry

        lax.fori_loop(0, dil * nb // DIL_UNROLL, blocks, 0)

    chunk = DIL_CHUNK
    for c in range(s_len // chunk):
        r = pl.ds(c * chunk, chunk)
        l0, l1, l2 = ls0[r, :], ls1[r, :], ls2[r, :]
        mx = jnp.maximum(jnp.maximum(l0, l1), l2)
        e0, e1, e2 = jnp.exp(l0 - mx), jnp.exp(l1 - mx), jnp.exp(l2 - mx)
        o_ref[0, r, :] = (e0 * ob0[r, :] + e1 * ob1[r, :] + e2 * ob2[r, :]) / (e0 + e1 + e2)


def _dilated_attention(qkv, bias):
    bsz, s, _ = qkv.shape
    col = lambda off: pl.BlockSpec((1, s, LANES), lambda b, h: (b, 0, off + h))
    buf = pltpu.VMEM((s, LANES), F32)
    return pl.pallas_call(
        functools.partial(_dil_kernel, s_len=s),
        grid=(bsz, HEAD_PAIRS),
        in_specs=[col(0), col(HEAD_PAIRS), col(2 * HEAD_PAIRS),
                  pl.BlockSpec((len(DILATIONS), 2, 1, 2 * DIL_BLOCK, 2 * DIL_BLOCK), lambda b, h: (0, 0, h, 0, 0))],
        out_specs=pl.BlockSpec((1, s, LANES), lambda b, h: (b, 0, h)),
        out_shape=jax.ShapeDtypeStruct((bsz, s, WIDTH), F32),
        scratch_shapes=[buf] * 9,
        compiler_params=_params(("parallel", "arbitrary")),
        name="dilated_attention",
    )(qkv, qkv, qkv, bias)


def _bf16_bits(x):
    return pltpu.bitcast(x.astype(BF16).astype(F32), U32)


def _to_row_tiles(dst_ref, x, base=0):
    rows, d = x.shape
    for c in range(TILE_ROWS):
        lo = _bf16_bits(x[:, c * LANES:(c + 1) * LANES]) >> 16
        hi = _bf16_bits(x[:, d // 2 + c * LANES:d // 2 + (c + 1) * LANES])
        dst_ref[pl.ds(base + c, rows, stride=TILE_ROWS), :] = lo | hi


def _from_row_tiles(src_ref, rows, base=0):
    lo, hi = [], []
    for c in range(TILE_ROWS):
        w = src_ref[pl.ds(base + c, rows, stride=TILE_ROWS), :]
        lo.append(pltpu.bitcast(w << 16, F32))
        hi.append(pltpu.bitcast(w & jnp.uint32(0xFFFF0000), F32))
    return lo + hi


def _outproj_kernel(yf_ref, yd_ref, x_ref, g1_ref, sc_ref, sh_ref, g2_ref, gf_ref, gd_ref, wo1_ref, wo2_ref,
                    wrh_ref, wrl_ref, wsg_ref, wsu_ref, wsd_ref, xp_ref, h3_ref, lg_ref):
    nf = (_rms(yf_ref[0]) * gf_ref[...]).astype(BF16)
    nd = (_rms(yd_ref[0]) * gd_ref[...]).astype(BF16)
    mix = (jnp.dot(nf, wo1_ref[...], preferred_element_type=F32)
           + jnp.dot(nd, wo2_ref[...], preferred_element_type=F32))
    x1 = x_ref[0] + g1_ref[0] * mix
    h2 = _rms(x1) * (1.0 + sc_ref[0]) + sh_ref[0]
    hb = h2.astype(BF16)
    hl = (h2 - hb.astype(F32)).astype(BF16)
    lg_ref[...] = _dot_nt(wrh_ref[...], hb) + _dot_nt(wrh_ref[...], hl) + _dot_nt(wrl_ref[...], hb)
    act = _silu(jnp.dot(hb, wsg_ref[...], preferred_element_type=F32)) * jnp.dot(
        hb, wsu_ref[...], preferred_element_type=F32)
    shared = jnp.dot(act.astype(BF16), wsd_ref[...], preferred_element_type=F32)
    xp_ref[0] = x1 + g2_ref[0] * shared
    _to_row_tiles(h3_ref, h2)


def _out_projection(y_fox, y_dil, x, gate1, scale2, shift2, gate2, g_fox, g_dil, wo1, wo2, wr_hi, wr_lo,
                    wsg, wsu, wsd):
    bsz, s, d = x.shape
    tm = ROW_TILE
    nt = s // tm
    vec = pl.BlockSpec((1, 1, d), lambda b, i: (b, 0, 0))
    full = lambda w: pl.BlockSpec(w.shape, lambda b, i: (0,) * w.ndim)
    row = lambda n: pl.BlockSpec((1, tm, n), lambda b, i: (b, i, 0))
    return pl.pallas_call(
        _outproj_kernel,
        grid=(bsz, nt),
        in_specs=[row(WIDTH), row(WIDTH), row(d), vec, vec, vec, vec, full(g_fox), full(g_dil), full(wo1),
                  full(wo2), full(wr_hi), full(wr_lo), full(wsg), full(wsu), full(wsd)],
        out_specs=[row(d),
                   pl.BlockSpec((tm * TILE_ROWS, LANES), lambda b, i: (b * nt + i, 0)),
                   pl.BlockSpec((N_EXPERTS, tm), lambda b, i: (0, b * nt + i))],
        out_shape=[jax.ShapeDtypeStruct((bsz, s, d), F32),
                   jax.ShapeDtypeStruct((bsz * s * TILE_ROWS, LANES), U32),
                   jax.ShapeDtypeStruct((N_EXPERTS, bsz * s), F32)],
        compiler_params=_params(("parallel", "arbitrary")),
        name="outproj_norm2_router_shared",
    )(y_fox, y_dil, x, gate1, scale2, shift2, gate2, g_fox, g_dil, wo1, wo2, wr_hi, wr_lo, wsg, wsu, wsd)


def _first_argmax(v, row, size):
    m = jnp.max(v, axis=0, keepdims=True)
    return m, jnp.min(jnp.where(v == m, row, size), axis=0, keepdims=True)


def _route_kernel(lg_ref, rb_ref, tri_ref, e_ref, w_ref, rk_ref, cnt_ref, cnt_s):
    @pl.when(pl.program_id(0) == 0)
    def _():
        cnt_s[...] = jnp.zeros(cnt_s.shape, F32)

    t = lg_ref.shape[1]
    gsz = N_EXPERTS // N_GROUPS
    scores = jax.nn.sigmoid(lg_ref[...])
    sel = scores + rb_ref[...]
    row_g = lax.broadcasted_iota(jnp.int32, (gsz, t), 0)
    grp = []
    for g in range(N_GROUPS):
        v = sel[g * gsz:(g + 1) * gsz]
        m1, i1 = _first_argmax(v, row_g, gsz)
        m2 = jnp.max(jnp.where(row_g == i1, -jnp.inf, v), axis=0, keepdims=True)
        grp.append(m1 + m2)
    gv = jnp.concatenate(grp, axis=0)
    row8 = lax.broadcasted_iota(jnp.int32, (N_GROUPS, t), 0)
    pen = jnp.full((N_GROUPS, t), -jnp.inf, F32)
    for _ in range(TOP_K_GROUPS):
        _, ix = _first_argmax(gv, row8, N_GROUPS)
        pen = jnp.where(row8 == ix, 0.0, pen)
        gv = jnp.where(row8 == ix, -jnp.inf, gv)
    selm = jnp.concatenate([sel[g * gsz:(g + 1) * gsz] + pen[g:g + 1] for g in range(N_GROUPS)], axis=0)

    row = lax.broadcasted_iota(jnp.int32, (N_EXPERTS, t), 0)
    v = selm
    idxs, scs = [], []
    for _ in range(TOP_K):
        _, ix = _first_argmax(v, row, N_EXPERTS)
        hit = row == ix
        idxs.append(ix)
        scs.append(jnp.sum(jnp.where(hit, scores, 0.0), axis=0, keepdims=True))
        v = jnp.where(hit, -jnp.inf, v)
    chosen = jnp.where(v != selm, 1.0, 0.0)
    before = jnp.dot(chosen.astype(BF16), tri_ref[...], preferred_element_type=F32) + cnt_s[...]
    rks = [jnp.sum(jnp.where(row == ix, before, 0.0), axis=0, keepdims=True) for ix in idxs]
    sc = jnp.concatenate(scs, axis=0)
    e_ref[...] = jnp.concatenate(idxs, axis=0)
    w_ref[...] = sc / jnp.sum(sc, axis=0, keepdims=True) * ROUTED_SCALE
    rk_ref[...] = jnp.concatenate(rks, axis=0).astype(jnp.int32)
    cnt_s[...] = cnt_s[...] + jnp.sum(chosen, axis=1, keepdims=True)
    cnt_ref[...] = cnt_s[...]


def _route(logits_t, router_bias):
    e, n = logits_t.shape
    t = ROUTE_TILE
    tri = jnp.triu(jnp.ones((t, t), BF16), k=1)
    tile = pl.BlockSpec((TOP_K, t), lambda i: (0, i))
    return pl.pallas_call(
        _route_kernel,
        grid=(n // t,),
        in_specs=[pl.BlockSpec((e, t), lambda i: (0, i)),
                  pl.BlockSpec((e, 1), lambda i: (0, 0)),
                  pl.BlockSpec((t, t), lambda i: (0, 0))],
        out_specs=[tile, tile, tile, pl.BlockSpec((e, 1), lambda i: (0, 0))],
        out_shape=[jax.ShapeDtypeStruct((TOP_K, n), jnp.int32), jax.ShapeDtypeStruct((TOP_K, n), F32),
                   jax.ShapeDtypeStruct((TOP_K, n), jnp.int32), jax.ShapeDtypeStruct((e, 1), F32)],
        scratch_shapes=[pltpu.VMEM((e, 1), F32)],
        compiler_params=_params(("arbitrary",)),
        name="route_topk_rank",
    )(logits_t, router_bias.reshape(e, 1).astype(F32), tri)


def _dest_kernel(e_ref, rk_ref, ps_ref, d_ref):
    t = e_ref.shape[1]
    row = lax.broadcasted_iota(jnp.int32, (N_EXPERTS, t), 0)
    ps = ps_ref[...]
    base = [jnp.sum(jnp.where(row == e_ref[k:k + 1, :], ps, 0.0), axis=0, keepdims=True) for k in range(TOP_K)]
    d_ref[...] = jnp.concatenate(base, axis=0).astype(jnp.int32) + rk_ref[...]


def _dest_rows(eidx, rank, pstart):
    _, n = eidx.shape
    t = ROUTE_TILE
    tile = pl.BlockSpec((TOP_K, t), lambda i: (0, i))
    return pl.pallas_call(
        _dest_kernel,
        grid=(n // t,),
        in_specs=[tile, tile, pl.BlockSpec((N_EXPERTS, 1), lambda i: (0, 0))],
        out_specs=tile,
        out_shape=jax.ShapeDtypeStruct((TOP_K, n), jnp.int32),
        compiler_params=_params(("parallel",)),
        name="dest_rows",
    )(eidx, rank, pstart.astype(F32).reshape(N_EXPERTS, 1))


def _block_plan(counts, nblk):
    counts = counts.reshape(N_EXPERTS).astype(jnp.int32)
    padded = (counts + FFN_BLOCK - 1) // FFN_BLOCK * FFN_BLOCK
    pend = jnp.cumsum(padded).astype(jnp.int32)
    pstart = pend - padded
    first_row = jnp.arange(nblk, dtype=jnp.int32) * FFN_BLOCK
    block_e = jnp.minimum(jnp.sum(pend[None, :] <= first_row[:, None], axis=1), N_EXPERTS - 1).astype(jnp.int32)
    nused = (pend[-1:] // FFN_BLOCK).astype(jnp.int32)
    ids = jnp.arange(N_EXPERTS, dtype=jnp.int32)
    following = lax.cummin(jnp.where(counts > 0, ids, N_EXPERTS), reverse=True)
    following = jnp.concatenate([following[1:], jnp.full((1,), N_EXPERTS, jnp.int32)])
    following = jnp.where(following >= N_EXPERTS, -1, following)
    next_e = jnp.sum(jnp.where(block_e[:, None] == ids[None, :], following[None, :], 0), axis=1).astype(jnp.int32)
    return pstart, pstart + counts, pend, block_e, next_e, nused


def _tile_copy(src_ref, src_row, dst_ref, dst_row, sem):
    return pltpu.make_async_copy(src_ref.at[pl.ds(pl.multiple_of(src_row * TILE_ROWS, TILE_ROWS), TILE_ROWS), :],
                                 dst_ref.at[pl.ds(pl.multiple_of(dst_row * TILE_ROWS, TILE_ROWS), TILE_ROWS), :],
                                 sem)


def _scatter_kernel(cend_ref, pend_ref, dst_ref, h3_ref, z_ref, xs_hbm, st0, st1, sem, zsem):
    i = pl.program_id(0)
    nsteps = pl.num_programs(0)
    tm = dst_ref.shape[1]
    stage = (st0, st1)

    def zero_fill(e, wait):
        pad = pend_ref[e] - cend_ref[e]
        for p in ZERO_CHUNKS:
            @pl.when((pad & p) != 0)
            def _(p=p):
                row = cend_ref[e] + (pad & (-2 * p))
                dst = xs_hbm.at[pl.ds(pl.multiple_of(row * TILE_ROWS, TILE_ROWS), p * TILE_ROWS), :]
                copy = pltpu.make_async_copy(z_ref.at[pl.ds(0, p * TILE_ROWS), :], dst, zsem)
                copy.wait() if wait else copy.start()

    def wait_slot(slot):
        for _ in range(TOP_K):
            pltpu.make_async_copy(stage[slot], xs_hbm.at[pl.ds(0, tm * TILE_ROWS), :], sem.at[slot]).wait()

    @pl.when(i == 0)
    def _():
        def per_expert(e, carry):
            zero_fill(e, False)

            @pl.when(e > 0)
            def _():
                zero_fill(e - 1, True)
            return carry

        lax.fori_loop(0, N_EXPERTS, per_expert, 0)
        zero_fill(N_EXPERTS - 1, True)

    for slot in range(2):
        @pl.when(i % 2 == slot)
        def _(slot=slot):
            @pl.when(i >= 2)
            def _():
                wait_slot(slot)
            stage[slot][...] = h3_ref[...]
            for k in range(TOP_K):
                def body(jj, carry, k=k):
                    for u in range(ISSUE_UNROLL):
                        j = jj * ISSUE_UNROLL + u
                        _tile_copy(stage[slot], j, xs_hbm, dst_ref[k, j], sem.at[slot]).start(priority=u % 2)
                    return carry
                lax.fori_loop(0, tm // ISSUE_UNROLL, body, 0)

    @pl.when(i == nsteps - 1)
    def _():
        wait_slot(0)
        wait_slot(1)


def _dispatch_rows(h3, dest, cend, pend, rows):
    _, n = dest.shape
    tm = SCATTER_TILE
    assert n // tm >= 2
    stage = pltpu.VMEM((tm * TILE_ROWS, LANES), U32)
    grid_spec = pltpu.PrefetchScalarGridSpec(
        num_scalar_prefetch=2,
        grid=(n // tm,),
        in_specs=[pl.BlockSpec((TOP_K, tm), lambda i, ce, pe: (0, i), memory_space=pltpu.SMEM),
                  pl.BlockSpec((tm * TILE_ROWS, LANES), lambda i, ce, pe: (i, 0)),
                  pl.BlockSpec((ZERO_CHUNKS[0] * TILE_ROWS, LANES), lambda i, ce, pe: (0, 0))],
        out_specs=pl.BlockSpec(memory_space=pl.ANY),
        scratch_shapes=[stage, stage, pltpu.SemaphoreType.DMA((2,)), pltpu.SemaphoreType.DMA],
    )
    return pl.pallas_call(
        _scatter_kernel,
        grid_spec=grid_spec,
        out_shape=jax.ShapeDtypeStruct((rows * TILE_ROWS, LANES), U32),
        compiler_params=_params(("arbitrary",)),
        name="dispatch_scatter",
    )(cend, pend, dest, h3, jnp.zeros((ZERO_CHUNKS[0] * TILE_ROWS, LANES), U32))


def _ffn_kernel(be_ref, nx_ref, nu_ref, x_ref, wg_hbm, wu_hbm, wd_hbm, y_ref,
                wgf, wuf, wdf, wgb, wub, wdb, sem, run):
    first = pl.program_id(0) * FFN_PER_STEP
    nused = nu_ref[0]

    def fetch(e, slot):
        return [pltpu.make_async_copy(src.at[e], dst.at[slot], sem.at[slot])
                for src, dst in ((wg_hbm, wgf), (wu_hbm, wuf), (wd_hbm, wdf))]

    @pl.when(first == 0)
    def _():
        run[0] = -1
        for c in fetch(be_ref[0], 0):
            c.start()

    slots = []
    for part in range(FFN_PER_STEP):
        blk = first + part

        @pl.when((blk < nused) & ((blk == 0) | (be_ref[blk] != be_ref[jnp.maximum(blk - 1, 0)])))
        def _(blk=blk):
            r = run[0] + 1
            run[0] = r
            fslot, slot = r % 2, r % FFN_PER_STEP
            for c in fetch(be_ref[blk], fslot):
                c.wait()
            wgb[slot] = wgf[fslot].astype(BF16)
            wub[slot] = wuf[fslot].astype(BF16)
            wdb[slot] = wdf[fslot].astype(BF16)

            @pl.when(nx_ref[blk] >= 0)
            def _():
                for c in fetch(nx_ref[blk], 1 - fslot):
                    c.start()

        slots.append(run[0] % FFN_PER_STEP)

    @pl.when(first < nused)
    def _():
        for part, slot in enumerate(slots):
            base = part * FFN_BLOCK * TILE_ROWS
            x = jnp.concatenate([c.astype(BF16) for c in _from_row_tiles(x_ref, FFN_BLOCK, base)], axis=1)
            act = _silu(jnp.dot(x, wgb[slot], preferred_element_type=F32)) * jnp.dot(
                x, wub[slot], preferred_element_type=F32)
            _to_row_tiles(y_ref, jnp.dot(act.astype(BF16), wdb[slot], preferred_element_type=F32), base)

    @pl.when(first >= nused)
    def _():
        y_ref[...] = jnp.zeros(y_ref.shape, U32)


def _routed_experts(xs, block_e, next_e, nused, w_gate, w_up, w_down):
    rows = xs.shape[0] // TILE_ROWS
    nblk = rows // FFN_BLOCK
    _, d, hid = w_gate.shape
    step_rows = FFN_PER_STEP * FFN_BLOCK * TILE_ROWS
    hbm = pl.BlockSpec(memory_space=pl.ANY)
    grid_spec = pltpu.PrefetchScalarGridSpec(
        num_scalar_prefetch=3,
        grid=(nblk // FFN_PER_STEP,),
        in_specs=[pl.BlockSpec((step_rows, LANES),
                               lambda i, be, nx, nu: (jnp.minimum(i, (nu[0] - 1) // FFN_PER_STEP), 0)),
                  hbm, hbm, hbm],
        out_specs=pl.BlockSpec((step_rows, LANES), lambda i, be, nx, nu: (i, 0)),
        scratch_shapes=[pltpu.VMEM((2, d, hid), F32), pltpu.VMEM((2, d, hid), F32), pltpu.VMEM((2, hid, d), F32),
                        pltpu.VMEM((FFN_PER_STEP, d, hid), BF16), pltpu.VMEM((FFN_PER_STEP, d, hid), BF16),
                        pltpu.VMEM((FFN_PER_STEP, hid, d), BF16),
                        pltpu.SemaphoreType.DMA((2,)), pltpu.SMEM((1,), jnp.int32)],
    )
    return pl.pallas_call(
        _ffn_kernel,
        grid_spec=grid_spec,
        out_shape=jax.ShapeDtypeStruct((rows * TILE_ROWS, LANES), U32),
        compiler_params=_params(("arbitrary",)),
        name="routed_experts",
    )(block_e, next_e, nused, xs, w_gate, w_up, w_down)


def _gather_start(idx_ref, src_hbm, dst_vmem, sem):
    nk, tm = idx_ref.shape
    for k in range(nk):
        def body(jj, carry, k=k):
            for u in range(ISSUE_UNROLL):
                j = jj * ISSUE_UNROLL + u
                _tile_copy(src_hbm, idx_ref[k, j], dst_vmem, k * tm + j, sem).start(priority=u % 2)
            return carry
        lax.fori_loop(0, tm // ISSUE_UNROLL, body, 0)


def _gather_wait(src_hbm, dst_vmem, sem):
    pltpu.make_async_copy(src_hbm.at[pl.ds(0, dst_vmem.shape[0]), :], dst_vmem, sem).wait()


def _combine_kernel(dst_ref, dstn_ref, wt_ref, xp_ref, g2_ref, gfin_ref, ys_hbm, o_ref, yb0, yb1, sem, *, nsteps):
    i = pl.program_id(0)
    tm = COMBINE_TILE
    bufs = (yb0, yb1)

    @pl.when(i == 0)
    def _():
        _gather_start(dst_ref, ys_hbm, yb0, sem.at[0])

    for slot in range(2):
        @pl.when((i + 1 < nsteps) & ((i + 1) % 2 == slot))
        def _(slot=slot):
            _gather_start(dstn_ref, ys_hbm, bufs[slot], sem.at[slot])

    for slot in range(2):
        @pl.when(i % 2 == slot)
        def _(slot=slot):
            _gather_wait(ys_hbm, bufs[slot], sem.at[slot])
            wt = wt_ref[...]
            cols = None
            for k in range(TOP_K):
                part = [c * wt[:, k:k + 1] for c in _from_row_tiles(bufs[slot], tm, base=k * tm * TILE_ROWS)]
                cols = part if cols is None else [a + b for a, b in zip(cols, part)]
            routed = jnp.concatenate(cols, axis=1)
            o_ref[0] = _rms(xp_ref[0] + g2_ref[0] * routed) * gfin_ref[...]


def _combine(ys, dest, wts_t, xp, gate2, g_final):
    bsz, s, d = xp.shape
    tm = COMBINE_TILE
    nt = s // tm
    nsteps = bsz * nt
    vec = pl.BlockSpec((1, 1, d), lambda i: (i // nt, 0, 0))
    return pl.pallas_call(
        functools.partial(_combine_kernel, nsteps=nsteps),
        grid=(nsteps,),
        in_specs=[pl.BlockSpec((TOP_K, tm), lambda i: (0, i), memory_space=pltpu.SMEM),
                  pl.BlockSpec((TOP_K, tm), lambda i: (0, jnp.minimum(i + 1, nsteps - 1)),
                               memory_space=pltpu.SMEM),
                  pl.BlockSpec((tm, TOP_K), lambda i: (i, 0)),
                  pl.BlockSpec((1, tm, d), lambda i: (i // nt, i % nt, 0)),
                  vec,
                  pl.BlockSpec((1, d), lambda i: (0, 0)),
                  pl.BlockSpec(memory_space=pl.ANY)],
        out_specs=pl.BlockSpec((1, tm, d), lambda i: (i // nt, i % nt, 0)),
        out_shape=jax.ShapeDtypeStruct((bsz, s, d), F32),
        scratch_shapes=[pltpu.VMEM((tm * TOP_K * TILE_ROWS, LANES), U32),
                        pltpu.VMEM((tm * TOP_K * TILE_ROWS, LANES), U32),
                        pltpu.SemaphoreType.DMA((2,))],
        compiler_params=_params(("arbitrary",)),
        name="combine_final_norm",
    )(dest, dest, wts_t, xp, gate2, g_final.reshape(1, d), ys)


def kernel(x, c, w_in, b_forget, g_fox_out, g_dil_out, w_out, w_ada, b_ada, w_router, router_bias,
           w_exp_gate, w_exp_up, w_exp_down, w_sh_gate, w_sh_up, w_sh_down, rel_bias, g_final):
    bsz, s, d = x.shape
    depth = w_in.shape[0]
    assert depth == 1 and d == 2 * TILE_ROWS * LANES and s % (DIL_BLOCK * DILATIONS[-1]) == 0
    assert all(s % t == 0 for t in (ROW_TILE, FORGET_BLOCK, FOX_TILE, DIL_CHUNK, COMBINE_TILE, SCATTER_TILE))
    assert (bsz * s) % ROUTE_TILE == 0 and (bsz * s * TOP_K // FFN_BLOCK + N_EXPERTS) % FFN_PER_STEP == 0
    l = 0
    mod = _modulation(c, w_ada[l], b_ada[l])
    shift1, scale1, gate1, shift2, scale2, gate2 = [m[:, None, :] for m in jnp.split(mod, 6, axis=-1)]

    qscale = HEAD_DIM ** -0.5
    o3 = 3 * WIDTH
    w = w_in[l]
    wq_t = (w[:, :WIDTH] * (qscale * LOG2E)).T.astype(BF16)
    wk = w[:, WIDTH:2 * WIDTH].astype(BF16)
    wv_t = w[:, 2 * WIDTH:o3].T.astype(BF16)
    w_flog = jnp.pad(w[:, o3:o3 + N_HEADS], ((0, 0), (0, LANES - N_HEADS))).astype(BF16)
    wd0 = o3 + N_HEADS
    w_dil = jnp.concatenate([w[:, wd0:wd0 + WIDTH] * qscale, w[:, wd0 + WIDTH:]], axis=1).astype(BF16)

    q_t, k_f, v_t, qkv_d, flog = _in_projection(x, scale1, shift1, wq_t, wk, wv_t, w_dil, w_flog)
    y_fox = _fox_attention(q_t, k_f, _forget_cumsum(flog, b_forget[l]), v_t)
    y_dil = _dilated_attention(qkv_d, _dilated_bias(rel_bias))

    wr = w_router[l].T
    wr_hi = wr.astype(BF16)
    wr_lo = (wr - wr_hi.astype(F32)).astype(BF16)
    wo = w_out[l].astype(BF16)
    xp, h3, logits_t = _out_projection(
        y_fox, y_dil, x, gate1, scale2, shift2, gate2, g_fox_out[l].reshape(1, WIDTH),
        g_dil_out[l].reshape(1, WIDTH), wo[:WIDTH], wo[WIDTH:], wr_hi, wr_lo,
        w_sh_gate[l].astype(BF16), w_sh_up[l].astype(BF16), w_sh_down[l].astype(BF16))

    eidx, wts, rank, counts = _route(logits_t, router_bias[l])
    rows = bsz * s * TOP_K + N_EXPERTS * FFN_BLOCK
    pstart, cend, pend, block_e, next_e, nused = _block_plan(counts, rows // FFN_BLOCK)
    dest = _dest_rows(eidx, rank, pstart)
    xs = _dispatch_rows(h3, dest, cend, pend, rows)
    ys = _routed_experts(xs, block_e, next_e, nused, w_exp_gate[l], w_exp_up[l], w_exp_down[l])
    return _combine(ys, dest, wts.T, xp, gate2, g_final)
```
